```python
import jax, jax.numpy as jnp
from jax import lax
import numpy as np

D_MODEL = 1024
BATCH = 32
SEQ = 256
DEPTH = 2
DEC_BATCH = 2
DEC_SEQ = 1024
PAST_LEN = 256

GRID_W = 64
HEAD_DIM = 64
N_HEADS_A = D_MODEL // (2 * HEAD_DIM)
N_KV_A = 2
GROUP_A = N_HEADS_A // N_KV_A
WINDOW = 128
Q_BLK = 128
N_HEADS_B = D_MODEL // (4 * HEAD_DIM)
NA_ROWS = 8
NA_COLS = 16
NA_QCOLS = 16
NA_KCOLS = 2 * NA_COLS
CONV_CH = D_MODEL // 4
CONV_W = 3
WIDTH_A = N_HEADS_A * HEAD_DIM
KV_WIDTH_A = N_KV_A * HEAD_DIM
WIDTH_B = N_HEADS_B * HEAD_DIM
MIX_WIDTH = WIDTH_A + WIDTH_B + CONV_CH
SPLIT_SIZES = (WIDTH_A, KV_WIDTH_A, KV_WIDTH_A, WIDTH_B, WIDTH_B, WIDTH_B, CONV_CH, CONV_CH, CONV_CH)
IN_COLS = WIDTH_A + 2 * KV_WIDTH_A + 3 * WIDTH_B + 3 * CONV_CH
N_EXPERTS = 32
TOP_K = 4
D_EXPERT = D_MODEL
SWIGLU_LIMIT = 7.0
SWIGLU_ALPHA = 1.702
MOE_BLK = 128
ROPE_BASE = 10000.0
EPS = 1e-6
NEG = -1e30
N_MOD = 6

kernel_name = 'hybrid_diffusion_prefix_trunk_step'


def rms_norm(x, g):
    x32 = x.astype(jnp.float32)
    y = x32 * lax.rsqrt(jnp.mean(x32 * x32, axis=-1, keepdims=True) + EPS)
    return (y * g.astype(jnp.float32)).astype(x.dtype)


def modulation(cond, w_mod, b_mod):
    return (jax.nn.silu(cond) @ w_mod + b_mod).reshape(cond.shape[0], N_MOD, D_MODEL)


def adaln(x, g, shift, scale):
    return rms_norm(x, g) * (1 + scale[:, None, :]) + shift[:, None, :]


def axial_rope(x):
    n = x.shape[1]
    half = HEAD_DIM // 2
    quarter = half // 2
    t = jnp.arange(n)
    inv = ROPE_BASE ** (-jnp.arange(quarter, dtype=jnp.float32) / quarter)
    bshape = (n,) + (1,) * (x.ndim - 3) + (quarter,)

    def rot(xh, pos):
        ang = (pos.astype(jnp.float32)[:, None] * inv).reshape(bshape)
        cos, sin = jnp.cos(ang), jnp.sin(ang)
        x1, x2 = xh[..., :quarter], xh[..., quarter:]
        return jnp.concatenate([x1 * cos - x2 * sin, x2 * cos + x1 * sin], axis=-1)

    x32 = x.astype(jnp.float32)
    out = jnp.concatenate([rot(x32[..., :half], t // GRID_W), rot(x32[..., half:], t % GRID_W)], axis=-1)
    return out.astype(x.dtype)


def project_heads(h, w_in, qn_a, kn_a, qn_b, kn_b):
    bsz, n, _ = h.shape
    p = h @ w_in
    offs = np.cumsum(SPLIT_SIZES)[:-1].tolist()
    qa, ka, va, qb, kb, vb, u, gb, gc = jnp.split(p, offs, axis=-1)
    qa = rms_norm(qa.reshape(bsz, n, N_KV_A, GROUP_A, HEAD_DIM), qn_a)
    ka = rms_norm(ka.reshape(bsz, n, N_KV_A, HEAD_DIM), kn_a)
    va = va.reshape(bsz, n, N_KV_A, HEAD_DIM)
    qb = rms_norm(qb.reshape(bsz, n, N_HEADS_B, HEAD_DIM), qn_b)
    kb = rms_norm(kb.reshape(bsz, n, N_HEADS_B, HEAD_DIM), kn_b)
    vb = vb.reshape(bsz, n, N_HEADS_B, HEAD_DIM)
    return qa, ka, va, qb, kb, vb, u, gb, gc


def gated_short_conv(u, gb, gc, conv_w):
    z = gc * u
    n = z.shape[1]
    pad = CONV_W // 2
    zp = jnp.pad(z, ((0, 0), (pad, pad), (0, 0)))
    y = sum(zp[:, i:i + n] * conv_w[i] for i in range(CONV_W))
    return gb * y


def context_attention(q, k, v, sink):
    bsz, s_len = q.shape[:2]
    nb = s_len // Q_BLK
    scale = HEAD_DIM ** -0.5
    qb = jnp.moveaxis(q.reshape((bsz, nb, Q_BLK) + q.shape[2:]), 1, 0)

    def block(qi):
        s = jnp.einsum('bqkgd,bpkd->bqkgp', qi, k).astype(jnp.float32) * scale
        if sink is None:
            p = jax.nn.softmax(s, axis=-1)
        else:
            sk = jnp.broadcast_to(sink.astype(jnp.float32)[:, :, None], s.shape[:-1] + (1,))
            p = jax.nn.softmax(jnp.concatenate([sk, s], axis=-1), axis=-1)[..., 1:]
        return jnp.einsum('bqkgp,bpkd->bqkgd', p.astype(v.dtype), v)

    out = lax.map(block, qb)
    return jnp.moveaxis(out, 0, 1).reshape(q.shape)


def window_attention(q, k, v, k_ctx, v_ctx, sink):
    bsz, n = q.shape[:2]
    nb = n // Q_BLK
    span = Q_BLK + 2 * WINDOW
    scale = HEAD_DIM ** -0.5
    pad = ((0, 0), (WINDOW, WINDOW), (0, 0), (0, 0))
    kidx = jnp.arange(nb)[:, None] * Q_BLK + jnp.arange(span)[None, :]
    kb = jnp.pad(k, pad)[:, kidx]
    vb = jnp.pad(v, pad)[:, kidx]
    qpos = jnp.arange(nb)[:, None] * Q_BLK + jnp.arange(Q_BLK)[None, :]
    kpos = kidx - WINDOW
    ok = (jnp.abs(qpos[:, :, None] - kpos[:, None, :]) <= WINDOW) & (kpos[:, None, :] >= 0) & (kpos[:, None, :] < n)
    qb = q.reshape((bsz, nb, Q_BLK) + q.shape[2:])
    s_win = jnp.einsum('bnqkgd,bnskd->bnqkgs', qb, kb).astype(jnp.float32) * scale
    s_win = jnp.where(ok[None, :, :, None, None, :], s_win, NEG)
    s_ctx = jnp.einsum('bnqkgd,bpkd->bnqkgp', qb, k_ctx).astype(jnp.float32) * scale
    sk = jnp.broadcast_to(sink.astype(jnp.float32)[:, :, None], s_win.shape[:-1] + (1,))
    p = jax.nn.softmax(jnp.concatenate([sk, s_win, s_ctx], axis=-1), axis=-1).astype(v.dtype)
    p_win, p_ctx = p[..., 1:1 + span], p[..., 1 + span:]
    out = jnp.einsum('bnqkgs,bnskd->bnqkgd', p_win, vb) + jnp.einsum('bnqkgp,bpkd->bnqkgd', p_ctx, v_ctx)
    return out.reshape(q.shape)


def neighbourhood_attention(q, k, v, k_ctx, v_ctx, rel_bias):
    bsz, n, nh, hd = q.shape
    rows = n // GRID_W
    kr = min(NA_ROWS, rows)
    ncb = GRID_W // NA_QCOLS
    scale = HEAD_DIM ** -0.5
    r = jnp.arange(rows)
    row_start = jnp.clip(r - kr // 2, 0, rows - kr)
    key_rows = row_start[:, None] + jnp.arange(kr)[None, :]
    qcol = jnp.arange(GRID_W).reshape(ncb, NA_QCOLS)
    col_start = jnp.clip(qcol - NA_COLS // 2, 0, GRID_W - NA_COLS)
    blk_start = jnp.clip(qcol[:, 0] - NA_COLS // 2, 0, GRID_W - NA_KCOLS)
    key_cols = blk_start[:, None] + jnp.arange(NA_KCOLS)[None, :]
    kg = k.reshape(bsz, rows, GRID_W, nh, hd)
    vg = v.reshape(bsz, rows, GRID_W, nh, hd)
    ridx = key_rows[:, None, :, None]
    cidx = key_cols[None, :, None, :]
    kb = kg[:, ridx, cidx]
    vb = vg[:, ridx, cidx]
    qb = q.reshape(bsz, rows, ncb, NA_QCOLS, nh, hd)
    s = jnp.einsum('brjqhd,brjmnhd->brjqhmn', qb, kb).astype(jnp.float32) * scale
    dr = key_rows - r[:, None]
    dc = jnp.clip(key_cols[:, None, :] - qcol[:, :, None], -(NA_COLS - 1), NA_COLS - 1)
    bias = rel_bias.astype(jnp.float32)[:, dr[:, None, None, :, None] + NA_ROWS - 1, dc[None, :, :, None, :] + NA_COLS - 1]
    s = s + jnp.moveaxis(bias, 0, 3)[None]
    col_ok = (key_cols[:, None, :] >= col_start[:, :, None]) & (key_cols[:, None, :] < col_start[:, :, None] + NA_COLS)
    s = jnp.where(col_ok[None, None, :, :, None, None, :], s, NEG)
    s_ctx = jnp.einsum('brjqhd,bphd->brjqhp', qb, k_ctx).astype(jnp.float32) * scale
    nwin = kr * NA_KCOLS
    logits = jnp.concatenate([s.reshape(s.shape[:5] + (nwin,)), s_ctx], axis=-1)
    p = jax.nn.softmax(logits, axis=-1).astype(v.dtype)
    p_win = p[..., :nwin].reshape(s.shape)
    p_ctx = p[..., nwin:]
    out = jnp.einsum('brjqhmn,brjmnhd->brjqhd', p_win, vb) + jnp.einsum('brjqhp,bphd->brjqhd', p_ctx, v_ctx)
    return out.reshape(bsz, n, nh, hd)


def moe_ffn(h, w_router, b_router, w_gate, b_gate, w_up, b_up, w_down, b_down):
    bsz, n, d = h.shape
    x = h.reshape(bsz * n, d)
    t = x.shape[0]
    logits = (x @ w_router + b_router).astype(jnp.float32)
    top_logit, top_e = lax.top_k(logits, TOP_K)
    gates = jax.nn.softmax(top_logit, axis=-1).astype(x.dtype)
    flat_e = top_e.reshape(-1)
    n_assign = t * TOP_K
    order = jnp.argsort(flat_e)
    sorted_e = flat_e[order]
    tok = (order // TOP_K).astype(jnp.int32)
    gate_sorted = gates.reshape(-1)[order]
    counts = jnp.bincount(flat_e, length=N_EXPERTS)
    padded = (counts + MOE_BLK - 1) // MOE_BLK * MOE_BLK
    pad_end = jnp.cumsum(padded)
    pad_start = pad_end - padded
    start = jnp.cumsum(counts) - counts
    dest = pad_start[sorted_e] + jnp.arange(n_assign) - start[sorted_e]
    n_blocks = -(-n_assign // MOE_BLK) + N_EXPERTS
    n_rows = n_blocks * MOE_BLK
    row_tok = jnp.full((n_rows,), t, jnp.int32).at[dest].set(tok)
    row_gate = jnp.zeros((n_rows,), x.dtype).at[dest].set(gate_sorted)
    blk_e = jnp.minimum(jnp.searchsorted(pad_end, jnp.arange(n_blocks) * MOE_BLK, side='right'), N_EXPERTS - 1)
    x_rows = jnp.concatenate([x, jnp.zeros((1, d), x.dtype)], axis=0)[row_tok].reshape(n_blocks, MOE_BLK, d)

    def expert_block(args):
        xb, e = args
        g = jnp.minimum(xb @ w_gate[e] + b_gate[e], SWIGLU_LIMIT)
        u = jnp.clip(xb @ w_up[e] + b_up[e], -SWIGLU_LIMIT, SWIGLU_LIMIT)
        a = g * jax.nn.sigmoid(SWIGLU_ALPHA * g) * (u + 1)
        return a @ w_down[e] + b_down[e]

    y_rows = lax.map(expert_block, (x_rows, blk_e)).reshape(n_rows, d)
    y = jnp.zeros((t + 1, d), x.dtype).at[row_tok].add(y_rows * row_gate[:, None])[:t]
    return y.reshape(bsz, n, d)


def setup_inputs(seed: int = 0) -> dict:
    key = jax.random.key(seed)
    ks = jax.random.split(key, 29)
    f32 = jnp.float32

    def nrm(k, shape, s):
        return jax.random.normal(k, shape, f32) * s

    d, nl, ne, fe = D_MODEL, DEPTH, N_EXPERTS, D_EXPERT
    return {
        'x_prompt': nrm(ks[0], (BATCH, SEQ, d), 1.0),
        'x_sample': nrm(ks[1], (DEC_BATCH, DEC_SEQ, d), 1.0),
        'cache_k_win': nrm(ks[2], (DEC_BATCH, DEPTH, PAST_LEN, N_KV_A, HEAD_DIM), 1.0),
        'cache_v_win': nrm(ks[3], (DEC_BATCH, DEPTH, PAST_LEN, N_KV_A, HEAD_DIM), 1.0),
        'cache_k_nbr': nrm(ks[4], (DEC_BATCH, DEPTH, PAST_LEN, N_HEADS_B, HEAD_DIM), 1.0),
        'cache_v_nbr': nrm(ks[5], (DEC_BATCH, DEPTH, PAST_LEN, N_HEADS_B, HEAD_DIM), 1.0),
        'c': nrm(ks[6], (DEC_BATCH, d), 1.0),
        'c_ctx': nrm(ks[7], (d,), 1.0),
        'w_mod': nrm(ks[8], (nl, d, N_MOD * d), 0.5 * d ** -0.5),
        'b_mod': nrm(ks[9], (nl, N_MOD * d), 0.02),
        'g_mix': 1.0 + nrm(ks[10], (nl, d), 0.05),
        'g_ffn': 1.0 + nrm(ks[11], (nl, d), 0.05),
        'w_in': nrm(ks[12], (nl, d, IN_COLS), d ** -0.5),
        'w_out': nrm(ks[13], (nl, MIX_WIDTH, d), MIX_WIDTH ** -0.5),
        'qn_win': 1.0 + nrm(ks[14], (nl, HEAD_DIM), 0.05),
        'kn_win': 1.0 + nrm(ks[15], (nl, HEAD_DIM), 0.05),
        'qn_nbr': 1.0 + nrm(ks[16], (nl, HEAD_DIM), 0.05),
        'kn_nbr': 1.0 + nrm(ks[17], (nl, HEAD_DIM), 0.05),
        'sink_win': nrm(ks[18], (nl, N_HEADS_A), 0.5),
        'rel_bias_nbr': nrm(ks[19], (nl, N_HEADS_B, 2 * NA_ROWS - 1, 2 * NA_COLS - 1), 0.1),
        'conv_w': nrm(ks[20], (nl, CONV_W, CONV_CH), CONV_W ** -0.5),
        'w_router': nrm(ks[21], (nl, d, ne), d ** -0.5),
        'b_router': nrm(ks[22], (nl, ne), 0.01),
        'w_gate': nrm(ks[23], (nl, ne, d, fe), d ** -0.5),
        'b_gate': nrm(ks[24], (nl, ne, fe), 0.01),
        'w_up': nrm(ks[25], (nl, ne, d, fe), d ** -0.5),
        'b_up': nrm(ks[26], (nl, ne, fe), 0.01),
        'w_down': nrm(ks[27], (nl, ne, fe, d), fe ** -0.5),
        'b_down': nrm(ks[28], (nl, ne, d), 0.01),
    }


def reference(x_prompt, x_sample, cache_k_win, cache_v_win, cache_k_nbr, cache_v_nbr, c, c_ctx,
              w_mod, b_mod, g_mix, g_ffn, w_in, w_out, qn_win, kn_win, qn_nbr, kn_nbr, sink_win,
              rel_bias_nbr, conv_w, w_router, b_router, w_gate, b_gate, w_up, b_up, w_down, b_down):
    xp, xs = x_prompt, x_sample
    bsz, n_ctx = xp.shape[:2]
    dbs, n_lat = xs.shape[:2]
    k_win_l, v_win_l, k_nbr_l, v_nbr_l = [], [], [], []
    for l in range(DEPTH):
        heads = (w_in[l], qn_win[l], kn_win[l], qn_nbr[l], kn_nbr[l])
        experts = (w_router[l], b_router[l], w_gate[l], b_gate[l], w_up[l], b_up[l], w_down[l], b_down[l])
        sink = sink_win[l].reshape(N_KV_A, GROUP_A)

        m = modulation(c_ctx[None], w_mod[l], b_mod[l])
        h = adaln(xp, g_mix[l], m[:, 0], m[:, 1])
        qa, ka, va, qb, kb, vb, u, gb, gc = project_heads(h, *heads)
        oa = context_attention(qa, ka, va, sink)
        ob = context_attention(qb[:, :, :, None], kb, vb, None)
        oc = gated_short_conv(u, gb, gc, conv_w[l])
        mix = jnp.concatenate([oa.reshape(bsz, n_ctx, WIDTH_A), ob.reshape(bsz, n_ctx, WIDTH_B), oc], axis=-1) @ w_out[l]
        xp = xp + m[:, 2][:, None, :] * mix
        h = adaln(xp, g_ffn[l], m[:, 3], m[:, 4])
        xp = xp + m[:, 5][:, None, :] * moe_ffn(h, *experts)
        k_win_l.append(ka)
        v_win_l.append(va)
        k_nbr_l.append(kb)
        v_nbr_l.append(vb)

        m = modulation(c, w_mod[l], b_mod[l])
        h = adaln(xs, g_mix[l], m[:, 0], m[:, 1])
        qa, ka, va, qb, kb, vb, u, gb, gc = project_heads(h, *heads)
        qa = axial_rope(qa)
        ka = axial_rope(ka)
        oa = window_attention(qa, ka, va, cache_k_win[:, l], cache_v_win[:, l], sink)
        ob = neighbourhood_attention(qb, kb, vb, cache_k_nbr[:, l], cache_v_nbr[:, l], rel_bias_nbr[l])
        oc = gated_short_conv(u, gb, gc, conv_w[l])
        mix = jnp.concatenate([oa.reshape(dbs, n_lat, WIDTH_A), ob.reshape(dbs, n_lat, WIDTH_B), oc], axis=-1) @ w_out[l]
        xs = xs + m[:, 2][:, None, :] * mix
        h = adaln(xs, g_ffn[l], m[:, 3], m[:, 4])
        xs = xs + m[:, 5][:, None, :] * moe_ffn(h, *experts)

    new_k_win = jnp.stack(k_win_l, axis=1)
    new_v_win = jnp.stack(v_win_l, axis=1)
    new_k_nbr = jnp.stack(k_nbr_l, axis=1)
    new_v_nbr = jnp.stack(v_nbr_l, axis=1)
    return (xp, xs, new_k_win, new_v_win, new_k_nbr, new_v_nbr)
```

```python
import functools

import numpy as np
import jax
import jax.numpy as jnp
from jax import lax
from jax.experimental import pallas as pl
from jax.experimental.pallas import tpu as pltpu

F32 = jnp.float32
BF16 = jnp.bfloat16

D_MODEL = 1024
HEAD_DIM = 64
GRID_W = 64
N_HEADS_A = 8
N_KV_A = 2
GROUP_A = N_HEADS_A // N_KV_A
WINDOW = 128
Q_BLK = 128
N_HEADS_B = 4
NA_ROWS = 8
NA_COLS = 16
CONV_CH = 256
CONV_W = 3
WIDTH_A = N_HEADS_A * HEAD_DIM
KV_WIDTH_A = N_KV_A * HEAD_DIM
WIDTH_B = N_HEADS_B * HEAD_DIM
IN_COLS = WIDTH_A + 2 * KV_WIDTH_A + 3 * WIDTH_B + 3 * CONV_CH
N_EXPERTS = 32
TOP_K = 4
SWIGLU_LIMIT = 7.0
SWIGLU_ALPHA = 1.702
ROPE_BASE = 10000.0
EPS = 1e-6
NEG = -1e30
N_MOD = 6

C_QA = 0
C_KA = C_QA + WIDTH_A
C_VA = C_KA + KV_WIDTH_A
C_QB = C_VA + KV_WIDTH_A
C_KB = C_QB + WIDTH_B
C_VB = C_KB + WIDTH_B
C_U = C_VB + WIDTH_B
C_GB = C_U + CONV_CH
C_GC = C_GB + CONV_CH

MXU_COLS_V7X = 256
COND_ROWS = 8
MOE_BLK = 256
VMEM_LIMIT_V7X = 56 * 1024 * 1024

_NT = (((1,), (1,)), ((), ()))


def _params(sem, vmem=None):
    return pltpu.CompilerParams(dimension_semantics=sem, vmem_limit_bytes=vmem)


def _mod_kernel(c_ref, w_ref, b_ref, o_ref):
    c = c_ref[...]
    s = c * jax.nn.sigmoid(c)
    o_ref[...] = jnp.dot(s.astype(BF16), w_ref[...].astype(BF16), preferred_element_type=F32) + b_ref[...]


def _modulation(cond, w_mod, b_mod):
    depth = w_mod.shape[0]
    out = pl.pallas_call(
        _mod_kernel,
        out_shape=jax.ShapeDtypeStruct((depth, COND_ROWS, N_MOD * D_MODEL), F32),
        grid=(depth, N_MOD),
        in_specs=[
            pl.BlockSpec((COND_ROWS, D_MODEL), lambda l, j: (0, 0)),
            pl.BlockSpec((None, D_MODEL, D_MODEL), lambda l, j: (l, 0, j)),
            pl.BlockSpec((None, 1, D_MODEL), lambda l, j: (l, 0, j)),
        ],
        out_specs=pl.BlockSpec((None, COND_ROWS, D_MODEL), lambda l, j: (l, 0, j)),
        compiler_params=_params(("arbitrary", "arbitrary")),
        name="modulation",
    )(cond, w_mod, b_mod.reshape(depth, 1, N_MOD * D_MODEL))
    return out.reshape(depth, COND_ROWS, N_MOD, D_MODEL)


def _head_norm(x, w_row, ones_ref):
    width = x.shape[1]
    sq = (x * x).astype(BF16)
    parts = []
    for c0 in range(0, width, MXU_COLS_V7X):
        wd = min(MXU_COLS_V7X, width - c0)
        parts.append(jnp.dot(sq[:, c0:c0 + wd], ones_ref[:wd, :wd], preferred_element_type=F32))
    ss = parts[0] if len(parts) == 1 else jnp.concatenate(parts, axis=-1)
    return x * lax.rsqrt(ss * (1.0 / HEAD_DIM) + EPS) * w_row


def _rope(x, cos, sin):
    width = x.shape[1]
    lane = lax.broadcasted_iota(jnp.int32, x.shape, 1)
    quarter = HEAD_DIM // 4
    partner = jnp.where((lane % (2 * quarter)) < quarter,
                        pltpu.roll(x, width - quarter, axis=1), pltpu.roll(x, quarter, axis=1))
    reps = width // cos.shape[1]
    cos_w = cos if reps == 1 else jnp.concatenate([cos] * reps, axis=-1)
    sin_w = sin if reps == 1 else jnp.concatenate([sin] * reps, axis=-1)
    return x * cos_w + partner * sin_w


def _inproj_kernel(*refs, seq_len, rope):
    if rope:
        (x_ref, mod_ref, g_ref, w_ref, nw_ref, ones_ref, cw_ref, cos_ref, sin_ref,
         qa_o, qb_o, ka_o, va_o, kb_o, vb_o, oc_o) = refs
    else:
        (x_ref, mod_ref, g_ref, w_ref, nw_ref, ones_ref, cw_ref,
         qa_o, qb_o, ka_o, va_o, kb_o, vb_o, oc_o) = refs
    x = x_ref[...]
    tm = x.shape[0]
    h = x * lax.rsqrt(jnp.mean(x * x, axis=-1, keepdims=True) + EPS) * g_ref[...]
    h = h * (1.0 + mod_ref[1:2, :]) + mod_ref[0:1, :]
    p = jnp.dot(h.astype(BF16), w_ref[...], preferred_element_type=F32)

    qa = _head_norm(p[:, C_QA:C_KA], nw_ref[:, 0:WIDTH_A], ones_ref)
    ka = _head_norm(p[:, C_KA:C_VA], nw_ref[:, WIDTH_A:WIDTH_A + KV_WIDTH_A], ones_ref)
    o_qb = WIDTH_A + KV_WIDTH_A
    qb = _head_norm(p[:, C_QB:C_KB], nw_ref[:, o_qb:o_qb + WIDTH_B], ones_ref)
    kb = _head_norm(p[:, C_KB:C_VB], nw_ref[:, o_qb + WIDTH_B:o_qb + 2 * WIDTH_B], ones_ref)
    if rope:
        cos, sin = cos_ref[...], sin_ref[...]
        qa = _rope(qa, cos, sin)
        ka = _rope(ka, cos, sin)
    qa_o[...] = qa.astype(BF16)
    qb_o[...] = qb.astype(BF16)
    ka_o[...] = ka
    va_o[...] = p[:, C_VA:C_QB]
    kb_o[...] = kb
    vb_o[...] = p[:, C_VB:C_U]

    z = p[:, C_GC:C_GC + CONV_CH] * p[:, C_U:C_GB]
    row = lax.broadcasted_iota(jnp.int32, z.shape, 0) % seq_len
    z_prev = jnp.where(row == 0, 0.0, pltpu.roll(z, 1, axis=0))
    z_next = jnp.where(row == seq_len - 1, 0.0, pltpu.roll(z, tm - 1, axis=0))
    y = z_prev * cw_ref[0:1, :] + z * cw_ref[1:2, :] + z_next * cw_ref[2:3, :]
    oc_o[...] = (p[:, C_GB:C_GC] * y).astype(BF16)


def _inproj(x, mod, group_of_tile, g_mix, w_in_bf, layer, norm_w, ones_bd, conv_w, tm, seq_len, rope_tabs=None):
    t = x.shape[0]
    rope = rope_tabs is not None
    in_specs = [
        pl.BlockSpec((tm, D_MODEL), lambda i: (i, 0)),
        pl.BlockSpec((None, None, N_MOD, D_MODEL), lambda i: (layer, group_of_tile(i), 0, 0)),
        pl.BlockSpec((None, 1, D_MODEL), lambda i: (layer, 0, 0)),
        pl.BlockSpec((None, D_MODEL, IN_COLS), lambda i: (layer, 0, 0)),
        pl.BlockSpec((None, 1, norm_w.shape[-1]), lambda i: (layer, 0, 0)),
        pl.BlockSpec(ones_bd.shape, lambda i: (0, 0)),
        pl.BlockSpec((None, CONV_W, CONV_CH), lambda i: (layer, 0, 0)),
    ]
    args = [x, mod, g_mix, w_in_bf, norm_w, ones_bd, conv_w]
    if rope:
        in_specs += [pl.BlockSpec(rope_tabs[0].shape, lambda i: (0, 0))] * 2
        args += list(rope_tabs)
    widths = (WIDTH_A, WIDTH_B, KV_WIDTH_A, KV_WIDTH_A, WIDTH_B, WIDTH_B, CONV_CH)
    dtypes = (BF16, BF16, F32, F32, F32, F32, BF16)
    return pl.pallas_call(
        functools.partial(_inproj_kernel, seq_len=seq_len, rope=rope),
        out_shape=[jax.ShapeDtypeStruct((t, w), dt) for w, dt in zip(widths, dtypes)],
        grid=(t // tm,),
        in_specs=in_specs,
        out_specs=[pl.BlockSpec((tm, w), lambda i: (i, 0)) for w in widths],
        compiler_params=_params(("arbitrary",), VMEM_LIMIT_V7X),
        name="inproj_rope" if rope else "inproj",
    )(*args)


def _softmax_pv(s, v, sink=None):
    m = jnp.max(s, axis=-1, keepdims=True)
    if sink is not None:
        m = jnp.maximum(m, sink)
    p = jnp.exp(s - m)
    den = jnp.sum(p, axis=-1, keepdims=True)
    if sink is not None:
        den = den + jnp.exp(sink - m)
    return jnp.dot(p.astype(BF16), v, preferred_element_type=F32) / den


def _store_heads(o_ref, outs):
    for i in range(0, len(outs), 2):
        o_ref[:, i * HEAD_DIM:(i + 2) * HEAD_DIM] = jnp.concatenate(outs[i:i + 2], axis=-1).astype(o_ref.dtype)


def _ctx_attn_kernel(sink_ref, qa_ref, ka_ref, va_ref, qb_ref, kb_ref, vb_ref, oa_o, ob_o):
    scale = HEAD_DIM ** -0.5
    ka = ka_ref[...].astype(BF16)
    va = va_ref[...].astype(BF16)
    outs = []
    for h in range(N_HEADS_A):
        j = h // GROUP_A
        q = qa_ref[:, h * HEAD_DIM:(h + 1) * HEAD_DIM] * scale
        s = lax.dot_general(q, ka[:, j * HEAD_DIM:(j + 1) * HEAD_DIM], _NT, preferred_element_type=F32)
        outs.append(_softmax_pv(s, va[:, j * HEAD_DIM:(j + 1) * HEAD_DIM], sink_ref[h]))
    _store_heads(oa_o, outs)
    kb = kb_ref[...].astype(BF16)
    vb = vb_ref[...].astype(BF16)
    outs = []
    for h in range(N_HEADS_B):
        sl = slice(h * HEAD_DIM, (h + 1) * HEAD_DIM)
        q = qb_ref[:, sl] * scale
        s = lax.dot_general(q, kb[:, sl], _NT, preferred_element_type=F32)
        outs.append(_softmax_pv(s, vb[:, sl]))
    _store_heads(ob_o, outs)


def _ctx_attention(sink, qa, ka, va, qb, kb, vb, seq_len):
    t = qa.shape[0]
    widths = (WIDTH_A, KV_WIDTH_A, KV_WIDTH_A, WIDTH_B, WIDTH_B, WIDTH_B)
    return pl.pallas_call(
        _ctx_attn_kernel,
        out_shape=[jax.ShapeDtypeStruct((t, WIDTH_A), BF16), jax.ShapeDtypeStruct((t, WIDTH_B), BF16)],
        grid=(t // seq_len,),
        in_specs=[pl.BlockSpec(memory_space=pltpu.SMEM)]
        + [pl.BlockSpec((seq_len, w), lambda i: (i, 0)) for w in widths],
        out_specs=[pl.BlockSpec((seq_len, WIDTH_A), lambda i: (i, 0)),
                   pl.BlockSpec((seq_len, WIDTH_B), lambda i: (i, 0))],
        compiler_params=_params(("arbitrary",)),
        name="ctx_attention",
    )(sink, qa, ka, va, qb, kb, vb)


def _win_attn_kernel(sink_ref, qa_ref, ka_ref, va_ref, ck_ref, cv_ref, oa_o, *, n_lat):
    scale = HEAD_DIM ** -0.5
    span = Q_BLK + 2 * WINDOW
    n = pl.program_id(1)
    start = pl.multiple_of(jnp.clip(n * Q_BLK - WINDOW, 0, n_lat - span), Q_BLK)
    kcat = jnp.concatenate([ka_ref[pl.ds(start, span), :], ck_ref[...]], axis=0).astype(BF16)
    vcat = jnp.concatenate([va_ref[pl.ds(start, span), :], cv_ref[...]], axis=0).astype(BF16)
    n_keys = kcat.shape[0]
    qpos = n * Q_BLK + lax.broadcasted_iota(jnp.int32, (Q_BLK, n_keys), 0)
    col = lax.broadcasted_iota(jnp.int32, (Q_BLK, n_keys), 1)
    ok = (col >= span) | (jnp.abs(qpos - (start + col)) <= WINDOW)
    outs = []
    for h in range(N_HEADS_A):
        j = h // GROUP_A
        q = qa_ref[:, h * HEAD_DIM:(h + 1) * HEAD_DIM] * scale
        s = lax.dot_general(q, kcat[:, j * HEAD_DIM:(j + 1) * HEAD_DIM], _NT, preferred_element_type=F32)
        s = jnp.where(ok, s, NEG)
        outs.append(_softmax_pv(s, vcat[:, j * HEAD_DIM:(j + 1) * HEAD_DIM], sink_ref[h]))
    _store_heads(oa_o, outs)


def _win_attention(sink, qa, ka, va, cache_k, cache_v, layer, n_lat):
    t = qa.shape[0]
    nb = n_lat // Q_BLK
    past = cache_k.shape[2]
    return pl.pallas_call(
        functools.partial(_win_attn_kernel, n_lat=n_lat),
        out_shape=jax.ShapeDtypeStruct((t, WIDTH_A), BF16),
        grid=(t // n_lat, nb),
        in_specs=[
            pl.BlockSpec(memory_space=pltpu.SMEM),
            pl.BlockSpec((Q_BLK, WIDTH_A), lambda b, n: (b * nb + n, 0)),
            pl.BlockSpec((n_lat, KV_WIDTH_A), lambda b, n: (b, 0)),
            pl.BlockSpec((n_lat, KV_WIDTH_A), lambda b, n: (b, 0)),
            pl.BlockSpec((None, None, past, KV_WIDTH_A), lambda b, n: (b, layer, 0, 0)),
            pl.BlockSpec((None, None, past, KV_WIDTH_A), lambda b, n: (b, layer, 0, 0)),
        ],
        out_specs=pl.BlockSpec((Q_BLK, WIDTH_A), lambda b, n: (b * nb + n, 0)),
        compiler_params=_params(("arbitrary", "arbitrary")),
        name="window_attention",
    )(sink, qa, ka, va, cache_k, cache_v)


def _nbr_attn_kernel(qb_ref, kb_ref, vb_ref, ck_ref, cv_ref, bias_ref, ob_o, *, rows, kr):
    scale = HEAD_DIM ** -0.5
    r = pl.program_id(1)
    start = pl.multiple_of(jnp.clip(r - kr // 2, 0, rows - kr) * GRID_W, GRID_W)
    nwin = kr * GRID_W
    kcat = jnp.concatenate([kb_ref[pl.ds(start, nwin), :], ck_ref[...]], axis=0).astype(BF16)
    vcat = jnp.concatenate([vb_ref[pl.ds(start, nwin), :], cv_ref[...]], axis=0).astype(BF16)
    past = ck_ref.shape[0]
    outs = []
    for h in range(N_HEADS_B):
        sl = slice(h * HEAD_DIM, (h + 1) * HEAD_DIM)
        q = qb_ref[:, sl] * scale
        s = lax.dot_general(q, kcat[:, sl], _NT, preferred_element_type=F32)
        bias = jnp.concatenate([bias_ref[h], jnp.zeros((GRID_W, past), F32)], axis=-1)
        outs.append(_softmax_pv(s + bias, vcat[:, sl]))
    _store_heads(ob_o, outs)


def _nbr_bias_table(rel_bias, rows):
    kr = min(NA_ROWS, rows)
    cls = np.arange(kr)[:, None, None, None]
    c = np.arange(GRID_W)[None, :, None, None]
    m = np.arange(kr)[None, None, :, None]
    kc = np.arange(GRID_W)[None, None, None, :]
    dr = np.broadcast_to(m - cls + NA_ROWS - 1, (kr, GRID_W, kr, GRID_W))
    dc = np.broadcast_to(np.clip(kc - c, -(NA_COLS - 1), NA_COLS - 1) + NA_COLS - 1, dr.shape)
    col_start = np.clip(c - NA_COLS // 2, 0, GRID_W - NA_COLS)
    ok = np.broadcast_to((kc >= col_start) & (kc < col_start + NA_COLS), dr.shape)
    valid = (dr >= 0) & (dr < 2 * NA_ROWS - 1)
    tab = rel_bias.astype(F32)[:, np.where(valid, dr, 0), dc]
    tab = jnp.where(ok & valid, tab, NEG)
    return tab.reshape(rel_bias.shape[0], kr, GRID_W, kr * GRID_W)


def _nbr_attention(qb, kb, vb, cache_k, cache_v, bias_tab, layer, n_lat):
    t = qb.shape[0]
    rows = n_lat // GRID_W
    kr = min(NA_ROWS, rows)
    past = cache_k.shape[2]

    def bias_map(b, r):
        return (0, r - jnp.clip(r - kr // 2, 0, rows - kr), 0, 0)

    return pl.pallas_call(
        functools.partial(_nbr_attn_kernel, rows=rows, kr=kr),
        out_shape=jax.ShapeDtypeStruct((t, WIDTH_B), BF16),
        grid=(t // n_lat, rows),
        in_specs=[
            pl.BlockSpec((GRID_W, WIDTH_B), lambda b, r: (b * rows + r, 0)),
            pl.BlockSpec((n_lat, WIDTH_B), lambda b, r: (b, 0)),
            pl.BlockSpec((n_lat, WIDTH_B), lambda b, r: (b, 0)),
            pl.BlockSpec((None, None, past, WIDTH_B), lambda b, r: (b, layer, 0, 0)),
            pl.BlockSpec((None, None, past, WIDTH_B), lambda b, r: (b, layer, 0, 0)),
            pl.BlockSpec((N_HEADS_B, None, GRID_W, kr * GRID_W), bias_map),
        ],
        out_specs=pl.BlockSpec((GRID_W, WIDTH_B), lambda b, r: (b * rows + r, 0)),
        compiler_params=_params(("arbitrary", "arbitrary")),
        name="neighbourhood_attention",
    )(qb, kb, vb, cache_k, cache_v, bias_tab)


def _outproj_kernel(x_ref, oa_ref, ob_ref, oc_ref, w_ref, mod_ref, g_ref, wr_ref, br_ref, xn_o, h_o, lg_o):
    mix = jnp.dot(oa_ref[...], w_ref[0:WIDTH_A, :], preferred_element_type=F32)
    mix += jnp.dot(ob_ref[...], w_ref[WIDTH_A:WIDTH_A + WIDTH_B, :], preferred_element_type=F32)
    mix += jnp.dot(oc_ref[...], w_ref[WIDTH_A + WIDTH_B:, :], preferred_element_type=F32)
    xn = x_ref[...] + mod_ref[2:3, :] * mix
    xn_o[...] = xn
    h = xn * lax.rsqrt(jnp.mean(xn * xn, axis=-1, keepdims=True) + EPS) * g_ref[...]
    h = h * (1.0 + mod_ref[4:5, :]) + mod_ref[3:4, :]
    h_hi = h.astype(BF16)
    h_o[...] = h_hi
    h_lo = (h - h_hi.astype(F32)).astype(BF16)
    wr = wr_ref[...]
    w_hi = wr.astype(BF16)
    w_lo = (wr - w_hi.astype(F32)).astype(BF16)
    lg = jnp.dot(h_hi, w_hi, preferred_element_type=F32)
    lg += jnp.dot(h_lo, w_hi, preferred_element_type=F32)
    lg += jnp.dot(h_hi, w_lo, preferred_element_type=F32)
    lg_o[...] = lg + br_ref[...]


def _outproj(x, oa, ob, oc, w_out_bf, mod, group_of_tile, g_ffn, w_router, b_router, layer, tm):
    t = x.shape[0]
    return pl.pallas_call(
        _outproj_kernel,
        out_shape=[jax.ShapeDtypeStruct((t, D_MODEL), F32), jax.ShapeDtypeStruct((t, D_MODEL), BF16),
                   jax.ShapeDtypeStruct((t, N_EXPERTS), F32)],
        grid=(t // tm,),
        in_specs=[
            pl.BlockSpec((tm, D_MODEL), lambda i: (i, 0)),
            pl.BlockSpec((tm, WIDTH_A), lambda i: (i, 0)),
            pl.BlockSpec((tm, WIDTH_B), lambda i: (i, 0)),
            pl.BlockSpec((tm, CONV_CH), lambda i: (i, 0)),
            pl.BlockSpec((None, D_MODEL, D_MODEL), lambda i: (layer, 0, 0)),
            pl.BlockSpec((None, None, N_MOD, D_MODEL), lambda i: (layer, group_of_tile(i), 0, 0)),
            pl.BlockSpec((None, 1, D_MODEL), lambda i: (layer, 0, 0)),
            pl.BlockSpec((None, D_MODEL, N_EXPERTS), lambda i: (layer, 0, 0)),
            pl.BlockSpec((None, 1, N_EXPERTS), lambda i: (layer, 0, 0)),
        ],
        out_specs=[pl.BlockSpec((tm, D_MODEL), lambda i: (i, 0)), pl.BlockSpec((tm, D_MODEL), lambda i: (i, 0)),
                   pl.BlockSpec((tm, N_EXPERTS), lambda i: (i, 0))],
        compiler_params=_params(("arbitrary",), VMEM_LIMIT_V7X),
        name="outproj_router",
    )(x, oa, ob, oc, w_out_bf, mod, g_ffn, w_router, b_router)


def _expert_kernel(blk_e_ref, n_used_ref, x_ref, gate_ref, wg_ref, bg_ref, wu_ref, bu_ref, wd_ref, bd_ref, y_o):
    i = pl.program_id(0)

    @pl.when(i < n_used_ref[0])
    def _():
        x = x_ref[...]
        g = jnp.dot(x, wg_ref[...].astype(BF16), preferred_element_type=F32) + bg_ref[...]
        u = jnp.dot(x, wu_ref[...].astype(BF16), preferred_element_type=F32) + bu_ref[...]
        g = jnp.minimum(g, SWIGLU_LIMIT)
        u = jnp.clip(u, -SWIGLU_LIMIT, SWIGLU_LIMIT)
        a = g * jax.nn.sigmoid(SWIGLU_ALPHA * g) * (u + 1.0)
        y = jnp.dot(a.astype(BF16), wd_ref[...].astype(BF16), preferred_element_type=F32) + bd_ref[...]
        y_o[...] = y * gate_ref[...]

    @pl.when(i >= n_used_ref[0])
    def _():
        y_o[...] = jnp.zeros_like(y_o)


def _experts(blk_e, n_used, x_rows, row_gate, w_gate, b_gate, w_up, b_up, w_down, b_down, layer):
    n_rows = x_rows.shape[0]
    n_blocks = n_rows // MOE_BLK
    wspec = pl.BlockSpec((None, None, D_MODEL, D_MODEL), lambda i, be, nu: (layer, be[i], 0, 0))
    bspec = pl.BlockSpec((None, None, 1, D_MODEL), lambda i, be, nu: (layer, be[i], 0, 0))
    depth = w_gate.shape[0]
    b4 = lambda b: b.reshape(depth, N_EXPERTS, 1, D_MODEL)
    return pl.pallas_call(
        _expert_kernel,
        out_shape=jax.ShapeDtypeStruct((n_rows, D_MODEL), F32),
        grid_spec=pltpu.PrefetchScalarGridSpec(
            num_scalar_prefetch=2,
            grid=(n_blocks,),
            in_specs=[
                pl.BlockSpec((MOE_BLK, D_MODEL), lambda i, be, nu: (i, 0)),
                pl.BlockSpec((MOE_BLK, 1), lambda i, be, nu: (i, 0)),
                wspec, bspec, wspec, bspec, wspec, bspec,
            ],
            out_specs=pl.BlockSpec((MOE_BLK, D_MODEL), lambda i, be, nu: (i, 0)),
        ),
        compiler_params=_params(("arbitrary",), VMEM_LIMIT_V7X),
        name="experts",
    )(blk_e, n_used, x_rows, row_gate, w_gate, b4(b_gate), w_up, b4(b_up), w_down, b4(b_down))


def _moe(h_all, logits, w_gate, b_gate, w_up, b_up, w_down, b_down, layer):
    t = h_all.shape[0]
    top_logit, top_e = lax.top_k(logits, TOP_K)
    gates = jax.nn.softmax(top_logit, axis=-1)
    flat_e = top_e.reshape(-1)
    n_assign = t * TOP_K
    onehot = (flat_e[:, None] == jnp.arange(N_EXPERTS)[None, :]).astype(jnp.int32)
    ranks = jnp.cumsum(onehot, axis=0) - onehot
    counts = jnp.sum(onehot, axis=0)
    padded = (counts + MOE_BLK - 1) // MOE_BLK * MOE_BLK
    pad_end = jnp.cumsum(padded)
    pad_start = pad_end - padded
    dest = (pad_start[flat_e] + jnp.sum(ranks * onehot, axis=1)).astype(jnp.int32)
    n_blocks = n_assign // MOE_BLK + N_EXPERTS
    n_rows = n_blocks * MOE_BLK
    tok = (jnp.arange(n_assign, dtype=jnp.int32) // TOP_K)
    row_tok = jnp.zeros((n_rows,), jnp.int32).at[dest].set(tok)
    row_gate = jnp.zeros((n_rows,), F32).at[dest].set(gates.reshape(-1))
    blk_e = jnp.minimum(jnp.searchsorted(pad_end, jnp.arange(n_blocks) * MOE_BLK, side='right'),
                        N_EXPERTS - 1).astype(jnp.int32)
    n_used = (pad_end[-1] // MOE_BLK).astype(jnp.int32).reshape(1)
    x_rows = h_all[row_tok]
    y_rows = _experts(blk_e, n_used, x_rows, row_gate[:, None], w_gate, b_gate, w_up, b_up, w_down, b_down, layer)
    return jnp.sum(y_rows[dest.reshape(t, TOP_K)], axis=1)


def _rope_tables(n_lat):
    quarter = HEAD_DIM // 4
    t = jnp.arange(n_lat)
    inv = ROPE_BASE ** (-jnp.arange(quarter, dtype=F32) / quarter)
    ang_r = (t // GRID_W).astype(F32)[:, None] * inv
    ang_c = (t % GRID_W).astype(F32)[:, None] * inv
    cos = jnp.concatenate([jnp.cos(ang_r)] * 2 + [jnp.cos(ang_c)] * 2, axis=-1)
    sin = jnp.concatenate([-jnp.sin(ang_r), jnp.sin(ang_r), -jnp.sin(ang_c), jnp.sin(ang_c)], axis=-1)
    return jnp.concatenate([cos, cos], axis=-1), jnp.concatenate([sin, sin], axis=-1)


def _block_diag_ones():
    idx = np.arange(MXU_COLS_V7X) // HEAD_DIM
    return jnp.asarray(idx[:, None] == idx[None, :], dtype=BF16)


def kernel(x_prompt, x_sample, cache_k_win, cache_v_win, cache_k_nbr, cache_v_nbr, c, c_ctx, w_mod, b_mod, g_mix, g_ffn, w_in, w_out, qn_win, kn_win, qn_nbr, kn_nbr, sink_win, rel_bias_nbr, conv_w, w_router, b_router, w_gate, b_gate, w_up, b_up, w_down, b_down):
    bsz, n_ctx, d = x_prompt.shape
    dbs, n_lat, _ = x_sample.shape
    depth = w_in.shape[0]
    past = cache_k_win.shape[2]
    assert d == D_MODEL and dbs + 1 <= COND_ROWS and n_lat % GRID_W == 0 and n_lat >= Q_BLK + 2 * WINDOW
    t_ctx, t_lat = bsz * n_ctx, dbs * n_lat

    cond = jnp.concatenate([c_ctx[None], c, jnp.zeros((COND_ROWS - 1 - dbs, d), F32)], axis=0)
    mod = _modulation(cond, w_mod, b_mod)

    w_in_bf = w_in.astype(BF16)
    w_out_bf = w_out.astype(BF16)
    norm_w = jnp.concatenate([jnp.tile(qn_win, (1, N_HEADS_A)), jnp.tile(kn_win, (1, N_KV_A)),
                              jnp.tile(qn_nbr, (1, N_HEADS_B)), jnp.tile(kn_nbr, (1, N_HEADS_B))], axis=-1)[:, None, :]
    ones_bd = _block_diag_ones()
    rope_tabs = _rope_tables(n_lat)
    g_mix3, g_ffn3 = g_mix[:, None, :], g_ffn[:, None, :]
    b_router3 = b_router[:, None, :]
    ck_win = cache_k_win.reshape(dbs, depth, past, KV_WIDTH_A)
    cv_win = cache_v_win.reshape(dbs, depth, past, KV_WIDTH_A)
    ck_nbr = cache_k_nbr.reshape(dbs, depth, past, WIDTH_B)
    cv_nbr = cache_v_nbr.reshape(dbs, depth, past, WIDTH_B)

    tm_ctx = 2 * n_ctx
    tm_lat = 512
    ctx_group = lambda i: 0
    lat_group_in = lambda i: 1 + i
    lat_group_out = lambda i: 1 + (i * tm_lat) // n_lat

    xp = x_prompt.reshape(t_ctx, d)
    xs = x_sample.reshape(t_lat, d)
    caches = [[], [], [], []]
    for l in range(depth):
        qa, qb, ka, va, kb, vb, oc = _inproj(xp, mod, ctx_group, g_mix3, w_in_bf, l, norm_w, ones_bd, conv_w,
                                             tm_ctx, n_ctx)
        oa, ob = _ctx_attention(sink_win[l], qa, ka, va, qb, kb, vb, n_ctx)
        xp_mid, hp, lgp = _outproj(xp, oa, ob, oc, w_out_bf, mod, ctx_group, g_ffn3, w_router, b_router3, l, tm_ctx)
        for lst, a in zip(caches, (ka, va, kb, vb)):
            lst.append(a)

        qa, qb, ka, va, kb, vb, oc = _inproj(xs, mod, lat_group_in, g_mix3, w_in_bf, l, norm_w, ones_bd, conv_w,
                                             n_lat, n_lat, rope_tabs)
        oa = _win_attention(sink_win[l], qa, ka, va, ck_win, cv_win, l, n_lat)
        ob = _nbr_attention(qb, kb, vb, ck_nbr, cv_nbr, _nbr_bias_table(rel_bias_nbr[l], n_lat // GRID_W), l, n_lat)
        xs_mid, hs, lgs = _outproj(xs, oa, ob, oc, w_out_bf, mod, lat_group_out, g_ffn3, w_router, b_router3, l,
                                   tm_lat)

        y = _moe(jnp.concatenate([hp, hs], axis=0), jnp.concatenate([lgp, lgs], axis=0),
                 w_gate, b_gate, w_up, b_up, w_down, b_down, l)
        xp = xp_mid + mod[l, 0, 5][None, :] * y[:t_ctx]
        xs = (xs_mid.reshape(dbs, n_lat, d) + mod[l, 1:1 + dbs, 5][:, None, :] * y[t_ctx:].reshape(dbs, n_lat, d)
              ).reshape(t_lat, d)

    new_v_win = jnp.stack([a.reshape(bsz, n_ctx, KV_WIDTH_A) for a in caches[1]], axis=1).reshape(
        bsz, depth, n_ctx, N_KV_A, HEAD_DIM)
    new_k_nbr = jnp.stack([a.reshape(bsz, n_ctx, WIDTH_B) for a in caches[2]], axis=1).reshape(
        bsz, depth, n_ctx, N_HEADS_B, HEAD_DIM)
    new_v_nbr = jnp.stack([a.reshape(bsz, n_ctx, WIDTH_B) for a in caches[3]], axis=1).reshape(
        bsz, depth, n_ctx, N_HEADS_B, HEAD_DIM)
    new_k_win = jnp.stack([a.reshape(bsz, n_ctx, KV_WIDTH_A) for a in caches[0]], axis=1).reshape(
        bsz, depth, n_ctx, N_KV_A, HEAD_DIM)
    return (xp.reshape(bsz, n_ctx, d), xs.reshape(dbs, n_lat, d), new_k_win, new_v_win, new_k_nbr, new_v_nbr)
```

```python
import functools

import numpy as np
import jax
import jax.numpy as jnp
from jax import lax
from jax.experimental import pallas as pl
from jax.experimental.pallas import tpu as pltpu

F32 = jnp.float32
BF16 = jnp.bfloat16

D_MODEL = 1024
HEAD_DIM = 64
GRID_W = 64
N_HEADS_A = 8
N_KV_A = 2
GROUP_A = N_HEADS_A // N_KV_A
WINDOW = 128
Q_BLK = 128
N_HEADS_B = 4
NA_ROWS = 8
NA_COLS = 16
CONV_CH = 256
CONV_W = 3
WIDTH_A = N_HEADS_A * HEAD_DIM
KV_WIDTH_A = N_KV_A * HEAD_DIM
WIDTH_B = N_HEADS_B * HEAD_DIM
IN_COLS = WIDTH_A + 2 * KV_WIDTH_A + 3 * WIDTH_B + 3 * CONV_CH
N_EXPERTS = 32
TOP_K = 4
SWIGLU_LIMIT = 7.0
SWIGLU_ALPHA = 1.702
ROPE_BASE = 10000.0
EPS = 1e-6
NEG = -1e30
N_MOD = 6

C_QA = 0
C_KA = C_QA + WIDTH_A
C_VA = C_KA + KV_WIDTH_A
C_QB = C_VA + KV_WIDTH_A
C_KB = C_QB + WIDTH_B
C_VB = C_KB + WIDTH_B
C_U = C_VB + WIDTH_B
C_GB = C_U + CONV_CH
C_GC = C_GB + CONV_CH

MXU_COLS_V7X = 256
COND_ROWS = 8
MOE_BLK = 256
VMEM_LIMIT_V7X = 56 * 1024 * 1024

_NT = (((1,), (1,)), ((), ()))


def _params(sem, vmem=None):
    return pltpu.CompilerParams(dimension_semantics=sem, vmem_limit_bytes=vmem)


def _mod_kernel(c_ref, w_ref, b_ref, o_ref):
    c = c_ref[...]
    s = c * jax.nn.sigmoid(c)
    o_ref[...] = jnp.dot(s.astype(BF16), w_ref[...].astype(BF16), preferred_element_type=F32) + b_ref[...]


def _modulation(cond, w_mod, b_mod):
    depth = w_mod.shape[0]
    out = pl.pallas_call(
        _mod_kernel,
        out_shape=jax.ShapeDtypeStruct((depth, COND_ROWS, N_MOD * D_MODEL), F32),
        grid=(depth, N_MOD),
        in_specs=[
            pl.BlockSpec((COND_ROWS, D_MODEL), lambda l, j: (0, 0)),
            pl.BlockSpec((None, D_MODEL, D_MODEL), lambda l, j: (l, 0, j)),
            pl.BlockSpec((None, 1, D_MODEL), lambda l, j: (l, 0, j)),
        ],
        out_specs=pl.BlockSpec((None, COND_ROWS, D_MODEL), lambda l, j: (l, 0, j)),
        compiler_params=_params(("arbitrary", "arbitrary")),
        name="modulation",
    )(cond, w_mod, b_mod.reshape(depth, 1, N_MOD * D_MODEL))
    return out.reshape(depth, COND_ROWS, N_MOD, D_MODEL)


def _head_norm(x, w_row, ones_ref):
    width = x.shape[1]
    sq = (x * x).astype(BF16)
    parts = []
    for c0 in range(0, width, MXU_COLS_V7X):
        wd = min(MXU_COLS_V7X, width - c0)
        parts.append(jnp.dot(sq[:, c0:c0 + wd], ones_ref[:wd, :wd], preferred_element_type=F32))
    ss = parts[0] if len(parts) == 1 else jnp.concatenate(parts, axis=-1)
    return x * lax.rsqrt(ss * (1.0 / HEAD_DIM) + EPS) * w_row


def _rope(x, cos, sin):
    width = x.shape[1]
    lane = lax.broadcasted_iota(jnp.int32, x.shape, 1)
    quarter = HEAD_DIM // 4
    partner = jnp.where((lane % (2 * quarter)) < quarter,
                        pltpu.roll(x, width - quarter, axis=1), pltpu.roll(x, quarter, axis=1))
    reps = width // cos.shape[1]
    cos_w = cos if reps == 1 else jnp.concatenate([cos] * reps, axis=-1)
    sin_w = sin if reps == 1 else jnp.concatenate([sin] * reps, axis=-1)
    return x * cos_w + partner * sin_w


def _inproj_kernel(*refs, seq_len, rope):
    if rope:
        (x_ref, mod_ref, g_ref, w_ref, nw_ref, ones_ref, cw_ref, cos_ref, sin_ref,
         qa_o, qb_o, ka_o, va_o, kb_o, vb_o, oc_o) = refs
    else:
        (x_ref, mod_ref, g_ref, w_ref, nw_ref, ones_ref, cw_ref,
         qa_o, qb_o, ka_o, va_o, kb_o, vb_o, oc_o) = refs
    x = x_ref[...]
    tm = x.shape[0]
    h = x * lax.rsqrt(jnp.mean(x * x, axis=-1, keepdims=True) + EPS) * g_ref[...]
    h = h * (1.0 + mod_ref[1:2, :]) + mod_ref[0:1, :]
    p = jnp.dot(h.astype(BF16), w_ref[...], preferred_element_type=F32)

    qa = _head_norm(p[:, C_QA:C_KA], nw_ref[:, 0:WIDTH_A], ones_ref)
    ka = _head_norm(p[:, C_KA:C_VA], nw_ref[:, WIDTH_A:WIDTH_A + KV_WIDTH_A], ones_ref)
    o_qb = WIDTH_A + KV_WIDTH_A
    qb = _head_norm(p[:, C_QB:C_KB], nw_ref[:, o_qb:o_qb + WIDTH_B], ones_ref)
    kb = _head_norm(p[:, C_KB:C_VB], nw_ref[:, o_qb + WIDTH_B:o_qb + 2 * WIDTH_B], ones_ref)
    if rope:
        cos, sin = cos_ref[...], sin_ref[...]
        qa = _rope(qa, cos, sin)
        ka = _rope(ka, cos, sin)
    qa_o[...] = qa.astype(BF16)
    qb_o[...] = qb.astype(BF16)
    ka_o[...] = ka
    va_o[...] = p[:, C_VA:C_QB]
    kb_o[...] = kb
    vb_o[...] = p[:, C_VB:C_U]

    z = p[:, C_GC:C_GC + CONV_CH] * p[:, C_U:C_GB]
    row = lax.broadcasted_iota(jnp.int32, z.shape, 0) % seq_len
    z_prev = jnp.where(row == 0, 0.0, pltpu.roll(z, 1, axis=0))
    z_next = jnp.where(row == seq_len - 1, 0.0, pltpu.roll(z, tm - 1, axis=0))
    y = z_prev * cw_ref[0:1, :] + z * cw_ref[1:2, :] + z_next * cw_ref[2:3, :]
    oc_o[...] = (p[:, C_GB:C_GC] * y).astype(BF16)


def _inproj(x, mod, group_of_tile, g_mix, w_in_bf, layer, norm_w, ones_bd, conv_w, tm, seq_len, rope_tabs=None):
    t = x.shape[0]
    rope = rope_tabs is not None
    in_specs = [
        pl.BlockSpec((tm, D_MODEL), lambda i: (i, 0)),
        pl.BlockSpec((None, None, N_MOD, D_MODEL), lambda i: (layer, group_of_tile(i), 0, 0)),
        pl.BlockSpec((None, 1, D_MODEL), lambda i: (layer, 0, 0)),
        pl.BlockSpec((None, D_MODEL, IN_COLS), lambda i: (layer, 0, 0)),
        pl.BlockSpec((None, 1, norm_w.shape[-1]), lambda i: (layer, 0, 0)),
        pl.BlockSpec(ones_bd.shape, lambda i: (0, 0)),
        pl.BlockSpec((None, CONV_W, CONV_CH), lambda i: (layer, 0, 0)),
    ]
    args = [x, mod, g_mix, w_in_bf, norm_w, ones_bd, conv_w]
    if rope:
        in_specs += [pl.BlockSpec(rope_tabs[0].shape, lambda i: (0, 0))] * 2
        args += list(rope_tabs)
    widths = (WIDTH_A, WIDTH_B, KV_WIDTH_A, KV_WIDTH_A, WIDTH_B, WIDTH_B, CONV_CH)
    dtypes = (BF16, BF16, F32, F32, F32, F32, BF16)
    return pl.pallas_call(
        functools.partial(_inproj_kernel, seq_len=seq_len, rope=rope),
        out_shape=[jax.ShapeDtypeStruct((t, w), dt) for w, dt in zip(widths, dtypes)],
        grid=(t // tm,),
        in_specs=in_specs,
        out_specs=[pl.BlockSpec((tm, w), lambda i: (i, 0)) for w in widths],
        compiler_params=_params(("arbitrary",), VMEM_LIMIT_V7X),
        name="inproj_rope" if rope else "inproj",
    )(*args)


def _softmax_pv(s, v, sink=None):
    m = jnp.max(s, axis=-1, keepdims=True)
    if sink is not None:
        m = jnp.maximum(m, sink)
    p = jnp.exp(s - m)
    den = jnp.sum(p, axis=-1, keepdims=True)
    if sink is not None:
        den = den + jnp.exp(sink - m)
    return jnp.dot(p.astype(BF16), v, preferred_element_type=F32) / den


def _store_heads(o_ref, outs):
    for i in range(0, len(outs), 2):
        o_ref[:, i * HEAD_DIM:(i + 2) * HEAD_DIM] = jnp.concatenate(outs[i:i + 2], axis=-1).astype(o_ref.dtype)


def _ctx_attn_kernel(sink_ref, qa_ref, ka_ref, va_ref, qb_ref, kb_ref, vb_ref, oa_o, ob_o):
    scale = HEAD_DIM ** -0.5
    ka = ka_ref[...].astype(BF16)
    va = va_ref[...].astype(BF16)
    outs = []
    for h in range(N_HEADS_A):
        j = h // GROUP_A
        q = qa_ref[:, h * HEAD_DIM:(h + 1) * HEAD_DIM] * scale
        s = lax.dot_general(q, ka[:, j * HEAD_DIM:(j + 1) * HEAD_DIM], _NT, preferred_element_type=F32)
        outs.append(_softmax_pv(s, va[:, j * HEAD_DIM:(j + 1) * HEAD_DIM], sink_ref[h]))
    _store_heads(oa_o, outs)
    kb = kb_ref[...].astype(BF16)
    vb = vb_ref[...].astype(BF16)
    outs = []
    for h in range(N_HEADS_B):
        sl = slice(h * HEAD_DIM, (h + 1) * HEAD_DIM)
        q = qb_ref[:, sl] * scale
        s = lax.dot_general(q, kb[:, sl], _NT, preferred_element_type=F32)
        outs.append(_softmax_pv(s, vb[:, sl]))
    _store_heads(ob_o, outs)


def _ctx_attention(sink, qa, ka, va, qb, kb, vb, seq_len):
    t = qa.shape[0]
    widths = (WIDTH_A, KV_WIDTH_A, KV_WIDTH_A, WIDTH_B, WIDTH_B, WIDTH_B)
    return pl.pallas_call(
        _ctx_attn_kernel,
        out_shape=[jax.ShapeDtypeStruct((t, WIDTH_A), BF16), jax.ShapeDtypeStruct((t, WIDTH_B), BF16)],
        grid=(t // seq_len,),
        in_specs=[pl.BlockSpec(memory_space=pltpu.SMEM)]
        + [pl.BlockSpec((seq_len, w), lambda i: (i, 0)) for w in widths],
        out_specs=[pl.BlockSpec((seq_len, WIDTH_A), lambda i: (i, 0)),
                   pl.BlockSpec((seq_len, WIDTH_B), lambda i: (i, 0))],
        compiler_params=_params(("arbitrary",)),
        name="ctx_attention",
    )(sink, qa, ka, va, qb, kb, vb)


def _win_attn_kernel(sink_ref, qa_ref, ka_ref, va_ref, ck_ref, cv_ref, oa_o, *, n_lat):
    scale = HEAD_DIM ** -0.5
    span = Q_BLK + 2 * WINDOW
    n = pl.program_id(1)
    start = pl.multiple_of(jnp.clip(n * Q_BLK - WINDOW, 0, n_lat - span), Q_BLK)
    kcat = jnp.concatenate([ka_ref[pl.ds(start, span), :], ck_ref[...]], axis=0).astype(BF16)
    vcat = jnp.concatenate([va_ref[pl.ds(start, span), :], cv_ref[...]], axis=0).astype(BF16)
    n_keys = kcat.shape[0]
    qpos = n * Q_BLK + lax.broadcasted_iota(jnp.int32, (Q_BLK, n_keys), 0)
    col = lax.broadcasted_iota(jnp.int32, (Q_BLK, n_keys), 1)
    ok = (col >= span) | (jnp.abs(qpos - (start + col)) <= WINDOW)
    outs = []
    for h in range(N_HEADS_A):
        j = h // GROUP_A
        q = qa_ref[:, h * HEAD_DIM:(h + 1) * HEAD_DIM] * scale
        s = lax.dot_general(q, kcat[:, j * HEAD_DIM:(j + 1) * HEAD_DIM], _NT, preferred_element_type=F32)
        s = jnp.where(ok, s, NEG)
        outs.append(_softmax_pv(s, vcat[:, j * HEAD_DIM:(j + 1) * HEAD_DIM], sink_ref[h]))
    _store_heads(oa_o, outs)


def _win_attention(sink, qa, ka, va, cache_k, cache_v, layer, n_lat):
    t = qa.shape[0]
    nb = n_lat // Q_BLK
    past = cache_k.shape[2]
    return pl.pallas_call(
        functools.partial(_win_attn_kernel, n_lat=n_lat),
        out_shape=jax.ShapeDtypeStruct((t, WIDTH_A), BF16),
        grid=(t // n_lat, nb),
        in_specs=[
            pl.BlockSpec(memory_space=pltpu.SMEM),
            pl.BlockSpec((Q_BLK, WIDTH_A), lambda b, n: (b * nb + n, 0)),
            pl.BlockSpec((n_lat, KV_WIDTH_A), lambda b, n: (b, 0)),
            pl.BlockSpec((n_lat, KV_WIDTH_A), lambda b, n: (b, 0)),
            pl.BlockSpec((None, None, past, KV_WIDTH_A), lambda b, n: (b, layer, 0, 0)),
            pl.BlockSpec((None, None, past, KV_WIDTH_A), lambda b, n: (b, layer, 0, 0)),
        ],
        out_specs=pl.BlockSpec((Q_BLK, WIDTH_A), lambda b, n: (b * nb + n, 0)),
        compiler_params=_params(("arbitrary", "arbitrary")),
        name="window_attention",
    )(sink, qa, ka, va, cache_k, cache_v)


def _nbr_attn_kernel(qb_ref, kb_ref, vb_ref, ck_ref, cv_ref, bias_ref, ob_o, *, rows, kr):
    scale = HEAD_DIM ** -0.5
    r = pl.program_id(1)
    start = pl.multiple_of(jnp.clip(r - kr // 2, 0, rows - kr) * GRID_W, GRID_W)
    nwin = kr * GRID_W
    kcat = jnp.concatenate([kb_ref[pl.ds(start, nwin), :], ck_ref[...]], axis=0).astype(BF16)
    vcat = jnp.concatenate([vb_ref[pl.ds(start, nwin), :], cv_ref[...]], axis=0).astype(BF16)
    past = ck_ref.shape[0]
    outs = []
    for h in range(N_HEADS_B):
        sl = slice(h * HEAD_DIM, (h + 1) * HEAD_DIM)
        q = qb_ref[:, sl] * scale
        s = lax.dot_general(q, kcat[:, sl], _NT, preferred_element_type=F32)
        bias = jnp.concatenate([bias_ref[h], jnp.zeros((GRID_W, past), F32)], axis=-1)
        outs.append(_softmax_pv(s + bias, vcat[:, sl]))
    _store_heads(ob_o, outs)


def _nbr_bias_table(rel_bias, rows):
    kr = min(NA_ROWS, rows)
    cls = np.arange(kr)[:, None, None, None]
    c = np.arange(GRID_W)[None, :, None, None]
    m = np.arange(kr)[None, None, :, None]
    kc = np.arange(GRID_W)[None, None, None, :]
    dr = np.broadcast_to(m - cls + NA_ROWS - 1, (kr, GRID_W, kr, GRID_W))
    dc = np.broadcast_to(np.clip(kc - c, -(NA_COLS - 1), NA_COLS - 1) + NA_COLS - 1, dr.shape)
    col_start = np.clip(c - NA_COLS // 2, 0, GRID_W - NA_COLS)
    ok = np.broadcast_to((kc >= col_start) & (kc < col_start + NA_COLS), dr.shape)
    valid = (dr >= 0) & (dr < 2 * NA_ROWS - 1)
    tab = rel_bias.astype(F32)[:, np.where(valid, dr, 0), dc]
    tab = jnp.where(ok & valid, tab, NEG)
    return tab.reshape(rel_bias.shape[0], kr, GRID_W, kr * GRID_W)


def _nbr_attention(qb, kb, vb, cache_k, cache_v, bias_tab, layer, n_lat):
    t = qb.shape[0]
    rows = n_lat // GRID_W
    kr = min(NA_ROWS, rows)
    past = cache_k.shape[2]

    def bias_map(b, r):
        return (0, r - jnp.clip(r - kr // 2, 0, rows - kr), 0, 0)

    return pl.pallas_call(
        functools.partial(_nbr_attn_kernel, rows=rows, kr=kr),
        out_shape=jax.ShapeDtypeStruct((t, WIDTH_B), BF16),
        grid=(t // n_lat, rows),
        in_specs=[
            pl.BlockSpec((GRID_W, WIDTH_B), lambda b, r: (b * rows + r, 0)),
            pl.BlockSpec((n_lat, WIDTH_B), lambda b, r: (b, 0)),
            pl.BlockSpec((n_lat, WIDTH_B), lambda b, r: (b, 0)),
            pl.BlockSpec((None, None, past, WIDTH_B), lambda b, r: (b, layer, 0, 0)),
            pl.BlockSpec((None, None, past, WIDTH_B), lambda b, r: (b, layer, 0, 0)),
            pl.BlockSpec((N_HEADS_B, None, GRID_W, kr * GRID_W), bias_map),
        ],
        out_specs=pl.BlockSpec((GRID_W, WIDTH_B), lambda b, r: (b * rows + r, 0)),
        compiler_params=_params(("arbitrary", "arbitrary")),
        name="neighbourhood_attention",
    )(qb, kb, vb, cache_k, cache_v, bias_tab)


ROW_TILE = 8
LANES = 128
assert ROW_TILE * LANES == D_MODEL


def _store_row_tiles(ref, row0, x):
    n = x.shape[0]
    for c in range(ROW_TILE):
        ref[pl.ds(row0 * ROW_TILE + c, n, stride=ROW_TILE), :] = x[:, c * LANES:(c + 1) * LANES]


def _load_row_tiles(ref, row0, n, row_stride=1, chunk0=0):
    return jnp.concatenate(
        [ref[pl.ds(row0 * ROW_TILE + chunk0 + c, n, stride=ROW_TILE * row_stride), :] for c in range(ROW_TILE)],
        axis=-1)


def _outproj_kernel(x_ref, oa_ref, ob_ref, oc_ref, w_ref, mod_ref, g_ref, wr_ref, br_ref, xn_o, h_o, lg_o):
    mix = jnp.dot(oa_ref[...], w_ref[0:WIDTH_A, :], preferred_element_type=F32)
    mix += jnp.dot(ob_ref[...], w_ref[WIDTH_A:WIDTH_A + WIDTH_B, :], preferred_element_type=F32)
    mix += jnp.dot(oc_ref[...], w_ref[WIDTH_A + WIDTH_B:, :], preferred_element_type=F32)
    xn = x_ref[...] + mod_ref[2:3, :] * mix
    xn_o[...] = xn
    h = xn * lax.rsqrt(jnp.mean(xn * xn, axis=-1, keepdims=True) + EPS) * g_ref[...]
    h = h * (1.0 + mod_ref[4:5, :]) + mod_ref[3:4, :]
    _store_row_tiles(h_o, 0, h)
    h_hi = h.astype(BF16)
    h_lo = (h - h_hi.astype(F32)).astype(BF16)
    wr = wr_ref[...]
    w_hi = wr.astype(BF16)
    w_lo = (wr - w_hi.astype(F32)).astype(BF16)
    lg = jnp.dot(h_hi, w_hi, preferred_element_type=F32)
    lg += jnp.dot(h_lo, w_hi, preferred_element_type=F32)
    lg += jnp.dot(h_hi, w_lo, preferred_element_type=F32)
    lg_o[...] = lg + br_ref[...]


def _outproj(x, oa, ob, oc, w_out_bf, mod, group_of_tile, g_ffn, w_router, b_router, layer, tm):
    t = x.shape[0]
    return pl.pallas_call(
        _outproj_kernel,
        out_shape=[jax.ShapeDtypeStruct((t, D_MODEL), F32), jax.ShapeDtypeStruct((t * ROW_TILE, LANES), F32),
                   jax.ShapeDtypeStruct((t, N_EXPERTS), F32)],
        grid=(t // tm,),
        in_specs=[
            pl.BlockSpec((tm, D_MODEL), lambda i: (i, 0)),
            pl.BlockSpec((tm, WIDTH_A), lambda i: (i, 0)),
            pl.BlockSpec((tm, WIDTH_B), lambda i: (i, 0)),
            pl.BlockSpec((tm, CONV_CH), lambda i: (i, 0)),
            pl.BlockSpec((None, D_MODEL, D_MODEL), lambda i: (layer, 0, 0)),
            pl.BlockSpec((None, None, N_MOD, D_MODEL), lambda i: (layer, group_of_tile(i), 0, 0)),
            pl.BlockSpec((None, 1, D_MODEL), lambda i: (layer, 0, 0)),
            pl.BlockSpec((None, D_MODEL, N_EXPERTS), lambda i: (layer, 0, 0)),
            pl.BlockSpec((None, 1, N_EXPERTS), lambda i: (layer, 0, 0)),
        ],
        out_specs=[pl.BlockSpec((tm, D_MODEL), lambda i: (i, 0)),
                   pl.BlockSpec((tm * ROW_TILE, LANES), lambda i: (i, 0)),
                   pl.BlockSpec((tm, N_EXPERTS), lambda i: (i, 0))],
        compiler_params=_params(("arbitrary",), VMEM_LIMIT_V7X),
        name="outproj_router",
    )(x, oa, ob, oc, w_out_bf, mod, g_ffn, w_router, b_router)


IDX_CHUNK = 1024
IDX_SLOTS = 3
assert 2 * MOE_BLK <= IDX_CHUNK


def _expert_kernel(blk_e_ref, n_used_ref, idx_hbm, h_hbm, wg_ref, bg_ref, wu_ref, bu_ref, wd_ref, bd_ref, y_hbm,
                   idx_s, x_buf, y_buf, isem, gsem, ssem, *, n):
    i = pl.program_id(0)
    slot = i % 2
    blk_rows = MOE_BLK * ROW_TILE

    def idx_copy(blk):
        return pltpu.make_async_copy(idx_hbm.at[pl.ds(pl.multiple_of(blk * IDX_CHUNK, IDX_CHUNK), IDX_CHUNK)],
                                     idx_s.at[blk % IDX_SLOTS], isem.at[blk % IDX_SLOTS])

    def issue_gather(blk):
        s3, s2 = blk % IDX_SLOTS, blk % 2

        def body(r, carry):
            tok = idx_s[s3, r]
            pltpu.make_async_copy(h_hbm.at[pl.ds(pl.multiple_of(tok * ROW_TILE, ROW_TILE), ROW_TILE)],
                                  x_buf.at[s2, pl.ds(pl.multiple_of(r * ROW_TILE, ROW_TILE), ROW_TILE)],
                                  gsem.at[s2]).start()
            return carry

        lax.fori_loop(0, MOE_BLK, body, 0, unroll=8)

    def wait_gather(s2):
        pltpu.make_async_copy(h_hbm.at[pl.ds(0, blk_rows)], x_buf.at[s2], gsem.at[s2]).wait()

    def issue_scatter(blk):
        s3, s2 = blk % IDX_SLOTS, blk % 2

        def body(r, carry):
            dst = idx_s[s3, MOE_BLK + r]
            pltpu.make_async_copy(y_buf.at[s2, pl.ds(pl.multiple_of(r * ROW_TILE, ROW_TILE), ROW_TILE)],
                                  y_hbm.at[pl.ds(pl.multiple_of(dst * ROW_TILE, ROW_TILE), ROW_TILE)],
                                  ssem.at[s2]).start()
            return carry

        lax.fori_loop(0, MOE_BLK, body, 0, unroll=8)

    def wait_scatter(s2):
        pltpu.make_async_copy(y_buf.at[s2], y_hbm.at[pl.ds(0, blk_rows)], ssem.at[s2]).wait()

    @pl.when(i == 0)
    def _():
        idx_copy(0).start()
        idx_copy(0).wait()
        issue_gather(0)
        idx_copy(1).start()

    @pl.when(i + 1 < n)
    def _():
        idx_copy(i + 1).wait()
        issue_gather(i + 1)

    @pl.when(i + 2 < n)
    def _():
        idx_copy(i + 2).start()

    @pl.when(i >= 2)
    def _():
        wait_scatter(slot)

    wait_gather(slot)

    @pl.when(i < n_used_ref[0])
    def _():
        x = _load_row_tiles(x_buf.at[slot], 0, MOE_BLK).astype(BF16)
        g = jnp.dot(x, wg_ref[...].astype(BF16), preferred_element_type=F32) + bg_ref[...]
        u = jnp.dot(x, wu_ref[...].astype(BF16), preferred_element_type=F32) + bu_ref[...]
        g = jnp.minimum(g, SWIGLU_LIMIT)
        u = jnp.clip(u, -SWIGLU_LIMIT, SWIGLU_LIMIT)
        a = g * jax.nn.sigmoid(SWIGLU_ALPHA * g) * (u + 1.0)
        y = jnp.dot(a.astype(BF16), wd_ref[...].astype(BF16), preferred_element_type=F32) + bd_ref[...]
        _store_row_tiles(y_buf.at[slot], 0, y)

    @pl.when(i >= n_used_ref[0])
    def _():
        y_buf[slot] = jnp.zeros((blk_rows, LANES), F32)

    issue_scatter(i)

    @pl.when(i == n - 1)
    def _():
        wait_scatter(slot)
        wait_scatter(1 - slot)


def _experts(blk_e, n_used, idx, h_tiles, w_gate, b_gate, w_up, b_up, w_down, b_down, layer):
    n_blocks = idx.shape[0] // IDX_CHUNK
    assert n_blocks >= 2
    n_rows = n_blocks * MOE_BLK
    wspec = pl.BlockSpec((None, None, D_MODEL, D_MODEL), lambda i, be, nu: (layer, be[i], 0, 0))
    bspec = pl.BlockSpec((None, None, 1, D_MODEL), lambda i, be, nu: (layer, be[i], 0, 0))
    anyspec = pl.BlockSpec(memory_space=pl.ANY)
    depth = w_gate.shape[0]
    b4 = lambda b: b.reshape(depth, N_EXPERTS, 1, D_MODEL)
    return pl.pallas_call(
        functools.partial(_expert_kernel, n=n_blocks),
        out_shape=jax.ShapeDtypeStruct((n_rows * ROW_TILE, LANES), F32),
        grid_spec=pltpu.PrefetchScalarGridSpec(
            num_scalar_prefetch=2,
            grid=(n_blocks,),
            in_specs=[anyspec, anyspec, wspec, bspec, wspec, bspec, wspec, bspec],
            out_specs=anyspec,
            scratch_shapes=[
                pltpu.SMEM((IDX_SLOTS, IDX_CHUNK), jnp.int32),
                pltpu.VMEM((2, MOE_BLK * ROW_TILE, LANES), F32),
                pltpu.VMEM((2, MOE_BLK * ROW_TILE, LANES), F32),
                pltpu.SemaphoreType.DMA((IDX_SLOTS,)),
                pltpu.SemaphoreType.DMA((2,)),
                pltpu.SemaphoreType.DMA((2,)),
            ],
        ),
        compiler_params=_params(("arbitrary",), VMEM_LIMIT_V7X),
        name="experts",
    )(blk_e, n_used, idx, h_tiles, w_gate, b4(b_gate), w_up, b4(b_up), w_down, b4(b_down))


def _route(logits):
    t = logits.shape[0]
    top_logit, top_e = lax.top_k(logits, TOP_K)
    gates = jax.nn.softmax(top_logit, axis=-1)
    flat_e = top_e.reshape(-1)
    n_assign = t * TOP_K
    onehot = (flat_e[:, None] == jnp.arange(N_EXPERTS)[None, :]).astype(jnp.int32)
    ranks = jnp.cumsum(onehot, axis=0) - onehot
    counts = jnp.sum(onehot, axis=0)
    padded = (counts + MOE_BLK - 1) // MOE_BLK * MOE_BLK
    pad_end = jnp.cumsum(padded)
    pad_start = pad_end - padded
    dest = (pad_start[flat_e] + jnp.sum(ranks * onehot, axis=1)).astype(jnp.int32)
    n_blocks = n_assign // MOE_BLK + N_EXPERTS
    n_rows = n_blocks * MOE_BLK
    row_src = jnp.full((n_rows,), -1, jnp.int32).at[dest].set(jnp.arange(n_assign, dtype=jnp.int32))
    is_pad = row_src < 0
    pad_rank = jnp.cumsum(is_pad.astype(jnp.int32)) - 1
    row_tok = jnp.where(is_pad, 0, row_src // TOP_K).reshape(n_blocks, MOE_BLK)
    row_dst = jnp.where(is_pad, n_assign + pad_rank, row_src).reshape(n_blocks, MOE_BLK)
    idx = jnp.concatenate([row_tok, row_dst, jnp.zeros((n_blocks, IDX_CHUNK - 2 * MOE_BLK), jnp.int32)], axis=1)
    blk_e = jnp.minimum(jnp.searchsorted(pad_end, jnp.arange(n_blocks) * MOE_BLK, side='right'),
                        N_EXPERTS - 1).astype(jnp.int32)
    n_used = (pad_end[-1] // MOE_BLK).astype(jnp.int32).reshape(1)
    return gates, blk_e, n_used, idx.reshape(-1)


def _combine_kernel(x_ref, y_ref, gate_ref, mod_ref, o_ref):
    tm = x_ref.shape[0]
    acc = jnp.zeros(x_ref.shape, F32)
    for k in range(TOP_K):
        acc += gate_ref[:, k:k + 1] * _load_row_tiles(y_ref, k, tm, row_stride=TOP_K)
    o_ref[...] = x_ref[...] + mod_ref[5:6, :] * acc


def _combine(x_mid, y_slots, gates, mod, group_of_tile, layer, tok0, tm):
    t = x_mid.shape[0]
    b0 = tok0 // tm
    return pl.pallas_call(
        _combine_kernel,
        out_shape=jax.ShapeDtypeStruct((t, D_MODEL), F32),
        grid=(t // tm,),
        in_specs=[
            pl.BlockSpec((tm, D_MODEL), lambda i: (i, 0)),
            pl.BlockSpec((tm * TOP_K * ROW_TILE, LANES), lambda i: (b0 + i, 0)),
            pl.BlockSpec((tm, TOP_K), lambda i: (b0 + i, 0)),
            pl.BlockSpec((None, None, N_MOD, D_MODEL), lambda i: (layer, group_of_tile(i), 0, 0)),
        ],
        out_specs=pl.BlockSpec((tm, D_MODEL), lambda i: (i, 0)),
        compiler_params=_params(("arbitrary",), VMEM_LIMIT_V7X),
        name="combine",
    )(x_mid, y_slots, gates, mod)


def _rope_tables(n_lat):
    quarter = HEAD_DIM // 4
    t = jnp.arange(n_lat)
    inv = ROPE_BASE ** (-jnp.arange(quarter, dtype=F32) / quarter)
    ang_r = (t // GRID_W).astype(F32)[:, None] * inv
    ang_c = (t % GRID_W).astype(F32)[:, None] * inv
    cos = jnp.concatenate([jnp.cos(ang_r)] * 2 + [jnp.cos(ang_c)] * 2, axis=-1)
    sin = jnp.concatenate([-jnp.sin(ang_r), jnp.sin(ang_r), -jnp.sin(ang_c), jnp.sin(ang_c)], axis=-1)
    return jnp.concatenate([cos, cos], axis=-1), jnp.concatenate([sin, sin], axis=-1)


def _block_diag_ones():
    idx = np.arange(MXU_COLS_V7X) // HEAD_DIM
    return jnp.asarray(idx[:, None] == idx[None, :], dtype=BF16)


def kernel(x_prompt, x_sample, cache_k_win, cache_v_win, cache_k_nbr, cache_v_nbr, c, c_ctx, w_mod, b_mod, g_mix, g_ffn, w_in, w_out, qn_win, kn_win, qn_nbr, kn_nbr, sink_win, rel_bias_nbr, conv_w, w_router, b_router, w_gate, b_gate, w_up, b_up, w_down, b_down):
    bsz, n_ctx, d = x_prompt.shape
    dbs, n_lat, _ = x_sample.shape
    depth = w_in.shape[0]
    past = cache_k_win.shape[2]
    assert d == D_MODEL and dbs + 1 <= COND_ROWS and n_lat % GRID_W == 0 and n_lat >= Q_BLK + 2 * WINDOW
    t_ctx, t_lat = bsz * n_ctx, dbs * n_lat

    cond = jnp.concatenate([c_ctx[None], c, jnp.zeros((COND_ROWS - 1 - dbs, d), F32)], axis=0)
    mod = _modulation(cond, w_mod, b_mod)

    w_in_bf = w_in.astype(BF16)
    w_out_bf = w_out.astype(BF16)
    norm_w = jnp.concatenate([jnp.tile(qn_win, (1, N_HEADS_A)), jnp.tile(kn_win, (1, N_KV_A)),
                              jnp.tile(qn_nbr, (1, N_HEADS_B)), jnp.tile(kn_nbr, (1, N_HEADS_B))], axis=-1)[:, None, :]
    ones_bd = _block_diag_ones()
    rope_tabs = _rope_tables(n_lat)
    g_mix3, g_ffn3 = g_mix[:, None, :], g_ffn[:, None, :]
    b_router3 = b_router[:, None, :]
    ck_win = cache_k_win.reshape(dbs, depth, past, KV_WIDTH_A)
    cv_win = cache_v_win.reshape(dbs, depth, past, KV_WIDTH_A)
    ck_nbr = cache_k_nbr.reshape(dbs, depth, past, WIDTH_B)
    cv_nbr = cache_v_nbr.reshape(dbs, depth, past, WIDTH_B)

    tm_ctx = 2 * n_ctx
    tm_lat = 512
    ctx_group = lambda i: 0
    lat_group_in = lambda i: 1 + i
    lat_group_out = lambda i: 1 + (i * tm_lat) // n_lat
    tm_comb = 256
    lat_group_comb = lambda i: 1 + (i * tm_comb) // n_lat
    assert t_ctx % tm_comb == 0 and n_lat % tm_comb == 0

    xp = x_prompt.reshape(t_ctx, d)
    xs = x_sample.reshape(t_lat, d)
    caches = [[], [], [], []]
    for l in range(depth):
        qa, qb, ka, va, kb, vb, oc = _inproj(xp, mod, ctx_group, g_mix3, w_in_bf, l, norm_w, ones_bd, conv_w,
                                             tm_ctx, n_ctx)
        oa, ob = _ctx_attention(sink_win[l], qa, ka, va, qb, kb, vb, n_ctx)
        xp_mid, hp, lgp = _outproj(xp, oa, ob, oc, w_out_bf, mod, ctx_group, g_ffn3, w_router, b_router3, l, tm_ctx)
        for lst, a in zip(caches, (ka, va, kb, vb)):
            lst.append(a)

        qa, qb, ka, va, kb, vb, oc = _inproj(xs, mod, lat_group_in, g_mix3, w_in_bf, l, norm_w, ones_bd, conv_w,
                                             n_lat, n_lat, rope_tabs)
        oa = _win_attention(sink_win[l], qa, ka, va, ck_win, cv_win, l, n_lat)
        ob = _nbr_attention(qb, kb, vb, ck_nbr, cv_nbr, _nbr_bias_table(rel_bias_nbr[l], n_lat // GRID_W), l, n_lat)
        xs_mid, hs, lgs = _outproj(xs, oa, ob, oc, w_out_bf, mod, lat_group_out, g_ffn3, w_router, b_router3, l,
                                   tm_lat)

        gates, blk_e, n_used, idx = _route(jnp.concatenate([lgp, lgs], axis=0))
        y_slots = _experts(blk_e, n_used, idx, jnp.concatenate([hp, hs], axis=0),
                           w_gate, b_gate, w_up, b_up, w_down, b_down, l)
        xp = _combine(xp_mid, y_slots, gates, mod, ctx_group, l, 0, tm_comb)
        xs = _combine(xs_mid, y_slots, gates, mod, lat_group_comb, l, t_ctx, tm_comb)

    new_v_win = jnp.stack([a.reshape(bsz, n_ctx, KV_WIDTH_A) for a in caches[1]], axis=1).reshape(
        bsz, depth, n_ctx, N_KV_A, HEAD_DIM)
    new_k_nbr = jnp.stack([a.reshape(bsz, n_ctx, WIDTH_B) for a in caches[2]], axis=1).reshape(
        bsz, depth, n_ctx, N_HEADS_B, HEAD_DIM)
    new_v_nbr = jnp.stack([a.reshape(bsz, n_ctx, WIDTH_B) for a in caches[3]], axis=1).reshape(
        bsz, depth, n_ctx, N_HEADS_B, HEAD_DIM)
    new_k_win = jnp.stack([a.reshape(bsz, n_ctx, KV_WIDTH_A) for a in caches[0]], axis=1).reshape(
        bsz, depth, n_ctx, N_KV_A, HEAD_DIM)
    return (xp.reshape(bsz, n_ctx, d), xs.reshape(dbs, n_lat, d), new_k_win, new_v_win, new_k_nbr, new_v_nbr)
```

```python
import functools

import numpy as np
import jax
import jax.numpy as jnp
from jax import lax
from jax.experimental import pallas as pl
from jax.experimental.pallas import tpu as pltpu

F32 = jnp.float32
BF16 = jnp.bfloat16

D_MODEL = 1024
HEAD_DIM = 64
GRID_W = 64
N_HEADS_A = 8
N_KV_A = 2
GROUP_A = N_HEADS_A // N_KV_A
WINDOW = 128
Q_BLK = 128
N_HEADS_B = 4
NA_ROWS = 8
NA_COLS = 16
CONV_CH = 256
CONV_W = 3
WIDTH_A = N_HEADS_A * HEAD_DIM
KV_WIDTH_A = N_KV_A * HEAD_DIM
WIDTH_B = N_HEADS_B * HEAD_DIM
IN_COLS = WIDTH_A + 2 * KV_WIDTH_A + 3 * WIDTH_B + 3 * CONV_CH
N_EXPERTS = 32
TOP_K = 4
SWIGLU_LIMIT = 7.0
SWIGLU_ALPHA = 1.702
ROPE_BASE = 10000.0
EPS = 1e-6
NEG = -1e30
N_MOD = 6

C_QA = 0
C_KA = C_QA + WIDTH_A
C_VA = C_KA + KV_WIDTH_A
C_QB = C_VA + KV_WIDTH_A
C_KB = C_QB + WIDTH_B
C_VB = C_KB + WIDTH_B
C_U = C_VB + WIDTH_B
C_GB = C_U + CONV_CH
C_GC = C_GB + CONV_CH

MXU_COLS_V7X = 256
COND_ROWS = 8
MOE_BLK = 256
MOE_SHIFT = MOE_BLK.bit_length() - 1
assert 1 << MOE_SHIFT == MOE_BLK
VMEM_LIMIT_V7X = 56 * 1024 * 1024

_NT = (((1,), (1,)), ((), ()))


def _params(sem, vmem=None):
    return pltpu.CompilerParams(dimension_semantics=sem, vmem_limit_bytes=vmem)


def _mod_kernel(c_ref, w_ref, b_ref, o_ref):
    c = c_ref[...]
    s = c * jax.nn.sigmoid(c)
    o_ref[...] = jnp.dot(s.astype(BF16), w_ref[...].astype(BF16), preferred_element_type=F32) + b_ref[...]


def _modulation(cond, w_mod, b_mod):
    depth = w_mod.shape[0]
    out = pl.pallas_call(
        _mod_kernel,
        out_shape=jax.ShapeDtypeStruct((depth, COND_ROWS, N_MOD * D_MODEL), F32),
        grid=(depth, N_MOD),
        in_specs=[
            pl.BlockSpec((COND_ROWS, D_MODEL), lambda l, j: (0, 0)),
            pl.BlockSpec((None, D_MODEL, D_MODEL), lambda l, j: (l, 0, j)),
            pl.BlockSpec((None, 1, D_MODEL), lambda l, j: (l, 0, j)),
        ],
        out_specs=pl.BlockSpec((None, COND_ROWS, D_MODEL), lambda l, j: (l, 0, j)),
        compiler_params=_params(("arbitrary", "arbitrary")),
        name="modulation",
    )(cond, w_mod, b_mod.reshape(depth, 1, N_MOD * D_MODEL))
    return out.reshape(depth, COND_ROWS, N_MOD, D_MODEL)


def _head_norm(x, w_row, ones_ref):
    width = x.shape[1]
    sq = (x * x).astype(BF16)
    parts = []
    for c0 in range(0, width, MXU_COLS_V7X):
        wd = min(MXU_COLS_V7X, width - c0)
        parts.append(jnp.dot(sq[:, c0:c0 + wd], ones_ref[:wd, :wd], preferred_element_type=F32))
    ss = parts[0] if len(parts) == 1 else jnp.concatenate(parts, axis=-1)
    return x * lax.rsqrt(ss * (1.0 / HEAD_DIM) + EPS) * w_row


def _rope(x, cos, sin):
    width = x.shape[1]
    lane = lax.broadcasted_iota(jnp.int32, x.shape, 1)
    quarter = HEAD_DIM // 4
    partner = jnp.where((lane % (2 * quarter)) < quarter,
                        pltpu.roll(x, width - quarter, axis=1), pltpu.roll(x, quarter, axis=1))
    reps = width // cos.shape[1]
    cos_w = cos if reps == 1 else jnp.concatenate([cos] * reps, axis=-1)
    sin_w = sin if reps == 1 else jnp.concatenate([sin] * reps, axis=-1)
    return x * cos_w + partner * sin_w


def _inproj_kernel(*refs, seq_len, rope):
    if rope:
        (x_ref, mod_ref, g_ref, w_ref, nw_ref, ones_ref, cw_ref, cos_ref, sin_ref,
         qa_o, qb_o, ka_o, va_o, kb_o, vb_o, oc_o) = refs
    else:
        (x_ref, mod_ref, g_ref, w_ref, nw_ref, ones_ref, cw_ref,
         qa_o, qb_o, ka_o, va_o, kb_o, vb_o, oc_o) = refs
    x = x_ref[...]
    tm = x.shape[0]
    h = x * lax.rsqrt(jnp.mean(x * x, axis=-1, keepdims=True) + EPS) * g_ref[...]
    h = h * (1.0 + mod_ref[1:2, :]) + mod_ref[0:1, :]
    p = jnp.dot(h.astype(BF16), w_ref[...], preferred_element_type=F32)

    qa = _head_norm(p[:, C_QA:C_KA], nw_ref[:, 0:WIDTH_A], ones_ref)
    ka = _head_norm(p[:, C_KA:C_VA], nw_ref[:, WIDTH_A:WIDTH_A + KV_WIDTH_A], ones_ref)
    o_qb = WIDTH_A + KV_WIDTH_A
    qb = _head_norm(p[:, C_QB:C_KB], nw_ref[:, o_qb:o_qb + WIDTH_B], ones_ref)
    kb = _head_norm(p[:, C_KB:C_VB], nw_ref[:, o_qb + WIDTH_B:o_qb + 2 * WIDTH_B], ones_ref)
    if rope:
        cos, sin = cos_ref[...], sin_ref[...]
        qa = _rope(qa, cos, sin)
        ka = _rope(ka, cos, sin)
    qa_o[...] = qa.astype(BF16)
    qb_o[...] = qb.astype(BF16)
    ka_o[...] = ka
    va_o[...] = p[:, C_VA:C_QB]
    kb_o[...] = kb
    vb_o[...] = p[:, C_VB:C_U]

    z = p[:, C_GC:C_GC + CONV_CH] * p[:, C_U:C_GB]
    row = lax.broadcasted_iota(jnp.int32, z.shape, 0) % seq_len
    z_prev = jnp.where(row == 0, 0.0, pltpu.roll(z, 1, axis=0))
    z_next = jnp.where(row == seq_len - 1, 0.0, pltpu.roll(z, tm - 1, axis=0))
    y = z_prev * cw_ref[0:1, :] + z * cw_ref[1:2, :] + z_next * cw_ref[2:3, :]
    oc_o[...] = (p[:, C_GB:C_GC] * y).astype(BF16)


def _inproj(x, mod, group_of_tile, g_mix, w_in_bf, layer, norm_w, ones_bd, conv_w, tm, seq_len, rope_tabs=None):
    t = x.shape[0]
    rope = rope_tabs is not None
    in_specs = [
        pl.BlockSpec((tm, D_MODEL), lambda i: (i, 0)),
        pl.BlockSpec((None, None, N_MOD, D_MODEL), lambda i: (layer, group_of_tile(i), 0, 0)),
        pl.BlockSpec((None, 1, D_MODEL), lambda i: (layer, 0, 0)),
        pl.BlockSpec((None, D_MODEL, IN_COLS), lambda i: (layer, 0, 0)),
        pl.BlockSpec((None, 1, norm_w.shape[-1]), lambda i: (layer, 0, 0)),
        pl.BlockSpec(ones_bd.shape, lambda i: (0, 0)),
        pl.BlockSpec((None, CONV_W, CONV_CH), lambda i: (layer, 0, 0)),
    ]
    args = [x, mod, g_mix, w_in_bf, norm_w, ones_bd, conv_w]
    if rope:
        in_specs += [pl.BlockSpec(rope_tabs[0].shape, lambda i: (0, 0))] * 2
        args += list(rope_tabs)
    widths = (WIDTH_A, WIDTH_B, KV_WIDTH_A, KV_WIDTH_A, WIDTH_B, WIDTH_B, CONV_CH)
    dtypes = (BF16, BF16, F32, F32, F32, F32, BF16)
    return pl.pallas_call(
        functools.partial(_inproj_kernel, seq_len=seq_len, rope=rope),
        out_shape=[jax.ShapeDtypeStruct((t, w), dt) for w, dt in zip(widths, dtypes)],
        grid=(t // tm,),
        in_specs=in_specs,
        out_specs=[pl.BlockSpec((tm, w), lambda i: (i, 0)) for w in widths],
        compiler_params=_params(("arbitrary",), VMEM_LIMIT_V7X),
        name="inproj_rope" if rope else "inproj",
    )(*args)


def _softmax_pv(s, v, sink=None):
    m = jnp.max(s, axis=-1, keepdims=True)
    if sink is not None:
        m = jnp.maximum(m, sink)
    p = jnp.exp(s - m)
    den = jnp.sum(p, axis=-1, keepdims=True)
    if sink is not None:
        den = den + jnp.exp(sink - m)
    return jnp.dot(p.astype(BF16), v, preferred_element_type=F32) / den


def _store_heads(o_ref, outs):
    for i in range(0, len(outs), 2):
        o_ref[:, i * HEAD_DIM:(i + 2) * HEAD_DIM] = jnp.concatenate(outs[i:i + 2], axis=-1).astype(o_ref.dtype)


def _ctx_attn_kernel(sink_ref, qa_ref, ka_ref, va_ref, qb_ref, kb_ref, vb_ref, oa_o, ob_o):
    scale = HEAD_DIM ** -0.5
    ka = ka_ref[...].astype(BF16)
    va = va_ref[...].astype(BF16)
    outs = []
    for h in range(N_HEADS_A):
        j = h // GROUP_A
        q = qa_ref[:, h * HEAD_DIM:(h + 1) * HEAD_DIM] * scale
        s = lax.dot_general(q, ka[:, j * HEAD_DIM:(j + 1) * HEAD_DIM], _NT, preferred_element_type=F32)
        outs.append(_softmax_pv(s, va[:, j * HEAD_DIM:(j + 1) * HEAD_DIM], sink_ref[h]))
    _store_heads(oa_o, outs)
    kb = kb_ref[...].astype(BF16)
    vb = vb_ref[...].astype(BF16)
    outs = []
    for h in range(N_HEADS_B):
        sl = slice(h * HEAD_DIM, (h + 1) * HEAD_DIM)
        q = qb_ref[:, sl] * scale
        s = lax.dot_general(q, kb[:, sl], _NT, preferred_element_type=F32)
        outs.append(_softmax_pv(s, vb[:, sl]))
    _store_heads(ob_o, outs)


def _ctx_attention(sink, qa, ka, va, qb, kb, vb, seq_len):
    t = qa.shape[0]
    widths = (WIDTH_A, KV_WIDTH_A, KV_WIDTH_A, WIDTH_B, WIDTH_B, WIDTH_B)
    return pl.pallas_call(
        _ctx_attn_kernel,
        out_shape=[jax.ShapeDtypeStruct((t, WIDTH_A), BF16), jax.ShapeDtypeStruct((t, WIDTH_B), BF16)],
        grid=(t // seq_len,),
        in_specs=[pl.BlockSpec(memory_space=pltpu.SMEM)]
        + [pl.BlockSpec((seq_len, w), lambda i: (i, 0)) for w in widths],
        out_specs=[pl.BlockSpec((seq_len, WIDTH_A), lambda i: (i, 0)),
                   pl.BlockSpec((seq_len, WIDTH_B), lambda i: (i, 0))],
        compiler_params=_params(("arbitrary",)),
        name="ctx_attention",
    )(sink, qa, ka, va, qb, kb, vb)


def _win_attn_kernel(sink_ref, qa_ref, ka_ref, va_ref, ck_ref, cv_ref, oa_o, *, n_lat):
    scale = HEAD_DIM ** -0.5
    span = Q_BLK + 2 * WINDOW
    n = pl.program_id(1)
    start = pl.multiple_of(jnp.clip(n * Q_BLK - WINDOW, 0, n_lat - span), Q_BLK)
    kcat = jnp.concatenate([ka_ref[pl.ds(start, span), :], ck_ref[...]], axis=0).astype(BF16)
    vcat = jnp.concatenate([va_ref[pl.ds(start, span), :], cv_ref[...]], axis=0).astype(BF16)
    n_keys = kcat.shape[0]
    qpos = n * Q_BLK + lax.broadcasted_iota(jnp.int32, (Q_BLK, n_keys), 0)
    col = lax.broadcasted_iota(jnp.int32, (Q_BLK, n_keys), 1)
    ok = (col >= span) | (jnp.abs(qpos - (start + col)) <= WINDOW)
    outs = []
    for h in range(N_HEADS_A):
        j = h // GROUP_A
        q = qa_ref[:, h * HEAD_DIM:(h + 1) * HEAD_DIM] * scale
        s = lax.dot_general(q, kcat[:, j * HEAD_DIM:(j + 1) * HEAD_DIM], _NT, preferred_element_type=F32)
        s = jnp.where(ok, s, NEG)
        outs.append(_softmax_pv(s, vcat[:, j * HEAD_DIM:(j + 1) * HEAD_DIM], sink_ref[h]))
    _store_heads(oa_o, outs)


def _win_attention(sink, qa, ka, va, cache_k, cache_v, layer, n_lat):
    t = qa.shape[0]
    nb = n_lat // Q_BLK
    past = cache_k.shape[2]
    return pl.pallas_call(
        functools.partial(_win_attn_kernel, n_lat=n_lat),
        out_shape=jax.ShapeDtypeStruct((t, WIDTH_A), BF16),
        grid=(t // n_lat, nb),
        in_specs=[
            pl.BlockSpec(memory_space=pltpu.SMEM),
            pl.BlockSpec((Q_BLK, WIDTH_A), lambda b, n: (b * nb + n, 0)),
            pl.BlockSpec((n_lat, KV_WIDTH_A), lambda b, n: (b, 0)),
            pl.BlockSpec((n_lat, KV_WIDTH_A), lambda b, n: (b, 0)),
            pl.BlockSpec((None, None, past, KV_WIDTH_A), lambda b, n: (b, layer, 0, 0)),
            pl.BlockSpec((None, None, past, KV_WIDTH_A), lambda b, n: (b, layer, 0, 0)),
        ],
        out_specs=pl.BlockSpec((Q_BLK, WIDTH_A), lambda b, n: (b * nb + n, 0)),
        compiler_params=_params(("arbitrary", "arbitrary")),
        name="window_attention",
    )(sink, qa, ka, va, cache_k, cache_v)


def _nbr_attn_kernel(qb_ref, kb_ref, vb_ref, ck_ref, cv_ref, bias_ref, ob_o, *, rows, kr):
    scale = HEAD_DIM ** -0.5
    r = pl.program_id(1)
    start = pl.multiple_of(jnp.clip(r - kr // 2, 0, rows - kr) * GRID_W, GRID_W)
    nwin = kr * GRID_W
    kcat = jnp.concatenate([kb_ref[pl.ds(start, nwin), :], ck_ref[...]], axis=0).astype(BF16)
    vcat = jnp.concatenate([vb_ref[pl.ds(start, nwin), :], cv_ref[...]], axis=0).astype(BF16)
    past = ck_ref.shape[0]
    outs = []
    for h in range(N_HEADS_B):
        sl = slice(h * HEAD_DIM, (h + 1) * HEAD_DIM)
        q = qb_ref[:, sl] * scale
        s = lax.dot_general(q, kcat[:, sl], _NT, preferred_element_type=F32)
        bias = jnp.concatenate([bias_ref[h], jnp.zeros((GRID_W, past), F32)], axis=-1)
        outs.append(_softmax_pv(s + bias, vcat[:, sl]))
    _store_heads(ob_o, outs)


def _nbr_bias_table(rel_bias, rows):
    kr = min(NA_ROWS, rows)
    cls = np.arange(kr)[:, None, None, None]
    c = np.arange(GRID_W)[None, :, None, None]
    m = np.arange(kr)[None, None, :, None]
    kc = np.arange(GRID_W)[None, None, None, :]
    dr = np.broadcast_to(m - cls + NA_ROWS - 1, (kr, GRID_W, kr, GRID_W))
    dc = np.broadcast_to(np.clip(kc - c, -(NA_COLS - 1), NA_COLS - 1) + NA_COLS - 1, dr.shape)
    col_start = np.clip(c - NA_COLS // 2, 0, GRID_W - NA_COLS)
    ok = np.broadcast_to((kc >= col_start) & (kc < col_start + NA_COLS), dr.shape)
    valid = (dr >= 0) & (dr < 2 * NA_ROWS - 1)
    tab = rel_bias.astype(F32)[:, np.where(valid, dr, 0), dc]
    tab = jnp.where(ok & valid, tab, NEG)
    return tab.reshape(rel_bias.shape[0], kr, GRID_W, kr * GRID_W)


def _nbr_attention(qb, kb, vb, cache_k, cache_v, bias_tab, layer, n_lat):
    t = qb.shape[0]
    rows = n_lat // GRID_W
    kr = min(NA_ROWS, rows)
    past = cache_k.shape[2]

    def bias_map(b, r):
        return (0, r - jnp.clip(r - kr // 2, 0, rows - kr), 0, 0)

    return pl.pallas_call(
        functools.partial(_nbr_attn_kernel, rows=rows, kr=kr),
        out_shape=jax.ShapeDtypeStruct((t, WIDTH_B), BF16),
        grid=(t // n_lat, rows),
        in_specs=[
            pl.BlockSpec((GRID_W, WIDTH_B), lambda b, r: (b * rows + r, 0)),
            pl.BlockSpec((n_lat, WIDTH_B), lambda b, r: (b, 0)),
            pl.BlockSpec((n_lat, WIDTH_B), lambda b, r: (b, 0)),
            pl.BlockSpec((None, None, past, WIDTH_B), lambda b, r: (b, layer, 0, 0)),
            pl.BlockSpec((None, None, past, WIDTH_B), lambda b, r: (b, layer, 0, 0)),
            pl.BlockSpec((N_HEADS_B, None, GRID_W, kr * GRID_W), bias_map),
        ],
        out_specs=pl.BlockSpec((GRID_W, WIDTH_B), lambda b, r: (b * rows + r, 0)),
        compiler_params=_params(("arbitrary", "arbitrary")),
        name="neighbourhood_attention",
    )(qb, kb, vb, cache_k, cache_v, bias_tab)


ROW_TILE = 8
LANES = 128
assert ROW_TILE * LANES == D_MODEL


def _store_row_tiles(ref, row0, x):
    n = x.shape[0]
    for c in range(ROW_TILE):
        ref[pl.ds(row0 * ROW_TILE + c, n, stride=ROW_TILE), :] = x[:, c * LANES:(c + 1) * LANES]


def _load_row_tiles(ref, row0, n, row_stride=1, chunk0=0):
    return jnp.concatenate(
        [ref[pl.ds(row0 * ROW_TILE + chunk0 + c, n, stride=ROW_TILE * row_stride), :] for c in range(ROW_TILE)],
        axis=-1)


def _outproj_kernel(x_ref, oa_ref, ob_ref, oc_ref, w_ref, mod_ref, g_ref, wr_ref, br_ref, xn_o, h_o, lg_o):
    mix = jnp.dot(oa_ref[...], w_ref[0:WIDTH_A, :], preferred_element_type=F32)
    mix += jnp.dot(ob_ref[...], w_ref[WIDTH_A:WIDTH_A + WIDTH_B, :], preferred_element_type=F32)
    mix += jnp.dot(oc_ref[...], w_ref[WIDTH_A + WIDTH_B:, :], preferred_element_type=F32)
    xn = x_ref[...] + mod_ref[2:3, :] * mix
    xn_o[...] = xn
    h = xn * lax.rsqrt(jnp.mean(xn * xn, axis=-1, keepdims=True) + EPS) * g_ref[...]
    h = h * (1.0 + mod_ref[4:5, :]) + mod_ref[3:4, :]
    _store_row_tiles(h_o, 0, h)
    h_hi = h.astype(BF16)
    h_lo = (h - h_hi.astype(F32)).astype(BF16)
    wr = wr_ref[...]
    w_hi = wr.astype(BF16)
    w_lo = (wr - w_hi.astype(F32)).astype(BF16)
    lg = jnp.dot(h_hi, w_hi, preferred_element_type=F32)
    lg += jnp.dot(h_lo, w_hi, preferred_element_type=F32)
    lg += jnp.dot(h_hi, w_lo, preferred_element_type=F32)
    lg_o[...] = lg + br_ref[...]


def _outproj(x, oa, ob, oc, w_out_bf, mod, group_of_tile, g_ffn, w_router, b_router, layer, tm):
    t = x.shape[0]
    return pl.pallas_call(
        _outproj_kernel,
        out_shape=[jax.ShapeDtypeStruct((t, D_MODEL), F32), jax.ShapeDtypeStruct((t * ROW_TILE, LANES), F32),
                   jax.ShapeDtypeStruct((t, N_EXPERTS), F32)],
        grid=(t // tm,),
        in_specs=[
            pl.BlockSpec((tm, D_MODEL), lambda i: (i, 0)),
            pl.BlockSpec((tm, WIDTH_A), lambda i: (i, 0)),
            pl.BlockSpec((tm, WIDTH_B), lambda i: (i, 0)),
            pl.BlockSpec((tm, CONV_CH), lambda i: (i, 0)),
            pl.BlockSpec((None, D_MODEL, D_MODEL), lambda i: (layer, 0, 0)),
            pl.BlockSpec((None, None, N_MOD, D_MODEL), lambda i: (layer, group_of_tile(i), 0, 0)),
            pl.BlockSpec((None, 1, D_MODEL), lambda i: (layer, 0, 0)),
            pl.BlockSpec((None, D_MODEL, N_EXPERTS), lambda i: (layer, 0, 0)),
            pl.BlockSpec((None, 1, N_EXPERTS), lambda i: (layer, 0, 0)),
        ],
        out_specs=[pl.BlockSpec((tm, D_MODEL), lambda i: (i, 0)),
                   pl.BlockSpec((tm * ROW_TILE, LANES), lambda i: (i, 0)),
                   pl.BlockSpec((tm, N_EXPERTS), lambda i: (i, 0))],
        compiler_params=_params(("arbitrary",), VMEM_LIMIT_V7X),
        name="outproj_router",
    )(x, oa, ob, oc, w_out_bf, mod, g_ffn, w_router, b_router)


IDX_CHUNK = 1024
IDX_SLOTS = 3
assert 2 * MOE_BLK <= IDX_CHUNK


def _expert_kernel(blk_e_ref, n_used_ref, idx_hbm, h_hbm, wg_ref, bg_ref, wu_ref, bu_ref, wd_ref, bd_ref, y_hbm,
                   idx_s, x_buf, y_buf, isem, gsem, ssem, *, n):
    i = pl.program_id(0)
    slot = i % 2
    blk_rows = MOE_BLK * ROW_TILE

    def idx_copy(blk):
        return pltpu.make_async_copy(idx_hbm.at[pl.ds(pl.multiple_of(blk * IDX_CHUNK, IDX_CHUNK), IDX_CHUNK)],
                                     idx_s.at[blk % IDX_SLOTS], isem.at[blk % IDX_SLOTS])

    def issue_gather(blk):
        s3, s2 = blk % IDX_SLOTS, blk % 2

        def body(r, carry):
            tok = idx_s[s3, r]
            pltpu.make_async_copy(h_hbm.at[pl.ds(pl.multiple_of(tok * ROW_TILE, ROW_TILE), ROW_TILE)],
                                  x_buf.at[s2, pl.ds(pl.multiple_of(r * ROW_TILE, ROW_TILE), ROW_TILE)],
                                  gsem.at[s2]).start()
            return carry

        lax.fori_loop(0, MOE_BLK, body, 0, unroll=8)

    def wait_gather(s2):
        pltpu.make_async_copy(h_hbm.at[pl.ds(0, blk_rows)], x_buf.at[s2], gsem.at[s2]).wait()

    def issue_scatter(blk):
        s3, s2 = blk % IDX_SLOTS, blk % 2

        def body(r, carry):
            dst = idx_s[s3, MOE_BLK + r]
            pltpu.make_async_copy(y_buf.at[s2, pl.ds(pl.multiple_of(r * ROW_TILE, ROW_TILE), ROW_TILE)],
                                  y_hbm.at[pl.ds(pl.multiple_of(dst * ROW_TILE, ROW_TILE), ROW_TILE)],
                                  ssem.at[s2]).start()
            return carry

        lax.fori_loop(0, MOE_BLK, body, 0, unroll=8)

    def wait_scatter(s2):
        pltpu.make_async_copy(y_buf.at[s2], y_hbm.at[pl.ds(0, blk_rows)], ssem.at[s2]).wait()

    @pl.when(i == 0)
    def _():
        idx_copy(0).start()
        idx_copy(0).wait()
        issue_gather(0)
        idx_copy(1).start()

    @pl.when(i + 1 < n)
    def _():
        idx_copy(i + 1).wait()
        issue_gather(i + 1)

    @pl.when(i + 2 < n)
    def _():
        idx_copy(i + 2).start()

    @pl.when(i >= 2)
    def _():
        wait_scatter(slot)

    wait_gather(slot)

    @pl.when(i < n_used_ref[0])
    def _():
        x = _load_row_tiles(x_buf.at[slot], 0, MOE_BLK).astype(BF16)
        g = jnp.dot(x, wg_ref[...].astype(BF16), preferred_element_type=F32) + bg_ref[...]
        u = jnp.dot(x, wu_ref[...].astype(BF16), preferred_element_type=F32) + bu_ref[...]
        g = jnp.minimum(g, SWIGLU_LIMIT)
        u = jnp.clip(u, -SWIGLU_LIMIT, SWIGLU_LIMIT)
        a = g * jax.nn.sigmoid(SWIGLU_ALPHA * g) * (u + 1.0)
        y = jnp.dot(a.astype(BF16), wd_ref[...].astype(BF16), preferred_element_type=F32) + bd_ref[...]
        _store_row_tiles(y_buf.at[slot], 0, y)

    @pl.when(i >= n_used_ref[0])
    def _():
        y_buf[slot] = jnp.zeros((blk_rows, LANES), F32)

    issue_scatter(i)

    @pl.when(i == n - 1)
    def _():
        wait_scatter(slot)
        wait_scatter(1 - slot)


def _experts(blk_e, n_used, idx, h_tiles, w_gate, b_gate, w_up, b_up, w_down, b_down, layer):
    n_blocks = idx.shape[0] // IDX_CHUNK
    assert n_blocks >= 2
    n_rows = n_blocks * MOE_BLK
    wspec = pl.BlockSpec((None, None, D_MODEL, D_MODEL), lambda i, be, nu: (layer, be[i], 0, 0))
    bspec = pl.BlockSpec((None, None, 1, D_MODEL), lambda i, be, nu: (layer, be[i], 0, 0))
    anyspec = pl.BlockSpec(memory_space=pl.ANY)
    depth = w_gate.shape[0]
    b4 = lambda b: b.reshape(depth, N_EXPERTS, 1, D_MODEL)
    return pl.pallas_call(
        functools.partial(_expert_kernel, n=n_blocks),
        out_shape=jax.ShapeDtypeStruct((n_rows * ROW_TILE, LANES), F32),
        grid_spec=pltpu.PrefetchScalarGridSpec(
            num_scalar_prefetch=2,
            grid=(n_blocks,),
            in_specs=[anyspec, anyspec, wspec, bspec, wspec, bspec, wspec, bspec],
            out_specs=anyspec,
            scratch_shapes=[
                pltpu.SMEM((IDX_SLOTS, IDX_CHUNK), jnp.int32),
                pltpu.VMEM((2, MOE_BLK * ROW_TILE, LANES), F32),
                pltpu.VMEM((2, MOE_BLK * ROW_TILE, LANES), F32),
                pltpu.SemaphoreType.DMA((IDX_SLOTS,)),
                pltpu.SemaphoreType.DMA((2,)),
                pltpu.SemaphoreType.DMA((2,)),
            ],
        ),
        compiler_params=_params(("arbitrary",), VMEM_LIMIT_V7X),
        name="experts",
    )(blk_e, n_used, idx, h_tiles, w_gate, b4(b_gate), w_up, b4(b_up), w_down, b4(b_down))


CUM_CHUNK = 256
BLK_LANES = 256


def _sublane_cumsum(x):
    row = lax.broadcasted_iota(jnp.int32, x.shape, 0)
    d = 1
    while d < x.shape[0]:
        x = x + jnp.where(row >= d, pltpu.roll(x, d, axis=0), 0)
        d *= 2
    return x


def _route_kernel(lg_ref, tri_ref, gate_o, dest_o, blk_o, *, n_blocks):
    lg = lg_ref[...]
    n_e, t = lg.shape
    e_iota = lax.broadcasted_iota(jnp.int32, lg.shape, 0)
    work = lg
    tops, hots = [], []
    for _ in range(TOP_K):
        m = jnp.max(work, axis=0, keepdims=True)
        first = jnp.min(jnp.where(work == m, e_iota, n_e), axis=0, keepdims=True)
        hot = e_iota == first
        work = jnp.where(hot, -jnp.inf, work)
        tops.append(m)
        hots.append(hot)
    ex = [jnp.exp(m - tops[0]) for m in tops]
    den = ex[0] + ex[1] + ex[2] + ex[3]
    for k in range(TOP_K):
        gate_o[k:k + 1, :] = ex[k] / den

    chosen = jnp.where(hots[0] | hots[1] | hots[2] | hots[3], 1.0, 0.0)
    tri = tri_ref[...]
    carry = jnp.zeros((n_e, 1), F32)
    before = []
    for c0 in range(0, t, CUM_CHUNK):
        chunk = chosen[:, c0:c0 + CUM_CHUNK]
        inc = jnp.dot(chunk.astype(BF16), tri, preferred_element_type=F32)
        before.append(inc - chunk + carry)
        carry = carry + inc[:, CUM_CHUNK - 1:CUM_CHUNK]
    before = jnp.concatenate(before, axis=-1).astype(jnp.int32)

    counts = jnp.broadcast_to(carry.astype(jnp.int32), (n_e, BLK_LANES))
    padded = ((counts + (MOE_BLK - 1)) >> MOE_SHIFT) << MOE_SHIFT
    pad_end = _sublane_cumsum(padded)
    pad_start = pad_end - padded
    row = pad_start[:, 0:1] + before
    for k in range(TOP_K):
        dest_o[k:k + 1, :] = jnp.sum(jnp.where(hots[k], row, 0), axis=0, keepdims=True)

    blk_row0 = lax.broadcasted_iota(jnp.int32, (n_e, BLK_LANES), 1) * MOE_BLK
    blk_o[0:1, :] = jnp.minimum(jnp.sum((pad_end <= blk_row0).astype(jnp.int32), axis=0, keepdims=True), n_e - 1)
    blk_o[1:2, :] = jnp.sum(jnp.where(pad_start <= blk_row0, counts, 0), axis=0, keepdims=True)
    blk_o[2:3, :] = pad_end[n_e - 1:n_e, :] >> MOE_SHIFT
    blk_o[3:8, :] = jnp.zeros((5, BLK_LANES), jnp.int32)


def _invert_kernel(dest_ref, src_o, *, n_assign, n_rows):
    def init(r, carry):
        src_o[r] = -1
        return carry

    lax.fori_loop(0, n_rows, init, 0, unroll=8)

    def put(a, carry):
        src_o[dest_ref[a]] = a
        return carry

    lax.fori_loop(0, n_assign, put, 0, unroll=8)


def _route(logits):
    t = logits.shape[0]
    n_assign = t * TOP_K
    n_blocks = n_assign // MOE_BLK + N_EXPERTS
    n_rows = n_blocks * MOE_BLK
    assert t % CUM_CHUNK == 0 and n_blocks <= BLK_LANES
    tri = jnp.asarray(np.triu(np.ones((CUM_CHUNK, CUM_CHUNK), np.float32)), dtype=BF16)
    gates, dest, blk = pl.pallas_call(
        functools.partial(_route_kernel, n_blocks=n_blocks),
        out_shape=[jax.ShapeDtypeStruct((TOP_K, t), F32), jax.ShapeDtypeStruct((TOP_K, t), jnp.int32),
                   jax.ShapeDtypeStruct((8, BLK_LANES), jnp.int32)],
        compiler_params=_params(None, VMEM_LIMIT_V7X),
        name="route",
    )(logits.T, tri)
    row_src = pl.pallas_call(
        functools.partial(_invert_kernel, n_assign=n_assign, n_rows=n_rows),
        out_shape=jax.ShapeDtypeStruct((n_rows,), jnp.int32),
        in_specs=[pl.BlockSpec(memory_space=pltpu.SMEM)],
        out_specs=pl.BlockSpec(memory_space=pltpu.SMEM),
        name="invert_rows",
    )(dest.T.reshape(-1))
    blk_e = blk[0, :n_blocks]
    n_used = blk[2, :1]
    pad_dst = n_assign + jnp.arange(n_rows, dtype=jnp.int32) - jnp.repeat(blk[1, :n_blocks], MOE_BLK)
    is_pad = row_src < 0
    row_tok = jnp.where(is_pad, 0, row_src // TOP_K).reshape(n_blocks, MOE_BLK)
    row_dst = jnp.where(is_pad, pad_dst, row_src).reshape(n_blocks, MOE_BLK)
    idx = jnp.concatenate([row_tok, row_dst, jnp.zeros((n_blocks, IDX_CHUNK - 2 * MOE_BLK), jnp.int32)], axis=1)
    return gates.T, blk_e, n_used, idx.reshape(-1)


def _combine_kernel(x_ref, y_ref, gate_ref, mod_ref, o_ref):
    tm = x_ref.shape[0]
    acc = jnp.zeros(x_ref.shape, F32)
    for k in range(TOP_K):
        acc += gate_ref[:, k:k + 1] * _load_row_tiles(y_ref, k, tm, row_stride=TOP_K)
    o_ref[...] = x_ref[...] + mod_ref[5:6, :] * acc


def _combine(x_mid, y_slots, gates, mod, group_of_tile, layer, tok0, tm):
    t = x_mid.shape[0]
    b0 = tok0 // tm
    return pl.pallas_call(
        _combine_kernel,
        out_shape=jax.ShapeDtypeStruct((t, D_MODEL), F32),
        grid=(t // tm,),
        in_specs=[
            pl.BlockSpec((tm, D_MODEL), lambda i: (i, 0)),
            pl.BlockSpec((tm * TOP_K * ROW_TILE, LANES), lambda i: (b0 + i, 0)),
            pl.BlockSpec((tm, TOP_K), lambda i: (b0 + i, 0)),
            pl.BlockSpec((None, None, N_MOD, D_MODEL), lambda i: (layer, group_of_tile(i), 0, 0)),
        ],
        out_specs=pl.BlockSpec((tm, D_MODEL), lambda i: (i, 0)),
        compiler_params=_params(("arbitrary",), VMEM_LIMIT_V7X),
        name="combine",
    )(x_mid, y_slots, gates, mod)


def _rope_tables(n_lat):
    quarter = HEAD_DIM // 4
    t = jnp.arange(n_lat)
    inv = ROPE_BASE ** (-jnp.arange(quarter, dtype=F32) / quarter)
    ang_r = (t // GRID_W).astype(F32)[:, None] * inv
    ang_c = (t % GRID_W).astype(F32)[:, None] * inv
    cos = jnp.concatenate([jnp.cos(ang_r)] * 2 + [jnp.cos(ang_c)] * 2, axis=-1)
    sin = jnp.concatenate([-jnp.sin(ang_r), jnp.sin(ang_r), -jnp.sin(ang_c), jnp.sin(ang_c)], axis=-1)
    return jnp.concatenate([cos, cos], axis=-1), jnp.concatenate([sin, sin], axis=-1)


def _block_diag_ones():
    idx = np.arange(MXU_COLS_V7X) // HEAD_DIM
    return jnp.asarray(idx[:, None] == idx[None, :], dtype=BF16)


def kernel(x_prompt, x_sample, cache_k_win, cache_v_win, cache_k_nbr, cache_v_nbr, c, c_ctx, w_mod, b_mod, g_mix, g_ffn, w_in, w_out, qn_win, kn_win, qn_nbr, kn_nbr, sink_win, rel_bias_nbr, conv_w, w_router, b_router, w_gate, b_gate, w_up, b_up, w_down, b_down):
    bsz, n_ctx, d = x_prompt.shape
    dbs, n_lat, _ = x_sample.shape
    depth = w_in.shape[0]
    past = cache_k_win.shape[2]
    assert d == D_MODEL and dbs + 1 <= COND_ROWS and n_lat % GRID_W == 0 and n_lat >= Q_BLK + 2 * WINDOW
    t_ctx, t_lat = bsz * n_ctx, dbs * n_lat

    cond = jnp.concatenate([c_ctx[None], c, jnp.zeros((COND_ROWS - 1 - dbs, d), F32)], axis=0)
    mod = _modulation(cond, w_mod, b_mod)

    w_in_bf = w_in.astype(BF16)
    w_out_bf = w_out.astype(BF16)
    norm_w = jnp.concatenate([jnp.tile(qn_win, (1, N_HEADS_A)), jnp.tile(kn_win, (1, N_KV_A)),
                              jnp.tile(qn_nbr, (1, N_HEADS_B)), jnp.tile(kn_nbr, (1, N_HEADS_B))], axis=-1)[:, None, :]
    ones_bd = _block_diag_ones()
    rope_tabs = _rope_tables(n_lat)
    g_mix3, g_ffn3 = g_mix[:, None, :], g_ffn[:, None, :]
    b_router3 = b_router[:, None, :]
    ck_win = cache_k_win.reshape(dbs, depth, past, KV_WIDTH_A)
    cv_win = cache_v_win.reshape(dbs, depth, past, KV_WIDTH_A)
    ck_nbr = cache_k_nbr.reshape(dbs, depth, past, WIDTH_B)
    cv_nbr = cache_v_nbr.reshape(dbs, depth, past, WIDTH_B)

    tm_ctx = 2 * n_ctx
    tm_lat = 512
    ctx_group = lambda i: 0
    lat_group_in = lambda i: 1 + i
    lat_group_out = lambda i: 1 + (i * tm_lat) // n_lat
    tm_comb = 256
    lat_group_comb = lambda i: 1 + (i * tm_comb) // n_lat
    assert t_ctx % tm_comb == 0 and n_lat % tm_comb == 0

    xp = x_prompt.reshape(t_ctx, d)
    xs = x_sample.reshape(t_lat, d)
    caches = [[], [], [], []]
    for l in range(depth):
        qa, qb, ka, va, kb, vb, oc = _inproj(xp, mod, ctx_group, g_mix3, w_in_bf, l, norm_w, ones_bd, conv_w,
                                             tm_ctx, n_ctx)
        oa, ob = _ctx_attention(sink_win[l], qa, ka, va, qb, kb, vb, n_ctx)
        xp_mid, hp, lgp = _outproj(xp, oa, ob, oc, w_out_bf, mod, ctx_group, g_ffn3, w_router, b_router3, l, tm_ctx)
        for lst, a in zip(caches, (ka, va, kb, vb)):
            lst.append(a)

        qa, qb, ka, va, kb, vb, oc = _inproj(xs, mod, lat_group_in, g_mix3, w_in_bf, l, norm_w, ones_bd, conv_w,
                                             n_lat, n_lat, rope_tabs)
        oa = _win_attention(sink_win[l], qa, ka, va, ck_win, cv_win, l, n_lat)
        ob = _nbr_attention(qb, kb, vb, ck_nbr, cv_nbr, _nbr_bias_table(rel_bias_nbr[l], n_lat // GRID_W), l, n_lat)
        xs_mid, hs, lgs = _outproj(xs, oa, ob, oc, w_out_bf, mod, lat_group_out, g_ffn3, w_router, b_router3, l,
                                   tm_lat)

        gates, blk_e, n_used, idx = _route(jnp.concatenate([lgp, lgs], axis=0))
        y_slots = _experts(blk_e, n_used, idx, jnp.concatenate([hp, hs], axis=0),
                           w_gate, b_gate, w_up, b_up, w_down, b_down, l)
        xp = _combine(xp_mid, y_slots, gates, mod, ctx_group, l, 0, tm_comb)
        xs = _combine(xs_mid, y_slots, gates, mod, lat_group_comb, l, t_ctx, tm_comb)

    new_v_win = jnp.stack([a.reshape(bsz, n_ctx, KV_WIDTH_A) for a in caches[1]], axis=1).reshape(
        bsz, depth, n_ctx, N_KV_A, HEAD_DIM)
    new_k_nbr = jnp.stack([a.reshape(bsz, n_ctx, WIDTH_B) for a in caches[2]], axis=1).reshape(
        bsz, depth, n_ctx, N_HEADS_B, HEAD_DIM)
    new_v_nbr = jnp.stack([a.reshape(bsz, n_ctx, WIDTH_B) for a in caches[3]], axis=1).reshape(
        bsz, depth, n_ctx, N_HEADS_B, HEAD_DIM)
    new_k_win = jnp.stack([a.reshape(bsz, n_ctx, KV_WIDTH_A) for a in caches[0]], axis=1).reshape(
        bsz, depth, n_ctx, N_KV_A, HEAD_DIM)
    return (xp.reshape(bsz, n_ctx, d), xs.reshape(dbs, n_lat, d), new_k_win, new_v_win, new_k_nbr, new_v_nbr)
```

```python
import functools

import numpy as np
import jax
import jax.numpy as jnp
from jax import lax
from jax.experimental import pallas as pl
from jax.experimental.pallas import tpu as pltpu

F32 = jnp.float32
BF16 = jnp.bfloat16

D_MODEL = 1024
HEAD_DIM = 64
GRID_W = 64
N_HEADS_A = 8
N_KV_A = 2
GROUP_A = N_HEADS_A // N_KV_A
WINDOW = 128
Q_BLK = 128
N_HEADS_B = 4
NA_ROWS = 8
NA_COLS = 16
CONV_CH = 256
CONV_W = 3
WIDTH_A = N_HEADS_A * HEAD_DIM
KV_WIDTH_A = N_KV_A * HEAD_DIM
WIDTH_B = N_HEADS_B * HEAD_DIM
IN_COLS = WIDTH_A + 2 * KV_WIDTH_A + 3 * WIDTH_B + 3 * CONV_CH
N_EXPERTS = 32
TOP_K = 4
SWIGLU_LIMIT = 7.0
SWIGLU_ALPHA = 1.702
ROPE_BASE = 10000.0
EPS = 1e-6
NEG = -1e30
N_MOD = 6

C_QA = 0
C_KA = C_QA + WIDTH_A
C_VA = C_KA + KV_WIDTH_A
C_QB = C_VA + KV_WIDTH_A
C_KB = C_QB + WIDTH_B
C_VB = C_KB + WIDTH_B
C_U = C_VB + WIDTH_B
C_GB = C_U + CONV_CH
C_GC = C_GB + CONV_CH

MXU_COLS_V7X = 256
COND_ROWS = 8
MOE_BLK = 256
MOE_SHIFT = MOE_BLK.bit_length() - 1
assert 1 << MOE_SHIFT == MOE_BLK
VMEM_LIMIT_V7X = 56 * 1024 * 1024

_NT = (((1,), (1,)), ((), ()))


def _params(sem, vmem=None):
    return pltpu.CompilerParams(dimension_semantics=sem, vmem_limit_bytes=vmem)


def _mod_kernel(c_ref, w_ref, b_ref, o_ref):
    c = c_ref[...]
    s = c * jax.nn.sigmoid(c)
    o_ref[...] = jnp.dot(s.astype(BF16), w_ref[...].astype(BF16), preferred_element_type=F32) + b_ref[...]


def _modulation(cond, w_mod, b_mod):
    depth = w_mod.shape[0]
    out = pl.pallas_call(
        _mod_kernel,
        out_shape=jax.ShapeDtypeStruct((depth, COND_ROWS, N_MOD * D_MODEL), F32),
        grid=(depth, N_MOD),
        in_specs=[
            pl.BlockSpec((COND_ROWS, D_MODEL), lambda l, j: (0, 0)),
            pl.BlockSpec((None, D_MODEL, D_MODEL), lambda l, j: (l, 0, j)),
            pl.BlockSpec((None, 1, D_MODEL), lambda l, j: (l, 0, j)),
        ],
        out_specs=pl.BlockSpec((None, COND_ROWS, D_MODEL), lambda l, j: (l, 0, j)),
        compiler_params=_params(("arbitrary", "arbitrary")),
        name="modulation",
    )(cond, w_mod, b_mod.reshape(depth, 1, N_MOD * D_MODEL))
    return out.reshape(depth, COND_ROWS, N_MOD, D_MODEL)


def _head_norm(x, w_row, ones_ref):
    width = x.shape[1]
    sq = (x * x).astype(BF16)
    parts = []
    for c0 in range(0, width, MXU_COLS_V7X):
        wd = min(MXU_COLS_V7X, width - c0)
        parts.append(jnp.dot(sq[:, c0:c0 + wd], ones_ref[:wd, :wd], preferred_element_type=F32))
    ss = parts[0] if len(parts) == 1 else jnp.concatenate(parts, axis=-1)
    return x * lax.rsqrt(ss * (1.0 / HEAD_DIM) + EPS) * w_row


def _rope(x, cos, sin):
    width = x.shape[1]
    lane = lax.broadcasted_iota(jnp.int32, x.shape, 1)
    quarter = HEAD_DIM // 4
    partner = jnp.where((lane % (2 * quarter)) < quarter,
                        pltpu.roll(x, width - quarter, axis=1), pltpu.roll(x, quarter, axis=1))
    reps = width // cos.shape[1]
    cos_w = cos if reps == 1 else jnp.concatenate([cos] * reps, axis=-1)
    sin_w = sin if reps == 1 else jnp.concatenate([sin] * reps, axis=-1)
    return x * cos_w + partner * sin_w


def _inproj_kernel(*refs, seq_len, rope):
    if rope:
        (x_ref, mod_ref, g_ref, w_ref, nw_ref, ones_ref, cw_ref, cos_ref, sin_ref,
         qa_o, qb_o, ka_o, va_o, kb_o, vb_o, oc_o) = refs
    else:
        (x_ref, mod_ref, g_ref, w_ref, nw_ref, ones_ref, cw_ref,
         qa_o, qb_o, ka_o, va_o, kb_o, vb_o, oc_o) = refs
    x = x_ref[...]
    tm = x.shape[0]
    h = x * lax.rsqrt(jnp.mean(x * x, axis=-1, keepdims=True) + EPS) * g_ref[...]
    h = h * (1.0 + mod_ref[1:2, :]) + mod_ref[0:1, :]
    p = jnp.dot(h.astype(BF16), w_ref[...], preferred_element_type=F32)

    qa = _head_norm(p[:, C_QA:C_KA], nw_ref[:, 0:WIDTH_A], ones_ref)
    ka = _head_norm(p[:, C_KA:C_VA], nw_ref[:, WIDTH_A:WIDTH_A + KV_WIDTH_A], ones_ref)
    o_qb = WIDTH_A + KV_WIDTH_A
    qb = _head_norm(p[:, C_QB:C_KB], nw_ref[:, o_qb:o_qb + WIDTH_B], ones_ref)
    kb = _head_norm(p[:, C_KB:C_VB], nw_ref[:, o_qb + WIDTH_B:o_qb + 2 * WIDTH_B], ones_ref)
    if rope:
        cos, sin = cos_ref[...], sin_ref[...]
        qa = _rope(qa, cos, sin)
        ka = _rope(ka, cos, sin)
    qa_o[...] = qa.astype(BF16)
    qb_o[...] = qb.astype(BF16)
    ka_o[...] = ka
    va_o[...] = p[:, C_VA:C_QB]
    kb_o[...] = kb
    vb_o[...] = p[:, C_VB:C_U]

    z = p[:, C_GC:C_GC + CONV_CH] * p[:, C_U:C_GB]
    row = lax.broadcasted_iota(jnp.int32, z.shape, 0) % seq_len
    z_prev = jnp.where(row == 0, 0.0, pltpu.roll(z, 1, axis=0))
    z_next = jnp.where(row == seq_len - 1, 0.0, pltpu.roll(z, tm - 1, axis=0))
    y = z_prev * cw_ref[0:1, :] + z * cw_ref[1:2, :] + z_next * cw_ref[2:3, :]
    oc_o[...] = (p[:, C_GB:C_GC] * y).astype(BF16)


def _inproj(x, mod, group_of_tile, g_mix, w_in_bf, layer, norm_w, ones_bd, conv_w, tm, seq_len, rope_tabs=None):
    t = x.shape[0]
    rope = rope_tabs is not None
    in_specs = [
        pl.BlockSpec((tm, D_MODEL), lambda i: (i, 0)),
        pl.BlockSpec((None, None, N_MOD, D_MODEL), lambda i: (layer, group_of_tile(i), 0, 0)),
        pl.BlockSpec((None, 1, D_MODEL), lambda i: (layer, 0, 0)),
        pl.BlockSpec((None, D_MODEL, IN_COLS), lambda i: (layer, 0, 0)),
        pl.BlockSpec((None, 1, norm_w.shape[-1]), lambda i: (layer, 0, 0)),
        pl.BlockSpec(ones_bd.shape, lambda i: (0, 0)),
        pl.BlockSpec((None, CONV_W, CONV_CH), lambda i: (layer, 0, 0)),
    ]
    args = [x, mod, g_mix, w_in_bf, norm_w, ones_bd, conv_w]
    if rope:
        in_specs += [pl.BlockSpec(rope_tabs[0].shape, lambda i: (0, 0))] * 2
        args += list(rope_tabs)
    widths = (WIDTH_A, WIDTH_B, KV_WIDTH_A, KV_WIDTH_A, WIDTH_B, WIDTH_B, CONV_CH)
    dtypes = (BF16, BF16, F32, F32, F32, F32, BF16)
    return pl.pallas_call(
        functools.partial(_inproj_kernel, seq_len=seq_len, rope=rope),
        out_shape=[jax.ShapeDtypeStruct((t, w), dt) for w, dt in zip(widths, dtypes)],
        grid=(t // tm,),
        in_specs=in_specs,
        out_specs=[pl.BlockSpec((tm, w), lambda i: (i, 0)) for w in widths],
        compiler_params=_params(("arbitrary",), VMEM_LIMIT_V7X),
        name="inproj_rope" if rope else "inproj",
    )(*args)


def _softmax_pv(s, v, sink=None):
    m = jnp.max(s, axis=-1, keepdims=True)
    if sink is not None:
        m = jnp.maximum(m, sink)
    p = jnp.exp(s - m)
    den = jnp.sum(p, axis=-1, keepdims=True)
    if sink is not None:
        den = den + jnp.exp(sink - m)
    return jnp.dot(p.astype(BF16), v, preferred_element_type=F32) / den


def _store_heads(o_ref, outs):
    for i in range(0, len(outs), 2):
        o_ref[:, i * HEAD_DIM:(i + 2) * HEAD_DIM] = jnp.concatenate(outs[i:i + 2], axis=-1).astype(o_ref.dtype)


def _ctx_attn_kernel(sink_ref, qa_ref, ka_ref, va_ref, qb_ref, kb_ref, vb_ref, oa_o, ob_o):
    scale = HEAD_DIM ** -0.5
    ka = ka_ref[...].astype(BF16)
    va = va_ref[...].astype(BF16)
    outs = []
    for h in range(N_HEADS_A):
        j = h // GROUP_A
        q = qa_ref[:, h * HEAD_DIM:(h + 1) * HEAD_DIM] * scale
        s = lax.dot_general(q, ka[:, j * HEAD_DIM:(j + 1) * HEAD_DIM], _NT, preferred_element_type=F32)
        outs.append(_softmax_pv(s, va[:, j * HEAD_DIM:(j + 1) * HEAD_DIM], sink_ref[h]))
    _store_heads(oa_o, outs)
    kb = kb_ref[...].astype(BF16)
    vb = vb_ref[...].astype(BF16)
    outs = []
    for h in range(N_HEADS_B):
        sl = slice(h * HEAD_DIM, (h + 1) * HEAD_DIM)
        q = qb_ref[:, sl] * scale
        s = lax.dot_general(q, kb[:, sl], _NT, preferred_element_type=F32)
        outs.append(_softmax_pv(s, vb[:, sl]))
    _store_heads(ob_o, outs)


def _ctx_attention(sink, qa, ka, va, qb, kb, vb, seq_len):
    t = qa.shape[0]
    widths = (WIDTH_A, KV_WIDTH_A, KV_WIDTH_A, WIDTH_B, WIDTH_B, WIDTH_B)
    return pl.pallas_call(
        _ctx_attn_kernel,
        out_shape=[jax.ShapeDtypeStruct((t, WIDTH_A), BF16), jax.ShapeDtypeStruct((t, WIDTH_B), BF16)],
        grid=(t // seq_len,),
        in_specs=[pl.BlockSpec(memory_space=pltpu.SMEM)]
        + [pl.BlockSpec((seq_len, w), lambda i: (i, 0)) for w in widths],
        out_specs=[pl.BlockSpec((seq_len, WIDTH_A), lambda i: (i, 0)),
                   pl.BlockSpec((seq_len, WIDTH_B), lambda i: (i, 0))],
        compiler_params=_params(("arbitrary",)),
        name="ctx_attention",
    )(sink, qa, ka, va, qb, kb, vb)


def _win_attn_kernel(sink_ref, qa_ref, ka_ref, va_ref, ck_ref, cv_ref, oa_o, *, n_lat):
    scale = HEAD_DIM ** -0.5
    span = Q_BLK + 2 * WINDOW
    n = pl.program_id(1)
    start = pl.multiple_of(jnp.clip(n * Q_BLK - WINDOW, 0, n_lat - span), Q_BLK)
    kcat = jnp.concatenate([ka_ref[pl.ds(start, span), :], ck_ref[...]], axis=0).astype(BF16)
    vcat = jnp.concatenate([va_ref[pl.ds(start, span), :], cv_ref[...]], axis=0).astype(BF16)
    n_keys = kcat.shape[0]
    qpos = n * Q_BLK + lax.broadcasted_iota(jnp.int32, (Q_BLK, n_keys), 0)
    col = lax.broadcasted_iota(jnp.int32, (Q_BLK, n_keys), 1)
    ok = (col >= span) | (jnp.abs(qpos - (start + col)) <= WINDOW)
    outs = []
    for h in range(N_HEADS_A):
        j = h // GROUP_A
        q = qa_ref[:, h * HEAD_DIM:(h + 1) * HEAD_DIM] * scale
        s = lax.dot_general(q, kcat[:, j * HEAD_DIM:(j + 1) * HEAD_DIM], _NT, preferred_element_type=F32)
        s = jnp.where(ok, s, NEG)
        outs.append(_softmax_pv(s, vcat[:, j * HEAD_DIM:(j + 1) * HEAD_DIM], sink_ref[h]))
    _store_heads(oa_o, outs)


def _win_attention(sink, qa, ka, va, cache_k, cache_v, layer, n_lat):
    t = qa.shape[0]
    nb = n_lat // Q_BLK
    past = cache_k.shape[2]
    return pl.pallas_call(
        functools.partial(_win_attn_kernel, n_lat=n_lat),
        out_shape=jax.ShapeDtypeStruct((t, WIDTH_A), BF16),
        grid=(t // n_lat, nb),
        in_specs=[
            pl.BlockSpec(memory_space=pltpu.SMEM),
            pl.BlockSpec((Q_BLK, WIDTH_A), lambda b, n: (b * nb + n, 0)),
            pl.BlockSpec((n_lat, KV_WIDTH_A), lambda b, n: (b, 0)),
            pl.BlockSpec((n_lat, KV_WIDTH_A), lambda b, n: (b, 0)),
            pl.BlockSpec((None, None, past, KV_WIDTH_A), lambda b, n: (b, layer, 0, 0)),
            pl.BlockSpec((None, None, past, KV_WIDTH_A), lambda b, n: (b, layer, 0, 0)),
        ],
        out_specs=pl.BlockSpec((Q_BLK, WIDTH_A), lambda b, n: (b * nb + n, 0)),
        compiler_params=_params(("arbitrary", "arbitrary")),
        name="window_attention",
    )(sink, qa, ka, va, cache_k, cache_v)


def _nbr_attn_kernel(qb_ref, kb_ref, vb_ref, ck_ref, cv_ref, bias_ref, ob_o, *, rows, kr):
    scale = HEAD_DIM ** -0.5
    r = pl.program_id(1)
    start = pl.multiple_of(jnp.clip(r - kr // 2, 0, rows - kr) * GRID_W, GRID_W)
    nwin = kr * GRID_W
    kcat = jnp.concatenate([kb_ref[pl.ds(start, nwin), :], ck_ref[...]], axis=0).astype(BF16)
    vcat = jnp.concatenate([vb_ref[pl.ds(start, nwin), :], cv_ref[...]], axis=0).astype(BF16)
    past = ck_ref.shape[0]
    outs = []
    for h in range(N_HEADS_B):
        sl = slice(h * HEAD_DIM, (h + 1) * HEAD_DIM)
        q = qb_ref[:, sl] * scale
        s = lax.dot_general(q, kcat[:, sl], _NT, preferred_element_type=F32)
        bias = jnp.concatenate([bias_ref[h], jnp.zeros((GRID_W, past), F32)], axis=-1)
        outs.append(_softmax_pv(s + bias, vcat[:, sl]))
    _store_heads(ob_o, outs)


def _nbr_bias_table(rel_bias, rows):
    kr = min(NA_ROWS, rows)
    cls = np.arange(kr)[:, None]
    m = np.arange(kr)[None, :]
    row_sel = (m - cls + NA_ROWS - 1)[:, :, None] == np.arange(2 * NA_ROWS - 1)[None, None, :]
    c = np.arange(GRID_W)[:, None]
    kc = np.arange(GRID_W)[None, :]
    dc = np.clip(kc - c, -(NA_COLS - 1), NA_COLS - 1) + NA_COLS - 1
    col_sel = dc[:, :, None] == np.arange(2 * NA_COLS - 1)[None, None, :]
    col_start = np.clip(c - NA_COLS // 2, 0, GRID_W - NA_COLS)
    ok = (kc >= col_start) & (kc < col_start + NA_COLS)
    tab = jnp.einsum('hrd,ymr,ckd->hycmk', rel_bias.astype(F32), row_sel.astype(np.float32),
                     col_sel.astype(np.float32), precision=lax.Precision.HIGHEST)
    tab = jnp.where(ok[None, None, :, None, :], tab, NEG)
    return tab.reshape(rel_bias.shape[0], kr, GRID_W, kr * GRID_W)


def _nbr_attention(qb, kb, vb, cache_k, cache_v, bias_tab, layer, n_lat):
    t = qb.shape[0]
    rows = n_lat // GRID_W
    kr = min(NA_ROWS, rows)
    past = cache_k.shape[2]

    def bias_map(b, r):
        return (0, r - jnp.clip(r - kr // 2, 0, rows - kr), 0, 0)

    return pl.pallas_call(
        functools.partial(_nbr_attn_kernel, rows=rows, kr=kr),
        out_shape=jax.ShapeDtypeStruct((t, WIDTH_B), BF16),
        grid=(t // n_lat, rows),
        in_specs=[
            pl.BlockSpec((GRID_W, WIDTH_B), lambda b, r: (b * rows + r, 0)),
            pl.BlockSpec((n_lat, WIDTH_B), lambda b, r: (b, 0)),
            pl.BlockSpec((n_lat, WIDTH_B), lambda b, r: (b, 0)),
            pl.BlockSpec((None, None, past, WIDTH_B), lambda b, r: (b, layer, 0, 0)),
            pl.BlockSpec((None, None, past, WIDTH_B), lambda b, r: (b, layer, 0, 0)),
            pl.BlockSpec((N_HEADS_B, None, GRID_W, kr * GRID_W), bias_map),
        ],
        out_specs=pl.BlockSpec((GRID_W, WIDTH_B), lambda b, r: (b * rows + r, 0)),
        compiler_params=_params(("arbitrary", "arbitrary")),
        name="neighbourhood_attention",
    )(qb, kb, vb, cache_k, cache_v, bias_tab)


ROW_TILE = 8
LANES = 128
assert ROW_TILE * LANES == D_MODEL


def _store_row_tiles(ref, row0, x):
    n = x.shape[0]
    for c in range(ROW_TILE):
        ref[pl.ds(row0 * ROW_TILE + c, n, stride=ROW_TILE), :] = x[:, c * LANES:(c + 1) * LANES]


def _load_row_tiles(ref, row0, n, row_stride=1, chunk0=0):
    return jnp.concatenate(
        [ref[pl.ds(row0 * ROW_TILE + chunk0 + c, n, stride=ROW_TILE * row_stride), :] for c in range(ROW_TILE)],
        axis=-1)


def _outproj_kernel(x_ref, oa_ref, ob_ref, oc_ref, w_ref, mod_ref, g_ref, wr_ref, br_ref, xn_o, h_o, lg_o):
    mix = jnp.dot(oa_ref[...], w_ref[0:WIDTH_A, :], preferred_element_type=F32)
    mix += jnp.dot(ob_ref[...], w_ref[WIDTH_A:WIDTH_A + WIDTH_B, :], preferred_element_type=F32)
    mix += jnp.dot(oc_ref[...], w_ref[WIDTH_A + WIDTH_B:, :], preferred_element_type=F32)
    xn = x_ref[...] + mod_ref[2:3, :] * mix
    xn_o[...] = xn
    h = xn * lax.rsqrt(jnp.mean(xn * xn, axis=-1, keepdims=True) + EPS) * g_ref[...]
    h = h * (1.0 + mod_ref[4:5, :]) + mod_ref[3:4, :]
    _store_row_tiles(h_o, 0, h)
    h_hi = h.astype(BF16)
    h_lo = (h - h_hi.astype(F32)).astype(BF16)
    wr = wr_ref[...]
    w_hi = wr.astype(BF16)
    w_lo = (wr - w_hi.astype(F32)).astype(BF16)
    lg = jnp.dot(h_hi, w_hi, preferred_element_type=F32)
    lg += jnp.dot(h_lo, w_hi, preferred_element_type=F32)
    lg += jnp.dot(h_hi, w_lo, preferred_element_type=F32)
    lg_o[...] = lg + br_ref[...]


def _outproj(x, oa, ob, oc, w_out_bf, mod, group_of_tile, g_ffn, w_router, b_router, layer, tm):
    t = x.shape[0]
    return pl.pallas_call(
        _outproj_kernel,
        out_shape=[jax.ShapeDtypeStruct((t, D_MODEL), F32), jax.ShapeDtypeStruct((t * ROW_TILE, LANES), F32),
                   jax.ShapeDtypeStruct((t, N_EXPERTS), F32)],
        grid=(t // tm,),
        in_specs=[
            pl.BlockSpec((tm, D_MODEL), lambda i: (i, 0)),
            pl.BlockSpec((tm, WIDTH_A), lambda i: (i, 0)),
            pl.BlockSpec((tm, WIDTH_B), lambda i: (i, 0)),
            pl.BlockSpec((tm, CONV_CH), lambda i: (i, 0)),
            pl.BlockSpec((None, D_MODEL, D_MODEL), lambda i: (layer, 0, 0)),
            pl.BlockSpec((None, None, N_MOD, D_MODEL), lambda i: (layer, group_of_tile(i), 0, 0)),
            pl.BlockSpec((None, 1, D_MODEL), lambda i: (layer, 0, 0)),
            pl.BlockSpec((None, D_MODEL, N_EXPERTS), lambda i: (layer, 0, 0)),
            pl.BlockSpec((None, 1, N_EXPERTS), lambda i: (layer, 0, 0)),
        ],
        out_specs=[pl.BlockSpec((tm, D_MODEL), lambda i: (i, 0)),
                   pl.BlockSpec((tm * ROW_TILE, LANES), lambda i: (i, 0)),
                   pl.BlockSpec((tm, N_EXPERTS), lambda i: (i, 0))],
        compiler_params=_params(("arbitrary",), VMEM_LIMIT_V7X),
        name="outproj_router",
    )(x, oa, ob, oc, w_out_bf, mod, g_ffn, w_router, b_router)


IDX_CHUNK = 1024
IDX_SLOTS = 3
assert 2 * MOE_BLK <= IDX_CHUNK


def _expert_kernel(blk_e_ref, n_used_ref, idx_hbm, h_hbm, wg_ref, bg_ref, wu_ref, bu_ref, wd_ref, bd_ref, y_hbm,
                   idx_s, x_buf, y_buf, isem, gsem, ssem, *, n):
    i = pl.program_id(0)
    slot = i % 2
    blk_rows = MOE_BLK * ROW_TILE

    def idx_copy(blk):
        return pltpu.make_async_copy(idx_hbm.at[pl.ds(pl.multiple_of(blk * IDX_CHUNK, IDX_CHUNK), IDX_CHUNK)],
                                     idx_s.at[blk % IDX_SLOTS], isem.at[blk % IDX_SLOTS])

    def issue_gather(blk):
        s3, s2 = blk % IDX_SLOTS, blk % 2

        def body(r, carry):
            tok = idx_s[s3, r]
            pltpu.make_async_copy(h_hbm.at[pl.ds(pl.multiple_of(tok * ROW_TILE, ROW_TILE), ROW_TILE)],
                                  x_buf.at[s2, pl.ds(pl.multiple_of(r * ROW_TILE, ROW_TILE), ROW_TILE)],
                                  gsem.at[s2]).start()
            return carry

        lax.fori_loop(0, MOE_BLK, body, 0, unroll=8)

    def wait_gather(s2):
        pltpu.make_async_copy(h_hbm.at[pl.ds(0, blk_rows)], x_buf.at[s2], gsem.at[s2]).wait()

    def issue_scatter(blk):
        s3, s2 = blk % IDX_SLOTS, blk % 2

        def body(r, carry):
            dst = idx_s[s3, MOE_BLK + r]
            pltpu.make_async_copy(y_buf.at[s2, pl.ds(pl.multiple_of(r * ROW_TILE, ROW_TILE), ROW_TILE)],
                                  y_hbm.at[pl.ds(pl.multiple_of(dst * ROW_TILE, ROW_TILE), ROW_TILE)],
                                  ssem.at[s2]).start()
            return carry

        lax.fori_loop(0, MOE_BLK, body, 0, unroll=8)

    def wait_scatter(s2):
        pltpu.make_async_copy(y_buf.at[s2], y_hbm.at[pl.ds(0, blk_rows)], ssem.at[s2]).wait()

    @pl.when(i == 0)
    def _():
        idx_copy(0).start()
        idx_copy(0).wait()
        issue_gather(0)
        idx_copy(1).start()

    @pl.when(i + 1 < n)
    def _():
        idx_copy(i + 1).wait()
        issue_gather(i + 1)

    @pl.when(i + 2 < n)
    def _():
        idx_copy(i + 2).start()

    @pl.when(i >= 2)
    def _():
        wait_scatter(slot)

    wait_gather(slot)

    @pl.when(i < n_used_ref[0])
    def _():
        x = _load_row_tiles(x_buf.at[slot], 0, MOE_BLK).astype(BF16)
        g = jnp.dot(x, wg_ref[...].astype(BF16), preferred_element_type=F32) + bg_ref[...]
        u = jnp.dot(x, wu_ref[...].astype(BF16), preferred_element_type=F32) + bu_ref[...]
        g = jnp.minimum(g, SWIGLU_LIMIT)
        u = jnp.clip(u, -SWIGLU_LIMIT, SWIGLU_LIMIT)
        a = g * jax.nn.sigmoid(SWIGLU_ALPHA * g) * (u + 1.0)
        y = jnp.dot(a.astype(BF16), wd_ref[...].astype(BF16), preferred_element_type=F32) + bd_ref[...]
        _store_row_tiles(y_buf.at[slot], 0, y)

    @pl.when(i >= n_used_ref[0])
    def _():
        y_buf[slot] = jnp.zeros((blk_rows, LANES), F32)

    issue_scatter(i)

    @pl.when(i == n - 1)
    def _():
        wait_scatter(slot)
        wait_scatter(1 - slot)


def _experts(blk_e, n_used, idx, h_tiles, w_gate, b_gate, w_up, b_up, w_down, b_down, layer):
    n_blocks = idx.shape[0] // IDX_CHUNK
    assert n_blocks >= 2
    n_rows = n_blocks * MOE_BLK
    wspec = pl.BlockSpec((None, None, D_MODEL, D_MODEL), lambda i, be, nu: (layer, be[i], 0, 0))
    bspec = pl.BlockSpec((None, None, 1, D_MODEL), lambda i, be, nu: (layer, be[i], 0, 0))
    anyspec = pl.BlockSpec(memory_space=pl.ANY)
    depth = w_gate.shape[0]
    b4 = lambda b: b.reshape(depth, N_EXPERTS, 1, D_MODEL)
    return pl.pallas_call(
        functools.partial(_expert_kernel, n=n_blocks),
        out_shape=jax.ShapeDtypeStruct((n_rows * ROW_TILE, LANES), F32),
        grid_spec=pltpu.PrefetchScalarGridSpec(
            num_scalar_prefetch=2,
            grid=(n_blocks,),
            in_specs=[anyspec, anyspec, wspec, bspec, wspec, bspec, wspec, bspec],
            out_specs=anyspec,
            scratch_shapes=[
                pltpu.SMEM((IDX_SLOTS, IDX_CHUNK), jnp.int32),
                pltpu.VMEM((2, MOE_BLK * ROW_TILE, LANES), F32),
                pltpu.VMEM((2, MOE_BLK * ROW_TILE, LANES), F32),
                pltpu.SemaphoreType.DMA((IDX_SLOTS,)),
                pltpu.SemaphoreType.DMA((2,)),
                pltpu.SemaphoreType.DMA((2,)),
            ],
        ),
        compiler_params=_params(("arbitrary",), VMEM_LIMIT_V7X),
        name="experts",
    )(blk_e, n_used, idx, h_tiles, w_gate, b4(b_gate), w_up, b4(b_up), w_down, b4(b_down))


CUM_CHUNK = 256
BLK_LANES = 256


def _sublane_cumsum(x):
    row = lax.broadcasted_iota(jnp.int32, x.shape, 0)
    d = 1
    while d < x.shape[0]:
        x = x + jnp.where(row >= d, pltpu.roll(x, d, axis=0), 0)
        d *= 2
    return x


def _route_kernel(lg_ref, tri_ref, gate_o, dest_o, blk_o, *, n_blocks):
    lg = lg_ref[...]
    n_e, t = lg.shape
    e_iota = lax.broadcasted_iota(jnp.int32, lg.shape, 0)
    work = lg
    tops, hots = [], []
    for _ in range(TOP_K):
        m = jnp.max(work, axis=0, keepdims=True)
        first = jnp.min(jnp.where(work == m, e_iota, n_e), axis=0, keepdims=True)
        hot = e_iota == first
        work = jnp.where(hot, -jnp.inf, work)
        tops.append(m)
        hots.append(hot)
    ex = [jnp.exp(m - tops[0]) for m in tops]
    den = ex[0] + ex[1] + ex[2] + ex[3]
    for k in range(TOP_K):
        gate_o[k:k + 1, :] = ex[k] / den

    chosen = jnp.where(hots[0] | hots[1] | hots[2] | hots[3], 1.0, 0.0)
    tri = tri_ref[...]
    carry = jnp.zeros((n_e, 1), F32)
    before = []
    for c0 in range(0, t, CUM_CHUNK):
        chunk = chosen[:, c0:c0 + CUM_CHUNK]
        inc = jnp.dot(chunk.astype(BF16), tri, preferred_element_type=F32)
        before.append(inc - chunk + carry)
        carry = carry + inc[:, CUM_CHUNK - 1:CUM_CHUNK]
    before = jnp.concatenate(before, axis=-1).astype(jnp.int32)

    counts = jnp.broadcast_to(carry.astype(jnp.int32), (n_e, BLK_LANES))
    padded = ((counts + (MOE_BLK - 1)) >> MOE_SHIFT) << MOE_SHIFT
    pad_end = _sublane_cumsum(padded)
    pad_start = pad_end - padded
    row = pad_start[:, 0:1] + before
    for k in range(TOP_K):
        dest_o[k:k + 1, :] = jnp.sum(jnp.where(hots[k], row, 0), axis=0, keepdims=True)

    blk_row0 = lax.broadcasted_iota(jnp.int32, (n_e, BLK_LANES), 1) * MOE_BLK
    blk_o[0:1, :] = jnp.minimum(jnp.sum((pad_end <= blk_row0).astype(jnp.int32), axis=0, keepdims=True), n_e - 1)
    blk_o[1:2, :] = jnp.sum(jnp.where(pad_start <= blk_row0, counts, 0), axis=0, keepdims=True)
    blk_o[2:3, :] = pad_end[n_e - 1:n_e, :] >> MOE_SHIFT
    blk_o[3:8, :] = jnp.zeros((5, BLK_LANES), jnp.int32)


def _invert_kernel(dest_ref, src_o, *, n_assign, n_rows):
    def init(r, carry):
        src_o[r] = -1
        return carry

    lax.fori_loop(0, n_rows, init, 0, unroll=8)

    def put(a, carry):
        src_o[dest_ref[a]] = a
        return carry

    lax.fori_loop(0, n_assign, put, 0, unroll=8)


def _route(logits):
    t = logits.shape[0]
    n_assign = t * TOP_K
    n_blocks = n_assign // MOE_BLK + N_EXPERTS
    n_rows = n_blocks * MOE_BLK
    assert t % CUM_CHUNK == 0 and n_blocks <= BLK_LANES
    tri = jnp.asarray(np.triu(np.ones((CUM_CHUNK, CUM_CHUNK), np.float32)), dtype=BF16)
    gates, dest, blk = pl.pallas_call(
        functools.partial(_route_kernel, n_blocks=n_blocks),
        out_shape=[jax.ShapeDtypeStruct((TOP_K, t), F32), jax.ShapeDtypeStruct((TOP_K, t), jnp.int32),
                   jax.ShapeDtypeStruct((8, BLK_LANES), jnp.int32)],
        compiler_params=_params(None, VMEM_LIMIT_V7X),
        name="route",
    )(logits.T, tri)
    row_src = pl.pallas_call(
        functools.partial(_invert_kernel, n_assign=n_assign, n_rows=n_rows),
        out_shape=jax.ShapeDtypeStruct((n_rows,), jnp.int32),
        in_specs=[pl.BlockSpec(memory_space=pltpu.SMEM)],
        out_specs=pl.BlockSpec(memory_space=pltpu.SMEM),
        name="invert_rows",
    )(dest.T.reshape(-1))
    blk_e = blk[0, :n_blocks]
    n_used = blk[2, :1]
    pad_dst = n_assign + jnp.arange(n_rows, dtype=jnp.int32) - jnp.repeat(blk[1, :n_blocks], MOE_BLK)
    is_pad = row_src < 0
    row_tok = jnp.where(is_pad, 0, row_src // TOP_K).reshape(n_blocks, MOE_BLK)
    row_dst = jnp.where(is_pad, pad_dst, row_src).reshape(n_blocks, MOE_BLK)
    idx = jnp.concatenate([row_tok, row_dst, jnp.zeros((n_blocks, IDX_CHUNK - 2 * MOE_BLK), jnp.int32)], axis=1)
    return gates.T, blk_e, n_used, idx.reshape(-1)


def _combine_kernel(x_ref, y_ref, gate_ref, mod_ref, o_ref):
    tm = x_ref.shape[0]
    acc = jnp.zeros(x_ref.shape, F32)
    for k in range(TOP_K):
        acc += gate_ref[:, k:k + 1] * _load_row_tiles(y_ref, k, tm, row_stride=TOP_K)
    o_ref[...] = x_ref[...] + mod_ref[5:6, :] * acc


def _combine(x_mid, y_slots, gates, mod, group_of_tile, layer, tok0, tm):
    t = x_mid.shape[0]
    b0 = tok0 // tm
    return pl.pallas_call(
        _combine_kernel,
        out_shape=jax.ShapeDtypeStruct((t, D_MODEL), F32),
        grid=(t // tm,),
        in_specs=[
            pl.BlockSpec((tm, D_MODEL), lambda i: (i, 0)),
            pl.BlockSpec((tm * TOP_K * ROW_TILE, LANES), lambda i: (b0 + i, 0)),
            pl.BlockSpec((tm, TOP_K), lambda i: (b0 + i, 0)),
            pl.BlockSpec((None, None, N_MOD, D_MODEL), lambda i: (layer, group_of_tile(i), 0, 0)),
        ],
        out_specs=pl.BlockSpec((tm, D_MODEL), lambda i: (i, 0)),
        compiler_params=_params(("arbitrary",), VMEM_LIMIT_V7X),
        name="combine",
    )(x_mid, y_slots, gates, mod)


def _rope_tables(n_lat):
    quarter = HEAD_DIM // 4
    t = jnp.arange(n_lat)
    inv = ROPE_BASE ** (-jnp.arange(quarter, dtype=F32) / quarter)
    ang_r = (t // GRID_W).astype(F32)[:, None] * inv
    ang_c = (t % GRID_W).astype(F32)[:, None] * inv
    cos = jnp.concatenate([jnp.cos(ang_r)] * 2 + [jnp.cos(ang_c)] * 2, axis=-1)
    sin = jnp.concatenate([-jnp.sin(ang_r), jnp.sin(ang_r), -jnp.sin(ang_c), jnp.sin(ang_c)], axis=-1)
    return jnp.concatenate([cos, cos], axis=-1), jnp.concatenate([sin, sin], axis=-1)


def _block_diag_ones():
    idx = np.arange(MXU_COLS_V7X) // HEAD_DIM
    return jnp.asarray(idx[:, None] == idx[None, :], dtype=BF16)


def kernel(x_prompt, x_sample, cache_k_win, cache_v_win, cache_k_nbr, cache_v_nbr, c, c_ctx, w_mod, b_mod, g_mix, g_ffn, w_in, w_out, qn_win, kn_win, qn_nbr, kn_nbr, sink_win, rel_bias_nbr, conv_w, w_router, b_router, w_gate, b_gate, w_up, b_up, w_down, b_down):
    bsz, n_ctx, d = x_prompt.shape
    dbs, n_lat, _ = x_sample.shape
    depth = w_in.shape[0]
    past = cache_k_win.shape[2]
    assert d == D_MODEL and dbs + 1 <= COND_ROWS and n_lat % GRID_W == 0 and n_lat >= Q_BLK + 2 * WINDOW
    t_ctx, t_lat = bsz * n_ctx, dbs * n_lat

    cond = jnp.concatenate([c_ctx[None], c, jnp.zeros((COND_ROWS - 1 - dbs, d), F32)], axis=0)
    mod = _modulation(cond, w_mod, b_mod)

    w_in_bf = w_in.astype(BF16)
    w_out_bf = w_out.astype(BF16)
    norm_w = jnp.concatenate([jnp.tile(qn_win, (1, N_HEADS_A)), jnp.tile(kn_win, (1, N_KV_A)),
                              jnp.tile(qn_nbr, (1, N_HEADS_B)), jnp.tile(kn_nbr, (1, N_HEADS_B))], axis=-1)[:, None, :]
    ones_bd = _block_diag_ones()
    rope_tabs = _rope_tables(n_lat)
    g_mix3, g_ffn3 = g_mix[:, None, :], g_ffn[:, None, :]
    b_router3 = b_router[:, None, :]
    ck_win = cache_k_win.reshape(dbs, depth, past, KV_WIDTH_A)
    cv_win = cache_v_win.reshape(dbs, depth, past, KV_WIDTH_A)
    ck_nbr = cache_k_nbr.reshape(dbs, depth, past, WIDTH_B)
    cv_nbr = cache_v_nbr.reshape(dbs, depth, past, WIDTH_B)

    tm_ctx = 2 * n_ctx
    tm_lat = 512
    ctx_group = lambda i: 0
    lat_group_in = lambda i: 1 + i
    lat_group_out = lambda i: 1 + (i * tm_lat) // n_lat
    tm_comb = 256
    lat_group_comb = lambda i: 1 + (i * tm_comb) // n_lat
    assert t_ctx % tm_comb == 0 and n_lat % tm_comb == 0

    xp = x_prompt.reshape(t_ctx, d)
    xs = x_sample.reshape(t_lat, d)
    caches = [[], [], [], []]
    for l in range(depth):
        qa, qb, ka, va, kb, vb, oc = _inproj(xp, mod, ctx_group, g_mix3, w_in_bf, l, norm_w, ones_bd, conv_w,
                                             tm_ctx, n_ctx)
        oa, ob = _ctx_attention(sink_win[l], qa, ka, va, qb, kb, vb, n_ctx)
        xp_mid, hp, lgp = _outproj(xp, oa, ob, oc, w_out_bf, mod, ctx_group, g_ffn3, w_router, b_router3, l, tm_ctx)
        for lst, a in zip(caches, (ka, va, kb, vb)):
            lst.append(a)

        qa, qb, ka, va, kb, vb, oc = _inproj(xs, mod, lat_group_in, g_mix3, w_in_bf, l, norm_w, ones_bd, conv_w,
                                             n_lat, n_lat, rope_tabs)
        oa = _win_attention(sink_win[l], qa, ka, va, ck_win, cv_win, l, n_lat)
        ob = _nbr_attention(qb, kb, vb, ck_nbr, cv_nbr, _nbr_bias_table(rel_bias_nbr[l], n_lat // GRID_W), l, n_lat)
        xs_mid, hs, lgs = _outproj(xs, oa, ob, oc, w_out_bf, mod, lat_group_out, g_ffn3, w_router, b_router3, l,
                                   tm_lat)

        gates, blk_e, n_used, idx = _route(jnp.concatenate([lgp, lgs], axis=0))
        y_slots = _experts(blk_e, n_used, idx, jnp.concatenate([hp, hs], axis=0),
                           w_gate, b_gate, w_up, b_up, w_down, b_down, l)
        xp = _combine(xp_mid, y_slots, gates, mod, ctx_group, l, 0, tm_comb)
        xs = _combine(xs_mid, y_slots, gates, mod, lat_group_comb, l, t_ctx, tm_comb)

    new_v_win = jnp.stack([a.reshape(bsz, n_ctx, KV_WIDTH_A) for a in caches[1]], axis=1).reshape(
        bsz, depth, n_ctx, N_KV_A, HEAD_DIM)
    new_k_nbr = jnp.stack([a.reshape(bsz, n_ctx, WIDTH_B) for a in caches[2]], axis=1).reshape(
        bsz, depth, n_ctx, N_HEADS_B, HEAD_DIM)
    new_v_nbr = jnp.stack([a.reshape(bsz, n_ctx, WIDTH_B) for a in caches[3]], axis=1).reshape(
        bsz, depth, n_ctx, N_HEADS_B, HEAD_DIM)
    new_k_win = jnp.stack([a.reshape(bsz, n_ctx, KV_WIDTH_A) for a in caches[0]], axis=1).reshape(
        bsz, depth, n_ctx, N_KV_A, HEAD_DIM)
    return (xp.reshape(bsz, n_ctx, d), xs.reshape(dbs, n_lat, d), new_k_win, new_v_win, new_k_nbr, new_v_nbr)
```

```python
import functools

import numpy as np
import jax
import jax.numpy as jnp
from jax import lax
from jax.experimental import pallas as pl
from jax.experimental.pallas import tpu as pltpu

F32 = jnp.float32
BF16 = jnp.bfloat16

D_MODEL = 1024
HEAD_DIM = 64
GRID_W = 64
N_HEADS_A = 8
N_KV_A = 2
GROUP_A = N_HEADS_A // N_KV_A
WINDOW = 128
Q_BLK = 128
N_HEADS_B = 4
NA_ROWS = 8
NA_COLS = 16
CONV_CH = 256
CONV_W = 3
WIDTH_A = N_HEADS_A * HEAD_DIM
KV_WIDTH_A = N_KV_A * HEAD_DIM
WIDTH_B = N_HEADS_B * HEAD_DIM
IN_COLS = WIDTH_A + 2 * KV_WIDTH_A + 3 * WIDTH_B + 3 * CONV_CH
N_EXPERTS = 32
TOP_K = 4
SWIGLU_LIMIT = 7.0
SWIGLU_ALPHA = 1.702
ROPE_BASE = 10000.0
EPS = 1e-6
NEG = -1e30
N_MOD = 6

C_QA = 0
C_KA = C_QA + WIDTH_A
C_VA = C_KA + KV_WIDTH_A
C_QB = C_VA + KV_WIDTH_A
C_KB = C_QB + WIDTH_B
C_VB = C_KB + WIDTH_B
C_U = C_VB + WIDTH_B
C_GB = C_U + CONV_CH
C_GC = C_GB + CONV_CH

MXU_COLS_V7X = 256
COND_ROWS = 8
MOE_BLK = 256
MOE_SHIFT = MOE_BLK.bit_length() - 1
assert 1 << MOE_SHIFT == MOE_BLK
VMEM_LIMIT_V7X = 56 * 1024 * 1024

_NT = (((1,), (1,)), ((), ()))


def _params(sem, vmem=None):
    return pltpu.CompilerParams(dimension_semantics=sem, vmem_limit_bytes=vmem)


def _mod_kernel(c_ref, w_ref, b_ref, o_ref):
    c = c_ref[...]
    s = c * jax.nn.sigmoid(c)
    o_ref[...] = jnp.dot(s.astype(BF16), w_ref[...].astype(BF16), preferred_element_type=F32) + b_ref[...]


def _modulation(cond, w_mod, b_mod):
    depth = w_mod.shape[0]
    out = pl.pallas_call(
        _mod_kernel,
        out_shape=jax.ShapeDtypeStruct((depth, COND_ROWS, N_MOD * D_MODEL), F32),
        grid=(depth, N_MOD),
        in_specs=[
            pl.BlockSpec((COND_ROWS, D_MODEL), lambda l, j: (0, 0)),
            pl.BlockSpec((None, D_MODEL, D_MODEL), lambda l, j: (l, 0, j)),
            pl.BlockSpec((None, 1, D_MODEL), lambda l, j: (l, 0, j)),
        ],
        out_specs=pl.BlockSpec((None, COND_ROWS, D_MODEL), lambda l, j: (l, 0, j)),
        compiler_params=_params(("arbitrary", "arbitrary")),
        name="modulation",
    )(cond, w_mod, b_mod.reshape(depth, 1, N_MOD * D_MODEL))
    return out.reshape(depth, COND_ROWS, N_MOD, D_MODEL)


def _head_norm(x, w_row, ones_ref):
    width = x.shape[1]
    sq = (x * x).astype(BF16)
    parts = []
    for c0 in range(0, width, MXU_COLS_V7X):
        wd = min(MXU_COLS_V7X, width - c0)
        parts.append(jnp.dot(sq[:, c0:c0 + wd], ones_ref[:wd, :wd], preferred_element_type=F32))
    ss = parts[0] if len(parts) == 1 else jnp.concatenate(parts, axis=-1)
    return x * lax.rsqrt(ss * (1.0 / HEAD_DIM) + EPS) * w_row


def _rope(x, cos, sin):
    width = x.shape[1]
    lane = lax.broadcasted_iota(jnp.int32, x.shape, 1)
    quarter = HEAD_DIM // 4
    partner = jnp.where((lane % (2 * quarter)) < quarter,
                        pltpu.roll(x, width - quarter, axis=1), pltpu.roll(x, quarter, axis=1))
    reps = width // cos.shape[1]
    cos_w = cos if reps == 1 else jnp.concatenate([cos] * reps, axis=-1)
    sin_w = sin if reps == 1 else jnp.concatenate([sin] * reps, axis=-1)
    return x * cos_w + partner * sin_w


def _inproj_kernel(*refs, seq_len, rope):
    if rope:
        (x_ref, mod_ref, g_ref, w_ref, nw_ref, ones_ref, cw_ref, cos_ref, sin_ref,
         qa_o, qb_o, ka_o, va_o, kb_o, vb_o, oc_o) = refs
    else:
        (x_ref, mod_ref, g_ref, w_ref, nw_ref, ones_ref, cw_ref,
         qa_o, qb_o, ka_o, va_o, kb_o, vb_o, oc_o) = refs
    x = x_ref[...]
    tm = x.shape[0]
    h = x * lax.rsqrt(jnp.mean(x * x, axis=-1, keepdims=True) + EPS) * g_ref[...]
    h = h * (1.0 + mod_ref[1:2, :]) + mod_ref[0:1, :]
    p = jnp.dot(h.astype(BF16), w_ref[...], preferred_element_type=F32)

    qa = _head_norm(p[:, C_QA:C_KA], nw_ref[:, 0:WIDTH_A], ones_ref)
    ka = _head_norm(p[:, C_KA:C_VA], nw_ref[:, WIDTH_A:WIDTH_A + KV_WIDTH_A], ones_ref)
    o_qb = WIDTH_A + KV_WIDTH_A
    qb = _head_norm(p[:, C_QB:C_KB], nw_ref[:, o_qb:o_qb + WIDTH_B], ones_ref)
    kb = _head_norm(p[:, C_KB:C_VB], nw_ref[:, o_qb + WIDTH_B:o_qb + 2 * WIDTH_B], ones_ref)
    if rope:
        cos, sin = cos_ref[...], sin_ref[...]
        qa = _rope(qa, cos, sin)
        ka = _rope(ka, cos, sin)
    qa_o[...] = qa.astype(BF16)
    qb_o[...] = qb.astype(BF16)
    ka_o[...] = ka
    va_o[...] = p[:, C_VA:C_QB]
    kb_o[...] = kb
    vb_o[...] = p[:, C_VB:C_U]

    z = p[:, C_GC:C_GC + CONV_CH] * p[:, C_U:C_GB]
    row = lax.broadcasted_iota(jnp.int32, z.shape, 0) % seq_len
    z_prev = jnp.where(row == 0, 0.0, pltpu.roll(z, 1, axis=0))
    z_next = jnp.where(row == seq_len - 1, 0.0, pltpu.roll(z, tm - 1, axis=0))
    y = z_prev * cw_ref[0:1, :] + z * cw_ref[1:2, :] + z_next * cw_ref[2:3, :]
    oc_o[...] = (p[:, C_GB:C_GC] * y).astype(BF16)


def _inproj(x, mod, group_of_tile, g_mix, w_in_bf, layer, norm_w, ones_bd, conv_w, tm, seq_len, rope_tabs=None):
    t = x.shape[0]
    rope = rope_tabs is not None
    in_specs = [
        pl.BlockSpec((tm, D_MODEL), lambda i: (i, 0)),
        pl.BlockSpec((None, None, N_MOD, D_MODEL), lambda i: (layer, group_of_tile(i), 0, 0)),
        pl.BlockSpec((None, 1, D_MODEL), lambda i: (layer, 0, 0)),
        pl.BlockSpec((None, D_MODEL, IN_COLS), lambda i: (layer, 0, 0)),
        pl.BlockSpec((None, 1, norm_w.shape[-1]), lambda i: (layer, 0, 0)),
        pl.BlockSpec(ones_bd.shape, lambda i: (0, 0)),
        pl.BlockSpec((None, CONV_W, CONV_CH), lambda i: (layer, 0, 0)),
    ]
    args = [x, mod, g_mix, w_in_bf, norm_w, ones_bd, conv_w]
    if rope:
        in_specs += [pl.BlockSpec(rope_tabs[0].shape, lambda i: (0, 0))] * 2
        args += list(rope_tabs)
    widths = (WIDTH_A, WIDTH_B, KV_WIDTH_A, KV_WIDTH_A, WIDTH_B, WIDTH_B, CONV_CH)
    dtypes = (BF16, BF16, F32, F32, F32, F32, BF16)
    return pl.pallas_call(
        functools.partial(_inproj_kernel, seq_len=seq_len, rope=rope),
        out_shape=[jax.ShapeDtypeStruct((t, w), dt) for w, dt in zip(widths, dtypes)],
        grid=(t // tm,),
        in_specs=in_specs,
        out_specs=[pl.BlockSpec((tm, w), lambda i: (i, 0)) for w in widths],
        compiler_params=_params(("arbitrary",), VMEM_LIMIT_V7X),
        name="inproj_rope" if rope else "inproj",
    )(*args)


def _softmax_pv(s, v, sink=None):
    m = jnp.max(s, axis=-1, keepdims=True)
    if sink is not None:
        m = jnp.maximum(m, sink)
    p = jnp.exp(s - m)
    den = jnp.sum(p, axis=-1, keepdims=True)
    if sink is not None:
        den = den + jnp.exp(sink - m)
    return jnp.dot(p.astype(BF16), v, preferred_element_type=F32) / den


def _store_heads(o_ref, outs):
    for i in range(0, len(outs), 2):
        o_ref[:, i * HEAD_DIM:(i + 2) * HEAD_DIM] = jnp.concatenate(outs[i:i + 2], axis=-1).astype(o_ref.dtype)


def _ctx_attn_kernel(sink_ref, qa_ref, ka_ref, va_ref, qb_ref, kb_ref, vb_ref, oa_o, ob_o):
    scale = HEAD_DIM ** -0.5
    ka = ka_ref[...].astype(BF16)
    va = va_ref[...].astype(BF16)
    outs = []
    for h in range(N_HEADS_A):
        j = h // GROUP_A
        q = qa_ref[:, h * HEAD_DIM:(h + 1) * HEAD_DIM] * scale
        s = lax.dot_general(q, ka[:, j * HEAD_DIM:(j + 1) * HEAD_DIM], _NT, preferred_element_type=F32)
        outs.append(_softmax_pv(s, va[:, j * HEAD_DIM:(j + 1) * HEAD_DIM], sink_ref[h]))
    _store_heads(oa_o, outs)
    kb = kb_ref[...].astype(BF16)
    vb = vb_ref[...].astype(BF16)
    outs = []
    for h in range(N_HEADS_B):
        sl = slice(h * HEAD_DIM, (h + 1) * HEAD_DIM)
        q = qb_ref[:, sl] * scale
        s = lax.dot_general(q, kb[:, sl], _NT, preferred_element_type=F32)
        outs.append(_softmax_pv(s, vb[:, sl]))
    _store_heads(ob_o, outs)


def _ctx_attention(sink, qa, ka, va, qb, kb, vb, seq_len):
    t = qa.shape[0]
    widths = (WIDTH_A, KV_WIDTH_A, KV_WIDTH_A, WIDTH_B, WIDTH_B, WIDTH_B)
    return pl.pallas_call(
        _ctx_attn_kernel,
        out_shape=[jax.ShapeDtypeStruct((t, WIDTH_A), BF16), jax.ShapeDtypeStruct((t, WIDTH_B), BF16)],
        grid=(t // seq_len,),
        in_specs=[pl.BlockSpec(memory_space=pltpu.SMEM)]
        + [pl.BlockSpec((seq_len, w), lambda i: (i, 0)) for w in widths],
        out_specs=[pl.BlockSpec((seq_len, WIDTH_A), lambda i: (i, 0)),
                   pl.BlockSpec((seq_len, WIDTH_B), lambda i: (i, 0))],
        compiler_params=_params(("arbitrary",)),
        name="ctx_attention",
    )(sink, qa, ka, va, qb, kb, vb)


def _win_attn_kernel(sink_ref, qa_ref, ka_ref, va_ref, ck_ref, cv_ref, oa_o, *, n_lat):
    scale = HEAD_DIM ** -0.5
    span = Q_BLK + 2 * WINDOW
    n = pl.program_id(1)
    start = pl.multiple_of(jnp.clip(n * Q_BLK - WINDOW, 0, n_lat - span), Q_BLK)
    kcat = jnp.concatenate([ka_ref[pl.ds(start, span), :], ck_ref[...]], axis=0).astype(BF16)
    vcat = jnp.concatenate([va_ref[pl.ds(start, span), :], cv_ref[...]], axis=0).astype(BF16)
    n_keys = kcat.shape[0]
    qpos = n * Q_BLK + lax.broadcasted_iota(jnp.int32, (Q_BLK, n_keys), 0)
    col = lax.broadcasted_iota(jnp.int32, (Q_BLK, n_keys), 1)
    ok = (col >= span) | (jnp.abs(qpos - (start + col)) <= WINDOW)
    outs = []
    for h in range(N_HEADS_A):
        j = h // GROUP_A
        q = qa_ref[:, h * HEAD_DIM:(h + 1) * HEAD_DIM] * scale
        s = lax.dot_general(q, kcat[:, j * HEAD_DIM:(j + 1) * HEAD_DIM], _NT, preferred_element_type=F32)
        s = jnp.where(ok, s, NEG)
        outs.append(_softmax_pv(s, vcat[:, j * HEAD_DIM:(j + 1) * HEAD_DIM], sink_ref[h]))
    _store_heads(oa_o, outs)


def _win_attention(sink, qa, ka, va, cache_k, cache_v, layer, n_lat):
    t = qa.shape[0]
    nb = n_lat // Q_BLK
    past = cache_k.shape[2]
    return pl.pallas_call(
        functools.partial(_win_attn_kernel, n_lat=n_lat),
        out_shape=jax.ShapeDtypeStruct((t, WIDTH_A), BF16),
        grid=(t // n_lat, nb),
        in_specs=[
            pl.BlockSpec(memory_space=pltpu.SMEM),
            pl.BlockSpec((Q_BLK, WIDTH_A), lambda b, n: (b * nb + n, 0)),
            pl.BlockSpec((n_lat, KV_WIDTH_A), lambda b, n: (b, 0)),
            pl.BlockSpec((n_lat, KV_WIDTH_A), lambda b, n: (b, 0)),
            pl.BlockSpec((None, None, past, KV_WIDTH_A), lambda b, n: (b, layer, 0, 0)),
            pl.BlockSpec((None, None, past, KV_WIDTH_A), lambda b, n: (b, layer, 0, 0)),
        ],
        out_specs=pl.BlockSpec((Q_BLK, WIDTH_A), lambda b, n: (b * nb + n, 0)),
        compiler_params=_params(("arbitrary", "arbitrary")),
        name="window_attention",
    )(sink, qa, ka, va, cache_k, cache_v)


def _nbr_attn_kernel(qb_ref, kb_ref, vb_ref, ck_ref, cv_ref, bias_ref, ob_o, *, rows, kr):
    scale = HEAD_DIM ** -0.5
    r = pl.program_id(1)
    start = pl.multiple_of(jnp.clip(r - kr // 2, 0, rows - kr) * GRID_W, GRID_W)
    nwin = kr * GRID_W
    kcat = jnp.concatenate([kb_ref[pl.ds(start, nwin), :], ck_ref[...]], axis=0).astype(BF16)
    vcat = jnp.concatenate([vb_ref[pl.ds(start, nwin), :], cv_ref[...]], axis=0).astype(BF16)
    past = ck_ref.shape[0]
    outs = []
    for h in range(N_HEADS_B):
        sl = slice(h * HEAD_DIM, (h + 1) * HEAD_DIM)
        q = qb_ref[:, sl] * scale
        s = lax.dot_general(q, kcat[:, sl], _NT, preferred_element_type=F32)
        bias = jnp.concatenate([bias_ref[h], jnp.zeros((GRID_W, past), F32)], axis=-1)
        outs.append(_softmax_pv(s + bias, vcat[:, sl]))
    _store_heads(ob_o, outs)


def _nbr_bias_table(rel_bias, rows):
    kr = min(NA_ROWS, rows)
    cls = np.arange(kr)[:, None]
    m = np.arange(kr)[None, :]
    row_sel = (m - cls + NA_ROWS - 1)[:, :, None] == np.arange(2 * NA_ROWS - 1)[None, None, :]
    c = np.arange(GRID_W)[:, None]
    kc = np.arange(GRID_W)[None, :]
    dc = np.clip(kc - c, -(NA_COLS - 1), NA_COLS - 1) + NA_COLS - 1
    col_sel = dc[:, :, None] == np.arange(2 * NA_COLS - 1)[None, None, :]
    col_start = np.clip(c - NA_COLS // 2, 0, GRID_W - NA_COLS)
    ok = (kc >= col_start) & (kc < col_start + NA_COLS)
    tab = jnp.einsum('hrd,ymr,ckd->hycmk', rel_bias.astype(F32), row_sel.astype(np.float32),
                     col_sel.astype(np.float32), precision=lax.Precision.HIGHEST)
    tab = jnp.where(ok[None, None, :, None, :], tab, NEG)
    return tab.reshape(rel_bias.shape[0], kr, GRID_W, kr * GRID_W)


def _nbr_attention(qb, kb, vb, cache_k, cache_v, bias_tab, layer, n_lat):
    t = qb.shape[0]
    rows = n_lat // GRID_W
    kr = min(NA_ROWS, rows)
    past = cache_k.shape[2]

    def bias_map(b, r):
        return (0, r - jnp.clip(r - kr // 2, 0, rows - kr), 0, 0)

    return pl.pallas_call(
        functools.partial(_nbr_attn_kernel, rows=rows, kr=kr),
        out_shape=jax.ShapeDtypeStruct((t, WIDTH_B), BF16),
        grid=(t // n_lat, rows),
        in_specs=[
            pl.BlockSpec((GRID_W, WIDTH_B), lambda b, r: (b * rows + r, 0)),
            pl.BlockSpec((n_lat, WIDTH_B), lambda b, r: (b, 0)),
            pl.BlockSpec((n_lat, WIDTH_B), lambda b, r: (b, 0)),
            pl.BlockSpec((None, None, past, WIDTH_B), lambda b, r: (b, layer, 0, 0)),
            pl.BlockSpec((None, None, past, WIDTH_B), lambda b, r: (b, layer, 0, 0)),
            pl.BlockSpec((N_HEADS_B, None, GRID_W, kr * GRID_W), bias_map),
        ],
        out_specs=pl.BlockSpec((GRID_W, WIDTH_B), lambda b, r: (b * rows + r, 0)),
        compiler_params=_params(("arbitrary", "arbitrary")),
        name="neighbourhood_attention",
    )(qb, kb, vb, cache_k, cache_v, bias_tab)


ROW_TILE = 8
LANES = 128
assert ROW_TILE * LANES == D_MODEL


def _store_row_tiles(ref, row0, x):
    n = x.shape[0]
    for c in range(ROW_TILE):
        ref[pl.ds(row0 * ROW_TILE + c, n, stride=ROW_TILE), :] = x[:, c * LANES:(c + 1) * LANES]


def _load_row_tiles(ref, row0, n, row_stride=1, chunk0=0):
    return jnp.concatenate(
        [ref[pl.ds(row0 * ROW_TILE + chunk0 + c, n, stride=ROW_TILE * row_stride), :] for c in range(ROW_TILE)],
        axis=-1)


def _outproj_kernel(x_ref, oa_ref, ob_ref, oc_ref, w_ref, mod_ref, g_ref, wr_ref, br_ref, xn_o, h_o, lg_o):
    mix = jnp.dot(oa_ref[...], w_ref[0:WIDTH_A, :], preferred_element_type=F32)
    mix += jnp.dot(ob_ref[...], w_ref[WIDTH_A:WIDTH_A + WIDTH_B, :], preferred_element_type=F32)
    mix += jnp.dot(oc_ref[...], w_ref[WIDTH_A + WIDTH_B:, :], preferred_element_type=F32)
    xn = x_ref[...] + mod_ref[2:3, :] * mix
    xn_o[...] = xn
    h = xn * lax.rsqrt(jnp.mean(xn * xn, axis=-1, keepdims=True) + EPS) * g_ref[...]
    h = h * (1.0 + mod_ref[4:5, :]) + mod_ref[3:4, :]
    _store_row_tiles(h_o, 0, h)
    h_hi = h.astype(BF16)
    h_lo = (h - h_hi.astype(F32)).astype(BF16)
    wr = wr_ref[...]
    w_hi = wr.astype(BF16)
    w_lo = (wr - w_hi.astype(F32)).astype(BF16)
    lg = jnp.dot(h_hi, w_hi, preferred_element_type=F32)
    lg += jnp.dot(h_lo, w_hi, preferred_element_type=F32)
    lg += jnp.dot(h_hi, w_lo, preferred_element_type=F32)
    lg_o[...] = lg + br_ref[...]


def _outproj(x, oa, ob, oc, w_out_bf, mod, group_of_tile, g_ffn, w_router, b_router, layer, tm):
    t = x.shape[0]
    return pl.pallas_call(
        _outproj_kernel,
        out_shape=[jax.ShapeDtypeStruct((t, D_MODEL), F32), jax.ShapeDtypeStruct((t * ROW_TILE, LANES), F32),
                   jax.ShapeDtypeStruct((t, N_EXPERTS), F32)],
        grid=(t // tm,),
        in_specs=[
            pl.BlockSpec((tm, D_MODEL), lambda i: (i, 0)),
            pl.BlockSpec((tm, WIDTH_A), lambda i: (i, 0)),
            pl.BlockSpec((tm, WIDTH_B), lambda i: (i, 0)),
            pl.BlockSpec((tm, CONV_CH), lambda i: (i, 0)),
            pl.BlockSpec((None, D_MODEL, D_MODEL), lambda i: (layer, 0, 0)),
            pl.BlockSpec((None, None, N_MOD, D_MODEL), lambda i: (layer, group_of_tile(i), 0, 0)),
            pl.BlockSpec((None, 1, D_MODEL), lambda i: (layer, 0, 0)),
            pl.BlockSpec((None, D_MODEL, N_EXPERTS), lambda i: (layer, 0, 0)),
            pl.BlockSpec((None, 1, N_EXPERTS), lambda i: (layer, 0, 0)),
        ],
        out_specs=[pl.BlockSpec((tm, D_MODEL), lambda i: (i, 0)),
                   pl.BlockSpec((tm * ROW_TILE, LANES), lambda i: (i, 0)),
                   pl.BlockSpec((tm, N_EXPERTS), lambda i: (i, 0))],
        compiler_params=_params(("arbitrary",), VMEM_LIMIT_V7X),
        name="outproj_router",
    )(x, oa, ob, oc, w_out_bf, mod, g_ffn, w_router, b_router)


IDX_CHUNK = 1024
IDX_SLOTS = 4
assert 2 * MOE_BLK <= IDX_CHUNK


def _expert_kernel(blk_e_ref, n_used_ref, idx_hbm, h_hbm, wg_hbm, bg_ref, wu_hbm, bu_ref, wd_hbm, bd_ref, y_hbm,
                   idx_s, st_s, x_buf, y_buf, w_buf, isem, gsem, ssem, wsem, *, n, layer):
    i = pl.program_id(0)
    slot = i % 2
    blk_rows = MOE_BLK * ROW_TILE
    n_used = n_used_ref[0]

    def idx_copy(blk):
        return pltpu.make_async_copy(idx_hbm.at[pl.ds(pl.multiple_of(blk * IDX_CHUNK, IDX_CHUNK), IDX_CHUNK)],
                                     idx_s.at[blk % IDX_SLOTS], isem.at[blk % IDX_SLOTS])

    def gather_row(blk, r):
        tok = idx_s[blk % IDX_SLOTS, r]
        pltpu.make_async_copy(h_hbm.at[pl.ds(pl.multiple_of(tok * ROW_TILE, ROW_TILE), ROW_TILE)],
                              x_buf.at[blk % 2, pl.ds(pl.multiple_of(r * ROW_TILE, ROW_TILE), ROW_TILE)],
                              gsem.at[blk % 2]).start()

    def scatter_row(blk, r):
        dst = idx_s[blk % IDX_SLOTS, MOE_BLK + r]
        pltpu.make_async_copy(y_buf.at[blk % 2, pl.ds(pl.multiple_of(r * ROW_TILE, ROW_TILE), ROW_TILE)],
                              y_hbm.at[pl.ds(pl.multiple_of(dst * ROW_TILE, ROW_TILE), ROW_TILE)],
                              ssem.at[blk % 2]).start()

    def issue_rows(row_fn, blk, unrolled):
        if unrolled:
            for r in range(MOE_BLK):
                row_fn(blk, r)
        else:
            def body(r, carry):
                row_fn(blk, r)
                return carry
            lax.fori_loop(0, MOE_BLK, body, 0, unroll=8)

    def wait_gather(s2):
        pltpu.make_async_copy(h_hbm.at[pl.ds(0, blk_rows)], x_buf.at[s2], gsem.at[s2]).wait()

    def wait_scatter(s2):
        pltpu.make_async_copy(y_buf.at[s2], y_hbm.at[pl.ds(0, blk_rows)], ssem.at[s2]).wait()

    def weight_copies(e, ws):
        return [pltpu.make_async_copy(w_hbm.at[layer, e], w_buf.at[ws, m], wsem.at[ws])
                for m, w_hbm in enumerate((wg_hbm, wu_hbm, wd_hbm))]

    def compute():
        ws = st_s[1]
        x = _load_row_tiles(x_buf.at[slot], 0, MOE_BLK).astype(BF16)
        g = jnp.dot(x, w_buf[ws, 0].astype(BF16), preferred_element_type=F32) + bg_ref[...]
        u = jnp.dot(x, w_buf[ws, 1].astype(BF16), preferred_element_type=F32) + bu_ref[...]
        g = jnp.minimum(g, SWIGLU_LIMIT)
        u = jnp.clip(u, -SWIGLU_LIMIT, SWIGLU_LIMIT)
        a = g * jax.nn.sigmoid(SWIGLU_ALPHA * g) * (u + 1.0)
        y = jnp.dot(a.astype(BF16), w_buf[ws, 2].astype(BF16), preferred_element_type=F32) + bd_ref[...]
        _store_row_tiles(y_buf.at[slot], 0, y)

    @pl.when(i == 0)
    def _():
        st_s[0] = 0
        idx_copy(0).start()
        idx_copy(0).wait()
        issue_rows(gather_row, 0, False)
        idx_copy(1).start()

    @pl.when(i + 1 < n)
    def _():
        idx_copy(i + 1).wait()

    @pl.when(i + 2 < n)
    def _():
        idx_copy(i + 2).start()

    e_cur = blk_e_ref[i]
    first_of_run = (i == 0) | (blk_e_ref[jnp.maximum(i - 1, 0)] != e_cur)

    @pl.when(first_of_run & (i < n_used))
    def _():
        run = st_s[0]
        ws = run % 2

        @pl.when(i == 0)
        def _():
            for cp in weight_copies(e_cur, ws):
                cp.start()

        for cp in weight_copies(e_cur, ws):
            cp.wait()
        nxt = lax.while_loop(lambda j: (j < n_used) & (blk_e_ref[jnp.minimum(j, n - 1)] == e_cur),
                             lambda j: j + 1, i + 1)

        @pl.when(nxt < n_used)
        def _():
            for cp in weight_copies(blk_e_ref[jnp.minimum(nxt, n - 1)], 1 - ws):
                cp.start()

        st_s[0] = run + 1
        st_s[1] = ws

    @pl.when(i >= 2)
    def _():
        wait_scatter(slot)

    wait_gather(slot)

    steady = (i >= 1) & (i + 1 < n) & (i < n_used)

    @pl.when(steady)
    def _():
        issue_rows(gather_row, i + 1, True)
        issue_rows(scatter_row, i - 1, True)
        compute()

    @pl.when(jnp.logical_not(steady))
    def _():
        @pl.when(i + 1 < n)
        def _():
            issue_rows(gather_row, i + 1, False)

        @pl.when(i >= 1)
        def _():
            issue_rows(scatter_row, i - 1, False)

        @pl.when(i < n_used)
        def _():
            compute()

        @pl.when(i >= n_used)
        def _():
            y_buf[slot] = jnp.zeros((blk_rows, LANES), F32)

        @pl.when(i == n - 1)
        def _():
            issue_rows(scatter_row, i, False)
            wait_scatter(1 - slot)
            wait_scatter(slot)


def _experts(blk_e, n_used, idx, h_tiles, w_gate, b_gate, w_up, b_up, w_down, b_down, layer):
    n_blocks = idx.shape[0] // IDX_CHUNK
    assert n_blocks >= 2
    n_rows = n_blocks * MOE_BLK
    bspec = pl.BlockSpec((None, None, 1, D_MODEL), lambda i, be, nu: (layer, be[i], 0, 0))
    anyspec = pl.BlockSpec(memory_space=pl.ANY)
    depth = w_gate.shape[0]
    b4 = lambda b: b.reshape(depth, N_EXPERTS, 1, D_MODEL)
    return pl.pallas_call(
        functools.partial(_expert_kernel, n=n_blocks, layer=layer),
        out_shape=jax.ShapeDtypeStruct((n_rows * ROW_TILE, LANES), F32),
        grid_spec=pltpu.PrefetchScalarGridSpec(
            num_scalar_prefetch=2,
            grid=(n_blocks,),
            in_specs=[anyspec, anyspec, anyspec, bspec, anyspec, bspec, anyspec, bspec],
            out_specs=anyspec,
            scratch_shapes=[
                pltpu.SMEM((IDX_SLOTS, IDX_CHUNK), jnp.int32),
                pltpu.SMEM((2,), jnp.int32),
                pltpu.VMEM((2, MOE_BLK * ROW_TILE, LANES), F32),
                pltpu.VMEM((2, MOE_BLK * ROW_TILE, LANES), F32),
                pltpu.VMEM((2, 3, D_MODEL, D_MODEL), F32),
                pltpu.SemaphoreType.DMA((IDX_SLOTS,)),
                pltpu.SemaphoreType.DMA((2,)),
                pltpu.SemaphoreType.DMA((2,)),
                pltpu.SemaphoreType.DMA((2,)),
            ],
        ),
        compiler_params=_params(("arbitrary",), VMEM_LIMIT_V7X),
        name="experts",
    )(blk_e, n_used, idx, h_tiles, w_gate, b4(b_gate), w_up, b4(b_up), w_down, b4(b_down))


CUM_CHUNK = 256
BLK_LANES = 256


def _sublane_cumsum(x):
    row = lax.broadcasted_iota(jnp.int32, x.shape, 0)
    d = 1
    while d < x.shape[0]:
        x = x + jnp.where(row >= d, pltpu.roll(x, d, axis=0), 0)
        d *= 2
    return x


def _route_kernel(lg_ref, tri_ref, gate_o, dest_o, blk_o, *, n_blocks):
    lg = lg_ref[...]
    n_e, t = lg.shape
    e_iota = lax.broadcasted_iota(jnp.int32, lg.shape, 0)
    work = lg
    tops, hots = [], []
    for _ in range(TOP_K):
        m = jnp.max(work, axis=0, keepdims=True)
        first = jnp.min(jnp.where(work == m, e_iota, n_e), axis=0, keepdims=True)
        hot = e_iota == first
        work = jnp.where(hot, -jnp.inf, work)
        tops.append(m)
        hots.append(hot)
    ex = [jnp.exp(m - tops[0]) for m in tops]
    den = ex[0] + ex[1] + ex[2] + ex[3]
    for k in range(TOP_K):
        gate_o[k:k + 1, :] = ex[k] / den

    chosen = jnp.where(hots[0] | hots[1] | hots[2] | hots[3], 1.0, 0.0)
    tri = tri_ref[...]
    carry = jnp.zeros((n_e, 1), F32)
    before = []
    for c0 in range(0, t, CUM_CHUNK):
        chunk = chosen[:, c0:c0 + CUM_CHUNK]
        inc = jnp.dot(chunk.astype(BF16), tri, preferred_element_type=F32)
        before.append(inc - chunk + carry)
        carry = carry + inc[:, CUM_CHUNK - 1:CUM_CHUNK]
    before = jnp.concatenate(before, axis=-1).astype(jnp.int32)

    counts = jnp.broadcast_to(carry.astype(jnp.int32), (n_e, BLK_LANES))
    padded = ((counts + (MOE_BLK - 1)) >> MOE_SHIFT) << MOE_SHIFT
    pad_end = _sublane_cumsum(padded)
    pad_start = pad_end - padded
    row = pad_start[:, 0:1] + before
    for k in range(TOP_K):
        dest_o[k:k + 1, :] = jnp.sum(jnp.where(hots[k], row, 0), axis=0, keepdims=True)

    blk_row0 = lax.broadcasted_iota(jnp.int32, (n_e, BLK_LANES), 1) * MOE_BLK
    blk_o[0:1, :] = jnp.minimum(jnp.sum((pad_end <= blk_row0).astype(jnp.int32), axis=0, keepdims=True), n_e - 1)
    blk_o[1:2, :] = jnp.sum(jnp.where(pad_start <= blk_row0, counts, 0), axis=0, keepdims=True)
    blk_o[2:3, :] = pad_end[n_e - 1:n_e, :] >> MOE_SHIFT
    blk_o[3:8, :] = jnp.zeros((5, BLK_LANES), jnp.int32)


def _invert_kernel(dest_ref, src_o, *, n_assign, n_rows):
    def init(r, carry):
        src_o[r] = -1
        return carry

    lax.fori_loop(0, n_rows, init, 0, unroll=8)

    def put(a, carry):
        src_o[dest_ref[a]] = a
        return carry

    lax.fori_loop(0, n_assign, put, 0, unroll=8)


def _route(logits):
    t = logits.shape[0]
    n_assign = t * TOP_K
    n_blocks = n_assign // MOE_BLK + N_EXPERTS
    n_rows = n_blocks * MOE_BLK
    assert t % CUM_CHUNK == 0 and n_blocks <= BLK_LANES
    tri = jnp.asarray(np.triu(np.ones((CUM_CHUNK, CUM_CHUNK), np.float32)), dtype=BF16)
    gates, dest, blk = pl.pallas_call(
        functools.partial(_route_kernel, n_blocks=n_blocks),
        out_shape=[jax.ShapeDtypeStruct((TOP_K, t), F32), jax.ShapeDtypeStruct((TOP_K, t), jnp.int32),
                   jax.ShapeDtypeStruct((8, BLK_LANES), jnp.int32)],
        compiler_params=_params(None, VMEM_LIMIT_V7X),
        name="route",
    )(logits.T, tri)
    row_src = pl.pallas_call(
        functools.partial(_invert_kernel, n_assign=n_assign, n_rows=n_rows),
        out_shape=jax.ShapeDtypeStruct((n_rows,), jnp.int32),
        in_specs=[pl.BlockSpec(memory_space=pltpu.SMEM)],
        out_specs=pl.BlockSpec(memory_space=pltpu.SMEM),
        name="invert_rows",
    )(dest.T.reshape(-1))
    blk_e = blk[0, :n_blocks]
    n_used = blk[2, :1]
    pad_dst = n_assign + jnp.arange(n_rows, dtype=jnp.int32) - jnp.repeat(blk[1, :n_blocks], MOE_BLK)
    is_pad = row_src < 0
    row_tok = jnp.where(is_pad, 0, row_src // TOP_K).reshape(n_blocks, MOE_BLK)
    row_dst = jnp.where(is_pad, pad_dst, row_src).reshape(n_blocks, MOE_BLK)
    idx = jnp.concatenate([row_tok, row_dst, jnp.zeros((n_blocks, IDX_CHUNK - 2 * MOE_BLK), jnp.int32)], axis=1)
    return gates.T, blk_e, n_used, idx.reshape(-1)


def _combine_kernel(x_ref, y_ref, gate_ref, mod_ref, o_ref):
    tm = x_ref.shape[0]
    acc = jnp.zeros(x_ref.shape, F32)
    for k in range(TOP_K):
        acc += gate_ref[:, k:k + 1] * _load_row_tiles(y_ref, k, tm, row_stride=TOP_K)
    o_ref[...] = x_ref[...] + mod_ref[5:6, :] * acc


def _combine(x_mid, y_slots, gates, mod, group_of_tile, layer, tok0, tm):
    t = x_mid.shape[0]
    b0 = tok0 // tm
    return pl.pallas_call(
        _combine_kernel,
        out_shape=jax.ShapeDtypeStruct((t, D_MODEL), F32),
        grid=(t // tm,),
        in_specs=[
            pl.BlockSpec((tm, D_MODEL), lambda i: (i, 0)),
            pl.BlockSpec((tm * TOP_K * ROW_TILE, LANES), lambda i: (b0 + i, 0)),
            pl.BlockSpec((tm, TOP_K), lambda i: (b0 + i, 0)),
            pl.BlockSpec((None, None, N_MOD, D_MODEL), lambda i: (layer, group_of_tile(i), 0, 0)),
        ],
        out_specs=pl.BlockSpec((tm, D_MODEL), lambda i: (i, 0)),
        compiler_params=_params(("arbitrary",), VMEM_LIMIT_V7X),
        name="combine",
    )(x_mid, y_slots, gates, mod)


def _rope_tables(n_lat):
    quarter = HEAD_DIM // 4
    t = jnp.arange(n_lat)
    inv = ROPE_BASE ** (-jnp.arange(quarter, dtype=F32) / quarter)
    ang_r = (t // GRID_W).astype(F32)[:, None] * inv
    ang_c = (t % GRID_W).astype(F32)[:, None] * inv
    cos = jnp.concatenate([jnp.cos(ang_r)] * 2 + [jnp.cos(ang_c)] * 2, axis=-1)
    sin = jnp.concatenate([-jnp.sin(ang_r), jnp.sin(ang_r), -jnp.sin(ang_c), jnp.sin(ang_c)], axis=-1)
    return jnp.concatenate([cos, cos], axis=-1), jnp.concatenate([sin, sin], axis=-1)


def _block_diag_ones():
    idx = np.arange(MXU_COLS_V7X) // HEAD_DIM
    return jnp.asarray(idx[:, None] == idx[None, :], dtype=BF16)


def kernel(x_prompt, x_sample, cache_k_win, cache_v_win, cache_k_nbr, cache_v_nbr, c, c_ctx, w_mod, b_mod, g_mix, g_ffn, w_in, w_out, qn_win, kn_win, qn_nbr, kn_nbr, sink_win, rel_bias_nbr, conv_w, w_router, b_router, w_gate, b_gate, w_up, b_up, w_down, b_down):
    bsz, n_ctx, d = x_prompt.shape
    dbs, n_lat, _ = x_sample.shape
    depth = w_in.shape[0]
    past = cache_k_win.shape[2]
    assert d == D_MODEL and dbs + 1 <= COND_ROWS and n_lat % GRID_W == 0 and n_lat >= Q_BLK + 2 * WINDOW
    t_ctx, t_lat = bsz * n_ctx, dbs * n_lat

    cond = jnp.concatenate([c_ctx[None], c, jnp.zeros((COND_ROWS - 1 - dbs, d), F32)], axis=0)
    mod = _modulation(cond, w_mod, b_mod)

    w_in_bf = w_in.astype(BF16)
    w_out_bf = w_out.astype(BF16)
    norm_w = jnp.concatenate([jnp.tile(qn_win, (1, N_HEADS_A)), jnp.tile(kn_win, (1, N_KV_A)),
                              jnp.tile(qn_nbr, (1, N_HEADS_B)), jnp.tile(kn_nbr, (1, N_HEADS_B))], axis=-1)[:, None, :]
    ones_bd = _block_diag_ones()
    rope_tabs = _rope_tables(n_lat)
    g_mix3, g_ffn3 = g_mix[:, None, :], g_ffn[:, None, :]
    b_router3 = b_router[:, None, :]
    ck_win = cache_k_win.reshape(dbs, depth, past, KV_WIDTH_A)
    cv_win = cache_v_win.reshape(dbs, depth, past, KV_WIDTH_A)
    ck_nbr = cache_k_nbr.reshape(dbs, depth, past, WIDTH_B)
    cv_nbr = cache_v_nbr.reshape(dbs, depth, past, WIDTH_B)

    tm_ctx = 2 * n_ctx
    tm_lat = 512
    ctx_group = lambda i: 0
    lat_group_in = lambda i: 1 + i
    lat_group_out = lambda i: 1 + (i * tm_lat) // n_lat
    tm_comb = 256
    lat_group_comb = lambda i: 1 + (i * tm_comb) // n_lat
    assert t_ctx % tm_comb == 0 and n_lat % tm_comb == 0

    xp = x_prompt.reshape(t_ctx, d)
    xs = x_sample.reshape(t_lat, d)
    caches = [[], [], [], []]
    for l in range(depth):
        qa, qb, ka, va, kb, vb, oc = _inproj(xp, mod, ctx_group, g_mix3, w_in_bf, l, norm_w, ones_bd, conv_w,
                                             tm_ctx, n_ctx)
        oa, ob = _ctx_attention(sink_win[l], qa, ka, va, qb, kb, vb, n_ctx)
        xp_mid, hp, lgp = _outproj(xp, oa, ob, oc, w_out_bf, mod, ctx_group, g_ffn3, w_router, b_router3, l, tm_ctx)
        for lst, a in zip(caches, (ka, va, kb, vb)):
            lst.append(a)

        qa, qb, ka, va, kb, vb, oc = _inproj(xs, mod, lat_group_in, g_mix3, w_in_bf, l, norm_w, ones_bd, conv_w,
                                             n_lat, n_lat, rope_tabs)
        oa = _win_attention(sink_win[l], qa, ka, va, ck_win, cv_win, l, n_lat)
        ob = _nbr_attention(qb, kb, vb, ck_nbr, cv_nbr, _nbr_bias_table(rel_bias_nbr[l], n_lat // GRID_W), l, n_lat)
        xs_mid, hs, lgs = _outproj(xs, oa, ob, oc, w_out_bf, mod, lat_group_out, g_ffn3, w_router, b_router3, l,
                                   tm_lat)

        gates, blk_e, n_used, idx = _route(jnp.concatenate([lgp, lgs], axis=0))
        y_slots = _experts(blk_e, n_used, idx, jnp.concatenate([hp, hs], axis=0),
                           w_gate, b_gate, w_up, b_up, w_down, b_down, l)
        xp = _combine(xp_mid, y_slots, gates, mod, ctx_group, l, 0, tm_comb)
        xs = _combine(xs_mid, y_slots, gates, mod, lat_group_comb, l, t_ctx, tm_comb)

    new_v_win = jnp.stack([a.reshape(bsz, n_ctx, KV_WIDTH_A) for a in caches[1]], axis=1).reshape(
        bsz, depth, n_ctx, N_KV_A, HEAD_DIM)
    new_k_nbr = jnp.stack([a.reshape(bsz, n_ctx, WIDTH_B) for a in caches[2]], axis=1).reshape(
        bsz, depth, n_ctx, N_HEADS_B, HEAD_DIM)
    new_v_nbr = jnp.stack([a.reshape(bsz, n_ctx, WIDTH_B) for a in caches[3]], axis=1).reshape(
        bsz, depth, n_ctx, N_HEADS_B, HEAD_DIM)
    new_k_win = jnp.stack([a.reshape(bsz, n_ctx, KV_WIDTH_A) for a in caches[0]], axis=1).reshape(
        bsz, depth, n_ctx, N_KV_A, HEAD_DIM)
    return (xp.reshape(bsz, n_ctx, d), xs.reshape(dbs, n_lat, d), new_k_win, new_v_win, new_k_nbr, new_v_nbr)
```

```python
import functools

import numpy as np
import jax
import jax.numpy as jnp
from jax import lax
from jax.experimental import pallas as pl
from jax.experimental.pallas import tpu as pltpu

F32 = jnp.float32
BF16 = jnp.bfloat16

D_MODEL = 1024
HEAD_DIM = 64
GRID_W = 64
N_HEADS_A = 8
N_KV_A = 2
GROUP_A = N_HEADS_A // N_KV_A
WINDOW = 128
Q_BLK = 128
N_HEADS_B = 4
NA_ROWS = 8
NA_COLS = 16
CONV_CH = 256
CONV_W = 3
WIDTH_A = N_HEADS_A * HEAD_DIM
KV_WIDTH_A = N_KV_A * HEAD_DIM
WIDTH_B = N_HEADS_B * HEAD_DIM
IN_COLS = WIDTH_A + 2 * KV_WIDTH_A + 3 * WIDTH_B + 3 * CONV_CH
N_EXPERTS = 32
TOP_K = 4
SWIGLU_LIMIT = 7.0
SWIGLU_ALPHA = 1.702
ROPE_BASE = 10000.0
EPS = 1e-6
NEG = -1e30
N_MOD = 6

C_QA = 0
C_KA = C_QA + WIDTH_A
C_VA = C_KA + KV_WIDTH_A
C_QB = C_VA + KV_WIDTH_A
C_KB = C_QB + WIDTH_B
C_VB = C_KB + WIDTH_B
C_U = C_VB + WIDTH_B
C_GB = C_U + CONV_CH
C_GC = C_GB + CONV_CH

MXU_COLS_V7X = 256
COND_ROWS = 8
MOE_BLK = 256
MOE_SHIFT = MOE_BLK.bit_length() - 1
assert 1 << MOE_SHIFT == MOE_BLK
VMEM_LIMIT_V7X = 56 * 1024 * 1024

_NT = (((1,), (1,)), ((), ()))


def _params(sem, vmem=None):
    return pltpu.CompilerParams(dimension_semantics=sem, vmem_limit_bytes=vmem)


def _mod_kernel(c_ref, w_ref, b_ref, o_ref):
    c = c_ref[...]
    s = c * jax.nn.sigmoid(c)
    o_ref[...] = jnp.dot(s.astype(BF16), w_ref[...].astype(BF16), preferred_element_type=F32) + b_ref[...]


def _modulation(cond, w_mod, b_mod):
    depth = w_mod.shape[0]
    out = pl.pallas_call(
        _mod_kernel,
        out_shape=jax.ShapeDtypeStruct((depth, COND_ROWS, N_MOD * D_MODEL), F32),
        grid=(depth, N_MOD),
        in_specs=[
            pl.BlockSpec((COND_ROWS, D_MODEL), lambda l, j: (0, 0)),
            pl.BlockSpec((None, D_MODEL, D_MODEL), lambda l, j: (l, 0, j)),
            pl.BlockSpec((None, 1, D_MODEL), lambda l, j: (l, 0, j)),
        ],
        out_specs=pl.BlockSpec((None, COND_ROWS, D_MODEL), lambda l, j: (l, 0, j)),
        compiler_params=_params(("arbitrary", "arbitrary")),
        name="modulation",
    )(cond, w_mod, b_mod.reshape(depth, 1, N_MOD * D_MODEL))
    return out.reshape(depth, COND_ROWS, N_MOD, D_MODEL)


def _head_norm(x, w_row, ones_ref):
    width = x.shape[1]
    sq = (x * x).astype(BF16)
    parts = []
    for c0 in range(0, width, MXU_COLS_V7X):
        wd = min(MXU_COLS_V7X, width - c0)
        parts.append(jnp.dot(sq[:, c0:c0 + wd], ones_ref[:wd, :wd], preferred_element_type=F32))
    ss = parts[0] if len(parts) == 1 else jnp.concatenate(parts, axis=-1)
    return x * lax.rsqrt(ss * (1.0 / HEAD_DIM) + EPS) * w_row


def _rope(x, cos, sin):
    width = x.shape[1]
    lane = lax.broadcasted_iota(jnp.int32, x.shape, 1)
    quarter = HEAD_DIM // 4
    partner = jnp.where((lane % (2 * quarter)) < quarter,
                        pltpu.roll(x, width - quarter, axis=1), pltpu.roll(x, quarter, axis=1))
    reps = width // cos.shape[1]
    cos_w = cos if reps == 1 else jnp.concatenate([cos] * reps, axis=-1)
    sin_w = sin if reps == 1 else jnp.concatenate([sin] * reps, axis=-1)
    return x * cos_w + partner * sin_w


def _inproj_kernel(*refs, seq_len, rope):
    if rope:
        (x_ref, mod_ref, g_ref, w_ref, nw_ref, ones_ref, cw_ref, cos_ref, sin_ref,
         qa_o, qb_o, ka_o, va_o, kb_o, vb_o, oc_o) = refs
    else:
        (x_ref, mod_ref, g_ref, w_ref, nw_ref, ones_ref, cw_ref,
         qa_o, qb_o, ka_o, va_o, kb_o, vb_o, oc_o) = refs
    x = x_ref[...]
    tm = x.shape[0]
    h = x * lax.rsqrt(jnp.mean(x * x, axis=-1, keepdims=True) + EPS) * g_ref[...]
    h = h * (1.0 + mod_ref[1:2, :]) + mod_ref[0:1, :]
    p = jnp.dot(h.astype(BF16), w_ref[...], preferred_element_type=F32)

    qa = _head_norm(p[:, C_QA:C_KA], nw_ref[:, 0:WIDTH_A], ones_ref)
    ka = _head_norm(p[:, C_KA:C_VA], nw_ref[:, WIDTH_A:WIDTH_A + KV_WIDTH_A], ones_ref)
    o_qb = WIDTH_A + KV_WIDTH_A
    qb = _head_norm(p[:, C_QB:C_KB], nw_ref[:, o_qb:o_qb + WIDTH_B], ones_ref)
    kb = _head_norm(p[:, C_KB:C_VB], nw_ref[:, o_qb + WIDTH_B:o_qb + 2 * WIDTH_B], ones_ref)
    if rope:
        cos, sin = cos_ref[...], sin_ref[...]
        qa = _rope(qa, cos, sin)
        ka = _rope(ka, cos, sin)
    qa_o[...] = qa.astype(BF16)
    qb_o[...] = qb.astype(BF16)
    ka_o[...] = ka
    va_o[...] = p[:, C_VA:C_QB]
    kb_o[...] = kb
    vb_o[...] = p[:, C_VB:C_U]

    z = p[:, C_GC:C_GC + CONV_CH] * p[:, C_U:C_GB]
    row = lax.broadcasted_iota(jnp.int32, z.shape, 0) % seq_len
    z_prev = jnp.where(row == 0, 0.0, pltpu.roll(z, 1, axis=0))
    z_next = jnp.where(row == seq_len - 1, 0.0, pltpu.roll(z, tm - 1, axis=0))
    y = z_prev * cw_ref[0:1, :] + z * cw_ref[1:2, :] + z_next * cw_ref[2:3, :]
    oc_o[...] = (p[:, C_GB:C_GC] * y).astype(BF16)


def _inproj(x, mod, group_of_tile, g_mix, w_in_bf, layer, norm_w, ones_bd, conv_w, tm, seq_len, rope_tabs=None):
    t = x.shape[0]
    rope = rope_tabs is not None
    in_specs = [
        pl.BlockSpec((tm, D_MODEL), lambda i: (i, 0)),
        pl.BlockSpec((None, None, N_MOD, D_MODEL), lambda i: (layer, group_of_tile(i), 0, 0)),
        pl.BlockSpec((None, 1, D_MODEL), lambda i: (layer, 0, 0)),
        pl.BlockSpec((None, D_MODEL, IN_COLS), lambda i: (layer, 0, 0)),
        pl.BlockSpec((None, 1, norm_w.shape[-1]), lambda i: (layer, 0, 0)),
        pl.BlockSpec(ones_bd.shape, lambda i: (0, 0)),
        pl.BlockSpec((None, CONV_W, CONV_CH), lambda i: (layer, 0, 0)),
    ]
    args = [x, mod, g_mix, w_in_bf, norm_w, ones_bd, conv_w]
    if rope:
        in_specs += [pl.BlockSpec(rope_tabs[0].shape, lambda i: (0, 0))] * 2
        args += list(rope_tabs)
    widths = (WIDTH_A, WIDTH_B, KV_WIDTH_A, KV_WIDTH_A, WIDTH_B, WIDTH_B, CONV_CH)
    dtypes = (BF16, BF16, F32, F32, F32, F32, BF16)
    return pl.pallas_call(
        functools.partial(_inproj_kernel, seq_len=seq_len, rope=rope),
        out_shape=[jax.ShapeDtypeStruct((t, w), dt) for w, dt in zip(widths, dtypes)],
        grid=(t // tm,),
        in_specs=in_specs,
        out_specs=[pl.BlockSpec((tm, w), lambda i: (i, 0)) for w in widths],
        compiler_params=_params(("arbitrary",), VMEM_LIMIT_V7X),
        name="inproj_rope" if rope else "inproj",
    )(*args)


def _softmax_pv(s, v, sink=None):
    m = jnp.max(s, axis=-1, keepdims=True)
    if sink is not None:
        m = jnp.maximum(m, sink)
    p = jnp.exp(s - m)
    den = jnp.sum(p, axis=-1, keepdims=True)
    if sink is not None:
        den = den + jnp.exp(sink - m)
    return jnp.dot(p.astype(BF16), v, preferred_element_type=F32) / den


def _store_heads(o_ref, outs):
    for i in range(0, len(outs), 2):
        o_ref[:, i * HEAD_DIM:(i + 2) * HEAD_DIM] = jnp.concatenate(outs[i:i + 2], axis=-1).astype(o_ref.dtype)


def _ctx_attn_kernel(sink_ref, qa_ref, ka_ref, va_ref, qb_ref, kb_ref, vb_ref, oa_o, ob_o):
    scale = HEAD_DIM ** -0.5
    ka = ka_ref[...].astype(BF16)
    va = va_ref[...].astype(BF16)
    outs = []
    for h in range(N_HEADS_A):
        j = h // GROUP_A
        q = qa_ref[:, h * HEAD_DIM:(h + 1) * HEAD_DIM] * scale
        s = lax.dot_general(q, ka[:, j * HEAD_DIM:(j + 1) * HEAD_DIM], _NT, preferred_element_type=F32)
        outs.append(_softmax_pv(s, va[:, j * HEAD_DIM:(j + 1) * HEAD_DIM], sink_ref[h]))
    _store_heads(oa_o, outs)
    kb = kb_ref[...].astype(BF16)
    vb = vb_ref[...].astype(BF16)
    outs = []
    for h in range(N_HEADS_B):
        sl = slice(h * HEAD_DIM, (h + 1) * HEAD_DIM)
        q = qb_ref[:, sl] * scale
        s = lax.dot_general(q, kb[:, sl], _NT, preferred_element_type=F32)
        outs.append(_softmax_pv(s, vb[:, sl]))
    _store_heads(ob_o, outs)


def _ctx_attention(sink, qa, ka, va, qb, kb, vb, seq_len):
    t = qa.shape[0]
    widths = (WIDTH_A, KV_WIDTH_A, KV_WIDTH_A, WIDTH_B, WIDTH_B, WIDTH_B)
    return pl.pallas_call(
        _ctx_attn_kernel,
        out_shape=[jax.ShapeDtypeStruct((t, WIDTH_A), BF16), jax.ShapeDtypeStruct((t, WIDTH_B), BF16)],
        grid=(t // seq_len,),
        in_specs=[pl.BlockSpec(memory_space=pltpu.SMEM)]
        + [pl.BlockSpec((seq_len, w), lambda i: (i, 0)) for w in widths],
        out_specs=[pl.BlockSpec((seq_len, WIDTH_A), lambda i: (i, 0)),
                   pl.BlockSpec((seq_len, WIDTH_B), lambda i: (i, 0))],
        compiler_params=_params(("arbitrary",)),
        name="ctx_attention",
    )(sink, qa, ka, va, qb, kb, vb)


def _win_attn_kernel(sink_ref, qa_ref, ka_ref, va_ref, ck_ref, cv_ref, oa_o, *, n_lat):
    scale = HEAD_DIM ** -0.5
    span = Q_BLK + 2 * WINDOW
    n = pl.program_id(1)
    start = pl.multiple_of(jnp.clip(n * Q_BLK - WINDOW, 0, n_lat - span), Q_BLK)
    kcat = jnp.concatenate([ka_ref[pl.ds(start, span), :], ck_ref[...]], axis=0).astype(BF16)
    vcat = jnp.concatenate([va_ref[pl.ds(start, span), :], cv_ref[...]], axis=0).astype(BF16)
    n_keys = kcat.shape[0]
    qpos = n * Q_BLK + lax.broadcasted_iota(jnp.int32, (Q_BLK, n_keys), 0)
    col = lax.broadcasted_iota(jnp.int32, (Q_BLK, n_keys), 1)
    ok = (col >= span) | (jnp.abs(qpos - (start + col)) <= WINDOW)
    outs = []
    for h in range(N_HEADS_A):
        j = h // GROUP_A
        q = qa_ref[:, h * HEAD_DIM:(h + 1) * HEAD_DIM] * scale
        s = lax.dot_general(q, kcat[:, j * HEAD_DIM:(j + 1) * HEAD_DIM], _NT, preferred_element_type=F32)
        s = jnp.where(ok, s, NEG)
        outs.append(_softmax_pv(s, vcat[:, j * HEAD_DIM:(j + 1) * HEAD_DIM], sink_ref[h]))
    _store_heads(oa_o, outs)


def _win_attention(sink, qa, ka, va, cache_k, cache_v, layer, n_lat):
    t = qa.shape[0]
    nb = n_lat // Q_BLK
    past = cache_k.shape[2]
    return pl.pallas_call(
        functools.partial(_win_attn_kernel, n_lat=n_lat),
        out_shape=jax.ShapeDtypeStruct((t, WIDTH_A), BF16),
        grid=(t // n_lat, nb),
        in_specs=[
            pl.BlockSpec(memory_space=pltpu.SMEM),
            pl.BlockSpec((Q_BLK, WIDTH_A), lambda b, n: (b * nb + n, 0)),
            pl.BlockSpec((n_lat, KV_WIDTH_A), lambda b, n: (b, 0)),
            pl.BlockSpec((n_lat, KV_WIDTH_A), lambda b, n: (b, 0)),
            pl.BlockSpec((None, None, past, KV_WIDTH_A), lambda b, n: (b, layer, 0, 0)),
            pl.BlockSpec((None, None, past, KV_WIDTH_A), lambda b, n: (b, layer, 0, 0)),
        ],
        out_specs=pl.BlockSpec((Q_BLK, WIDTH_A), lambda b, n: (b * nb + n, 0)),
        compiler_params=_params(("arbitrary", "arbitrary")),
        name="window_attention",
    )(sink, qa, ka, va, cache_k, cache_v)


def _nbr_attn_kernel(qb_ref, kb_ref, vb_ref, ck_ref, cv_ref, bias_ref, ob_o, *, rows, kr):
    scale = HEAD_DIM ** -0.5
    r = pl.program_id(1)
    start = pl.multiple_of(jnp.clip(r - kr // 2, 0, rows - kr) * GRID_W, GRID_W)
    nwin = kr * GRID_W
    kcat = jnp.concatenate([kb_ref[pl.ds(start, nwin), :], ck_ref[...]], axis=0).astype(BF16)
    vcat = jnp.concatenate([vb_ref[pl.ds(start, nwin), :], cv_ref[...]], axis=0).astype(BF16)
    past = ck_ref.shape[0]
    outs = []
    for h in range(N_HEADS_B):
        sl = slice(h * HEAD_DIM, (h + 1) * HEAD_DIM)
        q = qb_ref[:, sl] * scale
        s = lax.dot_general(q, kcat[:, sl], _NT, preferred_element_type=F32)
        bias = jnp.concatenate([bias_ref[h], jnp.zeros((GRID_W, past), F32)], axis=-1)
        outs.append(_softmax_pv(s + bias, vcat[:, sl]))
    _store_heads(ob_o, outs)


def _nbr_bias_table(rel_bias, rows):
    kr = min(NA_ROWS, rows)
    cls = np.arange(kr)[:, None]
    m = np.arange(kr)[None, :]
    row_sel = (m - cls + NA_ROWS - 1)[:, :, None] == np.arange(2 * NA_ROWS - 1)[None, None, :]
    c = np.arange(GRID_W)[:, None]
    kc = np.arange(GRID_W)[None, :]
    dc = np.clip(kc - c, -(NA_COLS - 1), NA_COLS - 1) + NA_COLS - 1
    col_sel = dc[:, :, None] == np.arange(2 * NA_COLS - 1)[None, None, :]
    col_start = np.clip(c - NA_COLS // 2, 0, GRID_W - NA_COLS)
    ok = (kc >= col_start) & (kc < col_start + NA_COLS)
    tab = jnp.einsum('hrd,ymr,ckd->hycmk', rel_bias.astype(F32), row_sel.astype(np.float32),
                     col_sel.astype(np.float32), precision=lax.Precision.HIGHEST)
    tab = jnp.where(ok[None, None, :, None, :], tab, NEG)
    return tab.reshape(rel_bias.shape[0], kr, GRID_W, kr * GRID_W)


def _nbr_attention(qb, kb, vb, cache_k, cache_v, bias_tab, layer, n_lat):
    t = qb.shape[0]
    rows = n_lat // GRID_W
    kr = min(NA_ROWS, rows)
    past = cache_k.shape[2]

    def bias_map(b, r):
        return (0, r - jnp.clip(r - kr // 2, 0, rows - kr), 0, 0)

    return pl.pallas_call(
        functools.partial(_nbr_attn_kernel, rows=rows, kr=kr),
        out_shape=jax.ShapeDtypeStruct((t, WIDTH_B), BF16),
        grid=(t // n_lat, rows),
        in_specs=[
            pl.BlockSpec((GRID_W, WIDTH_B), lambda b, r: (b * rows + r, 0)),
            pl.BlockSpec((n_lat, WIDTH_B), lambda b, r: (b, 0)),
            pl.BlockSpec((n_lat, WIDTH_B), lambda b, r: (b, 0)),
            pl.BlockSpec((None, None, past, WIDTH_B), lambda b, r: (b, layer, 0, 0)),
            pl.BlockSpec((None, None, past, WIDTH_B), lambda b, r: (b, layer, 0, 0)),
            pl.BlockSpec((N_HEADS_B, None, GRID_W, kr * GRID_W), bias_map),
        ],
        out_specs=pl.BlockSpec((GRID_W, WIDTH_B), lambda b, r: (b * rows + r, 0)),
        compiler_params=_params(("arbitrary", "arbitrary")),
        name="neighbourhood_attention",
    )(qb, kb, vb, cache_k, cache_v, bias_tab)


ROW_TILE = 8
LANES = 128
assert ROW_TILE * LANES == D_MODEL


def _store_row_tiles(ref, row0, x):
    n = x.shape[0]
    for c in range(ROW_TILE):
        ref[pl.ds(row0 * ROW_TILE + c, n, stride=ROW_TILE), :] = x[:, c * LANES:(c + 1) * LANES]


def _load_row_tiles(ref, row0, n, row_stride=1, chunk0=0):
    return jnp.concatenate(
        [ref[pl.ds(row0 * ROW_TILE + chunk0 + c, n, stride=ROW_TILE * row_stride), :] for c in range(ROW_TILE)],
        axis=-1)


def _outproj_kernel(x_ref, oa_ref, ob_ref, oc_ref, w_ref, mod_ref, g_ref, wr_ref, br_ref, xn_o, h_o, lg_o):
    mix = jnp.dot(oa_ref[...], w_ref[0:WIDTH_A, :], preferred_element_type=F32)
    mix += jnp.dot(ob_ref[...], w_ref[WIDTH_A:WIDTH_A + WIDTH_B, :], preferred_element_type=F32)
    mix += jnp.dot(oc_ref[...], w_ref[WIDTH_A + WIDTH_B:, :], preferred_element_type=F32)
    xn = x_ref[...] + mod_ref[2:3, :] * mix
    xn_o[...] = xn
    h = xn * lax.rsqrt(jnp.mean(xn * xn, axis=-1, keepdims=True) + EPS) * g_ref[...]
    h = h * (1.0 + mod_ref[4:5, :]) + mod_ref[3:4, :]
    h_hi = h.astype(BF16)
    h_o[...] = h_hi
    h_lo = (h - h_hi.astype(F32)).astype(BF16)
    wr = wr_ref[...]
    w_hi = wr.astype(BF16)
    w_lo = (wr - w_hi.astype(F32)).astype(BF16)
    lg = jnp.dot(h_hi, w_hi, preferred_element_type=F32)
    lg += jnp.dot(h_lo, w_hi, preferred_element_type=F32)
    lg += jnp.dot(h_hi, w_lo, preferred_element_type=F32)
    lg_o[...] = lg + br_ref[...]


def _outproj(x, oa, ob, oc, w_out_bf, mod, group_of_tile, g_ffn, w_router, b_router, layer, tm):
    t = x.shape[0]
    return pl.pallas_call(
        _outproj_kernel,
        out_shape=[jax.ShapeDtypeStruct((t, D_MODEL), F32), jax.ShapeDtypeStruct((t, D_MODEL), BF16),
                   jax.ShapeDtypeStruct((t, N_EXPERTS), F32)],
        grid=(t // tm,),
        in_specs=[
            pl.BlockSpec((tm, D_MODEL), lambda i: (i, 0)),
            pl.BlockSpec((tm, WIDTH_A), lambda i: (i, 0)),
            pl.BlockSpec((tm, WIDTH_B), lambda i: (i, 0)),
            pl.BlockSpec((tm, CONV_CH), lambda i: (i, 0)),
            pl.BlockSpec((None, D_MODEL, D_MODEL), lambda i: (layer, 0, 0)),
            pl.BlockSpec((None, None, N_MOD, D_MODEL), lambda i: (layer, group_of_tile(i), 0, 0)),
            pl.BlockSpec((None, 1, D_MODEL), lambda i: (layer, 0, 0)),
            pl.BlockSpec((None, D_MODEL, N_EXPERTS), lambda i: (layer, 0, 0)),
            pl.BlockSpec((None, 1, N_EXPERTS), lambda i: (layer, 0, 0)),
        ],
        out_specs=[pl.BlockSpec((tm, D_MODEL), lambda i: (i, 0)),
                   pl.BlockSpec((tm, D_MODEL), lambda i: (i, 0)),
                   pl.BlockSpec((tm, N_EXPERTS), lambda i: (i, 0))],
        compiler_params=_params(("arbitrary",), VMEM_LIMIT_V7X),
        name="outproj_router",
    )(x, oa, ob, oc, w_out_bf, mod, g_ffn, w_router, b_router)


def _expert_kernel(blk_e_ref, n_used_ref, x_ref, wg_hbm, bg_ref, wu_hbm, bu_ref, wd_hbm, bd_ref, y_ref,
                   st_s, w_buf, wsem, *, n, layer):
    i = pl.program_id(0)
    n_used = n_used_ref[0]

    def weight_copies(e, ws):
        return [pltpu.make_async_copy(w_hbm.at[layer, e], w_buf.at[ws, m], wsem.at[ws])
                for m, w_hbm in enumerate((wg_hbm, wu_hbm, wd_hbm))]

    @pl.when(i == 0)
    def _():
        st_s[0] = 0

    e_cur = blk_e_ref[i]
    first_of_run = (i == 0) | (blk_e_ref[jnp.maximum(i - 1, 0)] != e_cur)

    @pl.when(first_of_run & (i < n_used))
    def _():
        run = st_s[0]
        ws = run % 2

        @pl.when(i == 0)
        def _():
            for cp in weight_copies(e_cur, ws):
                cp.start()

        for cp in weight_copies(e_cur, ws):
            cp.wait()
        nxt = lax.while_loop(lambda j: (j < n_used) & (blk_e_ref[jnp.minimum(j, n - 1)] == e_cur),
                             lambda j: j + 1, i + 1)

        @pl.when(nxt < n_used)
        def _():
            for cp in weight_copies(blk_e_ref[jnp.minimum(nxt, n - 1)], 1 - ws):
                cp.start()

        st_s[0] = run + 1
        st_s[1] = ws

    @pl.when(i < n_used)
    def _():
        ws = st_s[1]
        x = _load_row_tiles(x_ref, 0, MOE_BLK).astype(BF16)
        g = jnp.dot(x, w_buf[ws, 0].astype(BF16), preferred_element_type=F32) + bg_ref[...]
        u = jnp.dot(x, w_buf[ws, 1].astype(BF16), preferred_element_type=F32) + bu_ref[...]
        g = jnp.minimum(g, SWIGLU_LIMIT)
        u = jnp.clip(u, -SWIGLU_LIMIT, SWIGLU_LIMIT)
        a = g * jax.nn.sigmoid(SWIGLU_ALPHA * g) * (u + 1.0)
        y = jnp.dot(a.astype(BF16), w_buf[ws, 2].astype(BF16), preferred_element_type=F32) + bd_ref[...]
        _store_row_tiles(y_ref, 0, y)

    @pl.when(i >= n_used)
    def _():
        y_ref[...] = jnp.zeros(y_ref.shape, F32)


def _experts(blk_e, n_used, x_sorted, w_gate, b_gate, w_up, b_up, w_down, b_down, layer):
    n_blocks = x_sorted.shape[0] // (MOE_BLK * ROW_TILE)
    bspec = pl.BlockSpec((None, None, 1, D_MODEL), lambda i, be, nu: (layer, be[i], 0, 0))
    anyspec = pl.BlockSpec(memory_space=pl.ANY)
    depth = w_gate.shape[0]
    b4 = lambda b: b.reshape(depth, N_EXPERTS, 1, D_MODEL)
    return pl.pallas_call(
        functools.partial(_expert_kernel, n=n_blocks, layer=layer),
        out_shape=jax.ShapeDtypeStruct(x_sorted.shape, F32),
        grid_spec=pltpu.PrefetchScalarGridSpec(
            num_scalar_prefetch=2,
            grid=(n_blocks,),
            in_specs=[
                pl.BlockSpec((MOE_BLK * ROW_TILE, LANES), lambda i, be, nu: (jnp.minimum(i, nu[0] - 1), 0)),
                anyspec, bspec, anyspec, bspec, anyspec, bspec],
            out_specs=pl.BlockSpec((MOE_BLK * ROW_TILE, LANES), lambda i, be, nu: (i, 0)),
            scratch_shapes=[
                pltpu.SMEM((2,), jnp.int32),
                pltpu.VMEM((2, 3, D_MODEL, D_MODEL), F32),
                pltpu.SemaphoreType.DMA((2,)),
            ],
        ),
        compiler_params=_params(("arbitrary",), VMEM_LIMIT_V7X),
        name="experts",
    )(blk_e, n_used, x_sorted, w_gate, b4(b_gate), w_up, b4(b_up), w_down, b4(b_down))


CUM_CHUNK = 256
BLK_LANES = 256


def _sublane_cumsum(x):
    row = lax.broadcasted_iota(jnp.int32, x.shape, 0)
    d = 1
    while d < x.shape[0]:
        x = x + jnp.where(row >= d, pltpu.roll(x, d, axis=0), 0)
        d *= 2
    return x


TOK_TILE = CUM_CHUNK
TILE_ROWS = TOP_K * TOK_TILE
SEG_LANES = 128
SEG_PAD_FROM = SEG_LANES - 2
SEG_PAD_LEN = SEG_LANES - 1
SEG_SRC, SEG_CNT, SEG_DST = 0, 1, 2


def _route_kernel(lg_ref, tri_ref, gate_o, lpos_o, seg_o, blk_o):
    lg = lg_ref[...]
    n_e, t = lg.shape
    e_iota = lax.broadcasted_iota(jnp.int32, lg.shape, 0)
    work = lg
    tops, hots = [], []
    for _ in range(TOP_K):
        m = jnp.max(work, axis=0, keepdims=True)
        first = jnp.min(jnp.where(work == m, e_iota, n_e), axis=0, keepdims=True)
        hot = e_iota == first
        work = jnp.where(hot, -jnp.inf, work)
        tops.append(m)
        hots.append(hot)
    ex = [jnp.exp(m - tops[0]) for m in tops]
    den = ex[0] + ex[1] + ex[2] + ex[3]
    for k in range(TOP_K):
        gate_o[k:k + 1, :] = ex[k] / den

    chosen = jnp.where(hots[0] | hots[1] | hots[2] | hots[3], 1.0, 0.0)
    tri = tri_ref[...]
    lane = lax.broadcasted_iota(jnp.int32, (n_e, SEG_LANES), 1)
    seg_cnt = jnp.zeros((n_e, SEG_LANES), jnp.int32)
    seg_before = jnp.zeros((n_e, SEG_LANES), jnp.int32)
    carry = jnp.zeros((n_e, 1), F32)
    rank_in_tile = []
    n_tiles = t // TOK_TILE
    for c in range(n_tiles):
        chunk = chosen[:, c * TOK_TILE:(c + 1) * TOK_TILE]
        inc = jnp.dot(chunk.astype(BF16), tri, preferred_element_type=F32)
        cnt = inc[:, TOK_TILE - 1:TOK_TILE]
        rank_in_tile.append((inc - chunk).astype(jnp.int32))
        seg_cnt = jnp.where(lane == c, cnt.astype(jnp.int32), seg_cnt)
        seg_before = jnp.where(lane == c, carry.astype(jnp.int32), seg_before)
        carry = carry + cnt

    counts = jnp.broadcast_to(carry.astype(jnp.int32), (n_e, SEG_LANES))
    padded = ((counts + (MOE_BLK - 1)) >> MOE_SHIFT) << MOE_SHIFT
    pad_end = _sublane_cumsum(padded)
    pad_start = pad_end - padded
    seg_src = _sublane_cumsum(seg_cnt) - seg_cnt
    seg_o[SEG_SRC] = seg_src
    seg_o[SEG_CNT] = jnp.where(lane == SEG_PAD_LEN, padded - counts, seg_cnt)
    seg_o[SEG_DST] = jnp.where(lane == SEG_PAD_FROM, pad_start + counts, pad_start + seg_before)

    for c in range(n_tiles):
        pos = seg_src[:, c:c + 1] + rank_in_tile[c]
        for k in range(TOP_K):
            lpos_o[k:k + 1, c * TOK_TILE:(c + 1) * TOK_TILE] = jnp.sum(
                jnp.where(hots[k][:, c * TOK_TILE:(c + 1) * TOK_TILE], pos, 0), axis=0, keepdims=True)

    pad_end_b = jnp.concatenate([pad_end] * (BLK_LANES // SEG_LANES), axis=-1)
    blk_row0 = lax.broadcasted_iota(jnp.int32, (n_e, BLK_LANES), 1) * MOE_BLK
    blk_o[0:1, :] = jnp.minimum(jnp.sum((pad_end_b <= blk_row0).astype(jnp.int32), axis=0, keepdims=True), n_e - 1)
    blk_o[1:2, :] = pad_end_b[n_e - 1:n_e, :] >> MOE_SHIFT
    blk_o[2:8, :] = jnp.zeros((6, BLK_LANES), jnp.int32)


def _route(logits):
    t = logits.shape[0]
    n_blocks = t * TOP_K // MOE_BLK + N_EXPERTS
    assert t % TOK_TILE == 0 and t // TOK_TILE <= SEG_PAD_FROM and n_blocks <= BLK_LANES
    tri = jnp.asarray(np.triu(np.ones((CUM_CHUNK, CUM_CHUNK), np.float32)), dtype=BF16)
    gates, lpos, seg, blk = pl.pallas_call(
        _route_kernel,
        out_shape=[jax.ShapeDtypeStruct((TOP_K, t), F32), jax.ShapeDtypeStruct((TOP_K, t), jnp.int32),
                   jax.ShapeDtypeStruct((3, N_EXPERTS, SEG_LANES), jnp.int32),
                   jax.ShapeDtypeStruct((8, BLK_LANES), jnp.int32)],
        compiler_params=_params(None, VMEM_LIMIT_V7X),
        name="route",
    )(logits.T, tri)
    return gates, lpos, seg, blk[0, :n_blocks], blk[1, :1]


def _segment_copies(src, dst, src0, dst0, cnt, sem, top_bit, wait):
    for j in range(top_bit, -1, -1):
        @pl.when(((cnt >> j) & 1) == 1)
        def _():
            off = (cnt >> (j + 1)) << (j + 1)
            size = (1 << j) * ROW_TILE
            cp = pltpu.make_async_copy(
                src.at[pl.ds(pl.multiple_of((src0 + off) * ROW_TILE, ROW_TILE), size)],
                dst.at[pl.ds(pl.multiple_of((dst0 + off) * ROW_TILE, ROW_TILE), size)], sem)
            if wait:
                cp.wait()
            else:
                cp.start()


TILE_TOP_BIT = TOK_TILE.bit_length() - 1
PAD_TOP_BIT = MOE_SHIFT - 1


def _dispatch_kernel(seg_ref, nu_ref, h_ref, lpos_ref, x_hbm, z_buf, zero_buf, sem, zsem, *, n_tiles, n_blocks):
    c = pl.program_id(0)
    slot = c % 2

    def wait_tile(s):
        pltpu.make_async_copy(z_buf.at[s], x_hbm.at[pl.ds(0, TILE_ROWS * ROW_TILE)], sem.at[s]).wait()

    def padding(wait):
        def per_expert(e, carry):
            _segment_copies(zero_buf, x_hbm, 0, seg_ref[SEG_DST, e, SEG_PAD_FROM], seg_ref[SEG_CNT, e, SEG_PAD_LEN],
                            zsem, PAD_TOP_BIT, wait)
            return carry

        lax.fori_loop(0, N_EXPERTS, per_expert, 0)

        def per_block(b, carry):
            cp = pltpu.make_async_copy(
                zero_buf, x_hbm.at[pl.ds(pl.multiple_of(b * (MOE_BLK * ROW_TILE), MOE_BLK * ROW_TILE),
                                         MOE_BLK * ROW_TILE)], zsem)
            if wait:
                cp.wait()
            else:
                cp.start()
            return carry

        lax.fori_loop(nu_ref[0], n_blocks, per_block, 0)

    @pl.when(c == 0)
    def _():
        zero_buf[...] = jnp.zeros(zero_buf.shape, F32)
        padding(False)

    @pl.when(c >= 2)
    def _():
        wait_tile(slot)

    row = lax.broadcasted_iota(jnp.int32, (TILE_ROWS, TOK_TILE), 0)
    hit = row == lpos_ref[0:1, :]
    for k in range(1, TOP_K):
        hit = hit | (row == lpos_ref[k:k + 1, :])
    z = jnp.dot(jnp.where(hit, 1.0, 0.0).astype(BF16), h_ref[...], preferred_element_type=F32)
    _store_row_tiles(z_buf.at[slot], 0, z)

    def per_expert(e, carry):
        _segment_copies(z_buf.at[slot], x_hbm, seg_ref[SEG_SRC, e, c], seg_ref[SEG_DST, e, c], seg_ref[SEG_CNT, e, c],
                        sem.at[slot], TILE_TOP_BIT, False)
        return carry

    lax.fori_loop(0, N_EXPERTS, per_expert, 0)

    @pl.when(c == n_tiles - 1)
    def _():
        if n_tiles > 1:
            wait_tile(1 - slot)
        wait_tile(slot)
        padding(True)


def _dispatch(seg, n_used, h_all, lpos, n_blocks):
    t = h_all.shape[0]
    n_tiles = t // TOK_TILE
    return pl.pallas_call(
        functools.partial(_dispatch_kernel, n_tiles=n_tiles, n_blocks=n_blocks),
        out_shape=jax.ShapeDtypeStruct((n_blocks * MOE_BLK * ROW_TILE, LANES), F32),
        grid_spec=pltpu.PrefetchScalarGridSpec(
            num_scalar_prefetch=2,
            grid=(n_tiles,),
            in_specs=[pl.BlockSpec((TOK_TILE, D_MODEL), lambda c, sg, nu: (c, 0)),
                      pl.BlockSpec((TOP_K, TOK_TILE), lambda c, sg, nu: (0, c))],
            out_specs=pl.BlockSpec(memory_space=pl.ANY),
            scratch_shapes=[
                pltpu.VMEM((2, TILE_ROWS * ROW_TILE, LANES), F32),
                pltpu.VMEM((MOE_BLK * ROW_TILE, LANES), F32),
                pltpu.SemaphoreType.DMA((2,)),
                pltpu.SemaphoreType.DMA,
            ],
        ),
        compiler_params=_params(("arbitrary",), VMEM_LIMIT_V7X),
        name="dispatch",
    )(seg, n_used, h_all, lpos)


def _combine_kernel(seg_ref, x_ref, y_hbm, lpos_ref, lpos_t_ref, gate_ref, mod_ref, o_ref, y_buf, sem, *, tile0,
                    n_tiles):
    c = pl.program_id(0)
    slot = c % 2

    def fetch(cc):
        def per_expert(e, carry):
            _segment_copies(y_hbm, y_buf.at[cc % 2], seg_ref[SEG_DST, e, tile0 + cc], seg_ref[SEG_SRC, e, tile0 + cc],
                            seg_ref[SEG_CNT, e, tile0 + cc], sem.at[cc % 2], TILE_TOP_BIT, False)
            return carry

        lax.fori_loop(0, N_EXPERTS, per_expert, 0)

    @pl.when(c == 0)
    def _():
        fetch(0)

    @pl.when(c + 1 < n_tiles)
    def _():
        fetch(c + 1)

    pltpu.make_async_copy(y_hbm.at[pl.ds(0, TILE_ROWS * ROW_TILE)], y_buf.at[slot], sem.at[slot]).wait()

    row = lax.broadcasted_iota(jnp.int32, (TILE_ROWS, TOK_TILE), 0)
    row_gate = jnp.zeros((TILE_ROWS, 1), F32)
    for k in range(TOP_K):
        row_gate += jnp.sum(jnp.where(row == lpos_ref[k:k + 1, :], gate_ref[k:k + 1, :], 0.0), axis=1, keepdims=True)
    ys = _load_row_tiles(y_buf.at[slot], 0, TILE_ROWS) * row_gate
    ys_hi = ys.astype(BF16)
    ys_lo = (ys - ys_hi.astype(F32)).astype(BF16)
    col = lax.broadcasted_iota(jnp.int32, (TOK_TILE, TILE_ROWS), 1)
    hit = col == lpos_t_ref[:, 0:1]
    for k in range(1, TOP_K):
        hit = hit | (col == lpos_t_ref[:, k:k + 1])
    u = jnp.where(hit, 1.0, 0.0).astype(BF16)
    mix = jnp.dot(u, ys_hi, preferred_element_type=F32) + jnp.dot(u, ys_lo, preferred_element_type=F32)
    o_ref[...] = x_ref[...] + mod_ref[5:6, :] * mix


def _combine(seg, x_mid, y_sorted, lpos, lpos_t, gates, mod, group_of_tile, layer, tok0):
    t = x_mid.shape[0]
    tile0 = tok0 // TOK_TILE
    n_tiles = t // TOK_TILE
    return pl.pallas_call(
        functools.partial(_combine_kernel, tile0=tile0, n_tiles=n_tiles),
        out_shape=jax.ShapeDtypeStruct((t, D_MODEL), F32),
        grid_spec=pltpu.PrefetchScalarGridSpec(
            num_scalar_prefetch=1,
            grid=(n_tiles,),
            in_specs=[
                pl.BlockSpec((TOK_TILE, D_MODEL), lambda c, sg: (c, 0)),
                pl.BlockSpec(memory_space=pl.ANY),
                pl.BlockSpec((TOP_K, TOK_TILE), lambda c, sg: (0, tile0 + c)),
                pl.BlockSpec((TOK_TILE, TOP_K), lambda c, sg: (tile0 + c, 0)),
                pl.BlockSpec((TOP_K, TOK_TILE), lambda c, sg: (0, tile0 + c)),
                pl.BlockSpec((None, None, N_MOD, D_MODEL), lambda c, sg: (layer, group_of_tile(c), 0, 0)),
            ],
            out_specs=pl.BlockSpec((TOK_TILE, D_MODEL), lambda c, sg: (c, 0)),
            scratch_shapes=[pltpu.VMEM((2, TILE_ROWS * ROW_TILE, LANES), F32), pltpu.SemaphoreType.DMA((2,))],
        ),
        compiler_params=_params(("arbitrary",), VMEM_LIMIT_V7X),
        name="combine",
    )(seg, x_mid, y_sorted, lpos, lpos_t, gates, mod)


def _rope_tables(n_lat):
    quarter = HEAD_DIM // 4
    t = jnp.arange(n_lat)
    inv = ROPE_BASE ** (-jnp.arange(quarter, dtype=F32) / quarter)
    ang_r = (t // GRID_W).astype(F32)[:, None] * inv
    ang_c = (t % GRID_W).astype(F32)[:, None] * inv
    cos = jnp.concatenate([jnp.cos(ang_r)] * 2 + [jnp.cos(ang_c)] * 2, axis=-1)
    sin = jnp.concatenate([-jnp.sin(ang_r), jnp.sin(ang_r), -jnp.sin(ang_c), jnp.sin(ang_c)], axis=-1)
    return jnp.concatenate([cos, cos], axis=-1), jnp.concatenate([sin, sin], axis=-1)


def _block_diag_ones():
    idx = np.arange(MXU_COLS_V7X) // HEAD_DIM
    return jnp.asarray(idx[:, None] == idx[None, :], dtype=BF16)


def kernel(x_prompt, x_sample, cache_k_win, cache_v_win, cache_k_nbr, cache_v_nbr, c, c_ctx, w_mod, b_mod, g_mix, g_ffn, w_in, w_out, qn_win, kn_win, qn_nbr, kn_nbr, sink_win, rel_bias_nbr, conv_w, w_router, b_router, w_gate, b_gate, w_up, b_up, w_down, b_down):
    bsz, n_ctx, d = x_prompt.shape
    dbs, n_lat, _ = x_sample.shape
    depth = w_in.shape[0]
    past = cache_k_win.shape[2]
    assert d == D_MODEL and dbs + 1 <= COND_ROWS and n_lat % GRID_W == 0 and n_lat >= Q_BLK + 2 * WINDOW
    t_ctx, t_lat = bsz * n_ctx, dbs * n_lat

    cond = jnp.concatenate([c_ctx[None], c, jnp.zeros((COND_ROWS - 1 - dbs, d), F32)], axis=0)
    mod = _modulation(cond, w_mod, b_mod)

    w_in_bf = w_in.astype(BF16)
    w_out_bf = w_out.astype(BF16)
    norm_w = jnp.concatenate([jnp.tile(qn_win, (1, N_HEADS_A)), jnp.tile(kn_win, (1, N_KV_A)),
                              jnp.tile(qn_nbr, (1, N_HEADS_B)), jnp.tile(kn_nbr, (1, N_HEADS_B))], axis=-1)[:, None, :]
    ones_bd = _block_diag_ones()
    rope_tabs = _rope_tables(n_lat)
    g_mix3, g_ffn3 = g_mix[:, None, :], g_ffn[:, None, :]
    b_router3 = b_router[:, None, :]
    ck_win = cache_k_win.reshape(dbs, depth, past, KV_WIDTH_A)
    cv_win = cache_v_win.reshape(dbs, depth, past, KV_WIDTH_A)
    ck_nbr = cache_k_nbr.reshape(dbs, depth, past, WIDTH_B)
    cv_nbr = cache_v_nbr.reshape(dbs, depth, past, WIDTH_B)

    tm_ctx = 2 * n_ctx
    tm_lat = 512
    ctx_group = lambda i: 0
    lat_group_in = lambda i: 1 + i
    lat_group_out = lambda i: 1 + (i * tm_lat) // n_lat
    lat_group_comb = lambda i: 1 + (i * TOK_TILE) // n_lat
    assert t_ctx % TOK_TILE == 0 and n_lat % TOK_TILE == 0

    xp = x_prompt.reshape(t_ctx, d)
    xs = x_sample.reshape(t_lat, d)
    caches = [[], [], [], []]
    for l in range(depth):
        qa, qb, ka, va, kb, vb, oc = _inproj(xp, mod, ctx_group, g_mix3, w_in_bf, l, norm_w, ones_bd, conv_w,
                                             tm_ctx, n_ctx)
        oa, ob = _ctx_attention(sink_win[l], qa, ka, va, qb, kb, vb, n_ctx)
        xp_mid, hp, lgp = _outproj(xp, oa, ob, oc, w_out_bf, mod, ctx_group, g_ffn3, w_router, b_router3, l, tm_ctx)
        for lst, a in zip(caches, (ka, va, kb, vb)):
            lst.append(a)

        qa, qb, ka, va, kb, vb, oc = _inproj(xs, mod, lat_group_in, g_mix3, w_in_bf, l, norm_w, ones_bd, conv_w,
                                             n_lat, n_lat, rope_tabs)
        oa = _win_attention(sink_win[l], qa, ka, va, ck_win, cv_win, l, n_lat)
        ob = _nbr_attention(qb, kb, vb, ck_nbr, cv_nbr, _nbr_bias_table(rel_bias_nbr[l], n_lat // GRID_W), l, n_lat)
        xs_mid, hs, lgs = _outproj(xs, oa, ob, oc, w_out_bf, mod, lat_group_out, g_ffn3, w_router, b_router3, l,
                                   tm_lat)

        gates, lpos, seg, blk_e, n_used = _route(jnp.concatenate([lgp, lgs], axis=0))
        x_sorted = _dispatch(seg, n_used, jnp.concatenate([hp, hs], axis=0), lpos, blk_e.shape[0])
        y_sorted = _experts(blk_e, n_used, x_sorted, w_gate, b_gate, w_up, b_up, w_down, b_down, l)
        lpos_t = lpos.T
        xp = _combine(seg, xp_mid, y_sorted, lpos, lpos_t, gates, mod, ctx_group, l, 0)
        xs = _combine(seg, xs_mid, y_sorted, lpos, lpos_t, gates, mod, lat_group_comb, l, t_ctx)

    new_v_win = jnp.stack([a.reshape(bsz, n_ctx, KV_WIDTH_A) for a in caches[1]], axis=1).reshape(
        bsz, depth, n_ctx, N_KV_A, HEAD_DIM)
    new_k_nbr = jnp.stack([a.reshape(bsz, n_ctx, WIDTH_B) for a in caches[2]], axis=1).reshape(
        bsz, depth, n_ctx, N_HEADS_B, HEAD_DIM)
    new_v_nbr = jnp.stack([a.reshape(bsz, n_ctx, WIDTH_B) for a in caches[3]], axis=1).reshape(
        bsz, depth, n_ctx, N_HEADS_B, HEAD_DIM)
    new_k_win = jnp.stack([a.reshape(bsz, n_ctx, KV_WIDTH_A) for a in caches[0]], axis=1).reshape(
        bsz, depth, n_ctx, N_KV_A, HEAD_DIM)
    return (xp.reshape(bsz, n_ctx, d), xs.reshape(dbs, n_lat, d), new_k_win, new_v_win, new_k_nbr, new_v_nbr)
```

```python
import functools

import numpy as np
import jax
import jax.numpy as jnp
from jax import lax
from jax.experimental import pallas as pl
from jax.experimental.pallas import tpu as pltpu

F32 = jnp.float32
BF16 = jnp.bfloat16

D_MODEL = 1024
HEAD_DIM = 64
GRID_W = 64
N_HEADS_A = 8
N_KV_A = 2
GROUP_A = N_HEADS_A // N_KV_A
WINDOW = 128
Q_BLK = 128
N_HEADS_B = 4
NA_ROWS = 8
NA_COLS = 16
CONV_CH = 256
CONV_W = 3
WIDTH_A = N_HEADS_A * HEAD_DIM
KV_WIDTH_A = N_KV_A * HEAD_DIM
WIDTH_B = N_HEADS_B * HEAD_DIM
IN_COLS = WIDTH_A + 2 * KV_WIDTH_A + 3 * WIDTH_B + 3 * CONV_CH
N_EXPERTS = 32
TOP_K = 4
SWIGLU_LIMIT = 7.0
SWIGLU_ALPHA = 1.702
ROPE_BASE = 10000.0
EPS = 1e-6
NEG = -1e30
N_MOD = 6

C_QA = 0
C_KA = C_QA + WIDTH_A
C_VA = C_KA + KV_WIDTH_A
C_QB = C_VA + KV_WIDTH_A
C_KB = C_QB + WIDTH_B
C_VB = C_KB + WIDTH_B
C_U = C_VB + WIDTH_B
C_GB = C_U + CONV_CH
C_GC = C_GB + CONV_CH

MXU_COLS_V7X = 256
COND_ROWS = 8
MOE_BLK = 256
MOE_SHIFT = MOE_BLK.bit_length() - 1
assert 1 << MOE_SHIFT == MOE_BLK
VMEM_LIMIT_V7X = 56 * 1024 * 1024

_NT = (((1,), (1,)), ((), ()))


def _params(sem, vmem=None):
    return pltpu.CompilerParams(dimension_semantics=sem, vmem_limit_bytes=vmem)


def _mod_kernel(c_ref, w_ref, b_ref, o_ref):
    c = c_ref[...]
    s = c * jax.nn.sigmoid(c)
    o_ref[...] = jnp.dot(s.astype(BF16), w_ref[...].astype(BF16), preferred_element_type=F32) + b_ref[...]


def _modulation(cond, w_mod, b_mod):
    depth = w_mod.shape[0]
    out = pl.pallas_call(
        _mod_kernel,
        out_shape=jax.ShapeDtypeStruct((depth, COND_ROWS, N_MOD * D_MODEL), F32),
        grid=(depth, N_MOD),
        in_specs=[
            pl.BlockSpec((COND_ROWS, D_MODEL), lambda l, j: (0, 0)),
            pl.BlockSpec((None, D_MODEL, D_MODEL), lambda l, j: (l, 0, j)),
            pl.BlockSpec((None, 1, D_MODEL), lambda l, j: (l, 0, j)),
        ],
        out_specs=pl.BlockSpec((None, COND_ROWS, D_MODEL), lambda l, j: (l, 0, j)),
        compiler_params=_params(("arbitrary", "arbitrary")),
        name="modulation",
    )(cond, w_mod, b_mod.reshape(depth, 1, N_MOD * D_MODEL))
    return out.reshape(depth, COND_ROWS, N_MOD, D_MODEL)


def _head_norm(x, w_row, ones_ref):
    width = x.shape[1]
    sq = (x * x).astype(BF16)
    parts = []
    for c0 in range(0, width, MXU_COLS_V7X):
        wd = min(MXU_COLS_V7X, width - c0)
        parts.append(jnp.dot(sq[:, c0:c0 + wd], ones_ref[:wd, :wd], preferred_element_type=F32))
    ss = parts[0] if len(parts) == 1 else jnp.concatenate(parts, axis=-1)
    return x * lax.rsqrt(ss * (1.0 / HEAD_DIM) + EPS) * w_row


def _rope(x, cos, sin):
    width = x.shape[1]
    lane = lax.broadcasted_iota(jnp.int32, x.shape, 1)
    quarter = HEAD_DIM // 4
    partner = jnp.where((lane % (2 * quarter)) < quarter,
                        pltpu.roll(x, width - quarter, axis=1), pltpu.roll(x, quarter, axis=1))
    reps = width // cos.shape[1]
    cos_w = cos if reps == 1 else jnp.concatenate([cos] * reps, axis=-1)
    sin_w = sin if reps == 1 else jnp.concatenate([sin] * reps, axis=-1)
    return x * cos_w + partner * sin_w


def _inproj_kernel(*refs, seq_len, rope):
    if rope:
        (x_ref, mod_ref, g_ref, w_ref, nw_ref, ones_ref, cw_ref, cos_ref, sin_ref,
         qa_o, qb_o, ka_o, va_o, kb_o, vb_o, oc_o) = refs
    else:
        (x_ref, mod_ref, g_ref, w_ref, nw_ref, ones_ref, cw_ref,
         qa_o, qb_o, ka_o, va_o, kb_o, vb_o, oc_o) = refs
    x = x_ref[...]
    tm = x.shape[0]
    h = x * lax.rsqrt(jnp.mean(x * x, axis=-1, keepdims=True) + EPS) * g_ref[...]
    h = h * (1.0 + mod_ref[1:2, :]) + mod_ref[0:1, :]
    p = jnp.dot(h.astype(BF16), w_ref[...], preferred_element_type=F32)

    qa = _head_norm(p[:, C_QA:C_KA], nw_ref[:, 0:WIDTH_A], ones_ref)
    ka = _head_norm(p[:, C_KA:C_VA], nw_ref[:, WIDTH_A:WIDTH_A + KV_WIDTH_A], ones_ref)
    o_qb = WIDTH_A + KV_WIDTH_A
    qb = _head_norm(p[:, C_QB:C_KB], nw_ref[:, o_qb:o_qb + WIDTH_B], ones_ref)
    kb = _head_norm(p[:, C_KB:C_VB], nw_ref[:, o_qb + WIDTH_B:o_qb + 2 * WIDTH_B], ones_ref)
    if rope:
        cos, sin = cos_ref[...], sin_ref[...]
        qa = _rope(qa, cos, sin)
        ka = _rope(ka, cos, sin)
    qa_o[...] = qa.astype(BF16)
    qb_o[...] = qb.astype(BF16)
    ka_o[...] = ka
    va_o[...] = p[:, C_VA:C_QB]
    kb_o[...] = kb
    vb_o[...] = p[:, C_VB:C_U]

    z = p[:, C_GC:C_GC + CONV_CH] * p[:, C_U:C_GB]
    row = lax.broadcasted_iota(jnp.int32, z.shape, 0) % seq_len
    z_prev = jnp.where(row == 0, 0.0, pltpu.roll(z, 1, axis=0))
    z_next = jnp.where(row == seq_len - 1, 0.0, pltpu.roll(z, tm - 1, axis=0))
    y = z_prev * cw_ref[0:1, :] + z * cw_ref[1:2, :] + z_next * cw_ref[2:3, :]
    oc_o[...] = (p[:, C_GB:C_GC] * y).astype(BF16)


def _inproj(x, mod, group_of_tile, g_mix, w_in_bf, layer, norm_w, ones_bd, conv_w, tm, seq_len, rope_tabs=None):
    t = x.shape[0]
    rope = rope_tabs is not None
    in_specs = [
        pl.BlockSpec((tm, D_MODEL), lambda i: (i, 0)),
        pl.BlockSpec((None, None, N_MOD, D_MODEL), lambda i: (layer, group_of_tile(i), 0, 0)),
        pl.BlockSpec((None, 1, D_MODEL), lambda i: (layer, 0, 0)),
        pl.BlockSpec((None, D_MODEL, IN_COLS), lambda i: (layer, 0, 0)),
        pl.BlockSpec((None, 1, norm_w.shape[-1]), lambda i: (layer, 0, 0)),
        pl.BlockSpec(ones_bd.shape, lambda i: (0, 0)),
        pl.BlockSpec((None, CONV_W, CONV_CH), lambda i: (layer, 0, 0)),
    ]
    args = [x, mod, g_mix, w_in_bf, norm_w, ones_bd, conv_w]
    if rope:
        in_specs += [pl.BlockSpec(rope_tabs[0].shape, lambda i: (0, 0))] * 2
        args += list(rope_tabs)
    widths = (WIDTH_A, WIDTH_B, KV_WIDTH_A, KV_WIDTH_A, WIDTH_B, WIDTH_B, CONV_CH)
    dtypes = (BF16, BF16, F32, F32, F32, F32, BF16)
    return pl.pallas_call(
        functools.partial(_inproj_kernel, seq_len=seq_len, rope=rope),
        out_shape=[jax.ShapeDtypeStruct((t, w), dt) for w, dt in zip(widths, dtypes)],
        grid=(t // tm,),
        in_specs=in_specs,
        out_specs=[pl.BlockSpec((tm, w), lambda i: (i, 0)) for w in widths],
        compiler_params=_params(("arbitrary",), VMEM_LIMIT_V7X),
        name="inproj_rope" if rope else "inproj",
    )(*args)


def _softmax_pv(s, v, sink=None):
    m = jnp.max(s, axis=-1, keepdims=True)
    if sink is not None:
        m = jnp.maximum(m, sink)
    p = jnp.exp(s - m)
    den = jnp.sum(p, axis=-1, keepdims=True)
    if sink is not None:
        den = den + jnp.exp(sink - m)
    return jnp.dot(p.astype(BF16), v, preferred_element_type=F32) / den


def _store_heads(o_ref, outs):
    for i in range(0, len(outs), 2):
        o_ref[:, i * HEAD_DIM:(i + 2) * HEAD_DIM] = jnp.concatenate(outs[i:i + 2], axis=-1).astype(o_ref.dtype)


def _ctx_attn_kernel(sink_ref, qa_ref, ka_ref, va_ref, qb_ref, kb_ref, vb_ref, oa_o, ob_o):
    scale = HEAD_DIM ** -0.5
    ka = ka_ref[...].astype(BF16)
    va = va_ref[...].astype(BF16)
    outs = []
    for h in range(N_HEADS_A):
        j = h // GROUP_A
        q = qa_ref[:, h * HEAD_DIM:(h + 1) * HEAD_DIM] * scale
        s = lax.dot_general(q, ka[:, j * HEAD_DIM:(j + 1) * HEAD_DIM], _NT, preferred_element_type=F32)
        outs.append(_softmax_pv(s, va[:, j * HEAD_DIM:(j + 1) * HEAD_DIM], sink_ref[h]))
    _store_heads(oa_o, outs)
    kb = kb_ref[...].astype(BF16)
    vb = vb_ref[...].astype(BF16)
    outs = []
    for h in range(N_HEADS_B):
        sl = slice(h * HEAD_DIM, (h + 1) * HEAD_DIM)
        q = qb_ref[:, sl] * scale
        s = lax.dot_general(q, kb[:, sl], _NT, preferred_element_type=F32)
        outs.append(_softmax_pv(s, vb[:, sl]))
    _store_heads(ob_o, outs)


def _ctx_attention(sink, qa, ka, va, qb, kb, vb, seq_len):
    t = qa.shape[0]
    widths = (WIDTH_A, KV_WIDTH_A, KV_WIDTH_A, WIDTH_B, WIDTH_B, WIDTH_B)
    return pl.pallas_call(
        _ctx_attn_kernel,
        out_shape=[jax.ShapeDtypeStruct((t, WIDTH_A), BF16), jax.ShapeDtypeStruct((t, WIDTH_B), BF16)],
        grid=(t // seq_len,),
        in_specs=[pl.BlockSpec(memory_space=pltpu.SMEM)]
        + [pl.BlockSpec((seq_len, w), lambda i: (i, 0)) for w in widths],
        out_specs=[pl.BlockSpec((seq_len, WIDTH_A), lambda i: (i, 0)),
                   pl.BlockSpec((seq_len, WIDTH_B), lambda i: (i, 0))],
        compiler_params=_params(("arbitrary",)),
        name="ctx_attention",
    )(sink, qa, ka, va, qb, kb, vb)


def _win_attn_kernel(sink_ref, qa_ref, ka_ref, va_ref, ck_ref, cv_ref, oa_o, *, n_lat):
    scale = HEAD_DIM ** -0.5
    span = Q_BLK + 2 * WINDOW
    n = pl.program_id(1)
    start = pl.multiple_of(jnp.clip(n * Q_BLK - WINDOW, 0, n_lat - span), Q_BLK)
    kcat = jnp.concatenate([ka_ref[pl.ds(start, span), :], ck_ref[...]], axis=0).astype(BF16)
    vcat = jnp.concatenate([va_ref[pl.ds(start, span), :], cv_ref[...]], axis=0).astype(BF16)
    n_keys = kcat.shape[0]
    qpos = n * Q_BLK + lax.broadcasted_iota(jnp.int32, (Q_BLK, n_keys), 0)
    col = lax.broadcasted_iota(jnp.int32, (Q_BLK, n_keys), 1)
    ok = (col >= span) | (jnp.abs(qpos - (start + col)) <= WINDOW)
    outs = []
    for h in range(N_HEADS_A):
        j = h // GROUP_A
        q = qa_ref[:, h * HEAD_DIM:(h + 1) * HEAD_DIM] * scale
        s = lax.dot_general(q, kcat[:, j * HEAD_DIM:(j + 1) * HEAD_DIM], _NT, preferred_element_type=F32)
        s = jnp.where(ok, s, NEG)
        outs.append(_softmax_pv(s, vcat[:, j * HEAD_DIM:(j + 1) * HEAD_DIM], sink_ref[h]))
    _store_heads(oa_o, outs)


def _win_attention(sink, qa, ka, va, cache_k, cache_v, layer, n_lat):
    t = qa.shape[0]
    nb = n_lat // Q_BLK
    past = cache_k.shape[2]
    return pl.pallas_call(
        functools.partial(_win_attn_kernel, n_lat=n_lat),
        out_shape=jax.ShapeDtypeStruct((t, WIDTH_A), BF16),
        grid=(t // n_lat, nb),
        in_specs=[
            pl.BlockSpec(memory_space=pltpu.SMEM),
            pl.BlockSpec((Q_BLK, WIDTH_A), lambda b, n: (b * nb + n, 0)),
            pl.BlockSpec((n_lat, KV_WIDTH_A), lambda b, n: (b, 0)),
            pl.BlockSpec((n_lat, KV_WIDTH_A), lambda b, n: (b, 0)),
            pl.BlockSpec((None, None, past, KV_WIDTH_A), lambda b, n: (b, layer, 0, 0)),
            pl.BlockSpec((None, None, past, KV_WIDTH_A), lambda b, n: (b, layer, 0, 0)),
        ],
        out_specs=pl.BlockSpec((Q_BLK, WIDTH_A), lambda b, n: (b * nb + n, 0)),
        compiler_params=_params(("arbitrary", "arbitrary")),
        name="window_attention",
    )(sink, qa, ka, va, cache_k, cache_v)


def _nbr_attn_kernel(qb_ref, kb_ref, vb_ref, ck_ref, cv_ref, bias_ref, ob_o, *, rows, kr):
    scale = HEAD_DIM ** -0.5
    r = pl.program_id(1)
    start = pl.multiple_of(jnp.clip(r - kr // 2, 0, rows - kr) * GRID_W, GRID_W)
    nwin = kr * GRID_W
    kcat = jnp.concatenate([kb_ref[pl.ds(start, nwin), :], ck_ref[...]], axis=0).astype(BF16)
    vcat = jnp.concatenate([vb_ref[pl.ds(start, nwin), :], cv_ref[...]], axis=0).astype(BF16)
    past = ck_ref.shape[0]
    outs = []
    for h in range(N_HEADS_B):
        sl = slice(h * HEAD_DIM, (h + 1) * HEAD_DIM)
        q = qb_ref[:, sl] * scale
        s = lax.dot_general(q, kcat[:, sl], _NT, preferred_element_type=F32)
        bias = jnp.concatenate([bias_ref[h], jnp.zeros((GRID_W, past), F32)], axis=-1)
        outs.append(_softmax_pv(s + bias, vcat[:, sl]))
    _store_heads(ob_o, outs)


def _nbr_bias_table(rel_bias, rows):
    kr = min(NA_ROWS, rows)
    cls = np.arange(kr)[:, None]
    m = np.arange(kr)[None, :]
    row_sel = (m - cls + NA_ROWS - 1)[:, :, None] == np.arange(2 * NA_ROWS - 1)[None, None, :]
    c = np.arange(GRID_W)[:, None]
    kc = np.arange(GRID_W)[None, :]
    dc = np.clip(kc - c, -(NA_COLS - 1), NA_COLS - 1) + NA_COLS - 1
    col_sel = dc[:, :, None] == np.arange(2 * NA_COLS - 1)[None, None, :]
    col_start = np.clip(c - NA_COLS // 2, 0, GRID_W - NA_COLS)
    ok = (kc >= col_start) & (kc < col_start + NA_COLS)
    tab = jnp.einsum('hrd,ymr,ckd->hycmk', rel_bias.astype(F32), row_sel.astype(np.float32),
                     col_sel.astype(np.float32), precision=lax.Precision.HIGHEST)
    tab = jnp.where(ok[None, None, :, None, :], tab, NEG)
    return tab.reshape(rel_bias.shape[0], kr, GRID_W, kr * GRID_W)


def _nbr_attention(qb, kb, vb, cache_k, cache_v, bias_tab, layer, n_lat):
    t = qb.shape[0]
    rows = n_lat // GRID_W
    kr = min(NA_ROWS, rows)
    past = cache_k.shape[2]

    def bias_map(b, r):
        return (0, r - jnp.clip(r - kr // 2, 0, rows - kr), 0, 0)

    return pl.pallas_call(
        functools.partial(_nbr_attn_kernel, rows=rows, kr=kr),
        out_shape=jax.ShapeDtypeStruct((t, WIDTH_B), BF16),
        grid=(t // n_lat, rows),
        in_specs=[
            pl.BlockSpec((GRID_W, WIDTH_B), lambda b, r: (b * rows + r, 0)),
            pl.BlockSpec((n_lat, WIDTH_B), lambda b, r: (b, 0)),
            pl.BlockSpec((n_lat, WIDTH_B), lambda b, r: (b, 0)),
            pl.BlockSpec((None, None, past, WIDTH_B), lambda b, r: (b, layer, 0, 0)),
            pl.BlockSpec((None, None, past, WIDTH_B), lambda b, r: (b, layer, 0, 0)),
            pl.BlockSpec((N_HEADS_B, None, GRID_W, kr * GRID_W), bias_map),
        ],
        out_specs=pl.BlockSpec((GRID_W, WIDTH_B), lambda b, r: (b * rows + r, 0)),
        compiler_params=_params(("arbitrary", "arbitrary")),
        name="neighbourhood_attention",
    )(qb, kb, vb, cache_k, cache_v, bias_tab)


ROW_TILE = 8
LANES = 128
assert ROW_TILE * LANES == D_MODEL


def _store_row_tiles(ref, row0, x):
    n = x.shape[0]
    for c in range(ROW_TILE):
        ref[pl.ds(row0 * ROW_TILE + c, n, stride=ROW_TILE), :] = x[:, c * LANES:(c + 1) * LANES]


def _load_row_tiles(ref, row0, n, row_stride=1, chunk0=0):
    return jnp.concatenate(
        [ref[pl.ds(row0 * ROW_TILE + chunk0 + c, n, stride=ROW_TILE * row_stride), :] for c in range(ROW_TILE)],
        axis=-1)


def _outproj_kernel(x_ref, oa_ref, ob_ref, oc_ref, w_ref, mod_ref, g_ref, wr_ref, br_ref, *rest):
    xn_o, h_o, lg_o = rest[-3:]
    mix = jnp.dot(oa_ref[...], w_ref[0:WIDTH_A, :], preferred_element_type=F32)
    mix += jnp.dot(ob_ref[...], w_ref[WIDTH_A:WIDTH_A + WIDTH_B, :], preferred_element_type=F32)
    mix += jnp.dot(oc_ref[...], w_ref[WIDTH_A + WIDTH_B:, :], preferred_element_type=F32)
    xn = x_ref[...] + mod_ref[2:3, :] * mix
    xn_o[...] = xn
    h = xn * lax.rsqrt(jnp.mean(xn * xn, axis=-1, keepdims=True) + EPS) * g_ref[...]
    h = h * (1.0 + mod_ref[4:5, :]) + mod_ref[3:4, :]
    h_hi = h.astype(BF16)
    h_o[...] = h_hi
    h_lo = (h - h_hi.astype(F32)).astype(BF16)
    wr = wr_ref[...]
    w_hi = wr.astype(BF16)
    w_lo = (wr - w_hi.astype(F32)).astype(BF16)
    lg = lax.dot_general(w_hi, h_hi, _NT, preferred_element_type=F32)
    lg += lax.dot_general(w_hi, h_lo, _NT, preferred_element_type=F32)
    lg += lax.dot_general(w_lo, h_hi, _NT, preferred_element_type=F32)
    lg_o[...] = lg + br_ref[...]


def _outproj(x, oa, ob, oc, w_out_bf, mod, group_of_tile, g_ffn, w_router_t, b_router_t, layer, tm, t_all, tok0,
             shared=None):
    t = x.shape[0]
    b0 = tok0 // tm
    assert tok0 % tm == 0
    in_specs = [
        pl.BlockSpec((tm, D_MODEL), lambda i: (i, 0)),
        pl.BlockSpec((tm, WIDTH_A), lambda i: (i, 0)),
        pl.BlockSpec((tm, WIDTH_B), lambda i: (i, 0)),
        pl.BlockSpec((tm, CONV_CH), lambda i: (i, 0)),
        pl.BlockSpec((None, D_MODEL, D_MODEL), lambda i: (layer, 0, 0)),
        pl.BlockSpec((None, None, N_MOD, D_MODEL), lambda i: (layer, group_of_tile(i), 0, 0)),
        pl.BlockSpec((None, 1, D_MODEL), lambda i: (layer, 0, 0)),
        pl.BlockSpec((None, N_EXPERTS, D_MODEL), lambda i: (layer, 0, 0)),
        pl.BlockSpec((None, N_EXPERTS, 1), lambda i: (layer, 0, 0)),
    ]
    args = [x, oa, ob, oc, w_out_bf, mod, g_ffn, w_router_t, b_router_t]
    aliases = {}
    if shared is not None:
        aliases = {len(args): 1, len(args) + 1: 2}
        in_specs += [pl.BlockSpec(memory_space=pl.ANY)] * 2
        args += list(shared)
    return pl.pallas_call(
        _outproj_kernel,
        out_shape=[jax.ShapeDtypeStruct((t, D_MODEL), F32), jax.ShapeDtypeStruct((t_all, D_MODEL), BF16),
                   jax.ShapeDtypeStruct((N_EXPERTS, t_all), F32)],
        grid=(t // tm,),
        in_specs=in_specs,
        out_specs=[pl.BlockSpec((tm, D_MODEL), lambda i: (i, 0)),
                   pl.BlockSpec((tm, D_MODEL), lambda i: (b0 + i, 0)),
                   pl.BlockSpec((N_EXPERTS, tm), lambda i: (0, b0 + i))],
        input_output_aliases=aliases,
        compiler_params=_params(("arbitrary",), VMEM_LIMIT_V7X),
        name="outproj_router",
    )(*args)


def _expert_kernel(blk_e_ref, n_used_ref, x_ref, wg_hbm, bg_ref, wu_hbm, bu_ref, wd_hbm, bd_ref, y_ref,
                   st_s, w_buf, wsem, *, n, layer):
    i = pl.program_id(0)
    n_used = n_used_ref[0]

    def weight_copies(e, ws):
        return [pltpu.make_async_copy(w_hbm.at[layer, e], w_buf.at[ws, m], wsem.at[ws])
                for m, w_hbm in enumerate((wg_hbm, wu_hbm, wd_hbm))]

    @pl.when(i == 0)
    def _():
        st_s[0] = 0

    e_cur = blk_e_ref[i]
    first_of_run = (i == 0) | (blk_e_ref[jnp.maximum(i - 1, 0)] != e_cur)

    @pl.when(first_of_run & (i < n_used))
    def _():
        run = st_s[0]
        ws = run % 2

        @pl.when(i == 0)
        def _():
            for cp in weight_copies(e_cur, ws):
                cp.start()

        for cp in weight_copies(e_cur, ws):
            cp.wait()
        nxt = lax.while_loop(lambda j: (j < n_used) & (blk_e_ref[jnp.minimum(j, n - 1)] == e_cur),
                             lambda j: j + 1, i + 1)

        @pl.when(nxt < n_used)
        def _():
            for cp in weight_copies(blk_e_ref[jnp.minimum(nxt, n - 1)], 1 - ws):
                cp.start()

        st_s[0] = run + 1
        st_s[1] = ws

    @pl.when(i < n_used)
    def _():
        ws = st_s[1]
        x = _load_row_tiles(x_ref, 0, MOE_BLK).astype(BF16)
        g = jnp.dot(x, w_buf[ws, 0].astype(BF16), preferred_element_type=F32) + bg_ref[...]
        u = jnp.dot(x, w_buf[ws, 1].astype(BF16), preferred_element_type=F32) + bu_ref[...]
        g = jnp.minimum(g, SWIGLU_LIMIT)
        u = jnp.clip(u, -SWIGLU_LIMIT, SWIGLU_LIMIT)
        a = g * jax.nn.sigmoid(SWIGLU_ALPHA * g) * (u + 1.0)
        y = jnp.dot(a.astype(BF16), w_buf[ws, 2].astype(BF16), preferred_element_type=F32) + bd_ref[...]
        _store_row_tiles(y_ref, 0, y)

    @pl.when(i >= n_used)
    def _():
        y_ref[...] = jnp.zeros(y_ref.shape, F32)


def _experts(blk_e, n_used, x_sorted, w_gate, b_gate, w_up, b_up, w_down, b_down, layer):
    n_blocks = x_sorted.shape[0] // (MOE_BLK * ROW_TILE)
    bspec = pl.BlockSpec((None, None, 1, D_MODEL), lambda i, be, nu: (layer, be[i], 0, 0))
    anyspec = pl.BlockSpec(memory_space=pl.ANY)
    depth = w_gate.shape[0]
    b4 = lambda b: b.reshape(depth, N_EXPERTS, 1, D_MODEL)
    return pl.pallas_call(
        functools.partial(_expert_kernel, n=n_blocks, layer=layer),
        out_shape=jax.ShapeDtypeStruct(x_sorted.shape, F32),
        grid_spec=pltpu.PrefetchScalarGridSpec(
            num_scalar_prefetch=2,
            grid=(n_blocks,),
            in_specs=[
                pl.BlockSpec((MOE_BLK * ROW_TILE, LANES), lambda i, be, nu: (jnp.minimum(i, nu[0] - 1), 0)),
                anyspec, bspec, anyspec, bspec, anyspec, bspec],
            out_specs=pl.BlockSpec((MOE_BLK * ROW_TILE, LANES), lambda i, be, nu: (i, 0)),
            scratch_shapes=[
                pltpu.SMEM((2,), jnp.int32),
                pltpu.VMEM((2, 3, D_MODEL, D_MODEL), F32),
                pltpu.SemaphoreType.DMA((2,)),
            ],
        ),
        compiler_params=_params(("arbitrary",), VMEM_LIMIT_V7X),
        name="experts",
    )(blk_e, n_used, x_sorted, w_gate, b4(b_gate), w_up, b4(b_up), w_down, b4(b_down))


CUM_CHUNK = 256
BLK_LANES = 256


def _sublane_cumsum(x):
    row = lax.broadcasted_iota(jnp.int32, x.shape, 0)
    d = 1
    while d < x.shape[0]:
        x = x + jnp.where(row >= d, pltpu.roll(x, d, axis=0), 0)
        d *= 2
    return x


TOK_TILE = CUM_CHUNK
TILE_ROWS = TOP_K * TOK_TILE
SEG_LANES = 128
SEG_PAD_FROM = SEG_LANES - 2
SEG_PAD_LEN = SEG_LANES - 1
SEG_SRC, SEG_CNT, SEG_DST = 0, 1, 2


def _route_kernel(lg_ref, tri_ref, gate_o, lpos_o, seg_o, blk_o):
    lg = lg_ref[...]
    n_e, t = lg.shape
    e_iota = lax.broadcasted_iota(jnp.int32, lg.shape, 0)
    work = lg
    tops, hots = [], []
    for _ in range(TOP_K):
        m = jnp.max(work, axis=0, keepdims=True)
        first = jnp.min(jnp.where(work == m, e_iota, n_e), axis=0, keepdims=True)
        hot = e_iota == first
        work = jnp.where(hot, -jnp.inf, work)
        tops.append(m)
        hots.append(hot)
    ex = [jnp.exp(m - tops[0]) for m in tops]
    den = ex[0] + ex[1] + ex[2] + ex[3]
    for k in range(TOP_K):
        gate_o[k:k + 1, :] = ex[k] / den

    chosen = jnp.where(hots[0] | hots[1] | hots[2] | hots[3], 1.0, 0.0)
    tri = tri_ref[...]
    lane = lax.broadcasted_iota(jnp.int32, (n_e, SEG_LANES), 1)
    seg_cnt = jnp.zeros((n_e, SEG_LANES), jnp.int32)
    seg_before = jnp.zeros((n_e, SEG_LANES), jnp.int32)
    carry = jnp.zeros((n_e, 1), F32)
    rank_in_tile = []
    n_tiles = t // TOK_TILE
    for c in range(n_tiles):
        chunk = chosen[:, c * TOK_TILE:(c + 1) * TOK_TILE]
        inc = jnp.dot(chunk.astype(BF16), tri, preferred_element_type=F32)
        cnt = inc[:, TOK_TILE - 1:TOK_TILE]
        rank_in_tile.append((inc - chunk).astype(jnp.int32))
        seg_cnt = jnp.where(lane == c, cnt.astype(jnp.int32), seg_cnt)
        seg_before = jnp.where(lane == c, carry.astype(jnp.int32), seg_before)
        carry = carry + cnt

    counts = jnp.broadcast_to(carry.astype(jnp.int32), (n_e, SEG_LANES))
    padded = ((counts + (MOE_BLK - 1)) >> MOE_SHIFT) << MOE_SHIFT
    pad_end = _sublane_cumsum(padded)
    pad_start = pad_end - padded
    seg_src = _sublane_cumsum(seg_cnt) - seg_cnt
    seg_o[SEG_SRC] = seg_src
    seg_o[SEG_CNT] = jnp.where(lane == SEG_PAD_LEN, padded - counts, seg_cnt)
    seg_o[SEG_DST] = jnp.where(lane == SEG_PAD_FROM, pad_start + counts, pad_start + seg_before)

    for c in range(n_tiles):
        pos = seg_src[:, c:c + 1] + rank_in_tile[c]
        for k in range(TOP_K):
            lpos_o[k:k + 1, c * TOK_TILE:(c + 1) * TOK_TILE] = jnp.sum(
                jnp.where(hots[k][:, c * TOK_TILE:(c + 1) * TOK_TILE], pos, 0), axis=0, keepdims=True)

    pad_end_b = jnp.concatenate([pad_end] * (BLK_LANES // SEG_LANES), axis=-1)
    blk_row0 = lax.broadcasted_iota(jnp.int32, (n_e, BLK_LANES), 1) * MOE_BLK
    blk_o[0:1, :] = jnp.minimum(jnp.sum((pad_end_b <= blk_row0).astype(jnp.int32), axis=0, keepdims=True), n_e - 1)
    blk_o[1:2, :] = pad_end_b[n_e - 1:n_e, :] >> MOE_SHIFT
    blk_o[2:8, :] = jnp.zeros((6, BLK_LANES), jnp.int32)


def _route(logits):
    t = logits.shape[1]
    n_blocks = t * TOP_K // MOE_BLK + N_EXPERTS
    assert t % TOK_TILE == 0 and t // TOK_TILE <= SEG_PAD_FROM and n_blocks <= BLK_LANES
    tri = jnp.asarray(np.triu(np.ones((CUM_CHUNK, CUM_CHUNK), np.float32)), dtype=BF16)
    gates, lpos, seg, blk = pl.pallas_call(
        _route_kernel,
        out_shape=[jax.ShapeDtypeStruct((TOP_K, t), F32), jax.ShapeDtypeStruct((TOP_K, t), jnp.int32),
                   jax.ShapeDtypeStruct((3, N_EXPERTS, SEG_LANES), jnp.int32),
                   jax.ShapeDtypeStruct((8, BLK_LANES), jnp.int32)],
        compiler_params=_params(None, VMEM_LIMIT_V7X),
        name="route",
    )(logits, tri)
    return gates, lpos, seg, blk[0, :n_blocks], blk[1, :1]


def _segment_copies(src, dst, src0, dst0, cnt, sem, top_bit, wait):
    for j in range(top_bit, -1, -1):
        @pl.when(((cnt >> j) & 1) == 1)
        def _():
            off = (cnt >> (j + 1)) << (j + 1)
            size = (1 << j) * ROW_TILE
            cp = pltpu.make_async_copy(
                src.at[pl.ds(pl.multiple_of((src0 + off) * ROW_TILE, ROW_TILE), size)],
                dst.at[pl.ds(pl.multiple_of((dst0 + off) * ROW_TILE, ROW_TILE), size)], sem)
            if wait:
                cp.wait()
            else:
                cp.start()


TILE_TOP_BIT = TOK_TILE.bit_length() - 1
PAD_TOP_BIT = MOE_SHIFT - 1


def _dispatch_kernel(seg_ref, nu_ref, h_ref, lpos_ref, x_hbm, z_buf, zero_buf, sem, zsem, *, n_tiles, n_blocks):
    c = pl.program_id(0)
    slot = c % 2

    def wait_tile(s):
        pltpu.make_async_copy(z_buf.at[s], x_hbm.at[pl.ds(0, TILE_ROWS * ROW_TILE)], sem.at[s]).wait()

    def padding(wait):
        def per_expert(e, carry):
            _segment_copies(zero_buf, x_hbm, 0, seg_ref[SEG_DST, e, SEG_PAD_FROM], seg_ref[SEG_CNT, e, SEG_PAD_LEN],
                            zsem, PAD_TOP_BIT, wait)
            return carry

        lax.fori_loop(0, N_EXPERTS, per_expert, 0)

        def per_block(b, carry):
            cp = pltpu.make_async_copy(
                zero_buf, x_hbm.at[pl.ds(pl.multiple_of(b * (MOE_BLK * ROW_TILE), MOE_BLK * ROW_TILE),
                                         MOE_BLK * ROW_TILE)], zsem)
            if wait:
                cp.wait()
            else:
                cp.start()
            return carry

        lax.fori_loop(nu_ref[0], n_blocks, per_block, 0)

    @pl.when(c == 0)
    def _():
        zero_buf[...] = jnp.zeros(zero_buf.shape, F32)
        padding(False)

    @pl.when(c >= 2)
    def _():
        wait_tile(slot)

    row = lax.broadcasted_iota(jnp.int32, (TILE_ROWS, TOK_TILE), 0)
    hit = row == lpos_ref[0:1, :]
    for k in range(1, TOP_K):
        hit = hit | (row == lpos_ref[k:k + 1, :])
    z = jnp.dot(jnp.where(hit, 1.0, 0.0).astype(BF16), h_ref[...], preferred_element_type=F32)
    _store_row_tiles(z_buf.at[slot], 0, z)

    def per_expert(e, carry):
        _segment_copies(z_buf.at[slot], x_hbm, seg_ref[SEG_SRC, e, c], seg_ref[SEG_DST, e, c], seg_ref[SEG_CNT, e, c],
                        sem.at[slot], TILE_TOP_BIT, False)
        return carry

    lax.fori_loop(0, N_EXPERTS, per_expert, 0)

    @pl.when(c == n_tiles - 1)
    def _():
        if n_tiles > 1:
            wait_tile(1 - slot)
        wait_tile(slot)
        padding(True)


def _dispatch(seg, n_used, h_all, lpos, n_blocks):
    t = h_all.shape[0]
    n_tiles = t // TOK_TILE
    return pl.pallas_call(
        functools.partial(_dispatch_kernel, n_tiles=n_tiles, n_blocks=n_blocks),
        out_shape=jax.ShapeDtypeStruct((n_blocks * MOE_BLK * ROW_TILE, LANES), F32),
        grid_spec=pltpu.PrefetchScalarGridSpec(
            num_scalar_prefetch=2,
            grid=(n_tiles,),
            in_specs=[pl.BlockSpec((TOK_TILE, D_MODEL), lambda c, sg, nu: (c, 0)),
                      pl.BlockSpec((TOP_K, TOK_TILE), lambda c, sg, nu: (0, c))],
            out_specs=pl.BlockSpec(memory_space=pl.ANY),
            scratch_shapes=[
                pltpu.VMEM((2, TILE_ROWS * ROW_TILE, LANES), F32),
                pltpu.VMEM((MOE_BLK * ROW_TILE, LANES), F32),
                pltpu.SemaphoreType.DMA((2,)),
                pltpu.SemaphoreType.DMA,
            ],
        ),
        compiler_params=_params(("arbitrary",), VMEM_LIMIT_V7X),
        name="dispatch",
    )(seg, n_used, h_all, lpos)


def _combine_kernel(seg_ref, x_ref, y_hbm, lpos_ref, lpos_t_ref, gate_ref, mod_ref, o_ref, y_buf, sem, *, tile0,
                    n_tiles):
    c = pl.program_id(0)
    slot = c % 2

    def fetch(cc):
        def per_expert(e, carry):
            _segment_copies(y_hbm, y_buf.at[cc % 2], seg_ref[SEG_DST, e, tile0 + cc], seg_ref[SEG_SRC, e, tile0 + cc],
                            seg_ref[SEG_CNT, e, tile0 + cc], sem.at[cc % 2], TILE_TOP_BIT, False)
            return carry

        lax.fori_loop(0, N_EXPERTS, per_expert, 0)

    @pl.when(c == 0)
    def _():
        fetch(0)

    @pl.when(c + 1 < n_tiles)
    def _():
        fetch(c + 1)

    pltpu.make_async_copy(y_hbm.at[pl.ds(0, TILE_ROWS * ROW_TILE)], y_buf.at[slot], sem.at[slot]).wait()

    row = lax.broadcasted_iota(jnp.int32, (TILE_ROWS, TOK_TILE), 0)
    row_gate = jnp.zeros((TILE_ROWS, 1), F32)
    for k in range(TOP_K):
        row_gate += jnp.sum(jnp.where(row == lpos_ref[k:k + 1, :], gate_ref[k:k + 1, :], 0.0), axis=1, keepdims=True)
    ys = _load_row_tiles(y_buf.at[slot], 0, TILE_ROWS) * row_gate
    ys_hi = ys.astype(BF16)
    ys_lo = (ys - ys_hi.astype(F32)).astype(BF16)
    col = lax.broadcasted_iota(jnp.int32, (TOK_TILE, TILE_ROWS), 1)
    hit = col == lpos_t_ref[:, 0:1]
    for k in range(1, TOP_K):
        hit = hit | (col == lpos_t_ref[:, k:k + 1])
    u = jnp.where(hit, 1.0, 0.0).astype(BF16)
    mix = jnp.dot(u, ys_hi, preferred_element_type=F32) + jnp.dot(u, ys_lo, preferred_element_type=F32)
    o_ref[...] = x_ref[...] + mod_ref[5:6, :] * mix


def _combine(seg, x_mid, y_sorted, lpos, lpos_t, gates, mod, group_of_tile, layer, tok0):
    t = x_mid.shape[0]
    tile0 = tok0 // TOK_TILE
    n_tiles = t // TOK_TILE
    return pl.pallas_call(
        functools.partial(_combine_kernel, tile0=tile0, n_tiles=n_tiles),
        out_shape=jax.ShapeDtypeStruct((t, D_MODEL), F32),
        grid_spec=pltpu.PrefetchScalarGridSpec(
            num_scalar_prefetch=1,
            grid=(n_tiles,),
            in_specs=[
                pl.BlockSpec((TOK_TILE, D_MODEL), lambda c, sg: (c, 0)),
                pl.BlockSpec(memory_space=pl.ANY),
                pl.BlockSpec((TOP_K, TOK_TILE), lambda c, sg: (0, tile0 + c)),
                pl.BlockSpec((TOK_TILE, TOP_K), lambda c, sg: (tile0 + c, 0)),
                pl.BlockSpec((TOP_K, TOK_TILE), lambda c, sg: (0, tile0 + c)),
                pl.BlockSpec((None, None, N_MOD, D_MODEL), lambda c, sg: (layer, group_of_tile(c), 0, 0)),
            ],
            out_specs=pl.BlockSpec((TOK_TILE, D_MODEL), lambda c, sg: (c, 0)),
            scratch_shapes=[pltpu.VMEM((2, TILE_ROWS * ROW_TILE, LANES), F32), pltpu.SemaphoreType.DMA((2,))],
        ),
        compiler_params=_params(("arbitrary",), VMEM_LIMIT_V7X),
        name="combine",
    )(seg, x_mid, y_sorted, lpos, lpos_t, gates, mod)


def _rope_tables(n_lat):
    quarter = HEAD_DIM // 4
    t = jnp.arange(n_lat)
    inv = ROPE_BASE ** (-jnp.arange(quarter, dtype=F32) / quarter)
    ang_r = (t // GRID_W).astype(F32)[:, None] * inv
    ang_c = (t % GRID_W).astype(F32)[:, None] * inv
    cos = jnp.concatenate([jnp.cos(ang_r)] * 2 + [jnp.cos(ang_c)] * 2, axis=-1)
    sin = jnp.concatenate([-jnp.sin(ang_r), jnp.sin(ang_r), -jnp.sin(ang_c), jnp.sin(ang_c)], axis=-1)
    return jnp.concatenate([cos, cos], axis=-1), jnp.concatenate([sin, sin], axis=-1)


def _block_diag_ones():
    idx = np.arange(MXU_COLS_V7X) // HEAD_DIM
    return jnp.asarray(idx[:, None] == idx[None, :], dtype=BF16)


def kernel(x_prompt, x_sample, cache_k_win, cache_v_win, cache_k_nbr, cache_v_nbr, c, c_ctx, w_mod, b_mod, g_mix, g_ffn, w_in, w_out, qn_win, kn_win, qn_nbr, kn_nbr, sink_win, rel_bias_nbr, conv_w, w_router, b_router, w_gate, b_gate, w_up, b_up, w_down, b_down):
    bsz, n_ctx, d = x_prompt.shape
    dbs, n_lat, _ = x_sample.shape
    depth = w_in.shape[0]
    past = cache_k_win.shape[2]
    assert d == D_MODEL and dbs + 1 <= COND_ROWS and n_lat % GRID_W == 0 and n_lat >= Q_BLK + 2 * WINDOW
    t_ctx, t_lat = bsz * n_ctx, dbs * n_lat

    cond = jnp.concatenate([c_ctx[None], c, jnp.zeros((COND_ROWS - 1 - dbs, d), F32)], axis=0)
    mod = _modulation(cond, w_mod, b_mod)

    w_in_bf = w_in.astype(BF16)
    w_out_bf = w_out.astype(BF16)
    norm_w = jnp.concatenate([jnp.tile(qn_win, (1, N_HEADS_A)), jnp.tile(kn_win, (1, N_KV_A)),
                              jnp.tile(qn_nbr, (1, N_HEADS_B)), jnp.tile(kn_nbr, (1, N_HEADS_B))], axis=-1)[:, None, :]
    ones_bd = _block_diag_ones()
    rope_tabs = _rope_tables(n_lat)
    g_mix3, g_ffn3 = g_mix[:, None, :], g_ffn[:, None, :]
    w_router_t = jnp.swapaxes(w_router, 1, 2)
    b_router_t = b_router[:, :, None]
    t_all = t_ctx + t_lat
    ck_win = cache_k_win.reshape(dbs, depth, past, KV_WIDTH_A)
    cv_win = cache_v_win.reshape(dbs, depth, past, KV_WIDTH_A)
    ck_nbr = cache_k_nbr.reshape(dbs, depth, past, WIDTH_B)
    cv_nbr = cache_v_nbr.reshape(dbs, depth, past, WIDTH_B)

    tm_ctx = 2 * n_ctx
    tm_lat = 512
    ctx_group = lambda i: 0
    lat_group_in = lambda i: 1 + i
    lat_group_out = lambda i: 1 + (i * tm_lat) // n_lat
    lat_group_comb = lambda i: 1 + (i * TOK_TILE) // n_lat
    assert t_ctx % TOK_TILE == 0 and n_lat % TOK_TILE == 0

    xp = x_prompt.reshape(t_ctx, d)
    xs = x_sample.reshape(t_lat, d)
    caches = [[], [], [], []]
    for l in range(depth):
        qa, qb, ka, va, kb, vb, oc = _inproj(xp, mod, ctx_group, g_mix3, w_in_bf, l, norm_w, ones_bd, conv_w,
                                             tm_ctx, n_ctx)
        oa, ob = _ctx_attention(sink_win[l], qa, ka, va, qb, kb, vb, n_ctx)
        xp_mid, h_all, lg_all = _outproj(xp, oa, ob, oc, w_out_bf, mod, ctx_group, g_ffn3, w_router_t, b_router_t, l,
                                         tm_ctx, t_all, 0)
        for lst, a in zip(caches, (ka, va, kb, vb)):
            lst.append(a)

        qa, qb, ka, va, kb, vb, oc = _inproj(xs, mod, lat_group_in, g_mix3, w_in_bf, l, norm_w, ones_bd, conv_w,
                                             n_lat, n_lat, rope_tabs)
        oa = _win_attention(sink_win[l], qa, ka, va, ck_win, cv_win, l, n_lat)
        ob = _nbr_attention(qb, kb, vb, ck_nbr, cv_nbr, _nbr_bias_table(rel_bias_nbr[l], n_lat // GRID_W), l, n_lat)
        xs_mid, h_all, lg_all = _outproj(xs, oa, ob, oc, w_out_bf, mod, lat_group_out, g_ffn3, w_router_t, b_router_t,
                                         l, tm_lat, t_all, t_ctx, shared=(h_all, lg_all))

        gates, lpos, seg, blk_e, n_used = _route(lg_all)
        x_sorted = _dispatch(seg, n_used, h_all, lpos, blk_e.shape[0])
        y_sorted = _experts(blk_e, n_used, x_sorted, w_gate, b_gate, w_up, b_up, w_down, b_down, l)
        lpos_t = lpos.T
        xp = _combine(seg, xp_mid, y_sorted, lpos, lpos_t, gates, mod, ctx_group, l, 0)
        xs = _combine(seg, xs_mid, y_sorted, lpos, lpos_t, gates, mod, lat_group_comb, l, t_ctx)

    new_v_win = jnp.stack([a.reshape(bsz, n_ctx, KV_WIDTH_A) for a in caches[1]], axis=1).reshape(
        bsz, depth, n_ctx, N_KV_A, HEAD_DIM)
    new_k_nbr = jnp.stack([a.reshape(bsz, n_ctx, WIDTH_B) for a in caches[2]], axis=1).reshape(
        bsz, depth, n_ctx, N_HEADS_B, HEAD_DIM)
    new_v_nbr = jnp.stack([a.reshape(bsz, n_ctx, WIDTH_B) for a in caches[3]], axis=1).reshape(
        bsz, depth, n_ctx, N_HEADS_B, HEAD_DIM)
    new_k_win = jnp.stack([a.reshape(bsz, n_ctx, KV_WIDTH_A) for a in caches[0]], axis=1).reshape(
        bsz, depth, n_ctx, N_KV_A, HEAD_DIM)
    return (xp.reshape(bsz, n_ctx, d), xs.reshape(dbs, n_lat, d), new_k_win, new_v_win, new_k_nbr, new_v_nbr)
```

```python
import functools

import numpy as np
import jax
import jax.numpy as jnp
from jax import lax
from jax.experimental import pallas as pl
from jax.experimental.pallas import tpu as pltpu

F32 = jnp.float32
BF16 = jnp.bfloat16

D_MODEL = 1024
HEAD_DIM = 64
GRID_W = 64
N_HEADS_A = 8
N_KV_A = 2
GROUP_A = N_HEADS_A // N_KV_A
WINDOW = 128
Q_BLK = 128
N_HEADS_B = 4
NA_ROWS = 8
NA_COLS = 16
CONV_CH = 256
CONV_W = 3
WIDTH_A = N_HEADS_A * HEAD_DIM
KV_WIDTH_A = N_KV_A * HEAD_DIM
WIDTH_B = N_HEADS_B * HEAD_DIM
IN_COLS = WIDTH_A + 2 * KV_WIDTH_A + 3 * WIDTH_B + 3 * CONV_CH
N_EXPERTS = 32
TOP_K = 4
SWIGLU_LIMIT = 7.0
SWIGLU_ALPHA = 1.702
ROPE_BASE = 10000.0
EPS = 1e-6
NEG = -1e30
N_MOD = 6

C_QA = 0
C_KA = C_QA + WIDTH_A
C_VA = C_KA + KV_WIDTH_A
C_QB = C_VA + KV_WIDTH_A
C_KB = C_QB + WIDTH_B
C_VB = C_KB + WIDTH_B
C_U = C_VB + WIDTH_B
C_GB = C_U + CONV_CH
C_GC = C_GB + CONV_CH

MXU_COLS_V7X = 256
COND_ROWS = 8
MOE_BLK = 256
MOE_SHIFT = MOE_BLK.bit_length() - 1
assert 1 << MOE_SHIFT == MOE_BLK
VMEM_LIMIT_V7X = 56 * 1024 * 1024

_NT = (((1,), (1,)), ((), ()))


def _params(sem, vmem=None):
    return pltpu.CompilerParams(dimension_semantics=sem, vmem_limit_bytes=vmem)


def _mod_kernel(c_ref, w_ref, b_ref, o_ref):
    c = c_ref[...]
    s = c * jax.nn.sigmoid(c)
    o_ref[...] = jnp.dot(s.astype(BF16), w_ref[...].astype(BF16), preferred_element_type=F32) + b_ref[...]


def _modulation(cond, w_mod, b_mod):
    depth = w_mod.shape[0]
    out = pl.pallas_call(
        _mod_kernel,
        out_shape=jax.ShapeDtypeStruct((depth, COND_ROWS, N_MOD * D_MODEL), F32),
        grid=(depth, N_MOD),
        in_specs=[
            pl.BlockSpec((COND_ROWS, D_MODEL), lambda l, j: (0, 0)),
            pl.BlockSpec((None, D_MODEL, D_MODEL), lambda l, j: (l, 0, j)),
            pl.BlockSpec((None, 1, D_MODEL), lambda l, j: (l, 0, j)),
        ],
        out_specs=pl.BlockSpec((None, COND_ROWS, D_MODEL), lambda l, j: (l, 0, j)),
        compiler_params=_params(("arbitrary", "arbitrary")),
        name="modulation",
    )(cond, w_mod, b_mod.reshape(depth, 1, N_MOD * D_MODEL))
    return out.reshape(depth, COND_ROWS, N_MOD, D_MODEL)


def _head_norm(x, w_row, ones_ref):
    width = x.shape[1]
    sq = (x * x).astype(BF16)
    parts = []
    for c0 in range(0, width, MXU_COLS_V7X):
        wd = min(MXU_COLS_V7X, width - c0)
        parts.append(jnp.dot(sq[:, c0:c0 + wd], ones_ref[:wd, :wd], preferred_element_type=F32))
    ss = parts[0] if len(parts) == 1 else jnp.concatenate(parts, axis=-1)
    return x * lax.rsqrt(ss * (1.0 / HEAD_DIM) + EPS) * w_row


def _rope(x, cos, sin):
    width = x.shape[1]
    lane = lax.broadcasted_iota(jnp.int32, x.shape, 1)
    quarter = HEAD_DIM // 4
    partner = jnp.where((lane % (2 * quarter)) < quarter,
                        pltpu.roll(x, width - quarter, axis=1), pltpu.roll(x, quarter, axis=1))
    reps = width // cos.shape[1]
    cos_w = cos if reps == 1 else jnp.concatenate([cos] * reps, axis=-1)
    sin_w = sin if reps == 1 else jnp.concatenate([sin] * reps, axis=-1)
    return x * cos_w + partner * sin_w


def _inproj_kernel(*refs, seq_len, rope):
    if rope:
        (x_ref, mod_ref, g_ref, w_ref, nw_ref, ones_ref, cw_ref, cos_ref, sin_ref,
         qa_o, qb_o, ka_o, va_o, kb_o, vb_o, oc_o) = refs
    else:
        (x_ref, mod_ref, g_ref, w_ref, nw_ref, ones_ref, cw_ref,
         qa_o, qb_o, ka_o, va_o, kb_o, vb_o, oc_o) = refs
    x = x_ref[...]
    tm = x.shape[0]
    h = x * lax.rsqrt(jnp.mean(x * x, axis=-1, keepdims=True) + EPS) * g_ref[...]
    h = h * (1.0 + mod_ref[1:2, :]) + mod_ref[0:1, :]
    p = jnp.dot(h.astype(BF16), w_ref[...], preferred_element_type=F32)

    qa = _head_norm(p[:, C_QA:C_KA], nw_ref[:, 0:WIDTH_A], ones_ref)
    ka = _head_norm(p[:, C_KA:C_VA], nw_ref[:, WIDTH_A:WIDTH_A + KV_WIDTH_A], ones_ref)
    o_qb = WIDTH_A + KV_WIDTH_A
    qb = _head_norm(p[:, C_QB:C_KB], nw_ref[:, o_qb:o_qb + WIDTH_B], ones_ref)
    kb = _head_norm(p[:, C_KB:C_VB], nw_ref[:, o_qb + WIDTH_B:o_qb + 2 * WIDTH_B], ones_ref)
    if rope:
        cos, sin = cos_ref[...], sin_ref[...]
        qa = _rope(qa, cos, sin)
        ka = _rope(ka, cos, sin)
    qa_o[...] = qa.astype(BF16)
    qb_o[...] = qb.astype(BF16)
    ka_o[...] = ka
    va_o[...] = p[:, C_VA:C_QB]
    kb_o[...] = kb
    vb_o[...] = p[:, C_VB:C_U]

    z = p[:, C_GC:C_GC + CONV_CH] * p[:, C_U:C_GB]
    row = lax.broadcasted_iota(jnp.int32, z.shape, 0) % seq_len
    z_prev = jnp.where(row == 0, 0.0, pltpu.roll(z, 1, axis=0))
    z_next = jnp.where(row == seq_len - 1, 0.0, pltpu.roll(z, tm - 1, axis=0))
    y = z_prev * cw_ref[0:1, :] + z * cw_ref[1:2, :] + z_next * cw_ref[2:3, :]
    oc_o[...] = (p[:, C_GB:C_GC] * y).astype(BF16)


def _inproj(x, mod, group_of_tile, g_mix, w_in_bf, layer, norm_w, ones_bd, conv_w, tm, seq_len, rope_tabs=None):
    t = x.shape[0]
    rope = rope_tabs is not None
    in_specs = [
        pl.BlockSpec((tm, D_MODEL), lambda i: (i, 0)),
        pl.BlockSpec((None, None, N_MOD, D_MODEL), lambda i: (layer, group_of_tile(i), 0, 0)),
        pl.BlockSpec((None, 1, D_MODEL), lambda i: (layer, 0, 0)),
        pl.BlockSpec((None, D_MODEL, IN_COLS), lambda i: (layer, 0, 0)),
        pl.BlockSpec((None, 1, norm_w.shape[-1]), lambda i: (layer, 0, 0)),
        pl.BlockSpec(ones_bd.shape, lambda i: (0, 0)),
        pl.BlockSpec((None, CONV_W, CONV_CH), lambda i: (layer, 0, 0)),
    ]
    args = [x, mod, g_mix, w_in_bf, norm_w, ones_bd, conv_w]
    if rope:
        in_specs += [pl.BlockSpec(rope_tabs[0].shape, lambda i: (0, 0))] * 2
        args += list(rope_tabs)
    widths = (WIDTH_A, WIDTH_B, KV_WIDTH_A, KV_WIDTH_A, WIDTH_B, WIDTH_B, CONV_CH)
    dtypes = (BF16, BF16, F32, F32, F32, F32, BF16)
    return pl.pallas_call(
        functools.partial(_inproj_kernel, seq_len=seq_len, rope=rope),
        out_shape=[jax.ShapeDtypeStruct((t, w), dt) for w, dt in zip(widths, dtypes)],
        grid=(t // tm,),
        in_specs=in_specs,
        out_specs=[pl.BlockSpec((tm, w), lambda i: (i, 0)) for w in widths],
        compiler_params=_params(("arbitrary",), VMEM_LIMIT_V7X),
        name="inproj_rope" if rope else "inproj",
    )(*args)


def _softmax_pv(s, v, sink=None):
    m = jnp.max(s, axis=-1, keepdims=True)
    if sink is not None:
        m = jnp.maximum(m, sink)
    p = jnp.exp(s - m)
    den = jnp.sum(p, axis=-1, keepdims=True)
    if sink is not None:
        den = den + jnp.exp(sink - m)
    return jnp.dot(p.astype(BF16), v, preferred_element_type=F32) / den


def _store_heads(o_ref, outs):
    for i in range(0, len(outs), 2):
        o_ref[:, i * HEAD_DIM:(i + 2) * HEAD_DIM] = jnp.concatenate(outs[i:i + 2], axis=-1).astype(o_ref.dtype)


def _ctx_attn_kernel(sink_ref, qa_ref, ka_ref, va_ref, qb_ref, kb_ref, vb_ref, oa_o, ob_o):
    scale = HEAD_DIM ** -0.5
    ka = ka_ref[...].astype(BF16)
    va = va_ref[...].astype(BF16)
    outs = []
    for h in range(N_HEADS_A):
        j = h // GROUP_A
        q = qa_ref[:, h * HEAD_DIM:(h + 1) * HEAD_DIM] * scale
        s = lax.dot_general(q, ka[:, j * HEAD_DIM:(j + 1) * HEAD_DIM], _NT, preferred_element_type=F32)
        outs.append(_softmax_pv(s, va[:, j * HEAD_DIM:(j + 1) * HEAD_DIM], sink_ref[h]))
    _store_heads(oa_o, outs)
    kb = kb_ref[...].astype(BF16)
    vb = vb_ref[...].astype(BF16)
    outs = []
    for h in range(N_HEADS_B):
        sl = slice(h * HEAD_DIM, (h + 1) * HEAD_DIM)
        q = qb_ref[:, sl] * scale
        s = lax.dot_general(q, kb[:, sl], _NT, preferred_element_type=F32)
        outs.append(_softmax_pv(s, vb[:, sl]))
    _store_heads(ob_o, outs)


def _ctx_attention(sink, qa, ka, va, qb, kb, vb, seq_len):
    t = qa.shape[0]
    widths = (WIDTH_A, KV_WIDTH_A, KV_WIDTH_A, WIDTH_B, WIDTH_B, WIDTH_B)
    return pl.pallas_call(
        _ctx_attn_kernel,
        out_shape=[jax.ShapeDtypeStruct((t, WIDTH_A), BF16), jax.ShapeDtypeStruct((t, WIDTH_B), BF16)],
        grid=(t // seq_len,),
        in_specs=[pl.BlockSpec(memory_space=pltpu.SMEM)]
        + [pl.BlockSpec((seq_len, w), lambda i: (i, 0)) for w in widths],
        out_specs=[pl.BlockSpec((seq_len, WIDTH_A), lambda i: (i, 0)),
                   pl.BlockSpec((seq_len, WIDTH_B), lambda i: (i, 0))],
        compiler_params=_params(("arbitrary",)),
        name="ctx_attention",
    )(sink, qa, ka, va, qb, kb, vb)


def _win_attn_kernel(sink_ref, qa_ref, ka_ref, va_ref, ck_ref, cv_ref, oa_o, *, n_lat):
    scale = HEAD_DIM ** -0.5
    span = Q_BLK + 2 * WINDOW
    n = pl.program_id(1)
    start = pl.multiple_of(jnp.clip(n * Q_BLK - WINDOW, 0, n_lat - span), Q_BLK)
    kcat = jnp.concatenate([ka_ref[pl.ds(start, span), :], ck_ref[...]], axis=0).astype(BF16)
    vcat = jnp.concatenate([va_ref[pl.ds(start, span), :], cv_ref[...]], axis=0).astype(BF16)
    n_keys = kcat.shape[0]
    qpos = n * Q_BLK + lax.broadcasted_iota(jnp.int32, (Q_BLK, n_keys), 0)
    col = lax.broadcasted_iota(jnp.int32, (Q_BLK, n_keys), 1)
    ok = (col >= span) | (jnp.abs(qpos - (start + col)) <= WINDOW)
    outs = []
    for h in range(N_HEADS_A):
        j = h // GROUP_A
        q = qa_ref[:, h * HEAD_DIM:(h + 1) * HEAD_DIM] * scale
        s = lax.dot_general(q, kcat[:, j * HEAD_DIM:(j + 1) * HEAD_DIM], _NT, preferred_element_type=F32)
        s = jnp.where(ok, s, NEG)
        outs.append(_softmax_pv(s, vcat[:, j * HEAD_DIM:(j + 1) * HEAD_DIM], sink_ref[h]))
    _store_heads(oa_o, outs)


def _win_attention(sink, qa, ka, va, cache_k, cache_v, layer, n_lat):
    t = qa.shape[0]
    nb = n_lat // Q_BLK
    past = cache_k.shape[2]
    return pl.pallas_call(
        functools.partial(_win_attn_kernel, n_lat=n_lat),
        out_shape=jax.ShapeDtypeStruct((t, WIDTH_A), BF16),
        grid=(t // n_lat, nb),
        in_specs=[
            pl.BlockSpec(memory_space=pltpu.SMEM),
            pl.BlockSpec((Q_BLK, WIDTH_A), lambda b, n: (b * nb + n, 0)),
            pl.BlockSpec((n_lat, KV_WIDTH_A), lambda b, n: (b, 0)),
            pl.BlockSpec((n_lat, KV_WIDTH_A), lambda b, n: (b, 0)),
            pl.BlockSpec((None, None, past, KV_WIDTH_A), lambda b, n: (b, layer, 0, 0)),
            pl.BlockSpec((None, None, past, KV_WIDTH_A), lambda b, n: (b, layer, 0, 0)),
        ],
        out_specs=pl.BlockSpec((Q_BLK, WIDTH_A), lambda b, n: (b * nb + n, 0)),
        compiler_params=_params(("arbitrary", "arbitrary")),
        name="window_attention",
    )(sink, qa, ka, va, cache_k, cache_v)


def _nbr_attn_kernel(qb_ref, kb_ref, vb_ref, ck_ref, cv_ref, bias_ref, ob_o, *, rows, kr):
    scale = HEAD_DIM ** -0.5
    r = pl.program_id(1)
    start = pl.multiple_of(jnp.clip(r - kr // 2, 0, rows - kr) * GRID_W, GRID_W)
    nwin = kr * GRID_W
    kcat = jnp.concatenate([kb_ref[pl.ds(start, nwin), :], ck_ref[...]], axis=0).astype(BF16)
    vcat = jnp.concatenate([vb_ref[pl.ds(start, nwin), :], cv_ref[...]], axis=0).astype(BF16)
    past = ck_ref.shape[0]
    outs = []
    for h in range(N_HEADS_B):
        sl = slice(h * HEAD_DIM, (h + 1) * HEAD_DIM)
        q = qb_ref[:, sl] * scale
        s = lax.dot_general(q, kcat[:, sl], _NT, preferred_element_type=F32)
        bias = jnp.concatenate([bias_ref[h], jnp.zeros((GRID_W, past), F32)], axis=-1)
        outs.append(_softmax_pv(s + bias, vcat[:, sl]))
    _store_heads(ob_o, outs)


def _nbr_bias_table(rel_bias, rows):
    kr = min(NA_ROWS, rows)
    cls = np.arange(kr)[:, None]
    m = np.arange(kr)[None, :]
    row_sel = (m - cls + NA_ROWS - 1)[:, :, None] == np.arange(2 * NA_ROWS - 1)[None, None, :]
    c = np.arange(GRID_W)[:, None]
    kc = np.arange(GRID_W)[None, :]
    dc = np.clip(kc - c, -(NA_COLS - 1), NA_COLS - 1) + NA_COLS - 1
    col_sel = dc[:, :, None] == np.arange(2 * NA_COLS - 1)[None, None, :]
    col_start = np.clip(c - NA_COLS // 2, 0, GRID_W - NA_COLS)
    ok = (kc >= col_start) & (kc < col_start + NA_COLS)
    tab = jnp.einsum('hrd,ymr,ckd->hycmk', rel_bias.astype(F32), row_sel.astype(np.float32),
                     col_sel.astype(np.float32), precision=lax.Precision.HIGHEST)
    tab = jnp.where(ok[None, None, :, None, :], tab, NEG)
    return tab.reshape(rel_bias.shape[0], kr, GRID_W, kr * GRID_W)


def _nbr_attention(qb, kb, vb, cache_k, cache_v, bias_tab, layer, n_lat):
    t = qb.shape[0]
    rows = n_lat // GRID_W
    kr = min(NA_ROWS, rows)
    past = cache_k.shape[2]

    def bias_map(b, r):
        return (0, r - jnp.clip(r - kr // 2, 0, rows - kr), 0, 0)

    return pl.pallas_call(
        functools.partial(_nbr_attn_kernel, rows=rows, kr=kr),
        out_shape=jax.ShapeDtypeStruct((t, WIDTH_B), BF16),
        grid=(t // n_lat, rows),
        in_specs=[
            pl.BlockSpec((GRID_W, WIDTH_B), lambda b, r: (b * rows + r, 0)),
            pl.BlockSpec((n_lat, WIDTH_B), lambda b, r: (b, 0)),
            pl.BlockSpec((n_lat, WIDTH_B), lambda b, r: (b, 0)),
            pl.BlockSpec((None, None, past, WIDTH_B), lambda b, r: (b, layer, 0, 0)),
            pl.BlockSpec((None, None, past, WIDTH_B), lambda b, r: (b, layer, 0, 0)),
            pl.BlockSpec((N_HEADS_B, None, GRID_W, kr * GRID_W), bias_map),
        ],
        out_specs=pl.BlockSpec((GRID_W, WIDTH_B), lambda b, r: (b * rows + r, 0)),
        compiler_params=_params(("arbitrary", "arbitrary")),
        name="neighbourhood_attention",
    )(qb, kb, vb, cache_k, cache_v, bias_tab)


ROW_TILE = 8
LANES = 128
assert ROW_TILE * LANES == D_MODEL


def _store_row_tiles(ref, row0, x):
    n = x.shape[0]
    for c in range(ROW_TILE):
        ref[pl.ds(row0 * ROW_TILE + c, n, stride=ROW_TILE), :] = x[:, c * LANES:(c + 1) * LANES]


def _load_row_tiles(ref, row0, n, row_stride=1, chunk0=0):
    return jnp.concatenate(
        [ref[pl.ds(row0 * ROW_TILE + chunk0 + c, n, stride=ROW_TILE * row_stride), :] for c in range(ROW_TILE)],
        axis=-1)


def _outproj_kernel(x_ref, oa_ref, ob_ref, oc_ref, w_ref, mod_ref, g_ref, wr_ref, br_ref, *rest):
    xn_o, h_o, lg_o = rest[-3:]
    mix = jnp.dot(oa_ref[...], w_ref[0:WIDTH_A, :], preferred_element_type=F32)
    mix += jnp.dot(ob_ref[...], w_ref[WIDTH_A:WIDTH_A + WIDTH_B, :], preferred_element_type=F32)
    mix += jnp.dot(oc_ref[...], w_ref[WIDTH_A + WIDTH_B:, :], preferred_element_type=F32)
    xn = x_ref[...] + mod_ref[2:3, :] * mix
    xn_o[...] = xn
    h = xn * lax.rsqrt(jnp.mean(xn * xn, axis=-1, keepdims=True) + EPS) * g_ref[...]
    h = h * (1.0 + mod_ref[4:5, :]) + mod_ref[3:4, :]
    h_hi = h.astype(BF16)
    h_o[...] = h_hi
    h_lo = (h - h_hi.astype(F32)).astype(BF16)
    wr = wr_ref[...]
    w_hi = wr.astype(BF16)
    w_lo = (wr - w_hi.astype(F32)).astype(BF16)
    lg = lax.dot_general(w_hi, h_hi, _NT, preferred_element_type=F32)
    lg += lax.dot_general(w_hi, h_lo, _NT, preferred_element_type=F32)
    lg += lax.dot_general(w_lo, h_hi, _NT, preferred_element_type=F32)
    lg_o[...] = lg + br_ref[...]


def _outproj(x, oa, ob, oc, w_out_bf, mod, group_of_tile, g_ffn, w_router_t, b_router_t, layer, tm, t_all, tok0,
             shared):
    t = x.shape[0]
    b0 = tok0 // tm
    assert tok0 % tm == 0
    in_specs = [
        pl.BlockSpec((tm, D_MODEL), lambda i: (i, 0)),
        pl.BlockSpec((tm, WIDTH_A), lambda i: (i, 0)),
        pl.BlockSpec((tm, WIDTH_B), lambda i: (i, 0)),
        pl.BlockSpec((tm, CONV_CH), lambda i: (i, 0)),
        pl.BlockSpec((None, D_MODEL, D_MODEL), lambda i: (layer, 0, 0)),
        pl.BlockSpec((None, None, N_MOD, D_MODEL), lambda i: (layer, group_of_tile(i), 0, 0)),
        pl.BlockSpec((None, 1, D_MODEL), lambda i: (layer, 0, 0)),
        pl.BlockSpec((None, N_EXPERTS, D_MODEL), lambda i: (layer, 0, 0)),
        pl.BlockSpec((None, N_EXPERTS, 1), lambda i: (layer, 0, 0)),
    ]
    args = [x, oa, ob, oc, w_out_bf, mod, g_ffn, w_router_t, b_router_t]
    aliases = {len(args): 1, len(args) + 1: 2}
    in_specs += [pl.BlockSpec(memory_space=pl.ANY)] * 2
    args += list(shared)
    return pl.pallas_call(
        _outproj_kernel,
        out_shape=[jax.ShapeDtypeStruct((t, D_MODEL), F32), jax.ShapeDtypeStruct((t_all, D_MODEL), BF16),
                   jax.ShapeDtypeStruct((N_EXPERTS, t_all), F32)],
        grid=(t // tm,),
        in_specs=in_specs,
        out_specs=[pl.BlockSpec((tm, D_MODEL), lambda i: (i, 0)),
                   pl.BlockSpec((tm, D_MODEL), lambda i: (b0 + i, 0)),
                   pl.BlockSpec((N_EXPERTS, tm), lambda i: (0, b0 + i))],
        input_output_aliases=aliases,
        compiler_params=_params(("arbitrary",), VMEM_LIMIT_V7X),
        name="outproj_router",
    )(*args)


def _expert_kernel(blk_e_ref, n_used_ref, x_ref, wg_hbm, bg_ref, wu_hbm, bu_ref, wd_hbm, bd_ref, y_ref,
                   st_s, w_buf, wsem, *, n, layer):
    i = pl.program_id(0)
    n_used = n_used_ref[0]

    def weight_copies(e, ws):
        return [pltpu.make_async_copy(w_hbm.at[layer, e], w_buf.at[ws, m], wsem.at[ws])
                for m, w_hbm in enumerate((wg_hbm, wu_hbm, wd_hbm))]

    @pl.when(i == 0)
    def _():
        st_s[0] = 0

    e_cur = blk_e_ref[i]
    first_of_run = (i == 0) | (blk_e_ref[jnp.maximum(i - 1, 0)] != e_cur)

    @pl.when(first_of_run & (i < n_used))
    def _():
        run = st_s[0]
        ws = run % 2

        @pl.when(i == 0)
        def _():
            for cp in weight_copies(e_cur, ws):
                cp.start()

        for cp in weight_copies(e_cur, ws):
            cp.wait()
        nxt = lax.while_loop(lambda j: (j < n_used) & (blk_e_ref[jnp.minimum(j, n - 1)] == e_cur),
                             lambda j: j + 1, i + 1)

        @pl.when(nxt < n_used)
        def _():
            for cp in weight_copies(blk_e_ref[jnp.minimum(nxt, n - 1)], 1 - ws):
                cp.start()

        st_s[0] = run + 1
        st_s[1] = ws

    @pl.when(i < n_used)
    def _():
        ws = st_s[1]
        x = _load_row_tiles(x_ref, 0, MOE_BLK).astype(BF16)
        g = jnp.dot(x, w_buf[ws, 0].astype(BF16), preferred_element_type=F32) + bg_ref[...]
        u = jnp.dot(x, w_buf[ws, 1].astype(BF16), preferred_element_type=F32) + bu_ref[...]
        g = jnp.minimum(g, SWIGLU_LIMIT)
        u = jnp.clip(u, -SWIGLU_LIMIT, SWIGLU_LIMIT)
        a = g * jax.nn.sigmoid(SWIGLU_ALPHA * g) * (u + 1.0)
        y = jnp.dot(a.astype(BF16), w_buf[ws, 2].astype(BF16), preferred_element_type=F32) + bd_ref[...]
        _store_row_tiles(y_ref, 0, y)

    @pl.when(i >= n_used)
    def _():
        y_ref[...] = jnp.zeros(y_ref.shape, F32)


def _experts(blk_e, n_used, x_sorted, w_gate, b_gate, w_up, b_up, w_down, b_down, layer):
    n_blocks = x_sorted.shape[0] // (MOE_BLK * ROW_TILE)
    bspec = pl.BlockSpec((None, None, 1, D_MODEL), lambda i, be, nu: (layer, be[i], 0, 0))
    anyspec = pl.BlockSpec(memory_space=pl.ANY)
    depth = w_gate.shape[0]
    b4 = lambda b: b.reshape(depth, N_EXPERTS, 1, D_MODEL)
    return pl.pallas_call(
        functools.partial(_expert_kernel, n=n_blocks, layer=layer),
        out_shape=jax.ShapeDtypeStruct(x_sorted.shape, F32),
        grid_spec=pltpu.PrefetchScalarGridSpec(
            num_scalar_prefetch=2,
            grid=(n_blocks,),
            in_specs=[
                pl.BlockSpec((MOE_BLK * ROW_TILE, LANES), lambda i, be, nu: (jnp.minimum(i, nu[0] - 1), 0)),
                anyspec, bspec, anyspec, bspec, anyspec, bspec],
            out_specs=pl.BlockSpec((MOE_BLK * ROW_TILE, LANES), lambda i, be, nu: (i, 0)),
            scratch_shapes=[
                pltpu.SMEM((2,), jnp.int32),
                pltpu.VMEM((2, 3, D_MODEL, D_MODEL), F32),
                pltpu.SemaphoreType.DMA((2,)),
            ],
        ),
        compiler_params=_params(("arbitrary",), VMEM_LIMIT_V7X),
        name="experts",
    )(blk_e, n_used, x_sorted, w_gate, b4(b_gate), w_up, b4(b_up), w_down, b4(b_down))


CUM_CHUNK = 256
BLK_LANES = 256


def _sublane_cumsum(x):
    row = lax.broadcasted_iota(jnp.int32, x.shape, 0)
    d = 1
    while d < x.shape[0]:
        x = x + jnp.where(row >= d, pltpu.roll(x, d, axis=0), 0)
        d *= 2
    return x


TOK_TILE = CUM_CHUNK
TILE_ROWS = TOP_K * TOK_TILE
SEG_LANES = 128
SEG_PAD_FROM = SEG_LANES - 2
SEG_PAD_LEN = SEG_LANES - 1
SEG_SRC, SEG_CNT, SEG_DST = 0, 1, 2


def _route_kernel(lg_ref, tri_ref, gate_o, lpos_o, seg_o, blk_o):
    lg = lg_ref[...]
    n_e, t = lg.shape
    e_iota = lax.broadcasted_iota(jnp.int32, lg.shape, 0)
    work = lg
    tops, hots = [], []
    for _ in range(TOP_K):
        m = jnp.max(work, axis=0, keepdims=True)
        first = jnp.min(jnp.where(work == m, e_iota, n_e), axis=0, keepdims=True)
        hot = e_iota == first
        work = jnp.where(hot, -jnp.inf, work)
        tops.append(m)
        hots.append(hot)
    ex = [jnp.exp(m - tops[0]) for m in tops]
    den = ex[0] + ex[1] + ex[2] + ex[3]
    for k in range(TOP_K):
        gate_o[k:k + 1, :] = ex[k] / den

    chosen = jnp.where(hots[0] | hots[1] | hots[2] | hots[3], 1.0, 0.0)
    tri = tri_ref[...]
    lane = lax.broadcasted_iota(jnp.int32, (n_e, SEG_LANES), 1)
    seg_cnt = jnp.zeros((n_e, SEG_LANES), jnp.int32)
    seg_before = jnp.zeros((n_e, SEG_LANES), jnp.int32)
    carry = jnp.zeros((n_e, 1), F32)
    rank_in_tile = []
    n_tiles = t // TOK_TILE
    for c in range(n_tiles):
        chunk = chosen[:, c * TOK_TILE:(c + 1) * TOK_TILE]
        inc = jnp.dot(chunk.astype(BF16), tri, preferred_element_type=F32)
        cnt = inc[:, TOK_TILE - 1:TOK_TILE]
        rank_in_tile.append((inc - chunk).astype(jnp.int32))
        seg_cnt = jnp.where(lane == c, cnt.astype(jnp.int32), seg_cnt)
        seg_before = jnp.where(lane == c, carry.astype(jnp.int32), seg_before)
        carry = carry + cnt

    counts = jnp.broadcast_to(carry.astype(jnp.int32), (n_e, SEG_LANES))
    padded = ((counts + (MOE_BLK - 1)) >> MOE_SHIFT) << MOE_SHIFT
    pad_end = _sublane_cumsum(padded)
    pad_start = pad_end - padded
    seg_src = _sublane_cumsum(seg_cnt) - seg_cnt
    seg_o[SEG_SRC] = seg_src
    seg_o[SEG_CNT] = jnp.where(lane == SEG_PAD_LEN, padded - counts, seg_cnt)
    seg_o[SEG_DST] = jnp.where(lane == SEG_PAD_FROM, pad_start + counts, pad_start + seg_before)

    for c in range(n_tiles):
        pos = seg_src[:, c:c + 1] + rank_in_tile[c]
        for k in range(TOP_K):
            lpos_o[k:k + 1, c * TOK_TILE:(c + 1) * TOK_TILE] = jnp.sum(
                jnp.where(hots[k][:, c * TOK_TILE:(c + 1) * TOK_TILE], pos, 0), axis=0, keepdims=True)

    pad_end_b = jnp.concatenate([pad_end] * (BLK_LANES // SEG_LANES), axis=-1)
    blk_row0 = lax.broadcasted_iota(jnp.int32, (n_e, BLK_LANES), 1) * MOE_BLK
    blk_o[0:1, :] = jnp.minimum(jnp.sum((pad_end_b <= blk_row0).astype(jnp.int32), axis=0, keepdims=True), n_e - 1)
    blk_o[1:2, :] = pad_end_b[n_e - 1:n_e, :] >> MOE_SHIFT
    blk_o[2:8, :] = jnp.zeros((6, BLK_LANES), jnp.int32)


def _route(logits):
    t = logits.shape[1]
    n_blocks = t * TOP_K // MOE_BLK + N_EXPERTS
    assert t % TOK_TILE == 0 and t // TOK_TILE <= SEG_PAD_FROM and n_blocks <= BLK_LANES
    tri = jnp.asarray(np.triu(np.ones((CUM_CHUNK, CUM_CHUNK), np.float32)), dtype=BF16)
    gates, lpos, seg, blk = pl.pallas_call(
        _route_kernel,
        out_shape=[jax.ShapeDtypeStruct((TOP_K, t), F32), jax.ShapeDtypeStruct((TOP_K, t), jnp.int32),
                   jax.ShapeDtypeStruct((3, N_EXPERTS, SEG_LANES), jnp.int32),
                   jax.ShapeDtypeStruct((8, BLK_LANES), jnp.int32)],
        compiler_params=_params(None, VMEM_LIMIT_V7X),
        name="route",
    )(logits, tri)
    return gates, lpos, seg, blk[0, :n_blocks], blk[1, :1]


def _segment_copies(src, dst, src0, dst0, cnt, sem, wait):
    @pl.when(cnt > 0)
    def _():
        size = cnt * ROW_TILE
        cp = pltpu.make_async_copy(
            src.at[pl.ds(pl.multiple_of(src0 * ROW_TILE, ROW_TILE), size)],
            dst.at[pl.ds(pl.multiple_of(dst0 * ROW_TILE, ROW_TILE), size)], sem)
        if wait:
            cp.wait()
        else:
            cp.start()


def _dispatch_kernel(seg_ref, nu_ref, h_ref, lpos_ref, x_hbm, z_buf, zero_buf, sem, zsem, *, n_tiles, n_blocks):
    c = pl.program_id(0)
    slot = c % 2

    def wait_tile(s):
        pltpu.make_async_copy(z_buf.at[s], x_hbm.at[pl.ds(0, TILE_ROWS * ROW_TILE)], sem.at[s]).wait()

    def padding(wait):
        def per_expert(e, carry):
            _segment_copies(zero_buf, x_hbm, 0, seg_ref[SEG_DST, e, SEG_PAD_FROM], seg_ref[SEG_CNT, e, SEG_PAD_LEN],
                            zsem, wait)
            return carry

        lax.fori_loop(0, N_EXPERTS, per_expert, 0)

        def per_block(b, carry):
            cp = pltpu.make_async_copy(
                zero_buf, x_hbm.at[pl.ds(pl.multiple_of(b * (MOE_BLK * ROW_TILE), MOE_BLK * ROW_TILE),
                                         MOE_BLK * ROW_TILE)], zsem)
            if wait:
                cp.wait()
            else:
                cp.start()
            return carry

        lax.fori_loop(nu_ref[0], n_blocks, per_block, 0)

    @pl.when(c == 0)
    def _():
        zero_buf[...] = jnp.zeros(zero_buf.shape, F32)
        padding(False)

    @pl.when(c >= 2)
    def _():
        wait_tile(slot)

    row = lax.broadcasted_iota(jnp.int32, (TILE_ROWS, TOK_TILE), 0)
    hit = row == lpos_ref[0:1, :]
    for k in range(1, TOP_K):
        hit = hit | (row == lpos_ref[k:k + 1, :])
    z = jnp.dot(jnp.where(hit, 1.0, 0.0).astype(BF16), h_ref[...], preferred_element_type=F32)
    _store_row_tiles(z_buf.at[slot], 0, z)

    def per_expert(e, carry):
        _segment_copies(z_buf.at[slot], x_hbm, seg_ref[SEG_SRC, e, c], seg_ref[SEG_DST, e, c], seg_ref[SEG_CNT, e, c],
                        sem.at[slot], False)
        return carry

    lax.fori_loop(0, N_EXPERTS, per_expert, 0)

    @pl.when(c == n_tiles - 1)
    def _():
        if n_tiles > 1:
            wait_tile(1 - slot)
        wait_tile(slot)
        padding(True)


def _dispatch(seg, n_used, h_all, lpos, n_blocks):
    t = h_all.shape[0]
    n_tiles = t // TOK_TILE
    return pl.pallas_call(
        functools.partial(_dispatch_kernel, n_tiles=n_tiles, n_blocks=n_blocks),
        out_shape=jax.ShapeDtypeStruct((n_blocks * MOE_BLK * ROW_TILE, LANES), F32),
        grid_spec=pltpu.PrefetchScalarGridSpec(
            num_scalar_prefetch=2,
            grid=(n_tiles,),
            in_specs=[pl.BlockSpec((TOK_TILE, D_MODEL), lambda c, sg, nu: (c, 0)),
                      pl.BlockSpec((TOP_K, TOK_TILE), lambda c, sg, nu: (0, c))],
            out_specs=pl.BlockSpec(memory_space=pl.ANY),
            scratch_shapes=[
                pltpu.VMEM((2, TILE_ROWS * ROW_TILE, LANES), F32),
                pltpu.VMEM((MOE_BLK * ROW_TILE, LANES), F32),
                pltpu.SemaphoreType.DMA((2,)),
                pltpu.SemaphoreType.DMA,
            ],
        ),
        compiler_params=_params(("arbitrary",), VMEM_LIMIT_V7X),
        name="dispatch",
    )(seg, n_used, h_all, lpos)


def _combine_kernel(seg_ref, x_ref, y_hbm, lpos_ref, lpos_t_ref, gate_ref, mod_ref, o_ref, y_buf, sem, *, tile0,
                    n_tiles):
    c = pl.program_id(0)
    slot = c % 2

    def fetch(cc):
        def per_expert(e, carry):
            _segment_copies(y_hbm, y_buf.at[cc % 2], seg_ref[SEG_DST, e, tile0 + cc], seg_ref[SEG_SRC, e, tile0 + cc],
                            seg_ref[SEG_CNT, e, tile0 + cc], sem.at[cc % 2], False)
            return carry

        lax.fori_loop(0, N_EXPERTS, per_expert, 0)

    @pl.when(c == 0)
    def _():
        fetch(0)

    @pl.when(c + 1 < n_tiles)
    def _():
        fetch(c + 1)

    pltpu.make_async_copy(y_hbm.at[pl.ds(0, TILE_ROWS * ROW_TILE)], y_buf.at[slot], sem.at[slot]).wait()

    row = lax.broadcasted_iota(jnp.int32, (TILE_ROWS, TOK_TILE), 0)
    row_gate = jnp.zeros((TILE_ROWS, 1), F32)
    for k in range(TOP_K):
        row_gate += jnp.sum(jnp.where(row == lpos_ref[k:k + 1, :], gate_ref[k:k + 1, :], 0.0), axis=1, keepdims=True)
    ys = _load_row_tiles(y_buf.at[slot], 0, TILE_ROWS) * row_gate
    ys_hi = ys.astype(BF16)
    ys_lo = (ys - ys_hi.astype(F32)).astype(BF16)
    col = lax.broadcasted_iota(jnp.int32, (TOK_TILE, TILE_ROWS), 1)
    hit = col == lpos_t_ref[:, 0:1]
    for k in range(1, TOP_K):
        hit = hit | (col == lpos_t_ref[:, k:k + 1])
    u = jnp.where(hit, 1.0, 0.0).astype(BF16)
    mix = jnp.dot(u, ys_hi, preferred_element_type=F32) + jnp.dot(u, ys_lo, preferred_element_type=F32)
    o_ref[...] = x_ref[...] + mod_ref[5:6, :] * mix


def _combine(seg, x_mid, y_sorted, lpos, lpos_t, gates, mod, group_of_tile, layer, tok0):
    t = x_mid.shape[0]
    tile0 = tok0 // TOK_TILE
    n_tiles = t // TOK_TILE
    return pl.pallas_call(
        functools.partial(_combine_kernel, tile0=tile0, n_tiles=n_tiles),
        out_shape=jax.ShapeDtypeStruct((t, D_MODEL), F32),
        grid_spec=pltpu.PrefetchScalarGridSpec(
            num_scalar_prefetch=1,
            grid=(n_tiles,),
            in_specs=[
                pl.BlockSpec((TOK_TILE, D_MODEL), lambda c, sg: (c, 0)),
                pl.BlockSpec(memory_space=pl.ANY),
                pl.BlockSpec((TOP_K, TOK_TILE), lambda c, sg: (0, tile0 + c)),
                pl.BlockSpec((TOK_TILE, TOP_K), lambda c, sg: (tile0 + c, 0)),
                pl.BlockSpec((TOP_K, TOK_TILE), lambda c, sg: (0, tile0 + c)),
                pl.BlockSpec((None, None, N_MOD, D_MODEL), lambda c, sg: (layer, group_of_tile(c), 0, 0)),
            ],
            out_specs=pl.BlockSpec((TOK_TILE, D_MODEL), lambda c, sg: (c, 0)),
            scratch_shapes=[pltpu.VMEM((2, TILE_ROWS * ROW_TILE, LANES), F32), pltpu.SemaphoreType.DMA((2,))],
        ),
        compiler_params=_params(("arbitrary",), VMEM_LIMIT_V7X),
        name="combine",
    )(seg, x_mid, y_sorted, lpos, lpos_t, gates, mod)


def _rope_tables(n_lat):
    quarter = HEAD_DIM // 4
    t = jnp.arange(n_lat)
    inv = ROPE_BASE ** (-jnp.arange(quarter, dtype=F32) / quarter)
    ang_r = (t // GRID_W).astype(F32)[:, None] * inv
    ang_c = (t % GRID_W).astype(F32)[:, None] * inv
    cos = jnp.concatenate([jnp.cos(ang_r)] * 2 + [jnp.cos(ang_c)] * 2, axis=-1)
    sin = jnp.concatenate([-jnp.sin(ang_r), jnp.sin(ang_r), -jnp.sin(ang_c), jnp.sin(ang_c)], axis=-1)
    return jnp.concatenate([cos, cos], axis=-1), jnp.concatenate([sin, sin], axis=-1)


def _block_diag_ones():
    idx = np.arange(MXU_COLS_V7X) // HEAD_DIM
    return jnp.asarray(idx[:, None] == idx[None, :], dtype=BF16)


def kernel(x_prompt, x_sample, cache_k_win, cache_v_win, cache_k_nbr, cache_v_nbr, c, c_ctx, w_mod, b_mod, g_mix, g_ffn, w_in, w_out, qn_win, kn_win, qn_nbr, kn_nbr, sink_win, rel_bias_nbr, conv_w, w_router, b_router, w_gate, b_gate, w_up, b_up, w_down, b_down):
    bsz, n_ctx, d = x_prompt.shape
    dbs, n_lat, _ = x_sample.shape
    depth = w_in.shape[0]
    past = cache_k_win.shape[2]
    assert d == D_MODEL and dbs + 1 <= COND_ROWS and n_lat % GRID_W == 0 and n_lat >= Q_BLK + 2 * WINDOW
    t_ctx, t_lat = bsz * n_ctx, dbs * n_lat

    cond = jnp.concatenate([c_ctx[None], c, jnp.zeros((COND_ROWS - 1 - dbs, d), F32)], axis=0)
    mod = _modulation(cond, w_mod, b_mod)

    w_in_bf = w_in.astype(BF16)
    w_out_bf = w_out.astype(BF16)
    norm_w = jnp.concatenate([jnp.tile(qn_win, (1, N_HEADS_A)), jnp.tile(kn_win, (1, N_KV_A)),
                              jnp.tile(qn_nbr, (1, N_HEADS_B)), jnp.tile(kn_nbr, (1, N_HEADS_B))], axis=-1)[:, None, :]
    ones_bd = _block_diag_ones()
    rope_tabs = _rope_tables(n_lat)
    g_mix3, g_ffn3 = g_mix[:, None, :], g_ffn[:, None, :]
    w_router_t = jnp.swapaxes(w_router, 1, 2)
    b_router_t = b_router[:, :, None]
    t_all = t_ctx + t_lat
    ck_win = cache_k_win.reshape(dbs, depth, past, KV_WIDTH_A)
    cv_win = cache_v_win.reshape(dbs, depth, past, KV_WIDTH_A)
    ck_nbr = cache_k_nbr.reshape(dbs, depth, past, WIDTH_B)
    cv_nbr = cache_v_nbr.reshape(dbs, depth, past, WIDTH_B)

    tm_ctx = 2 * n_ctx
    tm_lat = 512
    ctx_group = lambda i: 0
    lat_group_in = lambda i: 1 + i
    lat_group_out = lambda i: 1 + (i * tm_lat) // n_lat
    lat_group_comb = lambda i: 1 + (i * TOK_TILE) // n_lat
    assert t_ctx % TOK_TILE == 0 and n_lat % TOK_TILE == 0

    xp = x_prompt.reshape(t_ctx, d)
    xs = x_sample.reshape(t_lat, d)
    caches = [[], [], [], []]
    for l in range(depth):
        qa, qb, ka, va, kb, vb, oc = _inproj(xp, mod, ctx_group, g_mix3, w_in_bf, l, norm_w, ones_bd, conv_w,
                                             tm_ctx, n_ctx)
        oa, ob = _ctx_attention(sink_win[l], qa, ka, va, qb, kb, vb, n_ctx)
        shared = (jnp.zeros((t_all, d), BF16), jnp.zeros((N_EXPERTS, t_all), F32))
        xp_mid, h_all, lg_all = _outproj(xp, oa, ob, oc, w_out_bf, mod, ctx_group, g_ffn3, w_router_t, b_router_t, l,
                                         tm_ctx, t_all, 0, shared)
        for lst, a in zip(caches, (ka, va, kb, vb)):
            lst.append(a)

        qa, qb, ka, va, kb, vb, oc = _inproj(xs, mod, lat_group_in, g_mix3, w_in_bf, l, norm_w, ones_bd, conv_w,
                                             n_lat, n_lat, rope_tabs)
        oa = _win_attention(sink_win[l], qa, ka, va, ck_win, cv_win, l, n_lat)
        ob = _nbr_attention(qb, kb, vb, ck_nbr, cv_nbr, _nbr_bias_table(rel_bias_nbr[l], n_lat // GRID_W), l, n_lat)
        xs_mid, h_all, lg_all = _outproj(xs, oa, ob, oc, w_out_bf, mod, lat_group_out, g_ffn3, w_router_t, b_router_t,
                                         l, tm_lat, t_all, t_ctx, shared=(h_all, lg_all))

        gates, lpos, seg, blk_e, n_used = _route(lg_all)
        x_sorted = _dispatch(seg, n_used, h_all, lpos, blk_e.shape[0])
        y_sorted = _experts(blk_e, n_used, x_sorted, w_gate, b_gate, w_up, b_up, w_down, b_down, l)
        lpos_t = lpos.T
        xp = _combine(seg, xp_mid, y_sorted, lpos, lpos_t, gates, mod, ctx_group, l, 0)
        xs = _combine(seg, xs_mid, y_sorted, lpos, lpos_t, gates, mod, lat_group_comb, l, t_ctx)

    new_v_win = jnp.stack([a.reshape(bsz, n_ctx, KV_WIDTH_A) for a in caches[1]], axis=1).reshape(
        bsz, depth, n_ctx, N_KV_A, HEAD_DIM)
    new_k_nbr = jnp.stack([a.reshape(bsz, n_ctx, WIDTH_B) for a in caches[2]], axis=1).reshape(
        bsz, depth, n_ctx, N_HEADS_B, HEAD_DIM)
    new_v_nbr = jnp.stack([a.reshape(bsz, n_ctx, WIDTH_B) for a in caches[3]], axis=1).reshape(
        bsz, depth, n_ctx, N_HEADS_B, HEAD_DIM)
    new_k_win = jnp.stack([a.reshape(bsz, n_ctx, KV_WIDTH_A) for a in caches[0]], axis=1).reshape(
        bsz, depth, n_ctx, N_KV_A, HEAD_DIM)
    return (xp.reshape(bsz, n_ctx, d), xs.reshape(dbs, n_lat, d), new_k_win, new_v_win, new_k_nbr, new_v_nbr)
```

```python
import functools

import numpy as np
import jax
import jax.numpy as jnp
from jax import lax
from jax.experimental import pallas as pl
from jax.experimental.pallas import tpu as pltpu

F32 = jnp.float32
BF16 = jnp.bfloat16

D_MODEL = 1024
HEAD_DIM = 64
GRID_W = 64
N_HEADS_A = 8
N_KV_A = 2
GROUP_A = N_HEADS_A // N_KV_A
WINDOW = 128
Q_BLK = 128
N_HEADS_B = 4
NA_ROWS = 8
NA_COLS = 16
CONV_CH = 256
CONV_W = 3
WIDTH_A = N_HEADS_A * HEAD_DIM
KV_WIDTH_A = N_KV_A * HEAD_DIM
WIDTH_B = N_HEADS_B * HEAD_DIM
IN_COLS = WIDTH_A + 2 * KV_WIDTH_A + 3 * WIDTH_B + 3 * CONV_CH
N_EXPERTS = 32
TOP_K = 4
SWIGLU_LIMIT = 7.0
SWIGLU_ALPHA = 1.702
ROPE_BASE = 10000.0
EPS = 1e-6
NEG = -1e30
N_MOD = 6

C_QA = 0
C_KA = C_QA + WIDTH_A
C_VA = C_KA + KV_WIDTH_A
C_QB = C_VA + KV_WIDTH_A
C_KB = C_QB + WIDTH_B
C_VB = C_KB + WIDTH_B
C_U = C_VB + WIDTH_B
C_GB = C_U + CONV_CH
C_GC = C_GB + CONV_CH

MXU_COLS_V7X = 256
COND_ROWS = 8
MOE_BLK = 256
MOE_SHIFT = MOE_BLK.bit_length() - 1
assert 1 << MOE_SHIFT == MOE_BLK
VMEM_LIMIT_V7X = 56 * 1024 * 1024

_NT = (((1,), (1,)), ((), ()))


def _params(sem, vmem=None):
    return pltpu.CompilerParams(dimension_semantics=sem, vmem_limit_bytes=vmem)


def _mod_kernel(c_ref, w_ref, b_ref, o_ref):
    c = c_ref[...]
    s = c * jax.nn.sigmoid(c)
    o_ref[...] = jnp.dot(s.astype(BF16), w_ref[...].astype(BF16), preferred_element_type=F32) + b_ref[...]


def _modulation(cond, w_mod, b_mod):
    depth = w_mod.shape[0]
    out = pl.pallas_call(
        _mod_kernel,
        out_shape=jax.ShapeDtypeStruct((depth, COND_ROWS, N_MOD * D_MODEL), F32),
        grid=(depth, N_MOD),
        in_specs=[
            pl.BlockSpec((COND_ROWS, D_MODEL), lambda l, j: (0, 0)),
            pl.BlockSpec((None, D_MODEL, D_MODEL), lambda l, j: (l, 0, j)),
            pl.BlockSpec((None, 1, D_MODEL), lambda l, j: (l, 0, j)),
        ],
        out_specs=pl.BlockSpec((None, COND_ROWS, D_MODEL), lambda l, j: (l, 0, j)),
        compiler_params=_params(("arbitrary", "arbitrary")),
        name="modulation",
    )(cond, w_mod, b_mod.reshape(depth, 1, N_MOD * D_MODEL))
    return out.reshape(depth, COND_ROWS, N_MOD, D_MODEL)


def _head_norm(x, w_row, ones_ref):
    width = x.shape[1]
    sq = (x * x).astype(BF16)
    parts = []
    for c0 in range(0, width, MXU_COLS_V7X):
        wd = min(MXU_COLS_V7X, width - c0)
        parts.append(jnp.dot(sq[:, c0:c0 + wd], ones_ref[:wd, :wd], preferred_element_type=F32))
    ss = parts[0] if len(parts) == 1 else jnp.concatenate(parts, axis=-1)
    return x * lax.rsqrt(ss * (1.0 / HEAD_DIM) + EPS) * w_row


def _rope(x, cos, sin):
    width = x.shape[1]
    lane = lax.broadcasted_iota(jnp.int32, x.shape, 1)
    quarter = HEAD_DIM // 4
    partner = jnp.where((lane % (2 * quarter)) < quarter,
                        pltpu.roll(x, width - quarter, axis=1), pltpu.roll(x, quarter, axis=1))
    reps = width // cos.shape[1]
    cos_w = cos if reps == 1 else jnp.concatenate([cos] * reps, axis=-1)
    sin_w = sin if reps == 1 else jnp.concatenate([sin] * reps, axis=-1)
    return x * cos_w + partner * sin_w


def _inproj_kernel(*refs, seq_len, rope):
    if rope:
        (x_ref, mod_ref, g_ref, w_ref, nw_ref, ones_ref, cw_ref, cos_ref, sin_ref,
         qa_o, qb_o, ka_o, va_o, kb_o, vb_o, oc_o) = refs
    else:
        (x_ref, mod_ref, g_ref, w_ref, nw_ref, ones_ref, cw_ref,
         qa_o, qb_o, ka_o, va_o, kb_o, vb_o, oc_o) = refs
    x = x_ref[...]
    tm = x.shape[0]
    h = x * lax.rsqrt(jnp.mean(x * x, axis=-1, keepdims=True) + EPS) * g_ref[...]
    h = h * (1.0 + mod_ref[1:2, :]) + mod_ref[0:1, :]
    p = jnp.dot(h.astype(BF16), w_ref[...], preferred_element_type=F32)

    qa = _head_norm(p[:, C_QA:C_KA], nw_ref[:, 0:WIDTH_A], ones_ref)
    ka = _head_norm(p[:, C_KA:C_VA], nw_ref[:, WIDTH_A:WIDTH_A + KV_WIDTH_A], ones_ref)
    o_qb = WIDTH_A + KV_WIDTH_A
    qb = _head_norm(p[:, C_QB:C_KB], nw_ref[:, o_qb:o_qb + WIDTH_B], ones_ref)
    kb = _head_norm(p[:, C_KB:C_VB], nw_ref[:, o_qb + WIDTH_B:o_qb + 2 * WIDTH_B], ones_ref)
    if rope:
        cos, sin = cos_ref[...], sin_ref[...]
        qa = _rope(qa, cos, sin)
        ka = _rope(ka, cos, sin)
    qa_o[...] = qa.astype(BF16)
    qb_o[...] = qb.astype(BF16)
    ka_o[...] = ka
    va_o[...] = p[:, C_VA:C_QB]
    kb_o[...] = kb
    vb_o[...] = p[:, C_VB:C_U]

    z = p[:, C_GC:C_GC + CONV_CH] * p[:, C_U:C_GB]
    row = lax.broadcasted_iota(jnp.int32, z.shape, 0) % seq_len
    z_prev = jnp.where(row == 0, 0.0, pltpu.roll(z, 1, axis=0))
    z_next = jnp.where(row == seq_len - 1, 0.0, pltpu.roll(z, tm - 1, axis=0))
    y = z_prev * cw_ref[0:1, :] + z * cw_ref[1:2, :] + z_next * cw_ref[2:3, :]
    oc_o[...] = (p[:, C_GB:C_GC] * y).astype(BF16)


def _inproj(x, mod, group_of_tile, g_mix, w_in_bf, layer, norm_w, ones_bd, conv_w, tm, seq_len, rope_tabs=None):
    t = x.shape[0]
    rope = rope_tabs is not None
    in_specs = [
        pl.BlockSpec((tm, D_MODEL), lambda i: (i, 0)),
        pl.BlockSpec((None, None, N_MOD, D_MODEL), lambda i: (layer, group_of_tile(i), 0, 0)),
        pl.BlockSpec((None, 1, D_MODEL), lambda i: (layer, 0, 0)),
        pl.BlockSpec((None, D_MODEL, IN_COLS), lambda i: (layer, 0, 0)),
        pl.BlockSpec((None, 1, norm_w.shape[-1]), lambda i: (layer, 0, 0)),
        pl.BlockSpec(ones_bd.shape, lambda i: (0, 0)),
        pl.BlockSpec((None, CONV_W, CONV_CH), lambda i: (layer, 0, 0)),
    ]
    args = [x, mod, g_mix, w_in_bf, norm_w, ones_bd, conv_w]
    if rope:
        in_specs += [pl.BlockSpec(rope_tabs[0].shape, lambda i: (0, 0))] * 2
        args += list(rope_tabs)
    widths = (WIDTH_A, WIDTH_B, KV_WIDTH_A, KV_WIDTH_A, WIDTH_B, WIDTH_B, CONV_CH)
    dtypes = (BF16, BF16, F32, F32, F32, F32, BF16)
    return pl.pallas_call(
        functools.partial(_inproj_kernel, seq_len=seq_len, rope=rope),
        out_shape=[jax.ShapeDtypeStruct((t, w), dt) for w, dt in zip(widths, dtypes)],
        grid=(t // tm,),
        in_specs=in_specs,
        out_specs=[pl.BlockSpec((tm, w), lambda i: (i, 0)) for w in widths],
        compiler_params=_params(("arbitrary",), VMEM_LIMIT_V7X),
        name="inproj_rope" if rope else "inproj",
    )(*args)


def _pair_attention(q2, k_lo, k_hi, v_lo, v_hi, sinks=None, adjust=None):
    m = k_lo.shape[0]
    k2 = jnp.concatenate([k_lo, k_hi], axis=0)
    v2 = jnp.concatenate([v_lo, v_hi], axis=0)
    s = lax.dot_general(q2, k2, _NT, preferred_element_type=F32)
    ps, dens = [], []
    for i in range(2):
        si = s[:, i * m:(i + 1) * m]
        if adjust is not None:
            si = adjust(i, si)
        mx = jnp.max(si, axis=-1, keepdims=True)
        if sinks is not None:
            mx = jnp.maximum(mx, sinks[i])
        p = jnp.exp(si - mx)
        den = jnp.sum(p, axis=-1, keepdims=True)
        if sinks is not None:
            den = den + jnp.exp(sinks[i] - mx)
        ps.append(p.astype(BF16))
        dens.append(den)
    o = jnp.dot(jnp.concatenate(ps, axis=-1), v2, preferred_element_type=F32)
    lane = lax.broadcasted_iota(jnp.int32, o.shape, 1)
    return o / jnp.where(lane < HEAD_DIM, dens[0], dens[1])


def _lane_halves(x):
    lane = lax.broadcasted_iota(jnp.int32, x.shape, 1)
    lo = lane < HEAD_DIM
    sw = pltpu.roll(x, HEAD_DIM, axis=1)
    z = jnp.zeros_like(x)
    return tuple(jnp.where(c, y, z).astype(BF16) for c, y in ((lo, x), (~lo, x), (lo, sw), (~lo, sw)))


def _ctx_attn_kernel(sink_ref, qa_ref, ka_ref, va_ref, qb_ref, kb_ref, vb_ref, oa_o, ob_o):
    scale = HEAD_DIM ** -0.5
    k0l, k1h, k1l, k0h = _lane_halves(ka_ref[...])
    v0l, v1h, v1l, v0h = _lane_halves(va_ref[...])
    for i in range(N_HEADS_A // 2):
        cols = slice(2 * i * HEAD_DIM, (2 * i + 2) * HEAD_DIM)
        kv = (k0l, k0h, v0l, v0h) if (2 * i) // GROUP_A == 0 else (k1l, k1h, v1l, v1h)
        o = _pair_attention(qa_ref[:, cols] * scale, *kv, sinks=(sink_ref[2 * i], sink_ref[2 * i + 1]))
        oa_o[:, cols] = o.astype(oa_o.dtype)
    for i in range(N_HEADS_B // 2):
        cols = slice(2 * i * HEAD_DIM, (2 * i + 2) * HEAD_DIM)
        kl, kh, _, _ = _lane_halves(kb_ref[:, cols])
        vl, vh, _, _ = _lane_halves(vb_ref[:, cols])
        o = _pair_attention(qb_ref[:, cols] * scale, kl, kh, vl, vh)
        ob_o[:, cols] = o.astype(ob_o.dtype)


def _ctx_attention(sink, qa, ka, va, qb, kb, vb, seq_len):
    t = qa.shape[0]
    widths = (WIDTH_A, KV_WIDTH_A, KV_WIDTH_A, WIDTH_B, WIDTH_B, WIDTH_B)
    return pl.pallas_call(
        _ctx_attn_kernel,
        out_shape=[jax.ShapeDtypeStruct((t, WIDTH_A), BF16), jax.ShapeDtypeStruct((t, WIDTH_B), BF16)],
        grid=(t // seq_len,),
        in_specs=[pl.BlockSpec(memory_space=pltpu.SMEM)]
        + [pl.BlockSpec((seq_len, w), lambda i: (i, 0)) for w in widths],
        out_specs=[pl.BlockSpec((seq_len, WIDTH_A), lambda i: (i, 0)),
                   pl.BlockSpec((seq_len, WIDTH_B), lambda i: (i, 0))],
        compiler_params=_params(("arbitrary",)),
        name="ctx_attention",
    )(sink, qa, ka, va, qb, kb, vb)


def _win_attn_kernel(sink_ref, qa_ref, ka_ref, va_ref, ck_ref, cv_ref, oa_o, *, n_lat):
    scale = HEAD_DIM ** -0.5
    span = Q_BLK + 2 * WINDOW
    n = pl.program_id(1)
    start = pl.multiple_of(jnp.clip(n * Q_BLK - WINDOW, 0, n_lat - span), Q_BLK)
    k0l, k1h, k1l, k0h = _lane_halves(jnp.concatenate([ka_ref[pl.ds(start, span), :], ck_ref[...]], axis=0))
    v0l, v1h, v1l, v0h = _lane_halves(jnp.concatenate([va_ref[pl.ds(start, span), :], cv_ref[...]], axis=0))
    n_keys = k0l.shape[0]
    qpos = n * Q_BLK + lax.broadcasted_iota(jnp.int32, (Q_BLK, n_keys), 0)
    col = lax.broadcasted_iota(jnp.int32, (Q_BLK, n_keys), 1)
    ok = (col >= span) | (jnp.abs(qpos - (start + col)) <= WINDOW)
    for i in range(N_HEADS_A // 2):
        cols = slice(2 * i * HEAD_DIM, (2 * i + 2) * HEAD_DIM)
        kv = (k0l, k0h, v0l, v0h) if (2 * i) // GROUP_A == 0 else (k1l, k1h, v1l, v1h)
        o = _pair_attention(qa_ref[:, cols] * scale, *kv, sinks=(sink_ref[2 * i], sink_ref[2 * i + 1]),
                            adjust=lambda _, s: jnp.where(ok, s, NEG))
        oa_o[:, cols] = o.astype(oa_o.dtype)


def _win_attention(sink, qa, ka, va, cache_k, cache_v, layer, n_lat):
    t = qa.shape[0]
    nb = n_lat // Q_BLK
    past = cache_k.shape[2]
    return pl.pallas_call(
        functools.partial(_win_attn_kernel, n_lat=n_lat),
        out_shape=jax.ShapeDtypeStruct((t, WIDTH_A), BF16),
        grid=(t // n_lat, nb),
        in_specs=[
            pl.BlockSpec(memory_space=pltpu.SMEM),
            pl.BlockSpec((Q_BLK, WIDTH_A), lambda b, n: (b * nb + n, 0)),
            pl.BlockSpec((n_lat, KV_WIDTH_A), lambda b, n: (b, 0)),
            pl.BlockSpec((n_lat, KV_WIDTH_A), lambda b, n: (b, 0)),
            pl.BlockSpec((None, None, past, KV_WIDTH_A), lambda b, n: (b, layer, 0, 0)),
            pl.BlockSpec((None, None, past, KV_WIDTH_A), lambda b, n: (b, layer, 0, 0)),
        ],
        out_specs=pl.BlockSpec((Q_BLK, WIDTH_A), lambda b, n: (b * nb + n, 0)),
        compiler_params=_params(("arbitrary", "arbitrary")),
        name="window_attention",
    )(sink, qa, ka, va, cache_k, cache_v)


def _nbr_attn_kernel(qb_ref, kb_ref, vb_ref, ck_ref, cv_ref, bias_ref, ob_o, *, rows, kr):
    scale = HEAD_DIM ** -0.5
    r = pl.program_id(1)
    start = pl.multiple_of(jnp.clip(r - kr // 2, 0, rows - kr) * GRID_W, GRID_W)
    nwin = kr * GRID_W
    kcat = jnp.concatenate([kb_ref[pl.ds(start, nwin), :], ck_ref[...]], axis=0)
    vcat = jnp.concatenate([vb_ref[pl.ds(start, nwin), :], cv_ref[...]], axis=0)
    past = ck_ref.shape[0]
    for i in range(N_HEADS_B // 2):
        cols = slice(2 * i * HEAD_DIM, (2 * i + 2) * HEAD_DIM)
        kl, kh, _, _ = _lane_halves(kcat[:, cols])
        vl, vh, _, _ = _lane_halves(vcat[:, cols])
        o = _pair_attention(
            qb_ref[:, cols] * scale, kl, kh, vl, vh,
            adjust=lambda hh, s, i=i: s + jnp.concatenate([bias_ref[2 * i + hh], jnp.zeros((GRID_W, past), F32)],
                                                          axis=-1))
        ob_o[:, cols] = o.astype(ob_o.dtype)


def _nbr_bias_table(rel_bias, rows):
    kr = min(NA_ROWS, rows)
    cls = np.arange(kr)[:, None]
    m = np.arange(kr)[None, :]
    row_sel = (m - cls + NA_ROWS - 1)[:, :, None] == np.arange(2 * NA_ROWS - 1)[None, None, :]
    c = np.arange(GRID_W)[:, None]
    kc = np.arange(GRID_W)[None, :]
    dc = np.clip(kc - c, -(NA_COLS - 1), NA_COLS - 1) + NA_COLS - 1
    col_sel = dc[:, :, None] == np.arange(2 * NA_COLS - 1)[None, None, :]
    col_start = np.clip(c - NA_COLS // 2, 0, GRID_W - NA_COLS)
    ok = (kc >= col_start) & (kc < col_start + NA_COLS)
    tab = jnp.einsum('hrd,ymr,ckd->hycmk', rel_bias.astype(F32), row_sel.astype(np.float32),
                     col_sel.astype(np.float32), precision=lax.Precision.HIGHEST)
    tab = jnp.where(ok[None, None, :, None, :], tab, NEG)
    return tab.reshape(rel_bias.shape[0], kr, GRID_W, kr * GRID_W)


def _nbr_attention(qb, kb, vb, cache_k, cache_v, bias_tab, layer, n_lat):
    t = qb.shape[0]
    rows = n_lat // GRID_W
    kr = min(NA_ROWS, rows)
    past = cache_k.shape[2]

    def bias_map(b, r):
        return (0, r - jnp.clip(r - kr // 2, 0, rows - kr), 0, 0)

    return pl.pallas_call(
        functools.partial(_nbr_attn_kernel, rows=rows, kr=kr),
        out_shape=jax.ShapeDtypeStruct((t, WIDTH_B), BF16),
        grid=(t // n_lat, rows),
        in_specs=[
            pl.BlockSpec((GRID_W, WIDTH_B), lambda b, r: (b * rows + r, 0)),
            pl.BlockSpec((n_lat, WIDTH_B), lambda b, r: (b, 0)),
            pl.BlockSpec((n_lat, WIDTH_B), lambda b, r: (b, 0)),
            pl.BlockSpec((None, None, past, WIDTH_B), lambda b, r: (b, layer, 0, 0)),
            pl.BlockSpec((None, None, past, WIDTH_B), lambda b, r: (b, layer, 0, 0)),
            pl.BlockSpec((N_HEADS_B, None, GRID_W, kr * GRID_W), bias_map),
        ],
        out_specs=pl.BlockSpec((GRID_W, WIDTH_B), lambda b, r: (b * rows + r, 0)),
        compiler_params=_params(("arbitrary", "arbitrary")),
        name="neighbourhood_attention",
    )(qb, kb, vb, cache_k, cache_v, bias_tab)


ROW_TILE = 8
LANES = 128
assert ROW_TILE * LANES == D_MODEL


def _store_row_tiles(ref, row0, x):
    n = x.shape[0]
    for c in range(ROW_TILE):
        ref[pl.ds(row0 * ROW_TILE + c, n, stride=ROW_TILE), :] = x[:, c * LANES:(c + 1) * LANES]


def _load_row_tiles(ref, row0, n, row_stride=1, chunk0=0):
    return jnp.concatenate(
        [ref[pl.ds(row0 * ROW_TILE + chunk0 + c, n, stride=ROW_TILE * row_stride), :] for c in range(ROW_TILE)],
        axis=-1)


def _outproj_kernel(x_ref, oa_ref, ob_ref, oc_ref, w_ref, mod_ref, g_ref, wr_ref, br_ref, *rest):
    xn_o, h_o, lg_o = rest[-3:]
    mix = jnp.dot(oa_ref[...], w_ref[0:WIDTH_A, :], preferred_element_type=F32)
    mix += jnp.dot(ob_ref[...], w_ref[WIDTH_A:WIDTH_A + WIDTH_B, :], preferred_element_type=F32)
    mix += jnp.dot(oc_ref[...], w_ref[WIDTH_A + WIDTH_B:, :], preferred_element_type=F32)
    xn = x_ref[...] + mod_ref[2:3, :] * mix
    xn_o[...] = xn
    h = xn * lax.rsqrt(jnp.mean(xn * xn, axis=-1, keepdims=True) + EPS) * g_ref[...]
    h = h * (1.0 + mod_ref[4:5, :]) + mod_ref[3:4, :]
    h_hi = h.astype(BF16)
    h_o[...] = h_hi
    h_lo = (h - h_hi.astype(F32)).astype(BF16)
    wr = wr_ref[...]
    w_hi = wr.astype(BF16)
    w_lo = (wr - w_hi.astype(F32)).astype(BF16)
    lg = lax.dot_general(w_hi, h_hi, _NT, preferred_element_type=F32)
    lg += lax.dot_general(w_hi, h_lo, _NT, preferred_element_type=F32)
    lg += lax.dot_general(w_lo, h_hi, _NT, preferred_element_type=F32)
    lg_o[...] = lg + br_ref[...]


def _outproj(x, oa, ob, oc, w_out_bf, mod, group_of_tile, g_ffn, w_router_t, b_router_t, layer, tm, t_all, tok0,
             shared):
    t = x.shape[0]
    b0 = tok0 // tm
    assert tok0 % tm == 0
    in_specs = [
        pl.BlockSpec((tm, D_MODEL), lambda i: (i, 0)),
        pl.BlockSpec((tm, WIDTH_A), lambda i: (i, 0)),
        pl.BlockSpec((tm, WIDTH_B), lambda i: (i, 0)),
        pl.BlockSpec((tm, CONV_CH), lambda i: (i, 0)),
        pl.BlockSpec((None, D_MODEL, D_MODEL), lambda i: (layer, 0, 0)),
        pl.BlockSpec((None, None, N_MOD, D_MODEL), lambda i: (layer, group_of_tile(i), 0, 0)),
        pl.BlockSpec((None, 1, D_MODEL), lambda i: (layer, 0, 0)),
        pl.BlockSpec((None, N_EXPERTS, D_MODEL), lambda i: (layer, 0, 0)),
        pl.BlockSpec((None, N_EXPERTS, 1), lambda i: (layer, 0, 0)),
    ]
    args = [x, oa, ob, oc, w_out_bf, mod, g_ffn, w_router_t, b_router_t]
    aliases = {len(args): 1, len(args) + 1: 2}
    in_specs += [pl.BlockSpec(memory_space=pl.ANY)] * 2
    args += list(shared)
    return pl.pallas_call(
        _outproj_kernel,
        out_shape=[jax.ShapeDtypeStruct((t, D_MODEL), F32), jax.ShapeDtypeStruct((t_all, D_MODEL), BF16),
                   jax.ShapeDtypeStruct((N_EXPERTS, t_all), F32)],
        grid=(t // tm,),
        in_specs=in_specs,
        out_specs=[pl.BlockSpec((tm, D_MODEL), lambda i: (i, 0)),
                   pl.BlockSpec((tm, D_MODEL), lambda i: (b0 + i, 0)),
                   pl.BlockSpec((N_EXPERTS, tm), lambda i: (0, b0 + i))],
        input_output_aliases=aliases,
        compiler_params=_params(("arbitrary",), VMEM_LIMIT_V7X),
        name="outproj_router",
    )(*args)


def _expert_kernel(blk_e_ref, n_used_ref, x_ref, wg_hbm, bg_ref, wu_hbm, bu_ref, wd_hbm, bd_ref, y_ref,
                   st_s, w_buf, wsem, *, n, layer):
    i = pl.program_id(0)
    n_used = n_used_ref[0]

    def weight_copies(e, ws):
        return [pltpu.make_async_copy(w_hbm.at[layer, e], w_buf.at[ws, m], wsem.at[ws])
                for m, w_hbm in enumerate((wg_hbm, wu_hbm, wd_hbm))]

    @pl.when(i == 0)
    def _():
        st_s[0] = 0

    e_cur = blk_e_ref[i]
    first_of_run = (i == 0) | (blk_e_ref[jnp.maximum(i - 1, 0)] != e_cur)

    @pl.when(first_of_run & (i < n_used))
    def _():
        run = st_s[0]
        ws = run % 2

        @pl.when(i == 0)
        def _():
            for cp in weight_copies(e_cur, ws):
                cp.start()

        for cp in weight_copies(e_cur, ws):
            cp.wait()
        nxt = lax.while_loop(lambda j: (j < n_used) & (blk_e_ref[jnp.minimum(j, n - 1)] == e_cur),
                             lambda j: j + 1, i + 1)

        @pl.when(nxt < n_used)
        def _():
            for cp in weight_copies(blk_e_ref[jnp.minimum(nxt, n - 1)], 1 - ws):
                cp.start()

        st_s[0] = run + 1
        st_s[1] = ws

    @pl.when(i < n_used)
    def _():
        ws = st_s[1]
        x = _load_row_tiles(x_ref, 0, MOE_BLK).astype(BF16)
        g = jnp.dot(x, w_buf[ws, 0].astype(BF16), preferred_element_type=F32) + bg_ref[...]
        u = jnp.dot(x, w_buf[ws, 1].astype(BF16), preferred_element_type=F32) + bu_ref[...]
        g = jnp.minimum(g, SWIGLU_LIMIT)
        u = jnp.clip(u, -SWIGLU_LIMIT, SWIGLU_LIMIT)
        a = g * jax.nn.sigmoid(SWIGLU_ALPHA * g) * (u + 1.0)
        y = jnp.dot(a.astype(BF16), w_buf[ws, 2].astype(BF16), preferred_element_type=F32) + bd_ref[...]
        _store_row_tiles(y_ref, 0, y)

    @pl.when(i >= n_used)
    def _():
        y_ref[...] = jnp.zeros(y_ref.shape, F32)


def _experts(blk_e, n_used, x_sorted, w_gate, b_gate, w_up, b_up, w_down, b_down, layer):
    n_blocks = x_sorted.shape[0] // (MOE_BLK * ROW_TILE)
    bspec = pl.BlockSpec((None, None, 1, D_MODEL), lambda i, be, nu: (layer, be[i], 0, 0))
    anyspec = pl.BlockSpec(memory_space=pl.ANY)
    depth = w_gate.shape[0]
    b4 = lambda b: b.reshape(depth, N_EXPERTS, 1, D_MODEL)
    return pl.pallas_call(
        functools.partial(_expert_kernel, n=n_blocks, layer=layer),
        out_shape=jax.ShapeDtypeStruct(x_sorted.shape, F32),
        grid_spec=pltpu.PrefetchScalarGridSpec(
            num_scalar_prefetch=2,
            grid=(n_blocks,),
            in_specs=[
                pl.BlockSpec((MOE_BLK * ROW_TILE, LANES), lambda i, be, nu: (jnp.minimum(i, nu[0] - 1), 0)),
                anyspec, bspec, anyspec, bspec, anyspec, bspec],
            out_specs=pl.BlockSpec((MOE_BLK * ROW_TILE, LANES), lambda i, be, nu: (i, 0)),
            scratch_shapes=[
                pltpu.SMEM((2,), jnp.int32),
                pltpu.VMEM((2, 3, D_MODEL, D_MODEL), F32),
                pltpu.SemaphoreType.DMA((2,)),
            ],
        ),
        compiler_params=_params(("arbitrary",), VMEM_LIMIT_V7X),
        name="experts",
    )(blk_e, n_used, x_sorted, w_gate, b4(b_gate), w_up, b4(b_up), w_down, b4(b_down))


CUM_CHUNK = 256
BLK_LANES = 256


def _sublane_cumsum(x):
    row = lax.broadcasted_iota(jnp.int32, x.shape, 0)
    d = 1
    while d < x.shape[0]:
        x = x + jnp.where(row >= d, pltpu.roll(x, d, axis=0), 0)
        d *= 2
    return x


TOK_TILE = CUM_CHUNK
TILE_ROWS = TOP_K * TOK_TILE
SEG_LANES = 128
SEG_PAD_FROM = SEG_LANES - 2
SEG_PAD_LEN = SEG_LANES - 1
SEG_SRC, SEG_CNT, SEG_DST = 0, 1, 2


def _route_kernel(lg_ref, tri_ref, gate_o, lpos_o, seg_o, blk_o):
    lg = lg_ref[...]
    n_e, t = lg.shape
    e_iota = lax.broadcasted_iota(jnp.int32, lg.shape, 0)
    work = lg
    tops, hots = [], []
    for _ in range(TOP_K):
        m = jnp.max(work, axis=0, keepdims=True)
        first = jnp.min(jnp.where(work == m, e_iota, n_e), axis=0, keepdims=True)
        hot = e_iota == first
        work = jnp.where(hot, -jnp.inf, work)
        tops.append(m)
        hots.append(hot)
    ex = [jnp.exp(m - tops[0]) for m in tops]
    den = ex[0] + ex[1] + ex[2] + ex[3]
    for k in range(TOP_K):
        gate_o[k:k + 1, :] = ex[k] / den

    chosen = jnp.where(hots[0] | hots[1] | hots[2] | hots[3], 1.0, 0.0)
    tri = tri_ref[...]
    lane = lax.broadcasted_iota(jnp.int32, (n_e, SEG_LANES), 1)
    seg_cnt = jnp.zeros((n_e, SEG_LANES), jnp.int32)
    seg_before = jnp.zeros((n_e, SEG_LANES), jnp.int32)
    carry = jnp.zeros((n_e, 1), F32)
    rank_in_tile = []
    n_tiles = t // TOK_TILE
    for c in range(n_tiles):
        chunk = chosen[:, c * TOK_TILE:(c + 1) * TOK_TILE]
        inc = jnp.dot(chunk.astype(BF16), tri, preferred_element_type=F32)
        cnt = inc[:, TOK_TILE - 1:TOK_TILE]
        rank_in_tile.append((inc - chunk).astype(jnp.int32))
        seg_cnt = jnp.where(lane == c, cnt.astype(jnp.int32), seg_cnt)
        seg_before = jnp.where(lane == c, carry.astype(jnp.int32), seg_before)
        carry = carry + cnt

    counts = jnp.broadcast_to(carry.astype(jnp.int32), (n_e, SEG_LANES))
    padded = ((counts + (MOE_BLK - 1)) >> MOE_SHIFT) << MOE_SHIFT
    pad_end = _sublane_cumsum(padded)
    pad_start = pad_end - padded
    seg_src = _sublane_cumsum(seg_cnt) - seg_cnt
    seg_o[SEG_SRC] = seg_src
    seg_o[SEG_CNT] = jnp.where(lane == SEG_PAD_LEN, padded - counts, seg_cnt)
    seg_o[SEG_DST] = jnp.where(lane == SEG_PAD_FROM, pad_start + counts, pad_start + seg_before)

    for c in range(n_tiles):
        pos = seg_src[:, c:c + 1] + rank_in_tile[c]
        for k in range(TOP_K):
            lpos_o[k:k + 1, c * TOK_TILE:(c + 1) * TOK_TILE] = jnp.sum(
                jnp.where(hots[k][:, c * TOK_TILE:(c + 1) * TOK_TILE], pos, 0), axis=0, keepdims=True)

    pad_end_b = jnp.concatenate([pad_end] * (BLK_LANES // SEG_LANES), axis=-1)
    blk_row0 = lax.broadcasted_iota(jnp.int32, (n_e, BLK_LANES), 1) * MOE_BLK
    blk_o[0:1, :] = jnp.minimum(jnp.sum((pad_end_b <= blk_row0).astype(jnp.int32), axis=0, keepdims=True), n_e - 1)
    blk_o[1:2, :] = pad_end_b[n_e - 1:n_e, :] >> MOE_SHIFT
    blk_o[2:8, :] = jnp.zeros((6, BLK_LANES), jnp.int32)


def _route(logits):
    t = logits.shape[1]
    n_blocks = t * TOP_K // MOE_BLK + N_EXPERTS
    assert t % TOK_TILE == 0 and t // TOK_TILE <= SEG_PAD_FROM and n_blocks <= BLK_LANES
    tri = jnp.asarray(np.triu(np.ones((CUM_CHUNK, CUM_CHUNK), np.float32)), dtype=BF16)
    gates, lpos, seg, blk = pl.pallas_call(
        _route_kernel,
        out_shape=[jax.ShapeDtypeStruct((TOP_K, t), F32), jax.ShapeDtypeStruct((TOP_K, t), jnp.int32),
                   jax.ShapeDtypeStruct((3, N_EXPERTS, SEG_LANES), jnp.int32),
                   jax.ShapeDtypeStruct((8, BLK_LANES), jnp.int32)],
        compiler_params=_params(None, VMEM_LIMIT_V7X),
        name="route",
    )(logits, tri)
    return gates, lpos, seg, blk[0, :n_blocks], blk[1, :1]


def _segment_copies(src, dst, src0, dst0, cnt, sem, wait):
    @pl.when(cnt > 0)
    def _():
        size = cnt * ROW_TILE
        cp = pltpu.make_async_copy(
            src.at[pl.ds(pl.multiple_of(src0 * ROW_TILE, ROW_TILE), size)],
            dst.at[pl.ds(pl.multiple_of(dst0 * ROW_TILE, ROW_TILE), size)], sem)
        if wait:
            cp.wait()
        else:
            cp.start()


def _dispatch_kernel(seg_ref, nu_ref, h_ref, lpos_ref, x_hbm, z_buf, zero_buf, sem, zsem, *, n_tiles, n_blocks):
    c = pl.program_id(0)
    slot = c % 2

    def wait_tile(s):
        pltpu.make_async_copy(z_buf.at[s], x_hbm.at[pl.ds(0, TILE_ROWS * ROW_TILE)], sem.at[s]).wait()

    def padding(wait):
        def per_expert(e, carry):
            _segment_copies(zero_buf, x_hbm, 0, seg_ref[SEG_DST, e, SEG_PAD_FROM], seg_ref[SEG_CNT, e, SEG_PAD_LEN],
                            zsem, wait)
            return carry

        lax.fori_loop(0, N_EXPERTS, per_expert, 0)

        def per_block(b, carry):
            cp = pltpu.make_async_copy(
                zero_buf, x_hbm.at[pl.ds(pl.multiple_of(b * (MOE_BLK * ROW_TILE), MOE_BLK * ROW_TILE),
                                         MOE_BLK * ROW_TILE)], zsem)
            if wait:
                cp.wait()
            else:
                cp.start()
            return carry

        lax.fori_loop(nu_ref[0], n_blocks, per_block, 0)

    @pl.when(c == 0)
    def _():
        zero_buf[...] = jnp.zeros(zero_buf.shape, F32)
        padding(False)

    @pl.when(c >= 2)
    def _():
        wait_tile(slot)

    row = lax.broadcasted_iota(jnp.int32, (TILE_ROWS, TOK_TILE), 0)
    hit = row == lpos_ref[0:1, :]
    for k in range(1, TOP_K):
        hit = hit | (row == lpos_ref[k:k + 1, :])
    z = jnp.dot(jnp.where(hit, 1.0, 0.0).astype(BF16), h_ref[...], preferred_element_type=F32)
    _store_row_tiles(z_buf.at[slot], 0, z)

    def per_expert(e, carry):
        _segment_copies(z_buf.at[slot], x_hbm, seg_ref[SEG_SRC, e, c], seg_ref[SEG_DST, e, c], seg_ref[SEG_CNT, e, c],
                        sem.at[slot], False)
        return carry

    lax.fori_loop(0, N_EXPERTS, per_expert, 0)

    @pl.when(c == n_tiles - 1)
    def _():
        if n_tiles > 1:
            wait_tile(1 - slot)
        wait_tile(slot)
        padding(True)


def _dispatch(seg, n_used, h_all, lpos, n_blocks):
    t = h_all.shape[0]
    n_tiles = t // TOK_TILE
    return pl.pallas_call(
        functools.partial(_dispatch_kernel, n_tiles=n_tiles, n_blocks=n_blocks),
        out_shape=jax.ShapeDtypeStruct((n_blocks * MOE_BLK * ROW_TILE, LANES), F32),
        grid_spec=pltpu.PrefetchScalarGridSpec(
            num_scalar_prefetch=2,
            grid=(n_tiles,),
            in_specs=[pl.BlockSpec((TOK_TILE, D_MODEL), lambda c, sg, nu: (c, 0)),
                      pl.BlockSpec((TOP_K, TOK_TILE), lambda c, sg, nu: (0, c))],
            out_specs=pl.BlockSpec(memory_space=pl.ANY),
            scratch_shapes=[
                pltpu.VMEM((2, TILE_ROWS * ROW_TILE, LANES), F32),
                pltpu.VMEM((MOE_BLK * ROW_TILE, LANES), F32),
                pltpu.SemaphoreType.DMA((2,)),
                pltpu.SemaphoreType.DMA,
            ],
        ),
        compiler_params=_params(("arbitrary",), VMEM_LIMIT_V7X),
        name="dispatch",
    )(seg, n_used, h_all, lpos)


def _combine_kernel(seg_ref, x_ref, y_hbm, lpos_ref, lpos_t_ref, gate_ref, mod_ref, o_ref, y_buf, sem, *, tile0,
                    n_tiles):
    c = pl.program_id(0)
    slot = c % 2

    def fetch(cc):
        def per_expert(e, carry):
            _segment_copies(y_hbm, y_buf.at[cc % 2], seg_ref[SEG_DST, e, tile0 + cc], seg_ref[SEG_SRC, e, tile0 + cc],
                            seg_ref[SEG_CNT, e, tile0 + cc], sem.at[cc % 2], False)
            return carry

        lax.fori_loop(0, N_EXPERTS, per_expert, 0)

    @pl.when(c == 0)
    def _():
        fetch(0)

    @pl.when(c + 1 < n_tiles)
    def _():
        fetch(c + 1)

    pltpu.make_async_copy(y_hbm.at[pl.ds(0, TILE_ROWS * ROW_TILE)], y_buf.at[slot], sem.at[slot]).wait()

    row = lax.broadcasted_iota(jnp.int32, (TILE_ROWS, TOK_TILE), 0)
    row_gate = jnp.zeros((TILE_ROWS, 1), F32)
    for k in range(TOP_K):
        row_gate += jnp.sum(jnp.where(row == lpos_ref[k:k + 1, :], gate_ref[k:k + 1, :], 0.0), axis=1, keepdims=True)
    ys = _load_row_tiles(y_buf.at[slot], 0, TILE_ROWS) * row_gate
    ys_hi = ys.astype(BF16)
    ys_lo = (ys - ys_hi.astype(F32)).astype(BF16)
    col = lax.broadcasted_iota(jnp.int32, (TOK_TILE, TILE_ROWS), 1)
    hit = col == lpos_t_ref[:, 0:1]
    for k in range(1, TOP_K):
        hit = hit | (col == lpos_t_ref[:, k:k + 1])
    u = jnp.where(hit, 1.0, 0.0).astype(BF16)
    mix = jnp.dot(u, ys_hi, preferred_element_type=F32) + jnp.dot(u, ys_lo, preferred_element_type=F32)
    o_ref[...] = x_ref[...] + mod_ref[5:6, :] * mix


def _combine(seg, x_mid, y_sorted, lpos, lpos_t, gates, mod, group_of_tile, layer, tok0):
    t = x_mid.shape[0]
    tile0 = tok0 // TOK_TILE
    n_tiles = t // TOK_TILE
    return pl.pallas_call(
        functools.partial(_combine_kernel, tile0=tile0, n_tiles=n_tiles),
        out_shape=jax.ShapeDtypeStruct((t, D_MODEL), F32),
        grid_spec=pltpu.PrefetchScalarGridSpec(
            num_scalar_prefetch=1,
            grid=(n_tiles,),
            in_specs=[
                pl.BlockSpec((TOK_TILE, D_MODEL), lambda c, sg: (c, 0)),
                pl.BlockSpec(memory_space=pl.ANY),
                pl.BlockSpec((TOP_K, TOK_TILE), lambda c, sg: (0, tile0 + c)),
                pl.BlockSpec((TOK_TILE, TOP_K), lambda c, sg: (tile0 + c, 0)),
                pl.BlockSpec((TOP_K, TOK_TILE), lambda c, sg: (0, tile0 + c)),
                pl.BlockSpec((None, None, N_MOD, D_MODEL), lambda c, sg: (layer, group_of_tile(c), 0, 0)),
            ],
            out_specs=pl.BlockSpec((TOK_TILE, D_MODEL), lambda c, sg: (c, 0)),
            scratch_shapes=[pltpu.VMEM((2, TILE_ROWS * ROW_TILE, LANES), F32), pltpu.SemaphoreType.DMA((2,))],
        ),
        compiler_params=_params(("arbitrary",), VMEM_LIMIT_V7X),
        name="combine",
    )(seg, x_mid, y_sorted, lpos, lpos_t, gates, mod)


def _rope_tables(n_lat):
    quarter = HEAD_DIM // 4
    t = jnp.arange(n_lat)
    inv = ROPE_BASE ** (-jnp.arange(quarter, dtype=F32) / quarter)
    ang_r = (t // GRID_W).astype(F32)[:, None] * inv
    ang_c = (t % GRID_W).astype(F32)[:, None] * inv
    cos = jnp.concatenate([jnp.cos(ang_r)] * 2 + [jnp.cos(ang_c)] * 2, axis=-1)
    sin = jnp.concatenate([-jnp.sin(ang_r), jnp.sin(ang_r), -jnp.sin(ang_c), jnp.sin(ang_c)], axis=-1)
    return jnp.concatenate([cos, cos], axis=-1), jnp.concatenate([sin, sin], axis=-1)


def _block_diag_ones():
    idx = np.arange(MXU_COLS_V7X) // HEAD_DIM
    return jnp.asarray(idx[:, None] == idx[None, :], dtype=BF16)


def kernel(x_prompt, x_sample, cache_k_win, cache_v_win, cache_k_nbr, cache_v_nbr, c, c_ctx, w_mod, b_mod, g_mix, g_ffn, w_in, w_out, qn_win, kn_win, qn_nbr, kn_nbr, sink_win, rel_bias_nbr, conv_w, w_router, b_router, w_gate, b_gate, w_up, b_up, w_down, b_down):
    bsz, n_ctx, d = x_prompt.shape
    dbs, n_lat, _ = x_sample.shape
    depth = w_in.shape[0]
    past = cache_k_win.shape[2]
    assert d == D_MODEL and dbs + 1 <= COND_ROWS and n_lat % GRID_W == 0 and n_lat >= Q_BLK + 2 * WINDOW
    t_ctx, t_lat = bsz * n_ctx, dbs * n_lat

    cond = jnp.concatenate([c_ctx[None], c, jnp.zeros((COND_ROWS - 1 - dbs, d), F32)], axis=0)
    mod = _modulation(cond, w_mod, b_mod)

    w_in_bf = w_in.astype(BF16)
    w_out_bf = w_out.astype(BF16)
    norm_w = jnp.concatenate([jnp.tile(qn_win, (1, N_HEADS_A)), jnp.tile(kn_win, (1, N_KV_A)),
                              jnp.tile(qn_nbr, (1, N_HEADS_B)), jnp.tile(kn_nbr, (1, N_HEADS_B))], axis=-1)[:, None, :]
    ones_bd = _block_diag_ones()
    rope_tabs = _rope_tables(n_lat)
    g_mix3, g_ffn3 = g_mix[:, None, :], g_ffn[:, None, :]
    w_router_t = jnp.swapaxes(w_router, 1, 2)
    b_router_t = b_router[:, :, None]
    t_all = t_ctx + t_lat
    ck_win = cache_k_win.reshape(dbs, depth, past, KV_WIDTH_A)
    cv_win = cache_v_win.reshape(dbs, depth, past, KV_WIDTH_A)
    ck_nbr = cache_k_nbr.reshape(dbs, depth, past, WIDTH_B)
    cv_nbr = cache_v_nbr.reshape(dbs, depth, past, WIDTH_B)

    tm_ctx = 2 * n_ctx
    tm_lat = 512
    ctx_group = lambda i: 0
    lat_group_in = lambda i: 1 + i
    lat_group_out = lambda i: 1 + (i * tm_lat) // n_lat
    lat_group_comb = lambda i: 1 + (i * TOK_TILE) // n_lat
    assert t_ctx % TOK_TILE == 0 and n_lat % TOK_TILE == 0

    xp = x_prompt.reshape(t_ctx, d)
    xs = x_sample.reshape(t_lat, d)
    caches = [[], [], [], []]
    for l in range(depth):
        qa, qb, ka, va, kb, vb, oc = _inproj(xp, mod, ctx_group, g_mix3, w_in_bf, l, norm_w, ones_bd, conv_w,
                                             tm_ctx, n_ctx)
        oa, ob = _ctx_attention(sink_win[l], qa, ka, va, qb, kb, vb, n_ctx)
        shared = (jnp.zeros((t_all, d), BF16), jnp.zeros((N_EXPERTS, t_all), F32))
        xp_mid, h_all, lg_all = _outproj(xp, oa, ob, oc, w_out_bf, mod, ctx_group, g_ffn3, w_router_t, b_router_t, l,
                                         tm_ctx, t_all, 0, shared)
        for lst, a in zip(caches, (ka, va, kb, vb)):
            lst.append(a)

        qa, qb, ka, va, kb, vb, oc = _inproj(xs, mod, lat_group_in, g_mix3, w_in_bf, l, norm_w, ones_bd, conv_w,
                                             n_lat, n_lat, rope_tabs)
        oa = _win_attention(sink_win[l], qa, ka, va, ck_win, cv_win, l, n_lat)
        ob = _nbr_attention(qb, kb, vb, ck_nbr, cv_nbr, _nbr_bias_table(rel_bias_nbr[l], n_lat // GRID_W), l, n_lat)
        xs_mid, h_all, lg_all = _outproj(xs, oa, ob, oc, w_out_bf, mod, lat_group_out, g_ffn3, w_router_t, b_router_t,
                                         l, tm_lat, t_all, t_ctx, shared=(h_all, lg_all))

        gates, lpos, seg, blk_e, n_used = _route(lg_all)
        x_sorted = _dispatch(seg, n_used, h_all, lpos, blk_e.shape[0])
        y_sorted = _experts(blk_e, n_used, x_sorted, w_gate, b_gate, w_up, b_up, w_down, b_down, l)
        lpos_t = lpos.T
        xp = _combine(seg, xp_mid, y_sorted, lpos, lpos_t, gates, mod, ctx_group, l, 0)
        xs = _combine(seg, xs_mid, y_sorted, lpos, lpos_t, gates, mod, lat_group_comb, l, t_ctx)

    new_v_win = jnp.stack([a.reshape(bsz, n_ctx, KV_WIDTH_A) for a in caches[1]], axis=1).reshape(
        bsz, depth, n_ctx, N_KV_A, HEAD_DIM)
    new_k_nbr = jnp.stack([a.reshape(bsz, n_ctx, WIDTH_B) for a in caches[2]], axis=1).reshape(
        bsz, depth, n_ctx, N_HEADS_B, HEAD_DIM)
    new_v_nbr = jnp.stack([a.reshape(bsz, n_ctx, WIDTH_B) for a in caches[3]], axis=1).reshape(
        bsz, depth, n_ctx, N_HEADS_B, HEAD_DIM)
    new_k_win = jnp.stack([a.reshape(bsz, n_ctx, KV_WIDTH_A) for a in caches[0]], axis=1).reshape(
        bsz, depth, n_ctx, N_KV_A, HEAD_DIM)
    return (xp.reshape(bsz, n_ctx, d), xs.reshape(dbs, n_lat, d), new_k_win, new_v_win, new_k_nbr, new_v_nbr)
```

```python
import functools

import numpy as np
import jax
import jax.numpy as jnp
from jax import lax
from jax.experimental import pallas as pl
from jax.experimental.pallas import tpu as pltpu

F32 = jnp.float32
BF16 = jnp.bfloat16

D_MODEL = 1024
HEAD_DIM = 64
GRID_W = 64
N_HEADS_A = 8
N_KV_A = 2
GROUP_A = N_HEADS_A // N_KV_A
WINDOW = 128
Q_BLK = 128
N_HEADS_B = 4
NA_ROWS = 8
NA_COLS = 16
CONV_CH = 256
CONV_W = 3
WIDTH_A = N_HEADS_A * HEAD_DIM
KV_WIDTH_A = N_KV_A * HEAD_DIM
WIDTH_B = N_HEADS_B * HEAD_DIM
IN_COLS = WIDTH_A + 2 * KV_WIDTH_A + 3 * WIDTH_B + 3 * CONV_CH
N_EXPERTS = 32
TOP_K = 4
SWIGLU_LIMIT = 7.0
SWIGLU_ALPHA = 1.702
ROPE_BASE = 10000.0
EPS = 1e-6
NEG = -1e30
N_MOD = 6

C_QA = 0
C_KA = C_QA + WIDTH_A
C_VA = C_KA + KV_WIDTH_A
C_QB = C_VA + KV_WIDTH_A
C_KB = C_QB + WIDTH_B
C_VB = C_KB + WIDTH_B
C_U = C_VB + WIDTH_B
C_GB = C_U + CONV_CH
C_GC = C_GB + CONV_CH

MXU_COLS_V7X = 256
COND_ROWS = 8
MOE_BLK = 256
MOE_SHIFT = MOE_BLK.bit_length() - 1
assert 1 << MOE_SHIFT == MOE_BLK
VMEM_LIMIT_V7X = 56 * 1024 * 1024

_NT = (((1,), (1,)), ((), ()))


def _params(sem, vmem=None):
    return pltpu.CompilerParams(dimension_semantics=sem, vmem_limit_bytes=vmem)


def _mod_kernel(c_ref, w_ref, b_ref, o_ref):
    c = c_ref[...]
    s = c * jax.nn.sigmoid(c)
    o_ref[...] = jnp.dot(s.astype(BF16), w_ref[...].astype(BF16), preferred_element_type=F32) + b_ref[...]


def _modulation(cond, w_mod, b_mod):
    depth = w_mod.shape[0]
    out = pl.pallas_call(
        _mod_kernel,
        out_shape=jax.ShapeDtypeStruct((depth, COND_ROWS, N_MOD * D_MODEL), F32),
        grid=(depth, N_MOD),
        in_specs=[
            pl.BlockSpec((COND_ROWS, D_MODEL), lambda l, j: (0, 0)),
            pl.BlockSpec((None, D_MODEL, D_MODEL), lambda l, j: (l, 0, j)),
            pl.BlockSpec((None, 1, D_MODEL), lambda l, j: (l, 0, j)),
        ],
        out_specs=pl.BlockSpec((None, COND_ROWS, D_MODEL), lambda l, j: (l, 0, j)),
        compiler_params=_params(("arbitrary", "arbitrary")),
        name="modulation",
    )(cond, w_mod, b_mod.reshape(depth, 1, N_MOD * D_MODEL))
    return out.reshape(depth, COND_ROWS, N_MOD, D_MODEL)


def _head_norm(x, w_row, ones_ref):
    width = x.shape[1]
    sq = (x * x).astype(BF16)
    parts = []
    for c0 in range(0, width, MXU_COLS_V7X):
        wd = min(MXU_COLS_V7X, width - c0)
        parts.append(jnp.dot(sq[:, c0:c0 + wd], ones_ref[:wd, :wd], preferred_element_type=F32))
    ss = parts[0] if len(parts) == 1 else jnp.concatenate(parts, axis=-1)
    return x * lax.rsqrt(ss * (1.0 / HEAD_DIM) + EPS) * w_row


def _rope(x, cos, sin):
    width = x.shape[1]
    lane = lax.broadcasted_iota(jnp.int32, x.shape, 1)
    quarter = HEAD_DIM // 4
    partner = jnp.where((lane % (2 * quarter)) < quarter,
                        pltpu.roll(x, width - quarter, axis=1), pltpu.roll(x, quarter, axis=1))
    reps = width // cos.shape[1]
    cos_w = cos if reps == 1 else jnp.concatenate([cos] * reps, axis=-1)
    sin_w = sin if reps == 1 else jnp.concatenate([sin] * reps, axis=-1)
    return x * cos_w + partner * sin_w


def _inproj_kernel(*refs, seq_len, rope):
    if rope:
        (x_ref, mod_ref, g_ref, w_ref, nw_ref, ones_ref, cw_ref, cos_ref, sin_ref,
         qa_o, qb_o, ka_o, va_o, kb_o, vb_o, oc_o) = refs
    else:
        (x_ref, mod_ref, g_ref, w_ref, nw_ref, ones_ref, cw_ref,
         qa_o, qb_o, ka_o, va_o, kb_o, vb_o, oc_o) = refs
    x = x_ref[...]
    tm = x.shape[0]
    h = x * lax.rsqrt(jnp.mean(x * x, axis=-1, keepdims=True) + EPS) * g_ref[...]
    h = h * (1.0 + mod_ref[1:2, :]) + mod_ref[0:1, :]
    p = jnp.dot(h.astype(BF16), w_ref[...], preferred_element_type=F32)

    qa = _head_norm(p[:, C_QA:C_KA], nw_ref[:, 0:WIDTH_A], ones_ref)
    ka = _head_norm(p[:, C_KA:C_VA], nw_ref[:, WIDTH_A:WIDTH_A + KV_WIDTH_A], ones_ref)
    o_qb = WIDTH_A + KV_WIDTH_A
    qb = _head_norm(p[:, C_QB:C_KB], nw_ref[:, o_qb:o_qb + WIDTH_B], ones_ref)
    kb = _head_norm(p[:, C_KB:C_VB], nw_ref[:, o_qb + WIDTH_B:o_qb + 2 * WIDTH_B], ones_ref)
    if rope:
        cos, sin = cos_ref[...], sin_ref[...]
        qa = _rope(qa, cos, sin)
        ka = _rope(ka, cos, sin)
    qa_o[...] = qa.astype(BF16)
    qb_o[...] = qb.astype(BF16)
    ka_o[...] = ka
    va_o[...] = p[:, C_VA:C_QB]
    kb_o[...] = kb
    vb_o[...] = p[:, C_VB:C_U]

    z = p[:, C_GC:C_GC + CONV_CH] * p[:, C_U:C_GB]
    row = lax.broadcasted_iota(jnp.int32, z.shape, 0) % seq_len
    z_prev = jnp.where(row == 0, 0.0, pltpu.roll(z, 1, axis=0))
    z_next = jnp.where(row == seq_len - 1, 0.0, pltpu.roll(z, tm - 1, axis=0))
    y = z_prev * cw_ref[0:1, :] + z * cw_ref[1:2, :] + z_next * cw_ref[2:3, :]
    oc_o[...] = (p[:, C_GB:C_GC] * y).astype(BF16)


def _inproj(x, mod, group_of_tile, g_mix, w_in_bf, layer, norm_w, ones_bd, conv_w, tm, seq_len, rope_tabs=None):
    t = x.shape[0]
    rope = rope_tabs is not None
    in_specs = [
        pl.BlockSpec((tm, D_MODEL), lambda i: (i, 0)),
        pl.BlockSpec((None, None, N_MOD, D_MODEL), lambda i: (layer, group_of_tile(i), 0, 0)),
        pl.BlockSpec((None, 1, D_MODEL), lambda i: (layer, 0, 0)),
        pl.BlockSpec((None, D_MODEL, IN_COLS), lambda i: (layer, 0, 0)),
        pl.BlockSpec((None, 1, norm_w.shape[-1]), lambda i: (layer, 0, 0)),
        pl.BlockSpec(ones_bd.shape, lambda i: (0, 0)),
        pl.BlockSpec((None, CONV_W, CONV_CH), lambda i: (layer, 0, 0)),
    ]
    args = [x, mod, g_mix, w_in_bf, norm_w, ones_bd, conv_w]
    if rope:
        in_specs += [pl.BlockSpec(rope_tabs[0].shape, lambda i: (0, 0))] * 2
        args += list(rope_tabs)
    widths = (WIDTH_A, WIDTH_B, KV_WIDTH_A, KV_WIDTH_A, WIDTH_B, WIDTH_B, CONV_CH)
    dtypes = (BF16, BF16, F32, F32, F32, F32, BF16)
    return pl.pallas_call(
        functools.partial(_inproj_kernel, seq_len=seq_len, rope=rope),
        out_shape=[jax.ShapeDtypeStruct((t, w), dt) for w, dt in zip(widths, dtypes)],
        grid=(t // tm,),
        in_specs=in_specs,
        out_specs=[pl.BlockSpec((tm, w), lambda i: (i, 0)) for w in widths],
        compiler_params=_params(("arbitrary",), VMEM_LIMIT_V7X),
        name="inproj_rope" if rope else "inproj",
    )(*args)


def _pair_attention(q2, k_lo, k_hi, v_lo, v_hi, sinks=None, adjust=None):
    m = k_lo.shape[0]
    k2 = jnp.concatenate([k_lo, k_hi], axis=0)
    v2 = jnp.concatenate([v_lo, v_hi], axis=0)
    s = lax.dot_general(q2, k2, _NT, preferred_element_type=F32)
    ps, dens = [], []
    for i in range(2):
        si = s[:, i * m:(i + 1) * m]
        if adjust is not None:
            si = adjust(i, si)
        mx = jnp.max(si, axis=-1, keepdims=True)
        if sinks is not None:
            mx = jnp.maximum(mx, sinks[i])
        p = jnp.exp(si - mx)
        den = jnp.sum(p, axis=-1, keepdims=True)
        if sinks is not None:
            den = den + jnp.exp(sinks[i] - mx)
        ps.append(p.astype(BF16))
        dens.append(den)
    o = jnp.dot(jnp.concatenate(ps, axis=-1), v2, preferred_element_type=F32)
    lane = lax.broadcasted_iota(jnp.int32, o.shape, 1)
    return o / jnp.where(lane < HEAD_DIM, dens[0], dens[1])


def _lane_halves(x):
    lane = lax.broadcasted_iota(jnp.int32, x.shape, 1)
    lo = lane < HEAD_DIM
    sw = pltpu.roll(x, HEAD_DIM, axis=1)
    z = jnp.zeros_like(x)
    return tuple(jnp.where(c, y, z).astype(BF16) for c, y in ((lo, x), (~lo, x), (lo, sw), (~lo, sw)))


def _ctx_attn_kernel(sink_ref, qa_ref, ka_ref, va_ref, qb_ref, kb_ref, vb_ref, oa_o, ob_o):
    scale = HEAD_DIM ** -0.5
    k0l, k1h, k1l, k0h = _lane_halves(ka_ref[...])
    v0l, v1h, v1l, v0h = _lane_halves(va_ref[...])
    for i in range(N_HEADS_A // 2):
        cols = slice(2 * i * HEAD_DIM, (2 * i + 2) * HEAD_DIM)
        kv = (k0l, k0h, v0l, v0h) if (2 * i) // GROUP_A == 0 else (k1l, k1h, v1l, v1h)
        o = _pair_attention(qa_ref[:, cols] * scale, *kv, sinks=(sink_ref[2 * i], sink_ref[2 * i + 1]))
        oa_o[:, cols] = o.astype(oa_o.dtype)
    for i in range(N_HEADS_B // 2):
        cols = slice(2 * i * HEAD_DIM, (2 * i + 2) * HEAD_DIM)
        kl, kh, _, _ = _lane_halves(kb_ref[:, cols])
        vl, vh, _, _ = _lane_halves(vb_ref[:, cols])
        o = _pair_attention(qb_ref[:, cols] * scale, kl, kh, vl, vh)
        ob_o[:, cols] = o.astype(ob_o.dtype)


def _ctx_attention(sink, qa, ka, va, qb, kb, vb, seq_len):
    t = qa.shape[0]
    widths = (WIDTH_A, KV_WIDTH_A, KV_WIDTH_A, WIDTH_B, WIDTH_B, WIDTH_B)
    return pl.pallas_call(
        _ctx_attn_kernel,
        out_shape=[jax.ShapeDtypeStruct((t, WIDTH_A), BF16), jax.ShapeDtypeStruct((t, WIDTH_B), BF16)],
        grid=(t // seq_len,),
        in_specs=[pl.BlockSpec(memory_space=pltpu.SMEM)]
        + [pl.BlockSpec((seq_len, w), lambda i: (i, 0)) for w in widths],
        out_specs=[pl.BlockSpec((seq_len, WIDTH_A), lambda i: (i, 0)),
                   pl.BlockSpec((seq_len, WIDTH_B), lambda i: (i, 0))],
        compiler_params=_params(("arbitrary",)),
        name="ctx_attention",
    )(sink, qa, ka, va, qb, kb, vb)


def _win_attn_kernel(sink_ref, qa_ref, ka_ref, va_ref, ck_ref, cv_ref, oa_o, *, n_lat):
    scale = HEAD_DIM ** -0.5
    span = Q_BLK + 2 * WINDOW
    n = pl.program_id(1)
    start = pl.multiple_of(jnp.clip(n * Q_BLK - WINDOW, 0, n_lat - span), Q_BLK)
    k0l, k1h, k1l, k0h = _lane_halves(jnp.concatenate([ka_ref[pl.ds(start, span), :], ck_ref[...]], axis=0))
    v0l, v1h, v1l, v0h = _lane_halves(jnp.concatenate([va_ref[pl.ds(start, span), :], cv_ref[...]], axis=0))
    n_keys = k0l.shape[0]
    qpos = n * Q_BLK + lax.broadcasted_iota(jnp.int32, (Q_BLK, n_keys), 0)
    col = lax.broadcasted_iota(jnp.int32, (Q_BLK, n_keys), 1)
    ok = (col >= span) | (jnp.abs(qpos - (start + col)) <= WINDOW)
    for i in range(N_HEADS_A // 2):
        cols = slice(2 * i * HEAD_DIM, (2 * i + 2) * HEAD_DIM)
        kv = (k0l, k0h, v0l, v0h) if (2 * i) // GROUP_A == 0 else (k1l, k1h, v1l, v1h)
        o = _pair_attention(qa_ref[:, cols] * scale, *kv, sinks=(sink_ref[2 * i], sink_ref[2 * i + 1]),
                            adjust=lambda _, s: jnp.where(ok, s, NEG))
        oa_o[:, cols] = o.astype(oa_o.dtype)


def _win_attention(sink, qa, ka, va, cache_k, cache_v, layer, n_lat):
    t = qa.shape[0]
    nb = n_lat // Q_BLK
    past = cache_k.shape[2]
    return pl.pallas_call(
        functools.partial(_win_attn_kernel, n_lat=n_lat),
        out_shape=jax.ShapeDtypeStruct((t, WIDTH_A), BF16),
        grid=(t // n_lat, nb),
        in_specs=[
            pl.BlockSpec(memory_space=pltpu.SMEM),
            pl.BlockSpec((Q_BLK, WIDTH_A), lambda b, n: (b * nb + n, 0)),
            pl.BlockSpec((n_lat, KV_WIDTH_A), lambda b, n: (b, 0)),
            pl.BlockSpec((n_lat, KV_WIDTH_A), lambda b, n: (b, 0)),
            pl.BlockSpec((None, None, past, KV_WIDTH_A), lambda b, n: (b, layer, 0, 0)),
            pl.BlockSpec((None, None, past, KV_WIDTH_A), lambda b, n: (b, layer, 0, 0)),
        ],
        out_specs=pl.BlockSpec((Q_BLK, WIDTH_A), lambda b, n: (b * nb + n, 0)),
        compiler_params=_params(("arbitrary", "arbitrary")),
        name="window_attention",
    )(sink, qa, ka, va, cache_k, cache_v)


def _nbr_attn_kernel(qb_ref, kb_ref, vb_ref, ck_ref, cv_ref, bias_ref, ob_o, *, rows, kr):
    scale = HEAD_DIM ** -0.5
    r = pl.program_id(1)
    start = pl.multiple_of(jnp.clip(r - kr // 2, 0, rows - kr) * GRID_W, GRID_W)
    nwin = kr * GRID_W
    kcat = jnp.concatenate([kb_ref[pl.ds(start, nwin), :], ck_ref[...]], axis=0)
    vcat = jnp.concatenate([vb_ref[pl.ds(start, nwin), :], cv_ref[...]], axis=0)
    past = ck_ref.shape[0]
    for i in range(N_HEADS_B // 2):
        cols = slice(2 * i * HEAD_DIM, (2 * i + 2) * HEAD_DIM)
        kl, kh, _, _ = _lane_halves(kcat[:, cols])
        vl, vh, _, _ = _lane_halves(vcat[:, cols])
        o = _pair_attention(
            qb_ref[:, cols] * scale, kl, kh, vl, vh,
            adjust=lambda hh, s, i=i: s + jnp.concatenate([bias_ref[2 * i + hh], jnp.zeros((GRID_W, past), F32)],
                                                          axis=-1))
        ob_o[:, cols] = o.astype(ob_o.dtype)


def _nbr_bias_table(rel_bias, rows):
    kr = min(NA_ROWS, rows)
    cls = np.arange(kr)[:, None]
    m = np.arange(kr)[None, :]
    row_sel = (m - cls + NA_ROWS - 1)[:, :, None] == np.arange(2 * NA_ROWS - 1)[None, None, :]
    c = np.arange(GRID_W)[:, None]
    kc = np.arange(GRID_W)[None, :]
    dc = np.clip(kc - c, -(NA_COLS - 1), NA_COLS - 1) + NA_COLS - 1
    col_sel = dc[:, :, None] == np.arange(2 * NA_COLS - 1)[None, None, :]
    col_start = np.clip(c - NA_COLS // 2, 0, GRID_W - NA_COLS)
    ok = (kc >= col_start) & (kc < col_start + NA_COLS)
    tab = jnp.einsum('hrd,ymr,ckd->hycmk', rel_bias.astype(F32), row_sel.astype(np.float32),
                     col_sel.astype(np.float32), precision=lax.Precision.HIGHEST)
    tab = jnp.where(ok[None, None, :, None, :], tab, NEG)
    return tab.reshape(rel_bias.shape[0], kr, GRID_W, kr * GRID_W)


def _nbr_attention(qb, kb, vb, cache_k, cache_v, bias_tab, layer, n_lat):
    t = qb.shape[0]
    rows = n_lat // GRID_W
    kr = min(NA_ROWS, rows)
    past = cache_k.shape[2]

    def bias_map(b, r):
        return (0, r - jnp.clip(r - kr // 2, 0, rows - kr), 0, 0)

    return pl.pallas_call(
        functools.partial(_nbr_attn_kernel, rows=rows, kr=kr),
        out_shape=jax.ShapeDtypeStruct((t, WIDTH_B), BF16),
        grid=(t // n_lat, rows),
        in_specs=[
            pl.BlockSpec((GRID_W, WIDTH_B), lambda b, r: (b * rows + r, 0)),
            pl.BlockSpec((n_lat, WIDTH_B), lambda b, r: (b, 0)),
            pl.BlockSpec((n_lat, WIDTH_B), lambda b, r: (b, 0)),
            pl.BlockSpec((None, None, past, WIDTH_B), lambda b, r: (b, layer, 0, 0)),
            pl.BlockSpec((None, None, past, WIDTH_B), lambda b, r: (b, layer, 0, 0)),
            pl.BlockSpec((N_HEADS_B, None, GRID_W, kr * GRID_W), bias_map),
        ],
        out_specs=pl.BlockSpec((GRID_W, WIDTH_B), lambda b, r: (b * rows + r, 0)),
        compiler_params=_params(("arbitrary", "arbitrary")),
        name="neighbourhood_attention",
    )(qb, kb, vb, cache_k, cache_v, bias_tab)


ROW_TILE = 8
LANES = 128
assert ROW_TILE * LANES == D_MODEL


def _store_row_tiles(ref, row0, x):
    n = x.shape[0]
    for c in range(ROW_TILE):
        ref[pl.ds(row0 * ROW_TILE + c, n, stride=ROW_TILE), :] = x[:, c * LANES:(c + 1) * LANES]


def _load_row_tiles(ref, row0, n, row_stride=1, chunk0=0):
    return jnp.concatenate(
        [ref[pl.ds(row0 * ROW_TILE + chunk0 + c, n, stride=ROW_TILE * row_stride), :] for c in range(ROW_TILE)],
        axis=-1)


def _outproj_kernel(x_ref, oa_ref, ob_ref, oc_ref, w_ref, mod_ref, g_ref, wr_ref, br_ref, *rest):
    xn_o, h_o, lg_o = rest[-3:]
    mix = jnp.dot(oa_ref[...], w_ref[0:WIDTH_A, :], preferred_element_type=F32)
    mix += jnp.dot(ob_ref[...], w_ref[WIDTH_A:WIDTH_A + WIDTH_B, :], preferred_element_type=F32)
    mix += jnp.dot(oc_ref[...], w_ref[WIDTH_A + WIDTH_B:, :], preferred_element_type=F32)
    xn = x_ref[...] + mod_ref[2:3, :] * mix
    xn_o[...] = xn
    h = xn * lax.rsqrt(jnp.mean(xn * xn, axis=-1, keepdims=True) + EPS) * g_ref[...]
    h = h * (1.0 + mod_ref[4:5, :]) + mod_ref[3:4, :]
    h_hi = h.astype(BF16)
    h_o[...] = h_hi
    h_lo = (h - h_hi.astype(F32)).astype(BF16)
    wr = wr_ref[...]
    w_hi = wr.astype(BF16)
    w_lo = (wr - w_hi.astype(F32)).astype(BF16)
    lg = lax.dot_general(w_hi, h_hi, _NT, preferred_element_type=F32)
    lg += lax.dot_general(w_hi, h_lo, _NT, preferred_element_type=F32)
    lg += lax.dot_general(w_lo, h_hi, _NT, preferred_element_type=F32)
    lg_o[...] = lg + br_ref[...]


def _outproj(x, oa, ob, oc, w_out_bf, mod, group_of_tile, g_ffn, w_router_t, b_router_t, layer, tm, t_all, tok0,
             shared):
    t = x.shape[0]
    b0 = tok0 // tm
    assert tok0 % tm == 0
    in_specs = [
        pl.BlockSpec((tm, D_MODEL), lambda i: (i, 0)),
        pl.BlockSpec((tm, WIDTH_A), lambda i: (i, 0)),
        pl.BlockSpec((tm, WIDTH_B), lambda i: (i, 0)),
        pl.BlockSpec((tm, CONV_CH), lambda i: (i, 0)),
        pl.BlockSpec((None, D_MODEL, D_MODEL), lambda i: (layer, 0, 0)),
        pl.BlockSpec((None, None, N_MOD, D_MODEL), lambda i: (layer, group_of_tile(i), 0, 0)),
        pl.BlockSpec((None, 1, D_MODEL), lambda i: (layer, 0, 0)),
        pl.BlockSpec((None, N_EXPERTS, D_MODEL), lambda i: (layer, 0, 0)),
        pl.BlockSpec((None, N_EXPERTS, 1), lambda i: (layer, 0, 0)),
    ]
    args = [x, oa, ob, oc, w_out_bf, mod, g_ffn, w_router_t, b_router_t]
    aliases = {len(args): 1, len(args) + 1: 2}
    in_specs += [pl.BlockSpec(memory_space=pl.ANY)] * 2
    args += list(shared)
    return pl.pallas_call(
        _outproj_kernel,
        out_shape=[jax.ShapeDtypeStruct((t, D_MODEL), F32), jax.ShapeDtypeStruct((t_all, D_MODEL), BF16),
                   jax.ShapeDtypeStruct((N_EXPERTS, t_all), F32)],
        grid=(t // tm,),
        in_specs=in_specs,
        out_specs=[pl.BlockSpec((tm, D_MODEL), lambda i: (i, 0)),
                   pl.BlockSpec((tm, D_MODEL), lambda i: (b0 + i, 0)),
                   pl.BlockSpec((N_EXPERTS, tm), lambda i: (0, b0 + i))],
        input_output_aliases=aliases,
        compiler_params=_params(("arbitrary",), VMEM_LIMIT_V7X),
        name="outproj_router",
    )(*args)


def _expert_kernel(seg_ref, n_used_ref, x_hbm, wg_ref, bg_ref, wu_ref, bu_ref, wd_ref, bd_ref, y_hbm,
                   x_buf, y_buf, xsem, ysem, zsem, *, n_blocks):
    e = pl.program_id(0)
    n_used = n_used_ref[0]
    blk_rows = MOE_BLK * ROW_TILE
    first_blk = seg_ref[SEG_DST, e, SEG_BLK]
    n_blk = seg_ref[SEG_CNT, e, SEG_BLK]

    def rows_of(g):
        return pl.ds(pl.multiple_of(g * blk_rows, blk_rows), blk_rows)

    def x_copy(g):
        return pltpu.make_async_copy(x_hbm.at[rows_of(g)], x_buf.at[g % 2], xsem.at[g % 2])

    def y_copy(g):
        return pltpu.make_async_copy(y_buf.at[g % 2], y_hbm.at[rows_of(g)], ysem.at[g % 2])

    def zero_copy(g):
        return pltpu.make_async_copy(y_buf.at[0], y_hbm.at[rows_of(g)], zsem)

    @pl.when((e == 0) & (n_used > 0))
    def _():
        x_copy(0).start()

    def block(b, carry):
        g = first_blk + b
        slot = g % 2

        @pl.when(g + 1 < n_used)
        def _():
            x_copy(g + 1).start()

        x_copy(g).wait()

        @pl.when(g >= 2)
        def _():
            y_copy(g - 2).wait()

        x = _load_row_tiles(x_buf.at[slot], 0, MOE_BLK).astype(BF16)
        gt = jnp.dot(x, wg_ref[...].astype(BF16), preferred_element_type=F32) + bg_ref[...]
        up = jnp.dot(x, wu_ref[...].astype(BF16), preferred_element_type=F32) + bu_ref[...]
        gt = jnp.minimum(gt, SWIGLU_LIMIT)
        up = jnp.clip(up, -SWIGLU_LIMIT, SWIGLU_LIMIT)
        a = gt * jax.nn.sigmoid(SWIGLU_ALPHA * gt) * (up + 1.0)
        y = jnp.dot(a.astype(BF16), wd_ref[...].astype(BF16), preferred_element_type=F32) + bd_ref[...]
        _store_row_tiles(y_buf.at[slot], 0, y)
        y_copy(g).start()
        return carry

    lax.fori_loop(0, n_blk, block, 0)

    @pl.when(e == pl.num_programs(0) - 1)
    def _():
        @pl.when(n_used >= 2)
        def _():
            y_copy(n_used - 2).wait()

        @pl.when(n_used >= 1)
        def _():
            y_copy(n_used - 1).wait()

        y_buf[0] = jnp.zeros((blk_rows, LANES), F32)

        def start_zero(g, carry):
            zero_copy(g).start()
            return carry

        def wait_zero(g, carry):
            zero_copy(g).wait()
            return carry

        lax.fori_loop(n_used, n_blocks, start_zero, 0)
        lax.fori_loop(n_used, n_blocks, wait_zero, 0)


def _experts(seg, n_used, x_sorted, w_gate, b_gate, w_up, b_up, w_down, b_down, layer):
    n_blocks = x_sorted.shape[0] // (MOE_BLK * ROW_TILE)
    wspec = pl.BlockSpec((None, None, D_MODEL, D_MODEL), lambda e, sg, nu: (layer, e, 0, 0))
    bspec = pl.BlockSpec((None, None, 1, D_MODEL), lambda e, sg, nu: (layer, e, 0, 0))
    anyspec = pl.BlockSpec(memory_space=pl.ANY)
    depth = w_gate.shape[0]
    b4 = lambda b: b.reshape(depth, N_EXPERTS, 1, D_MODEL)
    return pl.pallas_call(
        functools.partial(_expert_kernel, n_blocks=n_blocks),
        out_shape=jax.ShapeDtypeStruct(x_sorted.shape, F32),
        grid_spec=pltpu.PrefetchScalarGridSpec(
            num_scalar_prefetch=2,
            grid=(N_EXPERTS,),
            in_specs=[anyspec, wspec, bspec, wspec, bspec, wspec, bspec],
            out_specs=anyspec,
            scratch_shapes=[
                pltpu.VMEM((2, MOE_BLK * ROW_TILE, LANES), F32),
                pltpu.VMEM((2, MOE_BLK * ROW_TILE, LANES), F32),
                pltpu.SemaphoreType.DMA((2,)),
                pltpu.SemaphoreType.DMA((2,)),
                pltpu.SemaphoreType.DMA,
            ],
        ),
        compiler_params=_params(("arbitrary",), VMEM_LIMIT_V7X),
        name="experts",
    )(seg, n_used, x_sorted, w_gate, b4(b_gate), w_up, b4(b_up), w_down, b4(b_down))


CUM_CHUNK = 256


def _sublane_cumsum(x):
    row = lax.broadcasted_iota(jnp.int32, x.shape, 0)
    d = 1
    while d < x.shape[0]:
        x = x + jnp.where(row >= d, pltpu.roll(x, d, axis=0), 0)
        d *= 2
    return x


TOK_TILE = CUM_CHUNK
TILE_ROWS = TOP_K * TOK_TILE
SEG_LANES = 128
SEG_PAD_FROM = SEG_LANES - 2
SEG_PAD_LEN = SEG_LANES - 1
SEG_BLK = SEG_LANES - 3
SEG_SRC, SEG_CNT, SEG_DST = 0, 1, 2


def _route_kernel(lg_ref, tri_ref, gate_o, lpos_o, seg_o, used_o):
    lg = lg_ref[...]
    n_e, t = lg.shape
    e_iota = lax.broadcasted_iota(jnp.int32, lg.shape, 0)
    work = lg
    tops, hots = [], []
    for _ in range(TOP_K):
        m = jnp.max(work, axis=0, keepdims=True)
        first = jnp.min(jnp.where(work == m, e_iota, n_e), axis=0, keepdims=True)
        hot = e_iota == first
        work = jnp.where(hot, -jnp.inf, work)
        tops.append(m)
        hots.append(hot)
    ex = [jnp.exp(m - tops[0]) for m in tops]
    den = ex[0] + ex[1] + ex[2] + ex[3]
    for k in range(TOP_K):
        gate_o[k:k + 1, :] = ex[k] / den

    chosen = jnp.where(hots[0] | hots[1] | hots[2] | hots[3], 1.0, 0.0)
    tri = tri_ref[...]
    lane = lax.broadcasted_iota(jnp.int32, (n_e, SEG_LANES), 1)
    seg_cnt = jnp.zeros((n_e, SEG_LANES), jnp.int32)
    seg_before = jnp.zeros((n_e, SEG_LANES), jnp.int32)
    carry = jnp.zeros((n_e, 1), F32)
    rank_in_tile = []
    n_tiles = t // TOK_TILE
    for c in range(n_tiles):
        chunk = chosen[:, c * TOK_TILE:(c + 1) * TOK_TILE]
        inc = jnp.dot(chunk.astype(BF16), tri, preferred_element_type=F32)
        cnt = inc[:, TOK_TILE - 1:TOK_TILE]
        rank_in_tile.append((inc - chunk).astype(jnp.int32))
        seg_cnt = jnp.where(lane == c, cnt.astype(jnp.int32), seg_cnt)
        seg_before = jnp.where(lane == c, carry.astype(jnp.int32), seg_before)
        carry = carry + cnt

    counts = jnp.broadcast_to(carry.astype(jnp.int32), (n_e, SEG_LANES))
    padded = ((counts + (MOE_BLK - 1)) >> MOE_SHIFT) << MOE_SHIFT
    pad_end = _sublane_cumsum(padded)
    pad_start = pad_end - padded
    seg_src = _sublane_cumsum(seg_cnt) - seg_cnt
    seg_o[SEG_SRC] = seg_src
    seg_o[SEG_CNT] = jnp.where(lane == SEG_PAD_LEN, padded - counts,
                               jnp.where(lane == SEG_BLK, padded >> MOE_SHIFT, seg_cnt))
    seg_o[SEG_DST] = jnp.where(lane == SEG_PAD_FROM, pad_start + counts,
                               jnp.where(lane == SEG_BLK, pad_start >> MOE_SHIFT, pad_start + seg_before))

    for c in range(n_tiles):
        pos = seg_src[:, c:c + 1] + rank_in_tile[c]
        for k in range(TOP_K):
            lpos_o[k:k + 1, c * TOK_TILE:(c + 1) * TOK_TILE] = jnp.sum(
                jnp.where(hots[k][:, c * TOK_TILE:(c + 1) * TOK_TILE], pos, 0), axis=0, keepdims=True)

    used_o[...] = jnp.broadcast_to(pad_end[n_e - 1:n_e, :] >> MOE_SHIFT, used_o.shape)


def _route(logits):
    t = logits.shape[1]
    assert t % TOK_TILE == 0 and t // TOK_TILE <= SEG_BLK
    tri = jnp.asarray(np.triu(np.ones((CUM_CHUNK, CUM_CHUNK), np.float32)), dtype=BF16)
    gates, lpos, seg, used = pl.pallas_call(
        _route_kernel,
        out_shape=[jax.ShapeDtypeStruct((TOP_K, t), F32), jax.ShapeDtypeStruct((TOP_K, t), jnp.int32),
                   jax.ShapeDtypeStruct((3, N_EXPERTS, SEG_LANES), jnp.int32),
                   jax.ShapeDtypeStruct((8, SEG_LANES), jnp.int32)],
        compiler_params=_params(None, VMEM_LIMIT_V7X),
        name="route",
    )(logits, tri)
    return gates, lpos, seg, used[0, :1]


def _segment_copies(src, dst, src0, dst0, cnt, sem, wait):
    @pl.when(cnt > 0)
    def _():
        size = cnt * ROW_TILE
        cp = pltpu.make_async_copy(
            src.at[pl.ds(pl.multiple_of(src0 * ROW_TILE, ROW_TILE), size)],
            dst.at[pl.ds(pl.multiple_of(dst0 * ROW_TILE, ROW_TILE), size)], sem)
        if wait:
            cp.wait()
        else:
            cp.start()


def _dispatch_kernel(seg_ref, nu_ref, h_ref, lpos_ref, x_hbm, z_buf, zero_buf, sem, zsem, *, n_tiles, n_blocks):
    c = pl.program_id(0)
    slot = c % 2

    def wait_tile(s):
        pltpu.make_async_copy(z_buf.at[s], x_hbm.at[pl.ds(0, TILE_ROWS * ROW_TILE)], sem.at[s]).wait()

    def padding(wait):
        def per_expert(e, carry):
            _segment_copies(zero_buf, x_hbm, 0, seg_ref[SEG_DST, e, SEG_PAD_FROM], seg_ref[SEG_CNT, e, SEG_PAD_LEN],
                            zsem, wait)
            return carry

        lax.fori_loop(0, N_EXPERTS, per_expert, 0)

        def per_block(b, carry):
            cp = pltpu.make_async_copy(
                zero_buf, x_hbm.at[pl.ds(pl.multiple_of(b * (MOE_BLK * ROW_TILE), MOE_BLK * ROW_TILE),
                                         MOE_BLK * ROW_TILE)], zsem)
            if wait:
                cp.wait()
            else:
                cp.start()
            return carry

        lax.fori_loop(nu_ref[0], n_blocks, per_block, 0)

    @pl.when(c == 0)
    def _():
        zero_buf[...] = jnp.zeros(zero_buf.shape, F32)
        padding(False)

    @pl.when(c >= 2)
    def _():
        wait_tile(slot)

    row = lax.broadcasted_iota(jnp.int32, (TILE_ROWS, TOK_TILE), 0)
    hit = row == lpos_ref[0:1, :]
    for k in range(1, TOP_K):
        hit = hit | (row == lpos_ref[k:k + 1, :])
    z = jnp.dot(jnp.where(hit, 1.0, 0.0).astype(BF16), h_ref[...], preferred_element_type=F32)
    _store_row_tiles(z_buf.at[slot], 0, z)

    def per_expert(e, carry):
        _segment_copies(z_buf.at[slot], x_hbm, seg_ref[SEG_SRC, e, c], seg_ref[SEG_DST, e, c], seg_ref[SEG_CNT, e, c],
                        sem.at[slot], False)
        return carry

    lax.fori_loop(0, N_EXPERTS, per_expert, 0)

    @pl.when(c == n_tiles - 1)
    def _():
        if n_tiles > 1:
            wait_tile(1 - slot)
        wait_tile(slot)
        padding(True)


def _dispatch(seg, n_used, h_all, lpos, n_blocks):
    t = h_all.shape[0]
    n_tiles = t // TOK_TILE
    return pl.pallas_call(
        functools.partial(_dispatch_kernel, n_tiles=n_tiles, n_blocks=n_blocks),
        out_shape=jax.ShapeDtypeStruct((n_blocks * MOE_BLK * ROW_TILE, LANES), F32),
        grid_spec=pltpu.PrefetchScalarGridSpec(
            num_scalar_prefetch=2,
            grid=(n_tiles,),
            in_specs=[pl.BlockSpec((TOK_TILE, D_MODEL), lambda c, sg, nu: (c, 0)),
                      pl.BlockSpec((TOP_K, TOK_TILE), lambda c, sg, nu: (0, c))],
            out_specs=pl.BlockSpec(memory_space=pl.ANY),
            scratch_shapes=[
                pltpu.VMEM((2, TILE_ROWS * ROW_TILE, LANES), F32),
                pltpu.VMEM((MOE_BLK * ROW_TILE, LANES), F32),
                pltpu.SemaphoreType.DMA((2,)),
                pltpu.SemaphoreType.DMA,
            ],
        ),
        compiler_params=_params(("arbitrary",), VMEM_LIMIT_V7X),
        name="dispatch",
    )(seg, n_used, h_all, lpos)


def _combine_kernel(seg_ref, x_ref, y_hbm, lpos_ref, lpos_t_ref, gate_ref, mod_ref, o_ref, y_buf, sem, *, tile0,
                    n_tiles):
    c = pl.program_id(0)
    slot = c % 2

    def fetch(cc):
        def per_expert(e, carry):
            _segment_copies(y_hbm, y_buf.at[cc % 2], seg_ref[SEG_DST, e, tile0 + cc], seg_ref[SEG_SRC, e, tile0 + cc],
                            seg_ref[SEG_CNT, e, tile0 + cc], sem.at[cc % 2], False)
            return carry

        lax.fori_loop(0, N_EXPERTS, per_expert, 0)

    @pl.when(c == 0)
    def _():
        fetch(0)

    @pl.when(c + 1 < n_tiles)
    def _():
        fetch(c + 1)

    pltpu.make_async_copy(y_hbm.at[pl.ds(0, TILE_ROWS * ROW_TILE)], y_buf.at[slot], sem.at[slot]).wait()

    row = lax.broadcasted_iota(jnp.int32, (TILE_ROWS, TOK_TILE), 0)
    row_gate = jnp.zeros((TILE_ROWS, 1), F32)
    for k in range(TOP_K):
        row_gate += jnp.sum(jnp.where(row == lpos_ref[k:k + 1, :], gate_ref[k:k + 1, :], 0.0), axis=1, keepdims=True)
    ys = _load_row_tiles(y_buf.at[slot], 0, TILE_ROWS) * row_gate
    ys_hi = ys.astype(BF16)
    ys_lo = (ys - ys_hi.astype(F32)).astype(BF16)
    col = lax.broadcasted_iota(jnp.int32, (TOK_TILE, TILE_ROWS), 1)
    hit = col == lpos_t_ref[:, 0:1]
    for k in range(1, TOP_K):
        hit = hit | (col == lpos_t_ref[:, k:k + 1])
    u = jnp.where(hit, 1.0, 0.0).astype(BF16)
    mix = jnp.dot(u, ys_hi, preferred_element_type=F32) + jnp.dot(u, ys_lo, preferred_element_type=F32)
    o_ref[...] = x_ref[...] + mod_ref[5:6, :] * mix


def _combine(seg, x_mid, y_sorted, lpos, lpos_t, gates, mod, group_of_tile, layer, tok0):
    t = x_mid.shape[0]
    tile0 = tok0 // TOK_TILE
    n_tiles = t // TOK_TILE
    return pl.pallas_call(
        functools.partial(_combine_kernel, tile0=tile0, n_tiles=n_tiles),
        out_shape=jax.ShapeDtypeStruct((t, D_MODEL), F32),
        grid_spec=pltpu.PrefetchScalarGridSpec(
            num_scalar_prefetch=1,
            grid=(n_tiles,),
            in_specs=[
                pl.BlockSpec((TOK_TILE, D_MODEL), lambda c, sg: (c, 0)),
                pl.BlockSpec(memory_space=pl.ANY),
                pl.BlockSpec((TOP_K, TOK_TILE), lambda c, sg: (0, tile0 + c)),
                pl.BlockSpec((TOK_TILE, TOP_K), lambda c, sg: (tile0 + c, 0)),
                pl.BlockSpec((TOP_K, TOK_TILE), lambda c, sg: (0, tile0 + c)),
                pl.BlockSpec((None, None, N_MOD, D_MODEL), lambda c, sg: (layer, group_of_tile(c), 0, 0)),
            ],
            out_specs=pl.BlockSpec((TOK_TILE, D_MODEL), lambda c, sg: (c, 0)),
            scratch_shapes=[pltpu.VMEM((2, TILE_ROWS * ROW_TILE, LANES), F32), pltpu.SemaphoreType.DMA((2,))],
        ),
        compiler_params=_params(("arbitrary",), VMEM_LIMIT_V7X),
        name="combine",
    )(seg, x_mid, y_sorted, lpos, lpos_t, gates, mod)


def _rope_tables(n_lat):
    quarter = HEAD_DIM // 4
    t = jnp.arange(n_lat)
    inv = ROPE_BASE ** (-jnp.arange(quarter, dtype=F32) / quarter)
    ang_r = (t // GRID_W).astype(F32)[:, None] * inv
    ang_c = (t % GRID_W).astype(F32)[:, None] * inv
    cos = jnp.concatenate([jnp.cos(ang_r)] * 2 + [jnp.cos(ang_c)] * 2, axis=-1)
    sin = jnp.concatenate([-jnp.sin(ang_r), jnp.sin(ang_r), -jnp.sin(ang_c), jnp.sin(ang_c)], axis=-1)
    return jnp.concatenate([cos, cos], axis=-1), jnp.concatenate([sin, sin], axis=-1)


def _block_diag_ones():
    idx = np.arange(MXU_COLS_V7X) // HEAD_DIM
    return jnp.asarray(idx[:, None] == idx[None, :], dtype=BF16)


def kernel(x_prompt, x_sample, cache_k_win, cache_v_win, cache_k_nbr, cache_v_nbr, c, c_ctx, w_mod, b_mod, g_mix, g_ffn, w_in, w_out, qn_win, kn_win, qn_nbr, kn_nbr, sink_win, rel_bias_nbr, conv_w, w_router, b_router, w_gate, b_gate, w_up, b_up, w_down, b_down):
    bsz, n_ctx, d = x_prompt.shape
    dbs, n_lat, _ = x_sample.shape
    depth = w_in.shape[0]
    past = cache_k_win.shape[2]
    assert d == D_MODEL and dbs + 1 <= COND_ROWS and n_lat % GRID_W == 0 and n_lat >= Q_BLK + 2 * WINDOW
    t_ctx, t_lat = bsz * n_ctx, dbs * n_lat

    cond = jnp.concatenate([c_ctx[None], c, jnp.zeros((COND_ROWS - 1 - dbs, d), F32)], axis=0)
    mod = _modulation(cond, w_mod, b_mod)

    w_in_bf = w_in.astype(BF16)
    w_out_bf = w_out.astype(BF16)
    norm_w = jnp.concatenate([jnp.tile(qn_win, (1, N_HEADS_A)), jnp.tile(kn_win, (1, N_KV_A)),
                              jnp.tile(qn_nbr, (1, N_HEADS_B)), jnp.tile(kn_nbr, (1, N_HEADS_B))], axis=-1)[:, None, :]
    ones_bd = _block_diag_ones()
    rope_tabs = _rope_tables(n_lat)
    g_mix3, g_ffn3 = g_mix[:, None, :], g_ffn[:, None, :]
    w_router_t = jnp.swapaxes(w_router, 1, 2)
    b_router_t = b_router[:, :, None]
    t_all = t_ctx + t_lat
    ck_win = cache_k_win.reshape(dbs, depth, past, KV_WIDTH_A)
    cv_win = cache_v_win.reshape(dbs, depth, past, KV_WIDTH_A)
    ck_nbr = cache_k_nbr.reshape(dbs, depth, past, WIDTH_B)
    cv_nbr = cache_v_nbr.reshape(dbs, depth, past, WIDTH_B)

    tm_ctx = 2 * n_ctx
    tm_lat = 512
    ctx_group = lambda i: 0
    lat_group_in = lambda i: 1 + i
    lat_group_out = lambda i: 1 + (i * tm_lat) // n_lat
    lat_group_comb = lambda i: 1 + (i * TOK_TILE) // n_lat
    assert t_ctx % TOK_TILE == 0 and n_lat % TOK_TILE == 0

    xp = x_prompt.reshape(t_ctx, d)
    xs = x_sample.reshape(t_lat, d)
    caches = [[], [], [], []]
    for l in range(depth):
        qa, qb, ka, va, kb, vb, oc = _inproj(xp, mod, ctx_group, g_mix3, w_in_bf, l, norm_w, ones_bd, conv_w,
                                             tm_ctx, n_ctx)
        oa, ob = _ctx_attention(sink_win[l], qa, ka, va, qb, kb, vb, n_ctx)
        shared = (jnp.zeros((t_all, d), BF16), jnp.zeros((N_EXPERTS, t_all), F32))
        xp_mid, h_all, lg_all = _outproj(xp, oa, ob, oc, w_out_bf, mod, ctx_group, g_ffn3, w_router_t, b_router_t, l,
                                         tm_ctx, t_all, 0, shared)
        for lst, a in zip(caches, (ka, va, kb, vb)):
            lst.append(a)

        qa, qb, ka, va, kb, vb, oc = _inproj(xs, mod, lat_group_in, g_mix3, w_in_bf, l, norm_w, ones_bd, conv_w,
                                             n_lat, n_lat, rope_tabs)
        oa = _win_attention(sink_win[l], qa, ka, va, ck_win, cv_win, l, n_lat)
        ob = _nbr_attention(qb, kb, vb, ck_nbr, cv_nbr, _nbr_bias_table(rel_bias_nbr[l], n_lat // GRID_W), l, n_lat)
        xs_mid, h_all, lg_all = _outproj(xs, oa, ob, oc, w_out_bf, mod, lat_group_out, g_ffn3, w_router_t, b_router_t,
                                         l, tm_lat, t_all, t_ctx, shared=(h_all, lg_all))

        gates, lpos, seg, n_used = _route(lg_all)
        x_sorted = _dispatch(seg, n_used, h_all, lpos, t_all * TOP_K // MOE_BLK + N_EXPERTS)
        y_sorted = _experts(seg, n_used, x_sorted, w_gate, b_gate, w_up, b_up, w_down, b_down, l)
        lpos_t = lpos.T
        xp = _combine(seg, xp_mid, y_sorted, lpos, lpos_t, gates, mod, ctx_group, l, 0)
        xs = _combine(seg, xs_mid, y_sorted, lpos, lpos_t, gates, mod, lat_group_comb, l, t_ctx)

    new_v_win = jnp.stack([a.reshape(bsz, n_ctx, KV_WIDTH_A) for a in caches[1]], axis=1).reshape(
        bsz, depth, n_ctx, N_KV_A, HEAD_DIM)
    new_k_nbr = jnp.stack([a.reshape(bsz, n_ctx, WIDTH_B) for a in caches[2]], axis=1).reshape(
        bsz, depth, n_ctx, N_HEADS_B, HEAD_DIM)
    new_v_nbr = jnp.stack([a.reshape(bsz, n_ctx, WIDTH_B) for a in caches[3]], axis=1).reshape(
        bsz, depth, n_ctx, N_HEADS_B, HEAD_DIM)
    new_k_win = jnp.stack([a.reshape(bsz, n_ctx, KV_WIDTH_A) for a in caches[0]], axis=1).reshape(
        bsz, depth, n_ctx, N_KV_A, HEAD_DIM)
    return (xp.reshape(bsz, n_ctx, d), xs.reshape(dbs, n_lat, d), new_k_win, new_v_win, new_k_nbr, new_v_nbr)
```

```python
import functools

import numpy as np
import jax
import jax.numpy as jnp
from jax import lax
from jax.experimental import pallas as pl
from jax.experimental.pallas import tpu as pltpu

F32 = jnp.float32
BF16 = jnp.bfloat16

D_MODEL = 1024
HEAD_DIM = 64
GRID_W = 64
N_HEADS_A = 8
N_KV_A = 2
GROUP_A = N_HEADS_A // N_KV_A
WINDOW = 128
Q_BLK = 128
N_HEADS_B = 4
NA_ROWS = 8
NA_COLS = 16
CONV_CH = 256
CONV_W = 3
WIDTH_A = N_HEADS_A * HEAD_DIM
KV_WIDTH_A = N_KV_A * HEAD_DIM
WIDTH_B = N_HEADS_B * HEAD_DIM
IN_COLS = WIDTH_A + 2 * KV_WIDTH_A + 3 * WIDTH_B + 3 * CONV_CH
N_EXPERTS = 32
TOP_K = 4
SWIGLU_LIMIT = 7.0
SWIGLU_ALPHA = 1.702
ROPE_BASE = 10000.0
EPS = 1e-6
NEG = -1e30
N_MOD = 6

C_QA = 0
C_KA = C_QA + WIDTH_A
C_VA = C_KA + KV_WIDTH_A
C_QB = C_VA + KV_WIDTH_A
C_KB = C_QB + WIDTH_B
C_VB = C_KB + WIDTH_B
C_U = C_VB + WIDTH_B
C_GB = C_U + CONV_CH
C_GC = C_GB + CONV_CH

MXU_COLS_V7X = 256
COND_ROWS = 8
MOE_BLK = 256
MOE_SHIFT = MOE_BLK.bit_length() - 1
assert 1 << MOE_SHIFT == MOE_BLK
VMEM_LIMIT_V7X = 56 * 1024 * 1024

_NT = (((1,), (1,)), ((), ()))


def _params(sem, vmem=None):
    return pltpu.CompilerParams(dimension_semantics=sem, vmem_limit_bytes=vmem)


def _mod_kernel(c_ref, w_ref, b_ref, o_ref):
    c = c_ref[...]
    s = c * jax.nn.sigmoid(c)
    o_ref[...] = jnp.dot(s.astype(BF16), w_ref[...].astype(BF16), preferred_element_type=F32) + b_ref[...]


def _modulation(cond, w_mod, b_mod):
    depth = w_mod.shape[0]
    out = pl.pallas_call(
        _mod_kernel,
        out_shape=jax.ShapeDtypeStruct((depth, COND_ROWS, N_MOD * D_MODEL), F32),
        grid=(depth, N_MOD),
        in_specs=[
            pl.BlockSpec((COND_ROWS, D_MODEL), lambda l, j: (0, 0)),
            pl.BlockSpec((None, D_MODEL, D_MODEL), lambda l, j: (l, 0, j)),
            pl.BlockSpec((None, 1, D_MODEL), lambda l, j: (l, 0, j)),
        ],
        out_specs=pl.BlockSpec((None, COND_ROWS, D_MODEL), lambda l, j: (l, 0, j)),
        compiler_params=_params(("arbitrary", "arbitrary")),
        name="modulation",
    )(cond, w_mod, b_mod.reshape(depth, 1, N_MOD * D_MODEL))
    return out.reshape(depth, COND_ROWS, N_MOD, D_MODEL)


def _head_norm(x, w_row, ones_ref):
    width = x.shape[1]
    sq = (x * x).astype(BF16)
    parts = []
    for c0 in range(0, width, MXU_COLS_V7X):
        wd = min(MXU_COLS_V7X, width - c0)
        parts.append(jnp.dot(sq[:, c0:c0 + wd], ones_ref[:wd, :wd], preferred_element_type=F32))
    ss = parts[0] if len(parts) == 1 else jnp.concatenate(parts, axis=-1)
    return x * lax.rsqrt(ss * (1.0 / HEAD_DIM) + EPS) * w_row


def _rope(x, cos, sin):
    width = x.shape[1]
    lane = lax.broadcasted_iota(jnp.int32, x.shape, 1)
    quarter = HEAD_DIM // 4
    partner = jnp.where((lane % (2 * quarter)) < quarter,
                        pltpu.roll(x, width - quarter, axis=1), pltpu.roll(x, quarter, axis=1))
    reps = width // cos.shape[1]
    cos_w = cos if reps == 1 else jnp.concatenate([cos] * reps, axis=-1)
    sin_w = sin if reps == 1 else jnp.concatenate([sin] * reps, axis=-1)
    return x * cos_w + partner * sin_w


def _inproj_kernel(*refs, seq_len, rope):
    if rope:
        (x_ref, mod_ref, g_ref, w_ref, nw_ref, ones_ref, cw_ref, cos_ref, sin_ref,
         qa_o, qb_o, ka_o, va_o, kb_o, vb_o, oc_o) = refs
    else:
        (x_ref, mod_ref, g_ref, w_ref, nw_ref, ones_ref, cw_ref,
         qa_o, qb_o, ka_o, va_o, kb_o, vb_o, oc_o) = refs
    x = x_ref[...]
    tm = x.shape[0]
    h = x * lax.rsqrt(jnp.mean(x * x, axis=-1, keepdims=True) + EPS) * g_ref[...]
    h = h * (1.0 + mod_ref[1:2, :]) + mod_ref[0:1, :]
    p = jnp.dot(h.astype(BF16), w_ref[...], preferred_element_type=F32)

    qa = _head_norm(p[:, C_QA:C_KA], nw_ref[:, 0:WIDTH_A], ones_ref)
    ka = _head_norm(p[:, C_KA:C_VA], nw_ref[:, WIDTH_A:WIDTH_A + KV_WIDTH_A], ones_ref)
    o_qb = WIDTH_A + KV_WIDTH_A
    qb = _head_norm(p[:, C_QB:C_KB], nw_ref[:, o_qb:o_qb + WIDTH_B], ones_ref)
    kb = _head_norm(p[:, C_KB:C_VB], nw_ref[:, o_qb + WIDTH_B:o_qb + 2 * WIDTH_B], ones_ref)
    if rope:
        cos, sin = cos_ref[...], sin_ref[...]
        qa = _rope(qa, cos, sin)
        ka = _rope(ka, cos, sin)
    qa_o[...] = qa.astype(BF16)
    qb_o[...] = qb.astype(BF16)
    ka_o[...] = ka
    va_o[...] = p[:, C_VA:C_QB]
    kb_o[...] = kb
    vb_o[...] = p[:, C_VB:C_U]

    z = p[:, C_GC:C_GC + CONV_CH] * p[:, C_U:C_GB]
    row = lax.broadcasted_iota(jnp.int32, z.shape, 0) % seq_len
    z_prev = jnp.where(row == 0, 0.0, pltpu.roll(z, 1, axis=0))
    z_next = jnp.where(row == seq_len - 1, 0.0, pltpu.roll(z, tm - 1, axis=0))
    y = z_prev * cw_ref[0:1, :] + z * cw_ref[1:2, :] + z_next * cw_ref[2:3, :]
    oc_o[...] = (p[:, C_GB:C_GC] * y).astype(BF16)


def _inproj(x, mod, group_of_tile, g_mix, w_in_bf, layer, norm_w, ones_bd, conv_w, tm, seq_len, rope_tabs=None):
    t = x.shape[0]
    rope = rope_tabs is not None
    in_specs = [
        pl.BlockSpec((tm, D_MODEL), lambda i: (i, 0)),
        pl.BlockSpec((None, None, N_MOD, D_MODEL), lambda i: (layer, group_of_tile(i), 0, 0)),
        pl.BlockSpec((None, 1, D_MODEL), lambda i: (layer, 0, 0)),
        pl.BlockSpec((None, D_MODEL, IN_COLS), lambda i: (layer, 0, 0)),
        pl.BlockSpec((None, 1, norm_w.shape[-1]), lambda i: (layer, 0, 0)),
        pl.BlockSpec(ones_bd.shape, lambda i: (0, 0)),
        pl.BlockSpec((None, CONV_W, CONV_CH), lambda i: (layer, 0, 0)),
    ]
    args = [x, mod, g_mix, w_in_bf, norm_w, ones_bd, conv_w]
    if rope:
        in_specs += [pl.BlockSpec(rope_tabs[0].shape, lambda i: (0, 0))] * 2
        args += list(rope_tabs)
    widths = (WIDTH_A, WIDTH_B, KV_WIDTH_A, KV_WIDTH_A, WIDTH_B, WIDTH_B, CONV_CH)
    dtypes = (BF16, BF16, F32, F32, F32, F32, BF16)
    return pl.pallas_call(
        functools.partial(_inproj_kernel, seq_len=seq_len, rope=rope),
        out_shape=[jax.ShapeDtypeStruct((t, w), dt) for w, dt in zip(widths, dtypes)],
        grid=(t // tm,),
        in_specs=in_specs,
        out_specs=[pl.BlockSpec((tm, w), lambda i: (i, 0)) for w in widths],
        compiler_params=_params(("arbitrary",), VMEM_LIMIT_V7X),
        name="inproj_rope" if rope else "inproj",
    )(*args)


def _pair_attention(q2, k_lo, k_hi, v_lo, v_hi, sinks=None, adjust=None):
    m = k_lo.shape[0]
    k2 = jnp.concatenate([k_lo, k_hi], axis=0)
    v2 = jnp.concatenate([v_lo, v_hi], axis=0)
    s = lax.dot_general(q2, k2, _NT, preferred_element_type=F32)
    ps, dens = [], []
    for i in range(2):
        si = s[:, i * m:(i + 1) * m]
        if adjust is not None:
            si = adjust(i, si)
        mx = jnp.max(si, axis=-1, keepdims=True)
        if sinks is not None:
            mx = jnp.maximum(mx, sinks[i])
        p = jnp.exp(si - mx)
        den = jnp.sum(p, axis=-1, keepdims=True)
        if sinks is not None:
            den = den + jnp.exp(sinks[i] - mx)
        ps.append(p.astype(BF16))
        dens.append(den)
    o = jnp.dot(jnp.concatenate(ps, axis=-1), v2, preferred_element_type=F32)
    lane = lax.broadcasted_iota(jnp.int32, o.shape, 1)
    return o / jnp.where(lane < HEAD_DIM, dens[0], dens[1])


def _lane_halves(x):
    lane = lax.broadcasted_iota(jnp.int32, x.shape, 1)
    lo = lane < HEAD_DIM
    sw = pltpu.roll(x, HEAD_DIM, axis=1)
    z = jnp.zeros_like(x)
    return tuple(jnp.where(c, y, z).astype(BF16) for c, y in ((lo, x), (~lo, x), (lo, sw), (~lo, sw)))


def _ctx_attn_kernel(sink_ref, qa_ref, ka_ref, va_ref, qb_ref, kb_ref, vb_ref, oa_o, ob_o):
    scale = HEAD_DIM ** -0.5
    k0l, k1h, k1l, k0h = _lane_halves(ka_ref[...])
    v0l, v1h, v1l, v0h = _lane_halves(va_ref[...])
    for i in range(N_HEADS_A // 2):
        cols = slice(2 * i * HEAD_DIM, (2 * i + 2) * HEAD_DIM)
        kv = (k0l, k0h, v0l, v0h) if (2 * i) // GROUP_A == 0 else (k1l, k1h, v1l, v1h)
        o = _pair_attention(qa_ref[:, cols] * scale, *kv, sinks=(sink_ref[2 * i], sink_ref[2 * i + 1]))
        oa_o[:, cols] = o.astype(oa_o.dtype)
    for i in range(N_HEADS_B // 2):
        cols = slice(2 * i * HEAD_DIM, (2 * i + 2) * HEAD_DIM)
        kl, kh, _, _ = _lane_halves(kb_ref[:, cols])
        vl, vh, _, _ = _lane_halves(vb_ref[:, cols])
        o = _pair_attention(qb_ref[:, cols] * scale, kl, kh, vl, vh)
        ob_o[:, cols] = o.astype(ob_o.dtype)


def _ctx_attention(sink, qa, ka, va, qb, kb, vb, seq_len):
    t = qa.shape[0]
    widths = (WIDTH_A, KV_WIDTH_A, KV_WIDTH_A, WIDTH_B, WIDTH_B, WIDTH_B)
    return pl.pallas_call(
        _ctx_attn_kernel,
        out_shape=[jax.ShapeDtypeStruct((t, WIDTH_A), BF16), jax.ShapeDtypeStruct((t, WIDTH_B), BF16)],
        grid=(t // seq_len,),
        in_specs=[pl.BlockSpec(memory_space=pltpu.SMEM)]
        + [pl.BlockSpec((seq_len, w), lambda i: (i, 0)) for w in widths],
        out_specs=[pl.BlockSpec((seq_len, WIDTH_A), lambda i: (i, 0)),
                   pl.BlockSpec((seq_len, WIDTH_B), lambda i: (i, 0))],
        compiler_params=_params(("arbitrary",)),
        name="ctx_attention",
    )(sink, qa, ka, va, qb, kb, vb)


def _win_attn_kernel(sink_ref, qa_ref, ka_ref, va_ref, ck_ref, cv_ref, oa_o, *, n_lat):
    scale = HEAD_DIM ** -0.5
    span = Q_BLK + 2 * WINDOW
    n = pl.program_id(1)
    start = pl.multiple_of(jnp.clip(n * Q_BLK - WINDOW, 0, n_lat - span), Q_BLK)
    k0l, k1h, k1l, k0h = _lane_halves(jnp.concatenate([ka_ref[pl.ds(start, span), :], ck_ref[...]], axis=0))
    v0l, v1h, v1l, v0h = _lane_halves(jnp.concatenate([va_ref[pl.ds(start, span), :], cv_ref[...]], axis=0))
    n_keys = k0l.shape[0]
    qpos = n * Q_BLK + lax.broadcasted_iota(jnp.int32, (Q_BLK, n_keys), 0)
    col = lax.broadcasted_iota(jnp.int32, (Q_BLK, n_keys), 1)
    ok = (col >= span) | (jnp.abs(qpos - (start + col)) <= WINDOW)
    for i in range(N_HEADS_A // 2):
        cols = slice(2 * i * HEAD_DIM, (2 * i + 2) * HEAD_DIM)
        kv = (k0l, k0h, v0l, v0h) if (2 * i) // GROUP_A == 0 else (k1l, k1h, v1l, v1h)
        o = _pair_attention(qa_ref[:, cols] * scale, *kv, sinks=(sink_ref[2 * i], sink_ref[2 * i + 1]),
                            adjust=lambda _, s: jnp.where(ok, s, NEG))
        oa_o[:, cols] = o.astype(oa_o.dtype)


def _win_attention(sink, qa, ka, va, cache_k, cache_v, layer, n_lat):
    t = qa.shape[0]
    nb = n_lat // Q_BLK
    past = cache_k.shape[2]
    return pl.pallas_call(
        functools.partial(_win_attn_kernel, n_lat=n_lat),
        out_shape=jax.ShapeDtypeStruct((t, WIDTH_A), BF16),
        grid=(t // n_lat, nb),
        in_specs=[
            pl.BlockSpec(memory_space=pltpu.SMEM),
            pl.BlockSpec((Q_BLK, WIDTH_A), lambda b, n: (b * nb + n, 0)),
            pl.BlockSpec((n_lat, KV_WIDTH_A), lambda b, n: (b, 0)),
            pl.BlockSpec((n_lat, KV_WIDTH_A), lambda b, n: (b, 0)),
            pl.BlockSpec((None, None, past, KV_WIDTH_A), lambda b, n: (b, layer, 0, 0)),
            pl.BlockSpec((None, None, past, KV_WIDTH_A), lambda b, n: (b, layer, 0, 0)),
        ],
        out_specs=pl.BlockSpec((Q_BLK, WIDTH_A), lambda b, n: (b * nb + n, 0)),
        compiler_params=_params(("arbitrary", "arbitrary")),
        name="window_attention",
    )(sink, qa, ka, va, cache_k, cache_v)


def _nbr_attn_kernel(qb_ref, kb_ref, vb_ref, ck_ref, cv_ref, bias_ref, ob_o, *, rows, kr):
    scale = HEAD_DIM ** -0.5
    r = pl.program_id(1)
    start = pl.multiple_of(jnp.clip(r - kr // 2, 0, rows - kr) * GRID_W, GRID_W)
    nwin = kr * GRID_W
    kcat = jnp.concatenate([kb_ref[pl.ds(start, nwin), :], ck_ref[...]], axis=0)
    vcat = jnp.concatenate([vb_ref[pl.ds(start, nwin), :], cv_ref[...]], axis=0)
    past = ck_ref.shape[0]
    for i in range(N_HEADS_B // 2):
        cols = slice(2 * i * HEAD_DIM, (2 * i + 2) * HEAD_DIM)
        kl, kh, _, _ = _lane_halves(kcat[:, cols])
        vl, vh, _, _ = _lane_halves(vcat[:, cols])
        o = _pair_attention(
            qb_ref[:, cols] * scale, kl, kh, vl, vh,
            adjust=lambda hh, s, i=i: s + jnp.concatenate([bias_ref[2 * i + hh], jnp.zeros((GRID_W, past), F32)],
                                                          axis=-1))
        ob_o[:, cols] = o.astype(ob_o.dtype)


def _nbr_bias_table(rel_bias, rows):
    kr = min(NA_ROWS, rows)
    cls = np.arange(kr)[:, None]
    m = np.arange(kr)[None, :]
    row_sel = (m - cls + NA_ROWS - 1)[:, :, None] == np.arange(2 * NA_ROWS - 1)[None, None, :]
    c = np.arange(GRID_W)[:, None]
    kc = np.arange(GRID_W)[None, :]
    dc = np.clip(kc - c, -(NA_COLS - 1), NA_COLS - 1) + NA_COLS - 1
    col_sel = dc[:, :, None] == np.arange(2 * NA_COLS - 1)[None, None, :]
    col_start = np.clip(c - NA_COLS // 2, 0, GRID_W - NA_COLS)
    ok = (kc >= col_start) & (kc < col_start + NA_COLS)
    tab = jnp.einsum('hrd,ymr,ckd->hycmk', rel_bias.astype(F32), row_sel.astype(np.float32),
                     col_sel.astype(np.float32), precision=lax.Precision.HIGHEST)
    tab = jnp.where(ok[None, None, :, None, :], tab, NEG)
    return tab.reshape(rel_bias.shape[0], kr, GRID_W, kr * GRID_W)


def _nbr_attention(qb, kb, vb, cache_k, cache_v, bias_tab, layer, n_lat):
    t = qb.shape[0]
    rows = n_lat // GRID_W
    kr = min(NA_ROWS, rows)
    past = cache_k.shape[2]

    def bias_map(b, r):
        return (0, r - jnp.clip(r - kr // 2, 0, rows - kr), 0, 0)

    return pl.pallas_call(
        functools.partial(_nbr_attn_kernel, rows=rows, kr=kr),
        out_shape=jax.ShapeDtypeStruct((t, WIDTH_B), BF16),
        grid=(t // n_lat, rows),
        in_specs=[
            pl.BlockSpec((GRID_W, WIDTH_B), lambda b, r: (b * rows + r, 0)),
            pl.BlockSpec((n_lat, WIDTH_B), lambda b, r: (b, 0)),
            pl.BlockSpec((n_lat, WIDTH_B), lambda b, r: (b, 0)),
            pl.BlockSpec((None, None, past, WIDTH_B), lambda b, r: (b, layer, 0, 0)),
            pl.BlockSpec((None, None, past, WIDTH_B), lambda b, r: (b, layer, 0, 0)),
            pl.BlockSpec((N_HEADS_B, None, GRID_W, kr * GRID_W), bias_map),
        ],
        out_specs=pl.BlockSpec((GRID_W, WIDTH_B), lambda b, r: (b * rows + r, 0)),
        compiler_params=_params(("arbitrary", "arbitrary")),
        name="neighbourhood_attention",
    )(qb, kb, vb, cache_k, cache_v, bias_tab)


ROW_TILE = 8
LANES = 128
assert ROW_TILE * LANES == D_MODEL


def _store_row_tiles(ref, row0, x):
    n = x.shape[0]
    for c in range(ROW_TILE):
        ref[pl.ds(row0 * ROW_TILE + c, n, stride=ROW_TILE), :] = x[:, c * LANES:(c + 1) * LANES]


def _load_row_tiles(ref, row0, n, row_stride=1, chunk0=0):
    return jnp.concatenate(
        [ref[pl.ds(row0 * ROW_TILE + chunk0 + c, n, stride=ROW_TILE * row_stride), :] for c in range(ROW_TILE)],
        axis=-1)


def _outproj_kernel(x_ref, oa_ref, ob_ref, oc_ref, w_ref, mod_ref, g_ref, wr_ref, br_ref, *rest):
    xn_o, h_o, lg_o = rest[-3:]
    mix = jnp.dot(oa_ref[...], w_ref[0:WIDTH_A, :], preferred_element_type=F32)
    mix += jnp.dot(ob_ref[...], w_ref[WIDTH_A:WIDTH_A + WIDTH_B, :], preferred_element_type=F32)
    mix += jnp.dot(oc_ref[...], w_ref[WIDTH_A + WIDTH_B:, :], preferred_element_type=F32)
    xn = x_ref[...] + mod_ref[2:3, :] * mix
    xn_o[...] = xn
    h = xn * lax.rsqrt(jnp.mean(xn * xn, axis=-1, keepdims=True) + EPS) * g_ref[...]
    h = h * (1.0 + mod_ref[4:5, :]) + mod_ref[3:4, :]
    h_hi = h.astype(BF16)
    h_o[...] = h_hi
    h_lo = (h - h_hi.astype(F32)).astype(BF16)
    wr = wr_ref[...]
    w_hi = wr.astype(BF16)
    w_lo = (wr - w_hi.astype(F32)).astype(BF16)
    lg = lax.dot_general(w_hi, h_hi, _NT, preferred_element_type=F32)
    lg += lax.dot_general(w_hi, h_lo, _NT, preferred_element_type=F32)
    lg += lax.dot_general(w_lo, h_hi, _NT, preferred_element_type=F32)
    lg_o[...] = lg + br_ref[...]


def _outproj(x, oa, ob, oc, w_out_bf, mod, group_of_tile, g_ffn, w_router_t, b_router_t, layer, tm, t_all, tok0,
             shared):
    t = x.shape[0]
    b0 = tok0 // tm
    assert tok0 % tm == 0
    in_specs = [
        pl.BlockSpec((tm, D_MODEL), lambda i: (i, 0)),
        pl.BlockSpec((tm, WIDTH_A), lambda i: (i, 0)),
        pl.BlockSpec((tm, WIDTH_B), lambda i: (i, 0)),
        pl.BlockSpec((tm, CONV_CH), lambda i: (i, 0)),
        pl.BlockSpec((None, D_MODEL, D_MODEL), lambda i: (layer, 0, 0)),
        pl.BlockSpec((None, None, N_MOD, D_MODEL), lambda i: (layer, group_of_tile(i), 0, 0)),
        pl.BlockSpec((None, 1, D_MODEL), lambda i: (layer, 0, 0)),
        pl.BlockSpec((None, N_EXPERTS, D_MODEL), lambda i: (layer, 0, 0)),
        pl.BlockSpec((None, N_EXPERTS, 1), lambda i: (layer, 0, 0)),
    ]
    args = [x, oa, ob, oc, w_out_bf, mod, g_ffn, w_router_t, b_router_t]
    aliases = {len(args): 1, len(args) + 1: 2}
    in_specs += [pl.BlockSpec(memory_space=pl.ANY)] * 2
    args += list(shared)
    return pl.pallas_call(
        _outproj_kernel,
        out_shape=[jax.ShapeDtypeStruct((t, D_MODEL), F32), jax.ShapeDtypeStruct((t_all, D_MODEL), BF16),
                   jax.ShapeDtypeStruct((N_EXPERTS, t_all), F32)],
        grid=(t // tm,),
        in_specs=in_specs,
        out_specs=[pl.BlockSpec((tm, D_MODEL), lambda i: (i, 0)),
                   pl.BlockSpec((tm, D_MODEL), lambda i: (b0 + i, 0)),
                   pl.BlockSpec((N_EXPERTS, tm), lambda i: (0, b0 + i))],
        input_output_aliases=aliases,
        compiler_params=_params(("arbitrary",), VMEM_LIMIT_V7X),
        name="outproj_router",
    )(*args)


X_SLOTS = 4


def _expert_kernel(seg_ref, n_used_ref, x_hbm, wg_ref, bg_ref, wu_ref, bu_ref, wd_ref, bd_ref, y_hbm,
                   x_buf, y_buf, xsem, ysem, zsem, *, n_blocks):
    e = pl.program_id(0)
    n_used = n_used_ref[0]
    blk_rows = MOE_BLK * ROW_TILE
    first_blk = seg_ref[SEG_DST, e, SEG_BLK]
    n_blk = seg_ref[SEG_CNT, e, SEG_BLK]

    def rows_of(g):
        return pl.ds(pl.multiple_of(g * blk_rows, blk_rows), blk_rows)

    def x_copy(g):
        return pltpu.make_async_copy(x_hbm.at[rows_of(g)], x_buf.at[g % X_SLOTS], xsem.at[g % X_SLOTS])

    def y_copy(g):
        return pltpu.make_async_copy(y_buf.at[g % 2], y_hbm.at[rows_of(g)], ysem.at[g % 2])

    def zero_copy(g):
        return pltpu.make_async_copy(y_buf.at[0], y_hbm.at[rows_of(g)], zsem)

    @pl.when(e == 0)
    def _():
        for g in range(X_SLOTS - 1):
            @pl.when(g < n_used)
            def _():
                x_copy(g).start()

    def block(b, carry):
        g = first_blk + b
        slot = g % 2

        @pl.when(g + (X_SLOTS - 1) < n_used)
        def _():
            x_copy(g + (X_SLOTS - 1)).start()

        x_copy(g).wait()

        @pl.when(g >= 2)
        def _():
            y_copy(g - 2).wait()

        x = _load_row_tiles(x_buf.at[g % X_SLOTS], 0, MOE_BLK).astype(BF16)
        gt = jnp.dot(x, wg_ref[...].astype(BF16), preferred_element_type=F32) + bg_ref[...]
        up = jnp.dot(x, wu_ref[...].astype(BF16), preferred_element_type=F32) + bu_ref[...]
        gt = jnp.minimum(gt, SWIGLU_LIMIT)
        up = jnp.clip(up, -SWIGLU_LIMIT, SWIGLU_LIMIT)
        a = gt * jax.nn.sigmoid(SWIGLU_ALPHA * gt) * (up + 1.0)
        y = jnp.dot(a.astype(BF16), wd_ref[...].astype(BF16), preferred_element_type=F32) + bd_ref[...]
        _store_row_tiles(y_buf.at[slot], 0, y)
        y_copy(g).start()
        return carry

    lax.fori_loop(0, n_blk, block, 0)

    @pl.when(e == pl.num_programs(0) - 1)
    def _():
        @pl.when(n_used >= 2)
        def _():
            y_copy(n_used - 2).wait()

        @pl.when(n_used >= 1)
        def _():
            y_copy(n_used - 1).wait()

        y_buf[0] = jnp.zeros((blk_rows, LANES), F32)

        def start_zero(g, carry):
            zero_copy(g).start()
            return carry

        def wait_zero(g, carry):
            zero_copy(g).wait()
            return carry

        lax.fori_loop(n_used, n_blocks, start_zero, 0)
        lax.fori_loop(n_used, n_blocks, wait_zero, 0)


def _experts(seg, n_used, x_sorted, w_gate, b_gate, w_up, b_up, w_down, b_down, layer):
    n_blocks = x_sorted.shape[0] // (MOE_BLK * ROW_TILE)
    wspec = pl.BlockSpec((None, None, D_MODEL, D_MODEL), lambda e, sg, nu: (layer, e, 0, 0))
    bspec = pl.BlockSpec((None, None, 1, D_MODEL), lambda e, sg, nu: (layer, e, 0, 0))
    anyspec = pl.BlockSpec(memory_space=pl.ANY)
    depth = w_gate.shape[0]
    b4 = lambda b: b.reshape(depth, N_EXPERTS, 1, D_MODEL)
    return pl.pallas_call(
        functools.partial(_expert_kernel, n_blocks=n_blocks),
        out_shape=jax.ShapeDtypeStruct(x_sorted.shape, F32),
        grid_spec=pltpu.PrefetchScalarGridSpec(
            num_scalar_prefetch=2,
            grid=(N_EXPERTS,),
            in_specs=[anyspec, wspec, bspec, wspec, bspec, wspec, bspec],
            out_specs=anyspec,
            scratch_shapes=[
                pltpu.VMEM((X_SLOTS, MOE_BLK * ROW_TILE, LANES), F32),
                pltpu.VMEM((2, MOE_BLK * ROW_TILE, LANES), F32),
                pltpu.SemaphoreType.DMA((X_SLOTS,)),
                pltpu.SemaphoreType.DMA((2,)),
                pltpu.SemaphoreType.DMA,
            ],
        ),
        compiler_params=_params(("arbitrary",), VMEM_LIMIT_V7X),
        name="experts",
    )(seg, n_used, x_sorted, w_gate, b4(b_gate), w_up, b4(b_up), w_down, b4(b_down))


CUM_CHUNK = 256


def _sublane_cumsum(x):
    row = lax.broadcasted_iota(jnp.int32, x.shape, 0)
    d = 1
    while d < x.shape[0]:
        x = x + jnp.where(row >= d, pltpu.roll(x, d, axis=0), 0)
        d *= 2
    return x


TOK_TILE = CUM_CHUNK
TILE_ROWS = TOP_K * TOK_TILE
SEG_LANES = 128
SEG_PAD_FROM = SEG_LANES - 2
SEG_PAD_LEN = SEG_LANES - 1
SEG_BLK = SEG_LANES - 3
SEG_SRC, SEG_CNT, SEG_DST = 0, 1, 2


def _route_kernel(lg_ref, tri_ref, gate_o, lpos_o, seg_o, used_o):
    lg = lg_ref[...]
    n_e, t = lg.shape
    e_iota = lax.broadcasted_iota(jnp.int32, lg.shape, 0)
    work = lg
    tops, hots = [], []
    for _ in range(TOP_K):
        m = jnp.max(work, axis=0, keepdims=True)
        first = jnp.min(jnp.where(work == m, e_iota, n_e), axis=0, keepdims=True)
        hot = e_iota == first
        work = jnp.where(hot, -jnp.inf, work)
        tops.append(m)
        hots.append(hot)
    ex = [jnp.exp(m - tops[0]) for m in tops]
    den = ex[0] + ex[1] + ex[2] + ex[3]
    for k in range(TOP_K):
        gate_o[k:k + 1, :] = ex[k] / den

    chosen = jnp.where(hots[0] | hots[1] | hots[2] | hots[3], 1.0, 0.0)
    tri = tri_ref[...]
    lane = lax.broadcasted_iota(jnp.int32, (n_e, SEG_LANES), 1)
    seg_cnt = jnp.zeros((n_e, SEG_LANES), jnp.int32)
    seg_before = jnp.zeros((n_e, SEG_LANES), jnp.int32)
    carry = jnp.zeros((n_e, 1), F32)
    rank_in_tile = []
    n_tiles = t // TOK_TILE
    for c in range(n_tiles):
        chunk = chosen[:, c * TOK_TILE:(c + 1) * TOK_TILE]
        inc = jnp.dot(chunk.astype(BF16), tri, preferred_element_type=F32)
        cnt = inc[:, TOK_TILE - 1:TOK_TILE]
        rank_in_tile.append((inc - chunk).astype(jnp.int32))
        seg_cnt = jnp.where(lane == c, cnt.astype(jnp.int32), seg_cnt)
        seg_before = jnp.where(lane == c, carry.astype(jnp.int32), seg_before)
        carry = carry + cnt

    counts = jnp.broadcast_to(carry.astype(jnp.int32), (n_e, SEG_LANES))
    padded = ((counts + (MOE_BLK - 1)) >> MOE_SHIFT) << MOE_SHIFT
    pad_end = _sublane_cumsum(padded)
    pad_start = pad_end - padded
    seg_src = _sublane_cumsum(seg_cnt) - seg_cnt
    seg_o[SEG_SRC] = seg_src
    seg_o[SEG_CNT] = jnp.where(lane == SEG_PAD_LEN, padded - counts,
                               jnp.where(lane == SEG_BLK, padded >> MOE_SHIFT, seg_cnt))
    seg_o[SEG_DST] = jnp.where(lane == SEG_PAD_FROM, pad_start + counts,
                               jnp.where(lane == SEG_BLK, pad_start >> MOE_SHIFT, pad_start + seg_before))

    for c in range(n_tiles):
        pos = seg_src[:, c:c + 1] + rank_in_tile[c]
        for k in range(TOP_K):
            lpos_o[k:k + 1, c * TOK_TILE:(c + 1) * TOK_TILE] = jnp.sum(
                jnp.where(hots[k][:, c * TOK_TILE:(c + 1) * TOK_TILE], pos, 0), axis=0, keepdims=True)

    used_o[...] = jnp.broadcast_to(pad_end[n_e - 1:n_e, :] >> MOE_SHIFT, used_o.shape)


def _route(logits):
    t = logits.shape[1]
    assert t % TOK_TILE == 0 and t // TOK_TILE <= SEG_BLK
    tri = jnp.asarray(np.triu(np.ones((CUM_CHUNK, CUM_CHUNK), np.float32)), dtype=BF16)
    gates, lpos, seg, used = pl.pallas_call(
        _route_kernel,
        out_shape=[jax.ShapeDtypeStruct((TOP_K, t), F32), jax.ShapeDtypeStruct((TOP_K, t), jnp.int32),
                   jax.ShapeDtypeStruct((3, N_EXPERTS, SEG_LANES), jnp.int32),
                   jax.ShapeDtypeStruct((8, SEG_LANES), jnp.int32)],
        compiler_params=_params(None, VMEM_LIMIT_V7X),
        name="route",
    )(logits, tri)
    return gates, lpos, seg, used[0, :1]


def _segment_copies(src, dst, src0, dst0, cnt, sem, wait):
    @pl.when(cnt > 0)
    def _():
        size = cnt * ROW_TILE
        cp = pltpu.make_async_copy(
            src.at[pl.ds(pl.multiple_of(src0 * ROW_TILE, ROW_TILE), size)],
            dst.at[pl.ds(pl.multiple_of(dst0 * ROW_TILE, ROW_TILE), size)], sem)
        if wait:
            cp.wait()
        else:
            cp.start()


def _dispatch_kernel(seg_ref, nu_ref, h_ref, lpos_ref, x_hbm, z_buf, zero_buf, sem, zsem, *, n_tiles, n_blocks):
    c = pl.program_id(0)
    slot = c % 2

    def wait_tile(s):
        pltpu.make_async_copy(z_buf.at[s], x_hbm.at[pl.ds(0, TILE_ROWS * ROW_TILE)], sem.at[s]).wait()

    def padding(wait):
        def per_expert(e, carry):
            _segment_copies(zero_buf, x_hbm, 0, seg_ref[SEG_DST, e, SEG_PAD_FROM], seg_ref[SEG_CNT, e, SEG_PAD_LEN],
                            zsem, wait)
            return carry

        lax.fori_loop(0, N_EXPERTS, per_expert, 0)

        def per_block(b, carry):
            cp = pltpu.make_async_copy(
                zero_buf, x_hbm.at[pl.ds(pl.multiple_of(b * (MOE_BLK * ROW_TILE), MOE_BLK * ROW_TILE),
                                         MOE_BLK * ROW_TILE)], zsem)
            if wait:
                cp.wait()
            else:
                cp.start()
            return carry

        lax.fori_loop(nu_ref[0], n_blocks, per_block, 0)

    @pl.when(c == 0)
    def _():
        zero_buf[...] = jnp.zeros(zero_buf.shape, F32)
        padding(False)

    @pl.when(c >= 2)
    def _():
        wait_tile(slot)

    row = lax.broadcasted_iota(jnp.int32, (TILE_ROWS, TOK_TILE), 0)
    hit = row == lpos_ref[0:1, :]
    for k in range(1, TOP_K):
        hit = hit | (row == lpos_ref[k:k + 1, :])
    z = jnp.dot(jnp.where(hit, 1.0, 0.0).astype(BF16), h_ref[...], preferred_element_type=F32)
    _store_row_tiles(z_buf.at[slot], 0, z)

    def per_expert(e, carry):
        _segment_copies(z_buf.at[slot], x_hbm, seg_ref[SEG_SRC, e, c], seg_ref[SEG_DST, e, c], seg_ref[SEG_CNT, e, c],
                        sem.at[slot], False)
        return carry

    lax.fori_loop(0, N_EXPERTS, per_expert, 0)

    @pl.when(c == n_tiles - 1)
    def _():
        if n_tiles > 1:
            wait_tile(1 - slot)
        wait_tile(slot)
        padding(True)


def _dispatch(seg, n_used, h_all, lpos, n_blocks):
    t = h_all.shape[0]
    n_tiles = t // TOK_TILE
    return pl.pallas_call(
        functools.partial(_dispatch_kernel, n_tiles=n_tiles, n_blocks=n_blocks),
        out_shape=jax.ShapeDtypeStruct((n_blocks * MOE_BLK * ROW_TILE, LANES), F32),
        grid_spec=pltpu.PrefetchScalarGridSpec(
            num_scalar_prefetch=2,
            grid=(n_tiles,),
            in_specs=[pl.BlockSpec((TOK_TILE, D_MODEL), lambda c, sg, nu: (c, 0)),
                      pl.BlockSpec((TOP_K, TOK_TILE), lambda c, sg, nu: (0, c))],
            out_specs=pl.BlockSpec(memory_space=pl.ANY),
            scratch_shapes=[
                pltpu.VMEM((2, TILE_ROWS * ROW_TILE, LANES), F32),
                pltpu.VMEM((MOE_BLK * ROW_TILE, LANES), F32),
                pltpu.SemaphoreType.DMA((2,)),
                pltpu.SemaphoreType.DMA,
            ],
        ),
        compiler_params=_params(("arbitrary",), VMEM_LIMIT_V7X),
        name="dispatch",
    )(seg, n_used, h_all, lpos)


def _combine_kernel(seg_ref, x_ref, y_hbm, lpos_ref, lpos_t_ref, gate_ref, mod_ref, o_ref, y_buf, sem, *, tile0,
                    n_tiles):
    c = pl.program_id(0)
    slot = c % 2

    def fetch(cc):
        def per_expert(e, carry):
            _segment_copies(y_hbm, y_buf.at[cc % 2], seg_ref[SEG_DST, e, tile0 + cc], seg_ref[SEG_SRC, e, tile0 + cc],
                            seg_ref[SEG_CNT, e, tile0 + cc], sem.at[cc % 2], False)
            return carry

        lax.fori_loop(0, N_EXPERTS, per_expert, 0)

    @pl.when(c == 0)
    def _():
        fetch(0)

    @pl.when(c + 1 < n_tiles)
    def _():
        fetch(c + 1)

    pltpu.make_async_copy(y_hbm.at[pl.ds(0, TILE_ROWS * ROW_TILE)], y_buf.at[slot], sem.at[slot]).wait()

    row = lax.broadcasted_iota(jnp.int32, (TILE_ROWS, TOK_TILE), 0)
    row_gate = jnp.zeros((TILE_ROWS, 1), F32)
    for k in range(TOP_K):
        row_gate += jnp.sum(jnp.where(row == lpos_ref[k:k + 1, :], gate_ref[k:k + 1, :], 0.0), axis=1, keepdims=True)
    ys = _load_row_tiles(y_buf.at[slot], 0, TILE_ROWS) * row_gate
    ys_hi = ys.astype(BF16)
    ys_lo = (ys - ys_hi.astype(F32)).astype(BF16)
    col = lax.broadcasted_iota(jnp.int32, (TOK_TILE, TILE_ROWS), 1)
    hit = col == lpos_t_ref[:, 0:1]
    for k in range(1, TOP_K):
        hit = hit | (col == lpos_t_ref[:, k:k + 1])
    u = jnp.where(hit, 1.0, 0.0).astype(BF16)
    mix = jnp.dot(u, ys_hi, preferred_element_type=F32) + jnp.dot(u, ys_lo, preferred_element_type=F32)
    o_ref[...] = x_ref[...] + mod_ref[5:6, :] * mix


def _combine(seg, x_mid, y_sorted, lpos, lpos_t, gates, mod, group_of_tile, layer, tok0):
    t = x_mid.shape[0]
    tile0 = tok0 // TOK_TILE
    n_tiles = t // TOK_TILE
    return pl.pallas_call(
        functools.partial(_combine_kernel, tile0=tile0, n_tiles=n_tiles),
        out_shape=jax.ShapeDtypeStruct((t, D_MODEL), F32),
        grid_spec=pltpu.PrefetchScalarGridSpec(
            num_scalar_prefetch=1,
            grid=(n_tiles,),
            in_specs=[
                pl.BlockSpec((TOK_TILE, D_MODEL), lambda c, sg: (c, 0)),
                pl.BlockSpec(memory_space=pl.ANY),
                pl.BlockSpec((TOP_K, TOK_TILE), lambda c, sg: (0, tile0 + c)),
                pl.BlockSpec((TOK_TILE, TOP_K), lambda c, sg: (tile0 + c, 0)),
                pl.BlockSpec((TOP_K, TOK_TILE), lambda c, sg: (0, tile0 + c)),
                pl.BlockSpec((None, None, N_MOD, D_MODEL), lambda c, sg: (layer, group_of_tile(c), 0, 0)),
            ],
            out_specs=pl.BlockSpec((TOK_TILE, D_MODEL), lambda c, sg: (c, 0)),
            scratch_shapes=[pltpu.VMEM((2, TILE_ROWS * ROW_TILE, LANES), F32), pltpu.SemaphoreType.DMA((2,))],
        ),
        compiler_params=_params(("arbitrary",), VMEM_LIMIT_V7X),
        name="combine",
    )(seg, x_mid, y_sorted, lpos, lpos_t, gates, mod)


def _rope_tables(n_lat):
    quarter = HEAD_DIM // 4
    t = jnp.arange(n_lat)
    inv = ROPE_BASE ** (-jnp.arange(quarter, dtype=F32) / quarter)
    ang_r = (t // GRID_W).astype(F32)[:, None] * inv
    ang_c = (t % GRID_W).astype(F32)[:, None] * inv
    cos = jnp.concatenate([jnp.cos(ang_r)] * 2 + [jnp.cos(ang_c)] * 2, axis=-1)
    sin = jnp.concatenate([-jnp.sin(ang_r), jnp.sin(ang_r), -jnp.sin(ang_c), jnp.sin(ang_c)], axis=-1)
    return jnp.concatenate([cos, cos], axis=-1), jnp.concatenate([sin, sin], axis=-1)


def _block_diag_ones():
    idx = np.arange(MXU_COLS_V7X) // HEAD_DIM
    return jnp.asarray(idx[:, None] == idx[None, :], dtype=BF16)


def kernel(x_prompt, x_sample, cache_k_win, cache_v_win, cache_k_nbr, cache_v_nbr, c, c_ctx, w_mod, b_mod, g_mix, g_ffn, w_in, w_out, qn_win, kn_win, qn_nbr, kn_nbr, sink_win, rel_bias_nbr, conv_w, w_router, b_router, w_gate, b_gate, w_up, b_up, w_down, b_down):
    bsz, n_ctx, d = x_prompt.shape
    dbs, n_lat, _ = x_sample.shape
    depth = w_in.shape[0]
    past = cache_k_win.shape[2]
    assert d == D_MODEL and dbs + 1 <= COND_ROWS and n_lat % GRID_W == 0 and n_lat >= Q_BLK + 2 * WINDOW
    t_ctx, t_lat = bsz * n_ctx, dbs * n_lat

    cond = jnp.concatenate([c_ctx[None], c, jnp.zeros((COND_ROWS - 1 - dbs, d), F32)], axis=0)
    mod = _modulation(cond, w_mod, b_mod)

    w_in_bf = w_in.astype(BF16)
    w_out_bf = w_out.astype(BF16)
    norm_w = jnp.concatenate([jnp.tile(qn_win, (1, N_HEADS_A)), jnp.tile(kn_win, (1, N_KV_A)),
                              jnp.tile(qn_nbr, (1, N_HEADS_B)), jnp.tile(kn_nbr, (1, N_HEADS_B))], axis=-1)[:, None, :]
    ones_bd = _block_diag_ones()
    rope_tabs = _rope_tables(n_lat)
    g_mix3, g_ffn3 = g_mix[:, None, :], g_ffn[:, None, :]
    w_router_t = jnp.swapaxes(w_router, 1, 2)
    b_router_t = b_router[:, :, None]
    t_all = t_ctx + t_lat
    ck_win = cache_k_win.reshape(dbs, depth, past, KV_WIDTH_A)
    cv_win = cache_v_win.reshape(dbs, depth, past, KV_WIDTH_A)
    ck_nbr = cache_k_nbr.reshape(dbs, depth, past, WIDTH_B)
    cv_nbr = cache_v_nbr.reshape(dbs, depth, past, WIDTH_B)

    tm_ctx = 2 * n_ctx
    tm_lat = 512
    ctx_group = lambda i: 0
    lat_group_in = lambda i: 1 + i
    lat_group_out = lambda i: 1 + (i * tm_lat) // n_lat
    lat_group_comb = lambda i: 1 + (i * TOK_TILE) // n_lat
    assert t_ctx % TOK_TILE == 0 and n_lat % TOK_TILE == 0

    xp = x_prompt.reshape(t_ctx, d)
    xs = x_sample.reshape(t_lat, d)
    caches = [[], [], [], []]
    for l in range(depth):
        qa, qb, ka, va, kb, vb, oc = _inproj(xp, mod, ctx_group, g_mix3, w_in_bf, l, norm_w, ones_bd, conv_w,
                                             tm_ctx, n_ctx)
        oa, ob = _ctx_attention(sink_win[l], qa, ka, va, qb, kb, vb, n_ctx)
        shared = (jnp.zeros((t_all, d), BF16), jnp.zeros((N_EXPERTS, t_all), F32))
        xp_mid, h_all, lg_all = _outproj(xp, oa, ob, oc, w_out_bf, mod, ctx_group, g_ffn3, w_router_t, b_router_t, l,
                                         tm_ctx, t_all, 0, shared)
        for lst, a in zip(caches, (ka, va, kb, vb)):
            lst.append(a)

        qa, qb, ka, va, kb, vb, oc = _inproj(xs, mod, lat_group_in, g_mix3, w_in_bf, l, norm_w, ones_bd, conv_w,
                                             n_lat, n_lat, rope_tabs)
        oa = _win_attention(sink_win[l], qa, ka, va, ck_win, cv_win, l, n_lat)
        ob = _nbr_attention(qb, kb, vb, ck_nbr, cv_nbr, _nbr_bias_table(rel_bias_nbr[l], n_lat // GRID_W), l, n_lat)
        xs_mid, h_all, lg_all = _outproj(xs, oa, ob, oc, w_out_bf, mod, lat_group_out, g_ffn3, w_router_t, b_router_t,
                                         l, tm_lat, t_all, t_ctx, shared=(h_all, lg_all))

        gates, lpos, seg, n_used = _route(lg_all)
        x_sorted = _dispatch(seg, n_used, h_all, lpos, t_all * TOP_K // MOE_BLK + N_EXPERTS)
        y_sorted = _experts(seg, n_used, x_sorted, w_gate, b_gate, w_up, b_up, w_down, b_down, l)
        lpos_t = lpos.T
        xp = _combine(seg, xp_mid, y_sorted, lpos, lpos_t, gates, mod, ctx_group, l, 0)
        xs = _combine(seg, xs_mid, y_sorted, lpos, lpos_t, gates, mod, lat_group_comb, l, t_ctx)

    new_v_win = jnp.stack([a.reshape(bsz, n_ctx, KV_WIDTH_A) for a in caches[1]], axis=1).reshape(
        bsz, depth, n_ctx, N_KV_A, HEAD_DIM)
    new_k_nbr = jnp.stack([a.reshape(bsz, n_ctx, WIDTH_B) for a in caches[2]], axis=1).reshape(
        bsz, depth, n_ctx, N_HEADS_B, HEAD_DIM)
    new_v_nbr = jnp.stack([a.reshape(bsz, n_ctx, WIDTH_B) for a in caches[3]], axis=1).reshape(
        bsz, depth, n_ctx, N_HEADS_B, HEAD_DIM)
    new_k_win = jnp.stack([a.reshape(bsz, n_ctx, KV_WIDTH_A) for a in caches[0]], axis=1).reshape(
        bsz, depth, n_ctx, N_KV_A, HEAD_DIM)
    return (xp.reshape(bsz, n_ctx, d), xs.reshape(dbs, n_lat, d), new_k_win, new_v_win, new_k_nbr, new_v_nbr)
```

```python
import functools

import numpy as np
import jax
import jax.numpy as jnp
from jax import lax
from jax.experimental import pallas as pl
from jax.experimental.pallas import tpu as pltpu

F32 = jnp.float32
BF16 = jnp.bfloat16

D_MODEL = 1024
HEAD_DIM = 64
GRID_W = 64
N_HEADS_A = 8
N_KV_A = 2
GROUP_A = N_HEADS_A // N_KV_A
WINDOW = 128
Q_BLK = 128
N_HEADS_B = 4
NA_ROWS = 8
NA_COLS = 16
CONV_CH = 256
CONV_W = 3
WIDTH_A = N_HEADS_A * HEAD_DIM
KV_WIDTH_A = N_KV_A * HEAD_DIM
WIDTH_B = N_HEADS_B * HEAD_DIM
IN_COLS = WIDTH_A + 2 * KV_WIDTH_A + 3 * WIDTH_B + 3 * CONV_CH
N_EXPERTS = 32
TOP_K = 4
SWIGLU_LIMIT = 7.0
SWIGLU_ALPHA = 1.702
ROPE_BASE = 10000.0
EPS = 1e-6
NEG = -1e30
N_MOD = 6

C_QA = 0
C_KA = C_QA + WIDTH_A
C_VA = C_KA + KV_WIDTH_A
C_QB = C_VA + KV_WIDTH_A
C_KB = C_QB + WIDTH_B
C_VB = C_KB + WIDTH_B
C_U = C_VB + WIDTH_B
C_GB = C_U + CONV_CH
C_GC = C_GB + CONV_CH

MXU_COLS_V7X = 256
COND_ROWS = 8
MOE_BLK = 256
MOE_SHIFT = MOE_BLK.bit_length() - 1
assert 1 << MOE_SHIFT == MOE_BLK
VMEM_LIMIT_V7X = 56 * 1024 * 1024

_NT = (((1,), (1,)), ((), ()))


def _params(sem, vmem=None):
    return pltpu.CompilerParams(dimension_semantics=sem, vmem_limit_bytes=vmem)


def _mod_kernel(c_ref, w_ref, b_ref, o_ref):
    c = c_ref[...]
    s = c * jax.nn.sigmoid(c)
    o_ref[...] = jnp.dot(s.astype(BF16), w_ref[...].astype(BF16), preferred_element_type=F32) + b_ref[...]


def _modulation(cond, w_mod, b_mod):
    depth = w_mod.shape[0]
    out = pl.pallas_call(
        _mod_kernel,
        out_shape=jax.ShapeDtypeStruct((depth, COND_ROWS, N_MOD * D_MODEL), F32),
        grid=(depth, N_MOD),
        in_specs=[
            pl.BlockSpec((COND_ROWS, D_MODEL), lambda l, j: (0, 0)),
            pl.BlockSpec((None, D_MODEL, D_MODEL), lambda l, j: (l, 0, j)),
            pl.BlockSpec((None, 1, D_MODEL), lambda l, j: (l, 0, j)),
        ],
        out_specs=pl.BlockSpec((None, COND_ROWS, D_MODEL), lambda l, j: (l, 0, j)),
        compiler_params=_params(("arbitrary", "arbitrary")),
        name="modulation",
    )(cond, w_mod, b_mod.reshape(depth, 1, N_MOD * D_MODEL))
    return out.reshape(depth, COND_ROWS, N_MOD, D_MODEL)


def _head_norm(x, w_row, ones_ref):
    width = x.shape[1]
    sq = (x * x).astype(BF16)
    parts = []
    for c0 in range(0, width, MXU_COLS_V7X):
        wd = min(MXU_COLS_V7X, width - c0)
        parts.append(jnp.dot(sq[:, c0:c0 + wd], ones_ref[:wd, :wd], preferred_element_type=F32))
    ss = parts[0] if len(parts) == 1 else jnp.concatenate(parts, axis=-1)
    return x * lax.rsqrt(ss * (1.0 / HEAD_DIM) + EPS) * w_row


def _rope(x, cos, sin):
    width = x.shape[1]
    lane = lax.broadcasted_iota(jnp.int32, x.shape, 1)
    quarter = HEAD_DIM // 4
    partner = jnp.where((lane % (2 * quarter)) < quarter,
                        pltpu.roll(x, width - quarter, axis=1), pltpu.roll(x, quarter, axis=1))
    reps = width // cos.shape[1]
    cos_w = cos if reps == 1 else jnp.concatenate([cos] * reps, axis=-1)
    sin_w = sin if reps == 1 else jnp.concatenate([sin] * reps, axis=-1)
    return x * cos_w + partner * sin_w


def _inproj_kernel(*refs, seq_len, rope):
    if rope:
        (x_ref, mod_ref, g_ref, w_ref, nw_ref, ones_ref, cw_ref, cos_ref, sin_ref,
         qa_o, qb_o, ka_o, va_o, kb_o, vb_o, oc_o) = refs
    else:
        (x_ref, mod_ref, g_ref, w_ref, nw_ref, ones_ref, cw_ref,
         qa_o, qb_o, ka_o, va_o, kb_o, vb_o, oc_o) = refs
    x = x_ref[...]
    tm = x.shape[0]
    h = x * lax.rsqrt(jnp.mean(x * x, axis=-1, keepdims=True) + EPS) * g_ref[...]
    h = h * (1.0 + mod_ref[1:2, :]) + mod_ref[0:1, :]
    p = jnp.dot(h.astype(BF16), w_ref[...], preferred_element_type=F32)

    qa = _head_norm(p[:, C_QA:C_KA], nw_ref[:, 0:WIDTH_A], ones_ref)
    ka = _head_norm(p[:, C_KA:C_VA], nw_ref[:, WIDTH_A:WIDTH_A + KV_WIDTH_A], ones_ref)
    o_qb = WIDTH_A + KV_WIDTH_A
    qb = _head_norm(p[:, C_QB:C_KB], nw_ref[:, o_qb:o_qb + WIDTH_B], ones_ref)
    kb = _head_norm(p[:, C_KB:C_VB], nw_ref[:, o_qb + WIDTH_B:o_qb + 2 * WIDTH_B], ones_ref)
    if rope:
        cos, sin = cos_ref[...], sin_ref[...]
        qa = _rope(qa, cos, sin)
        ka = _rope(ka, cos, sin)
    qa_o[...] = qa.astype(BF16)
    qb_o[...] = qb.astype(BF16)
    ka_o[...] = ka
    va_o[...] = p[:, C_VA:C_QB]
    kb_o[...] = kb
    vb_o[...] = p[:, C_VB:C_U]

    z = p[:, C_GC:C_GC + CONV_CH] * p[:, C_U:C_GB]
    row = lax.broadcasted_iota(jnp.int32, z.shape, 0) % seq_len
    z_prev = jnp.where(row == 0, 0.0, pltpu.roll(z, 1, axis=0))
    z_next = jnp.where(row == seq_len - 1, 0.0, pltpu.roll(z, tm - 1, axis=0))
    y = z_prev * cw_ref[0:1, :] + z * cw_ref[1:2, :] + z_next * cw_ref[2:3, :]
    oc_o[...] = (p[:, C_GB:C_GC] * y).astype(BF16)


def _inproj(x, mod, group_of_tile, g_mix, w_in_bf, layer, norm_w, ones_bd, conv_w, tm, seq_len, rope_tabs=None):
    t = x.shape[0]
    rope = rope_tabs is not None
    in_specs = [
        pl.BlockSpec((tm, D_MODEL), lambda i: (i, 0)),
        pl.BlockSpec((None, None, N_MOD, D_MODEL), lambda i: (layer, group_of_tile(i), 0, 0)),
        pl.BlockSpec((None, 1, D_MODEL), lambda i: (layer, 0, 0)),
        pl.BlockSpec((None, D_MODEL, IN_COLS), lambda i: (layer, 0, 0)),
        pl.BlockSpec((None, 1, norm_w.shape[-1]), lambda i: (layer, 0, 0)),
        pl.BlockSpec(ones_bd.shape, lambda i: (0, 0)),
        pl.BlockSpec((None, CONV_W, CONV_CH), lambda i: (layer, 0, 0)),
    ]
    args = [x, mod, g_mix, w_in_bf, norm_w, ones_bd, conv_w]
    if rope:
        in_specs += [pl.BlockSpec(rope_tabs[0].shape, lambda i: (0, 0))] * 2
        args += list(rope_tabs)
    widths = (WIDTH_A, WIDTH_B, KV_WIDTH_A, KV_WIDTH_A, WIDTH_B, WIDTH_B, CONV_CH)
    dtypes = (BF16, BF16, F32, F32, F32, F32, BF16)
    return pl.pallas_call(
        functools.partial(_inproj_kernel, seq_len=seq_len, rope=rope),
        out_shape=[jax.ShapeDtypeStruct((t, w), dt) for w, dt in zip(widths, dtypes)],
        grid=(t // tm,),
        in_specs=in_specs,
        out_specs=[pl.BlockSpec((tm, w), lambda i: (i, 0)) for w in widths],
        compiler_params=_params(("arbitrary",), VMEM_LIMIT_V7X),
        name="inproj_rope" if rope else "inproj",
    )(*args)


def _pair_attention(q2, k_lo, k_hi, v_lo, v_hi, sinks=None, adjust=None):
    m = k_lo.shape[0]
    k2 = jnp.concatenate([k_lo, k_hi], axis=0)
    v2 = jnp.concatenate([v_lo, v_hi], axis=0)
    s = lax.dot_general(q2, k2, _NT, preferred_element_type=F32)
    ps, dens = [], []
    for i in range(2):
        si = s[:, i * m:(i + 1) * m]
        if adjust is not None:
            si = adjust(i, si)
        mx = jnp.max(si, axis=-1, keepdims=True)
        if sinks is not None:
            mx = jnp.maximum(mx, sinks[i])
        p = jnp.exp(si - mx)
        den = jnp.sum(p, axis=-1, keepdims=True)
        if sinks is not None:
            den = den + jnp.exp(sinks[i] - mx)
        ps.append(p.astype(BF16))
        dens.append(den)
    o = jnp.dot(jnp.concatenate(ps, axis=-1), v2, preferred_element_type=F32)
    lane = lax.broadcasted_iota(jnp.int32, o.shape, 1)
    return o / jnp.where(lane < HEAD_DIM, dens[0], dens[1])


def _lane_halves(x):
    lane = lax.broadcasted_iota(jnp.int32, x.shape, 1)
    lo = lane < HEAD_DIM
    sw = pltpu.roll(x, HEAD_DIM, axis=1)
    z = jnp.zeros_like(x)
    return tuple(jnp.where(c, y, z).astype(BF16) for c, y in ((lo, x), (~lo, x), (lo, sw), (~lo, sw)))


def _ctx_attn_kernel(sink_ref, qa_ref, ka_ref, va_ref, qb_ref, kb_ref, vb_ref, oa_o, ob_o):
    scale = HEAD_DIM ** -0.5
    k0l, k1h, k1l, k0h = _lane_halves(ka_ref[...])
    v0l, v1h, v1l, v0h = _lane_halves(va_ref[...])
    for i in range(N_HEADS_A // 2):
        cols = slice(2 * i * HEAD_DIM, (2 * i + 2) * HEAD_DIM)
        kv = (k0l, k0h, v0l, v0h) if (2 * i) // GROUP_A == 0 else (k1l, k1h, v1l, v1h)
        o = _pair_attention(qa_ref[:, cols] * scale, *kv, sinks=(sink_ref[2 * i], sink_ref[2 * i + 1]))
        oa_o[:, cols] = o.astype(oa_o.dtype)
    for i in range(N_HEADS_B // 2):
        cols = slice(2 * i * HEAD_DIM, (2 * i + 2) * HEAD_DIM)
        kl, kh, _, _ = _lane_halves(kb_ref[:, cols])
        vl, vh, _, _ = _lane_halves(vb_ref[:, cols])
        o = _pair_attention(qb_ref[:, cols] * scale, kl, kh, vl, vh)
        ob_o[:, cols] = o.astype(ob_o.dtype)


def _ctx_attention(sink, qa, ka, va, qb, kb, vb, seq_len):
    t = qa.shape[0]
    widths = (WIDTH_A, KV_WIDTH_A, KV_WIDTH_A, WIDTH_B, WIDTH_B, WIDTH_B)
    return pl.pallas_call(
        _ctx_attn_kernel,
        out_shape=[jax.ShapeDtypeStruct((t, WIDTH_A), BF16), jax.ShapeDtypeStruct((t, WIDTH_B), BF16)],
        grid=(t // seq_len,),
        in_specs=[pl.BlockSpec(memory_space=pltpu.SMEM)]
        + [pl.BlockSpec((seq_len, w), lambda i: (i, 0)) for w in widths],
        out_specs=[pl.BlockSpec((seq_len, WIDTH_A), lambda i: (i, 0)),
                   pl.BlockSpec((seq_len, WIDTH_B), lambda i: (i, 0))],
        compiler_params=_params(("arbitrary",)),
        name="ctx_attention",
    )(sink, qa, ka, va, qb, kb, vb)


def _win_attn_kernel(sink_ref, qa_ref, ka_ref, va_ref, ck_ref, cv_ref, oa_o, *, n_lat):
    scale = HEAD_DIM ** -0.5
    span = Q_BLK + 2 * WINDOW
    n = pl.program_id(1)
    start = pl.multiple_of(jnp.clip(n * Q_BLK - WINDOW, 0, n_lat - span), Q_BLK)
    k0l, k1h, k1l, k0h = _lane_halves(jnp.concatenate([ka_ref[pl.ds(start, span), :], ck_ref[...]], axis=0))
    v0l, v1h, v1l, v0h = _lane_halves(jnp.concatenate([va_ref[pl.ds(start, span), :], cv_ref[...]], axis=0))
    n_keys = k0l.shape[0]
    qpos = n * Q_BLK + lax.broadcasted_iota(jnp.int32, (Q_BLK, n_keys), 0)
    col = lax.broadcasted_iota(jnp.int32, (Q_BLK, n_keys), 1)
    ok = (col >= span) | (jnp.abs(qpos - (start + col)) <= WINDOW)
    for i in range(N_HEADS_A // 2):
        cols = slice(2 * i * HEAD_DIM, (2 * i + 2) * HEAD_DIM)
        kv = (k0l, k0h, v0l, v0h) if (2 * i) // GROUP_A == 0 else (k1l, k1h, v1l, v1h)
        o = _pair_attention(qa_ref[:, cols] * scale, *kv, sinks=(sink_ref[2 * i], sink_ref[2 * i + 1]),
                            adjust=lambda _, s: jnp.where(ok, s, NEG))
        oa_o[:, cols] = o.astype(oa_o.dtype)


def _win_attention(sink, qa, ka, va, cache_k, cache_v, layer, n_lat):
    t = qa.shape[0]
    nb = n_lat // Q_BLK
    past = cache_k.shape[2]
    return pl.pallas_call(
        functools.partial(_win_attn_kernel, n_lat=n_lat),
        out_shape=jax.ShapeDtypeStruct((t, WIDTH_A), BF16),
        grid=(t // n_lat, nb),
        in_specs=[
            pl.BlockSpec(memory_space=pltpu.SMEM),
            pl.BlockSpec((Q_BLK, WIDTH_A), lambda b, n: (b * nb + n, 0)),
            pl.BlockSpec((n_lat, KV_WIDTH_A), lambda b, n: (b, 0)),
            pl.BlockSpec((n_lat, KV_WIDTH_A), lambda b, n: (b, 0)),
            pl.BlockSpec((None, None, past, KV_WIDTH_A), lambda b, n: (b, layer, 0, 0)),
            pl.BlockSpec((None, None, past, KV_WIDTH_A), lambda b, n: (b, layer, 0, 0)),
        ],
        out_specs=pl.BlockSpec((Q_BLK, WIDTH_A), lambda b, n: (b * nb + n, 0)),
        compiler_params=_params(("arbitrary", "arbitrary")),
        name="window_attention",
    )(sink, qa, ka, va, cache_k, cache_v)


def _nbr_attn_kernel(qb_ref, kb_ref, vb_ref, ck_ref, cv_ref, bias_ref, ob_o, *, rows, kr):
    scale = HEAD_DIM ** -0.5
    r = pl.program_id(1)
    start = pl.multiple_of(jnp.clip(r - kr // 2, 0, rows - kr) * GRID_W, GRID_W)
    nwin = kr * GRID_W
    kcat = jnp.concatenate([kb_ref[pl.ds(start, nwin), :], ck_ref[...]], axis=0)
    vcat = jnp.concatenate([vb_ref[pl.ds(start, nwin), :], cv_ref[...]], axis=0)
    past = ck_ref.shape[0]
    for i in range(N_HEADS_B // 2):
        cols = slice(2 * i * HEAD_DIM, (2 * i + 2) * HEAD_DIM)
        kl, kh, _, _ = _lane_halves(kcat[:, cols])
        vl, vh, _, _ = _lane_halves(vcat[:, cols])
        o = _pair_attention(
            qb_ref[:, cols] * scale, kl, kh, vl, vh,
            adjust=lambda hh, s, i=i: s + jnp.concatenate([bias_ref[2 * i + hh], jnp.zeros((GRID_W, past), F32)],
                                                          axis=-1))
        ob_o[:, cols] = o.astype(ob_o.dtype)


def _nbr_bias_table(rel_bias, rows):
    kr = min(NA_ROWS, rows)
    cls = np.arange(kr)[:, None]
    m = np.arange(kr)[None, :]
    row_sel = (m - cls + NA_ROWS - 1)[:, :, None] == np.arange(2 * NA_ROWS - 1)[None, None, :]
    c = np.arange(GRID_W)[:, None]
    kc = np.arange(GRID_W)[None, :]
    dc = np.clip(kc - c, -(NA_COLS - 1), NA_COLS - 1) + NA_COLS - 1
    col_sel = dc[:, :, None] == np.arange(2 * NA_COLS - 1)[None, None, :]
    col_start = np.clip(c - NA_COLS // 2, 0, GRID_W - NA_COLS)
    ok = (kc >= col_start) & (kc < col_start + NA_COLS)
    tab = jnp.einsum('hrd,ymr,ckd->hycmk', rel_bias.astype(F32), row_sel.astype(np.float32),
                     col_sel.astype(np.float32), precision=lax.Precision.HIGHEST)
    tab = jnp.where(ok[None, None, :, None, :], tab, NEG)
    return tab.reshape(rel_bias.shape[0], kr, GRID_W, kr * GRID_W)


def _nbr_attention(qb, kb, vb, cache_k, cache_v, bias_tab, layer, n_lat):
    t = qb.shape[0]
    rows = n_lat // GRID_W
    kr = min(NA_ROWS, rows)
    past = cache_k.shape[2]

    def bias_map(b, r):
        return (0, r - jnp.clip(r - kr // 2, 0, rows - kr), 0, 0)

    return pl.pallas_call(
        functools.partial(_nbr_attn_kernel, rows=rows, kr=kr),
        out_shape=jax.ShapeDtypeStruct((t, WIDTH_B), BF16),
        grid=(t // n_lat, rows),
        in_specs=[
            pl.BlockSpec((GRID_W, WIDTH_B), lambda b, r: (b * rows + r, 0)),
            pl.BlockSpec((n_lat, WIDTH_B), lambda b, r: (b, 0)),
            pl.BlockSpec((n_lat, WIDTH_B), lambda b, r: (b, 0)),
            pl.BlockSpec((None, None, past, WIDTH_B), lambda b, r: (b, layer, 0, 0)),
            pl.BlockSpec((None, None, past, WIDTH_B), lambda b, r: (b, layer, 0, 0)),
            pl.BlockSpec((N_HEADS_B, None, GRID_W, kr * GRID_W), bias_map),
        ],
        out_specs=pl.BlockSpec((GRID_W, WIDTH_B), lambda b, r: (b * rows + r, 0)),
        compiler_params=_params(("arbitrary", "arbitrary")),
        name="neighbourhood_attention",
    )(qb, kb, vb, cache_k, cache_v, bias_tab)


ROW_TILE = 8
LANES = 128
assert ROW_TILE * LANES == D_MODEL


PACK_TILE = ROW_TILE // 2


def _store_row_tiles(ref, row0, x):
    n = x.shape[0]
    rt = x.shape[1] // LANES
    for c in range(rt):
        ref[pl.ds(row0 * rt + c, n, stride=rt), :] = x[:, c * LANES:(c + 1) * LANES]


def _pack_bf16_pairs(x):
    bits = lax.bitcast_convert_type(x, jnp.uint32)
    half = x.shape[1] // 2
    return (bits[:, half:] & jnp.uint32(0xFFFF0000)) | (bits[:, :half] >> 16)


def _unpack_bf16_pairs(w):
    lo = lax.bitcast_convert_type(w << 16, F32)
    hi = lax.bitcast_convert_type(w & jnp.uint32(0xFFFF0000), F32)
    return jnp.concatenate([lo, hi], axis=-1).astype(BF16)


def _load_row_tiles(ref, row0, n, rt=ROW_TILE):
    return jnp.concatenate([ref[pl.ds(row0 * rt + c, n, stride=rt), :] for c in range(rt)], axis=-1)


def _outproj_kernel(x_ref, oa_ref, ob_ref, oc_ref, w_ref, mod_ref, g_ref, wr_ref, br_ref, *rest):
    xn_o, h_o, lg_o = rest[-3:]
    mix = jnp.dot(oa_ref[...], w_ref[0:WIDTH_A, :], preferred_element_type=F32)
    mix += jnp.dot(ob_ref[...], w_ref[WIDTH_A:WIDTH_A + WIDTH_B, :], preferred_element_type=F32)
    mix += jnp.dot(oc_ref[...], w_ref[WIDTH_A + WIDTH_B:, :], preferred_element_type=F32)
    xn = x_ref[...] + mod_ref[2:3, :] * mix
    xn_o[...] = xn
    h = xn * lax.rsqrt(jnp.mean(xn * xn, axis=-1, keepdims=True) + EPS) * g_ref[...]
    h = h * (1.0 + mod_ref[4:5, :]) + mod_ref[3:4, :]
    h_hi = h.astype(BF16)
    h_o[...] = h_hi
    h_lo = (h - h_hi.astype(F32)).astype(BF16)
    wr = wr_ref[...]
    w_hi = wr.astype(BF16)
    w_lo = (wr - w_hi.astype(F32)).astype(BF16)
    lg = lax.dot_general(w_hi, h_hi, _NT, preferred_element_type=F32)
    lg += lax.dot_general(w_hi, h_lo, _NT, preferred_element_type=F32)
    lg += lax.dot_general(w_lo, h_hi, _NT, preferred_element_type=F32)
    lg_o[...] = lg + br_ref[...]


def _outproj(x, oa, ob, oc, w_out_bf, mod, group_of_tile, g_ffn, w_router_t, b_router_t, layer, tm, t_all, tok0,
             shared):
    t = x.shape[0]
    b0 = tok0 // tm
    assert tok0 % tm == 0
    in_specs = [
        pl.BlockSpec((tm, D_MODEL), lambda i: (i, 0)),
        pl.BlockSpec((tm, WIDTH_A), lambda i: (i, 0)),
        pl.BlockSpec((tm, WIDTH_B), lambda i: (i, 0)),
        pl.BlockSpec((tm, CONV_CH), lambda i: (i, 0)),
        pl.BlockSpec((None, D_MODEL, D_MODEL), lambda i: (layer, 0, 0)),
        pl.BlockSpec((None, None, N_MOD, D_MODEL), lambda i: (layer, group_of_tile(i), 0, 0)),
        pl.BlockSpec((None, 1, D_MODEL), lambda i: (layer, 0, 0)),
        pl.BlockSpec((None, N_EXPERTS, D_MODEL), lambda i: (layer, 0, 0)),
        pl.BlockSpec((None, N_EXPERTS, 1), lambda i: (layer, 0, 0)),
    ]
    args = [x, oa, ob, oc, w_out_bf, mod, g_ffn, w_router_t, b_router_t]
    aliases = {len(args): 1, len(args) + 1: 2}
    in_specs += [pl.BlockSpec(memory_space=pl.ANY)] * 2
    args += list(shared)
    return pl.pallas_call(
        _outproj_kernel,
        out_shape=[jax.ShapeDtypeStruct((t, D_MODEL), F32), jax.ShapeDtypeStruct((t_all, D_MODEL), BF16),
                   jax.ShapeDtypeStruct((N_EXPERTS, t_all), F32)],
        grid=(t // tm,),
        in_specs=in_specs,
        out_specs=[pl.BlockSpec((tm, D_MODEL), lambda i: (i, 0)),
                   pl.BlockSpec((tm, D_MODEL), lambda i: (b0 + i, 0)),
                   pl.BlockSpec((N_EXPERTS, tm), lambda i: (0, b0 + i))],
        input_output_aliases=aliases,
        compiler_params=_params(("arbitrary",), VMEM_LIMIT_V7X),
        name="outproj_router",
    )(*args)


X_SLOTS = 4


def _expert_kernel(seg_ref, n_used_ref, x_hbm, wg_ref, bg_ref, wu_ref, bu_ref, wd_ref, bd_ref, y_hbm,
                   x_buf, y_buf, xsem, ysem, zsem, *, n_blocks):
    e = pl.program_id(0)
    n_used = n_used_ref[0]
    blk_rows = MOE_BLK * ROW_TILE
    first_blk = seg_ref[SEG_DST, e, SEG_BLK]
    n_blk = seg_ref[SEG_CNT, e, SEG_BLK]

    def rows_of(g, rows=blk_rows):
        return pl.ds(pl.multiple_of(g * rows, rows), rows)

    def x_copy(g):
        return pltpu.make_async_copy(x_hbm.at[rows_of(g, MOE_BLK * PACK_TILE)], x_buf.at[g % X_SLOTS],
                                     xsem.at[g % X_SLOTS])

    def y_copy(g):
        return pltpu.make_async_copy(y_buf.at[g % 2], y_hbm.at[rows_of(g)], ysem.at[g % 2])

    def zero_copy(g):
        return pltpu.make_async_copy(y_buf.at[0], y_hbm.at[rows_of(g)], zsem)

    @pl.when(e == 0)
    def _():
        for g in range(X_SLOTS - 1):
            @pl.when(g < n_used)
            def _():
                x_copy(g).start()

    def block(b, carry):
        g = first_blk + b
        slot = g % 2

        @pl.when(g + (X_SLOTS - 1) < n_used)
        def _():
            x_copy(g + (X_SLOTS - 1)).start()

        x_copy(g).wait()

        @pl.when(g >= 2)
        def _():
            y_copy(g - 2).wait()

        x = _unpack_bf16_pairs(_load_row_tiles(x_buf.at[g % X_SLOTS], 0, MOE_BLK, PACK_TILE))
        gt = jnp.dot(x, wg_ref[...].astype(BF16), preferred_element_type=F32) + bg_ref[...]
        up = jnp.dot(x, wu_ref[...].astype(BF16), preferred_element_type=F32) + bu_ref[...]
        gt = jnp.minimum(gt, SWIGLU_LIMIT)
        up = jnp.clip(up, -SWIGLU_LIMIT, SWIGLU_LIMIT)
        a = gt * jax.nn.sigmoid(SWIGLU_ALPHA * gt) * (up + 1.0)
        y = jnp.dot(a.astype(BF16), wd_ref[...].astype(BF16), preferred_element_type=F32) + bd_ref[...]
        _store_row_tiles(y_buf.at[slot], 0, y)
        y_copy(g).start()
        return carry

    lax.fori_loop(0, n_blk, block, 0)

    @pl.when(e == pl.num_programs(0) - 1)
    def _():
        @pl.when(n_used >= 2)
        def _():
            y_copy(n_used - 2).wait()

        @pl.when(n_used >= 1)
        def _():
            y_copy(n_used - 1).wait()

        y_buf[0] = jnp.zeros((blk_rows, LANES), F32)

        def start_zero(g, carry):
            zero_copy(g).start()
            return carry

        def wait_zero(g, carry):
            zero_copy(g).wait()
            return carry

        lax.fori_loop(n_used, n_blocks, start_zero, 0)
        lax.fori_loop(n_used, n_blocks, wait_zero, 0)


def _experts(seg, n_used, x_sorted, w_gate, b_gate, w_up, b_up, w_down, b_down, layer):
    n_blocks = x_sorted.shape[0] // (MOE_BLK * PACK_TILE)
    wspec = pl.BlockSpec((None, None, D_MODEL, D_MODEL), lambda e, sg, nu: (layer, e, 0, 0))
    bspec = pl.BlockSpec((None, None, 1, D_MODEL), lambda e, sg, nu: (layer, e, 0, 0))
    anyspec = pl.BlockSpec(memory_space=pl.ANY)
    depth = w_gate.shape[0]
    b4 = lambda b: b.reshape(depth, N_EXPERTS, 1, D_MODEL)
    return pl.pallas_call(
        functools.partial(_expert_kernel, n_blocks=n_blocks),
        out_shape=jax.ShapeDtypeStruct((n_blocks * MOE_BLK * ROW_TILE, LANES), F32),
        grid_spec=pltpu.PrefetchScalarGridSpec(
            num_scalar_prefetch=2,
            grid=(N_EXPERTS,),
            in_specs=[anyspec, wspec, bspec, wspec, bspec, wspec, bspec],
            out_specs=anyspec,
            scratch_shapes=[
                pltpu.VMEM((X_SLOTS, MOE_BLK * PACK_TILE, LANES), jnp.uint32),
                pltpu.VMEM((2, MOE_BLK * ROW_TILE, LANES), F32),
                pltpu.SemaphoreType.DMA((X_SLOTS,)),
                pltpu.SemaphoreType.DMA((2,)),
                pltpu.SemaphoreType.DMA,
            ],
        ),
        compiler_params=_params(("arbitrary",), VMEM_LIMIT_V7X),
        name="experts",
    )(seg, n_used, x_sorted, w_gate, b4(b_gate), w_up, b4(b_up), w_down, b4(b_down))


CUM_CHUNK = 256


def _sublane_cumsum(x):
    row = lax.broadcasted_iota(jnp.int32, x.shape, 0)
    d = 1
    while d < x.shape[0]:
        x = x + jnp.where(row >= d, pltpu.roll(x, d, axis=0), 0)
        d *= 2
    return x


TOK_TILE = CUM_CHUNK
TILE_ROWS = TOP_K * TOK_TILE
SEG_LANES = 128
SEG_PAD_FROM = SEG_LANES - 2
SEG_PAD_LEN = SEG_LANES - 1
SEG_BLK = SEG_LANES - 3
SEG_SRC, SEG_CNT, SEG_DST = 0, 1, 2


def _route_kernel(lg_ref, tri_ref, gate_o, lpos_o, seg_o, used_o):
    lg = lg_ref[...]
    n_e, t = lg.shape
    e_iota = lax.broadcasted_iota(jnp.int32, lg.shape, 0)
    work = lg
    tops, hots = [], []
    for _ in range(TOP_K):
        m = jnp.max(work, axis=0, keepdims=True)
        first = jnp.min(jnp.where(work == m, e_iota, n_e), axis=0, keepdims=True)
        hot = e_iota == first
        work = jnp.where(hot, -jnp.inf, work)
        tops.append(m)
        hots.append(hot)
    ex = [jnp.exp(m - tops[0]) for m in tops]
    den = ex[0] + ex[1] + ex[2] + ex[3]
    for k in range(TOP_K):
        gate_o[k:k + 1, :] = ex[k] / den

    chosen = jnp.where(hots[0] | hots[1] | hots[2] | hots[3], 1.0, 0.0)
    tri = tri_ref[...]
    lane = lax.broadcasted_iota(jnp.int32, (n_e, SEG_LANES), 1)
    seg_cnt = jnp.zeros((n_e, SEG_LANES), jnp.int32)
    seg_before = jnp.zeros((n_e, SEG_LANES), jnp.int32)
    carry = jnp.zeros((n_e, 1), F32)
    rank_in_tile = []
    n_tiles = t // TOK_TILE
    for c in range(n_tiles):
        chunk = chosen[:, c * TOK_TILE:(c + 1) * TOK_TILE]
        inc = jnp.dot(chunk.astype(BF16), tri, preferred_element_type=F32)
        cnt = inc[:, TOK_TILE - 1:TOK_TILE]
        rank_in_tile.append((inc - chunk).astype(jnp.int32))
        seg_cnt = jnp.where(lane == c, cnt.astype(jnp.int32), seg_cnt)
        seg_before = jnp.where(lane == c, carry.astype(jnp.int32), seg_before)
        carry = carry + cnt

    counts = jnp.broadcast_to(carry.astype(jnp.int32), (n_e, SEG_LANES))
    padded = ((counts + (MOE_BLK - 1)) >> MOE_SHIFT) << MOE_SHIFT
    pad_end = _sublane_cumsum(padded)
    pad_start = pad_end - padded
    seg_src = _sublane_cumsum(seg_cnt) - seg_cnt
    seg_o[SEG_SRC] = seg_src
    seg_o[SEG_CNT] = jnp.where(lane == SEG_PAD_LEN, padded - counts,
                               jnp.where(lane == SEG_BLK, padded >> MOE_SHIFT, seg_cnt))
    seg_o[SEG_DST] = jnp.where(lane == SEG_PAD_FROM, pad_start + counts,
                               jnp.where(lane == SEG_BLK, pad_start >> MOE_SHIFT, pad_start + seg_before))

    for c in range(n_tiles):
        pos = seg_src[:, c:c + 1] + rank_in_tile[c]
        for k in range(TOP_K):
            lpos_o[k:k + 1, c * TOK_TILE:(c + 1) * TOK_TILE] = jnp.sum(
                jnp.where(hots[k][:, c * TOK_TILE:(c + 1) * TOK_TILE], pos, 0), axis=0, keepdims=True)

    used_o[...] = jnp.broadcast_to(pad_end[n_e - 1:n_e, :] >> MOE_SHIFT, used_o.shape)


def _route(logits):
    t = logits.shape[1]
    assert t % TOK_TILE == 0 and t // TOK_TILE <= SEG_BLK
    tri = jnp.asarray(np.triu(np.ones((CUM_CHUNK, CUM_CHUNK), np.float32)), dtype=BF16)
    gates, lpos, seg, used = pl.pallas_call(
        _route_kernel,
        out_shape=[jax.ShapeDtypeStruct((TOP_K, t), F32), jax.ShapeDtypeStruct((TOP_K, t), jnp.int32),
                   jax.ShapeDtypeStruct((3, N_EXPERTS, SEG_LANES), jnp.int32),
                   jax.ShapeDtypeStruct((8, SEG_LANES), jnp.int32)],
        compiler_params=_params(None, VMEM_LIMIT_V7X),
        name="route",
    )(logits, tri)
    return gates, lpos, seg, used[0, :1]


def _segment_copies(src, dst, src0, dst0, cnt, sem, wait, rt=ROW_TILE):
    @pl.when(cnt > 0)
    def _():
        size = cnt * rt
        cp = pltpu.make_async_copy(
            src.at[pl.ds(pl.multiple_of(src0 * rt, rt), size)],
            dst.at[pl.ds(pl.multiple_of(dst0 * rt, rt), size)], sem)
        if wait:
            cp.wait()
        else:
            cp.start()


def _dispatch_kernel(seg_ref, nu_ref, h_ref, lpos_ref, x_hbm, z_buf, zero_buf, sem, zsem, *, n_tiles, n_blocks):
    c = pl.program_id(0)
    slot = c % 2

    def wait_tile(s):
        pltpu.make_async_copy(z_buf.at[s], x_hbm.at[pl.ds(0, TILE_ROWS * PACK_TILE)], sem.at[s]).wait()

    def padding(wait):
        def per_expert(e, carry):
            _segment_copies(zero_buf, x_hbm, 0, seg_ref[SEG_DST, e, SEG_PAD_FROM], seg_ref[SEG_CNT, e, SEG_PAD_LEN],
                            zsem, wait, PACK_TILE)
            return carry

        lax.fori_loop(0, N_EXPERTS, per_expert, 0)

        def per_block(b, carry):
            cp = pltpu.make_async_copy(
                zero_buf, x_hbm.at[pl.ds(pl.multiple_of(b * (MOE_BLK * PACK_TILE), MOE_BLK * PACK_TILE),
                                         MOE_BLK * PACK_TILE)], zsem)
            if wait:
                cp.wait()
            else:
                cp.start()
            return carry

        lax.fori_loop(nu_ref[0], n_blocks, per_block, 0)

    @pl.when(c == 0)
    def _():
        zero_buf[...] = jnp.zeros(zero_buf.shape, jnp.uint32)
        padding(False)

    @pl.when(c >= 2)
    def _():
        wait_tile(slot)

    row = lax.broadcasted_iota(jnp.int32, (TILE_ROWS, TOK_TILE), 0)
    hit = row == lpos_ref[0:1, :]
    for k in range(1, TOP_K):
        hit = hit | (row == lpos_ref[k:k + 1, :])
    z = jnp.dot(jnp.where(hit, 1.0, 0.0).astype(BF16), h_ref[...], preferred_element_type=F32)
    _store_row_tiles(z_buf.at[slot], 0, _pack_bf16_pairs(z))

    def per_expert(e, carry):
        _segment_copies(z_buf.at[slot], x_hbm, seg_ref[SEG_SRC, e, c], seg_ref[SEG_DST, e, c], seg_ref[SEG_CNT, e, c],
                        sem.at[slot], False, PACK_TILE)
        return carry

    lax.fori_loop(0, N_EXPERTS, per_expert, 0)

    @pl.when(c == n_tiles - 1)
    def _():
        if n_tiles > 1:
            wait_tile(1 - slot)
        wait_tile(slot)
        padding(True)


def _dispatch(seg, n_used, h_all, lpos, n_blocks):
    t = h_all.shape[0]
    n_tiles = t // TOK_TILE
    return pl.pallas_call(
        functools.partial(_dispatch_kernel, n_tiles=n_tiles, n_blocks=n_blocks),
        out_shape=jax.ShapeDtypeStruct((n_blocks * MOE_BLK * PACK_TILE, LANES), jnp.uint32),
        grid_spec=pltpu.PrefetchScalarGridSpec(
            num_scalar_prefetch=2,
            grid=(n_tiles,),
            in_specs=[pl.BlockSpec((TOK_TILE, D_MODEL), lambda c, sg, nu: (c, 0)),
                      pl.BlockSpec((TOP_K, TOK_TILE), lambda c, sg, nu: (0, c))],
            out_specs=pl.BlockSpec(memory_space=pl.ANY),
            scratch_shapes=[
                pltpu.VMEM((2, TILE_ROWS * PACK_TILE, LANES), jnp.uint32),
                pltpu.VMEM((MOE_BLK * PACK_TILE, LANES), jnp.uint32),
                pltpu.SemaphoreType.DMA((2,)),
                pltpu.SemaphoreType.DMA,
            ],
        ),
        compiler_params=_params(("arbitrary",), VMEM_LIMIT_V7X),
        name="dispatch",
    )(seg, n_used, h_all, lpos)


def _combine_kernel(seg_ref, x_ref, y_hbm, lpos_ref, lpos_t_ref, gate_ref, mod_ref, o_ref, y_buf, sem, *, tile0,
                    n_tiles):
    c = pl.program_id(0)
    slot = c % 2

    def fetch(cc):
        def per_expert(e, carry):
            _segment_copies(y_hbm, y_buf.at[cc % 2], seg_ref[SEG_DST, e, tile0 + cc], seg_ref[SEG_SRC, e, tile0 + cc],
                            seg_ref[SEG_CNT, e, tile0 + cc], sem.at[cc % 2], False)
            return carry

        lax.fori_loop(0, N_EXPERTS, per_expert, 0)

    @pl.when(c == 0)
    def _():
        fetch(0)

    @pl.when(c + 1 < n_tiles)
    def _():
        fetch(c + 1)

    pltpu.make_async_copy(y_hbm.at[pl.ds(0, TILE_ROWS * ROW_TILE)], y_buf.at[slot], sem.at[slot]).wait()

    row = lax.broadcasted_iota(jnp.int32, (TILE_ROWS, TOK_TILE), 0)
    row_gate = jnp.zeros((TILE_ROWS, 1), F32)
    for k in range(TOP_K):
        row_gate += jnp.sum(jnp.where(row == lpos_ref[k:k + 1, :], gate_ref[k:k + 1, :], 0.0), axis=1, keepdims=True)
    ys = _load_row_tiles(y_buf.at[slot], 0, TILE_ROWS) * row_gate
    ys_hi = ys.astype(BF16)
    ys_lo = (ys - ys_hi.astype(F32)).astype(BF16)
    col = lax.broadcasted_iota(jnp.int32, (TOK_TILE, TILE_ROWS), 1)
    hit = col == lpos_t_ref[:, 0:1]
    for k in range(1, TOP_K):
        hit = hit | (col == lpos_t_ref[:, k:k + 1])
    u = jnp.where(hit, 1.0, 0.0).astype(BF16)
    mix = jnp.dot(u, ys_hi, preferred_element_type=F32) + jnp.dot(u, ys_lo, preferred_element_type=F32)
    o_ref[...] = x_ref[...] + mod_ref[5:6, :] * mix


def _combine(seg, x_mid, y_sorted, lpos, lpos_t, gates, mod, group_of_tile, layer, tok0):
    t = x_mid.shape[0]
    tile0 = tok0 // TOK_TILE
    n_tiles = t // TOK_TILE
    return pl.pallas_call(
        functools.partial(_combine_kernel, tile0=tile0, n_tiles=n_tiles),
        out_shape=jax.ShapeDtypeStruct((t, D_MODEL), F32),
        grid_spec=pltpu.PrefetchScalarGridSpec(
            num_scalar_prefetch=1,
            grid=(n_tiles,),
            in_specs=[
                pl.BlockSpec((TOK_TILE, D_MODEL), lambda c, sg: (c, 0)),
                pl.BlockSpec(memory_space=pl.ANY),
                pl.BlockSpec((TOP_K, TOK_TILE), lambda c, sg: (0, tile0 + c)),
                pl.BlockSpec((TOK_TILE, TOP_K), lambda c, sg: (tile0 + c, 0)),
                pl.BlockSpec((TOP_K, TOK_TILE), lambda c, sg: (0, tile0 + c)),
                pl.BlockSpec((None, None, N_MOD, D_MODEL), lambda c, sg: (layer, group_of_tile(c), 0, 0)),
            ],
            out_specs=pl.BlockSpec((TOK_TILE, D_MODEL), lambda c, sg: (c, 0)),
            scratch_shapes=[pltpu.VMEM((2, TILE_ROWS * ROW_TILE, LANES), F32), pltpu.SemaphoreType.DMA((2,))],
        ),
        compiler_params=_params(("arbitrary",), VMEM_LIMIT_V7X),
        name="combine",
    )(seg, x_mid, y_sorted, lpos, lpos_t, gates, mod)


def _rope_tables(n_lat):
    quarter = HEAD_DIM // 4
    t = jnp.arange(n_lat)
    inv = ROPE_BASE ** (-jnp.arange(quarter, dtype=F32) / quarter)
    ang_r = (t // GRID_W).astype(F32)[:, None] * inv
    ang_c = (t % GRID_W).astype(F32)[:, None] * inv
    cos = jnp.concatenate([jnp.cos(ang_r)] * 2 + [jnp.cos(ang_c)] * 2, axis=-1)
    sin = jnp.concatenate([-jnp.sin(ang_r), jnp.sin(ang_r), -jnp.sin(ang_c), jnp.sin(ang_c)], axis=-1)
    return jnp.concatenate([cos, cos], axis=-1), jnp.concatenate([sin, sin], axis=-1)


def _block_diag_ones():
    idx = np.arange(MXU_COLS_V7X) // HEAD_DIM
    return jnp.asarray(idx[:, None] == idx[None, :], dtype=BF16)


def kernel(x_prompt, x_sample, cache_k_win, cache_v_win, cache_k_nbr, cache_v_nbr, c, c_ctx, w_mod, b_mod, g_mix, g_ffn, w_in, w_out, qn_win, kn_win, qn_nbr, kn_nbr, sink_win, rel_bias_nbr, conv_w, w_router, b_router, w_gate, b_gate, w_up, b_up, w_down, b_down):
    bsz, n_ctx, d = x_prompt.shape
    dbs, n_lat, _ = x_sample.shape
    depth = w_in.shape[0]
    past = cache_k_win.shape[2]
    assert d == D_MODEL and dbs + 1 <= COND_ROWS and n_lat % GRID_W == 0 and n_lat >= Q_BLK + 2 * WINDOW
    t_ctx, t_lat = bsz * n_ctx, dbs * n_lat

    cond = jnp.concatenate([c_ctx[None], c, jnp.zeros((COND_ROWS - 1 - dbs, d), F32)], axis=0)
    mod = _modulation(cond, w_mod, b_mod)

    w_in_bf = w_in.astype(BF16)
    w_out_bf = w_out.astype(BF16)
    norm_w = jnp.concatenate([jnp.tile(qn_win, (1, N_HEADS_A)), jnp.tile(kn_win, (1, N_KV_A)),
                              jnp.tile(qn_nbr, (1, N_HEADS_B)), jnp.tile(kn_nbr, (1, N_HEADS_B))], axis=-1)[:, None, :]
    ones_bd = _block_diag_ones()
    rope_tabs = _rope_tables(n_lat)
    g_mix3, g_ffn3 = g_mix[:, None, :], g_ffn[:, None, :]
    w_router_t = jnp.swapaxes(w_router, 1, 2)
    b_router_t = b_router[:, :, None]
    t_all = t_ctx + t_lat
    ck_win = cache_k_win.reshape(dbs, depth, past, KV_WIDTH_A)
    cv_win = cache_v_win.reshape(dbs, depth, past, KV_WIDTH_A)
    ck_nbr = cache_k_nbr.reshape(dbs, depth, past, WIDTH_B)
    cv_nbr = cache_v_nbr.reshape(dbs, depth, past, WIDTH_B)

    tm_ctx = 2 * n_ctx
    tm_lat = 512
    ctx_group = lambda i: 0
    lat_group_in = lambda i: 1 + i
    lat_group_out = lambda i: 1 + (i * tm_lat) // n_lat
    lat_group_comb = lambda i: 1 + (i * TOK_TILE) // n_lat
    assert t_ctx % TOK_TILE == 0 and n_lat % TOK_TILE == 0

    xp = x_prompt.reshape(t_ctx, d)
    xs = x_sample.reshape(t_lat, d)
    caches = [[], [], [], []]
    for l in range(depth):
        qa, qb, ka, va, kb, vb, oc = _inproj(xp, mod, ctx_group, g_mix3, w_in_bf, l, norm_w, ones_bd, conv_w,
                                             tm_ctx, n_ctx)
        oa, ob = _ctx_attention(sink_win[l], qa, ka, va, qb, kb, vb, n_ctx)
        shared = (jnp.zeros((t_all, d), BF16), jnp.zeros((N_EXPERTS, t_all), F32))
        xp_mid, h_all, lg_all = _outproj(xp, oa, ob, oc, w_out_bf, mod, ctx_group, g_ffn3, w_router_t, b_router_t, l,
                                         tm_ctx, t_all, 0, shared)
        for lst, a in zip(caches, (ka, va, kb, vb)):
            lst.append(a)

        qa, qb, ka, va, kb, vb, oc = _inproj(xs, mod, lat_group_in, g_mix3, w_in_bf, l, norm_w, ones_bd, conv_w,
                                             n_lat, n_lat, rope_tabs)
        oa = _win_attention(sink_win[l], qa, ka, va, ck_win, cv_win, l, n_lat)
        ob = _nbr_attention(qb, kb, vb, ck_nbr, cv_nbr, _nbr_bias_table(rel_bias_nbr[l], n_lat // GRID_W), l, n_lat)
        xs_mid, h_all, lg_all = _outproj(xs, oa, ob, oc, w_out_bf, mod, lat_group_out, g_ffn3, w_router_t, b_router_t,
                                         l, tm_lat, t_all, t_ctx, shared=(h_all, lg_all))

        gates, lpos, seg, n_used = _route(lg_all)
        x_sorted = _dispatch(seg, n_used, h_all, lpos, t_all * TOP_K // MOE_BLK + N_EXPERTS)
        y_sorted = _experts(seg, n_used, x_sorted, w_gate, b_gate, w_up, b_up, w_down, b_down, l)
        lpos_t = lpos.T
        xp = _combine(seg, xp_mid, y_sorted, lpos, lpos_t, gates, mod, ctx_group, l, 0)
        xs = _combine(seg, xs_mid, y_sorted, lpos, lpos_t, gates, mod, lat_group_comb, l, t_ctx)

    new_v_win = jnp.stack([a.reshape(bsz, n_ctx, KV_WIDTH_A) for a in caches[1]], axis=1).reshape(
        bsz, depth, n_ctx, N_KV_A, HEAD_DIM)
    new_k_nbr = jnp.stack([a.reshape(bsz, n_ctx, WIDTH_B) for a in caches[2]], axis=1).reshape(
        bsz, depth, n_ctx, N_HEADS_B, HEAD_DIM)
    new_v_nbr = jnp.stack([a.reshape(bsz, n_ctx, WIDTH_B) for a in caches[3]], axis=1).reshape(
        bsz, depth, n_ctx, N_HEADS_B, HEAD_DIM)
    new_k_win = jnp.stack([a.reshape(bsz, n_ctx, KV_WIDTH_A) for a in caches[0]], axis=1).reshape(
        bsz, depth, n_ctx, N_KV_A, HEAD_DIM)
    return (xp.reshape(bsz, n_ctx, d), xs.reshape(dbs, n_lat, d), new_k_win, new_v_win, new_k_nbr, new_v_nbr)
```

```python
import functools

import numpy as np
import jax
import jax.numpy as jnp
from jax import lax
from jax.experimental import pallas as pl
from jax.experimental.pallas import tpu as pltpu

F32 = jnp.float32
BF16 = jnp.bfloat16

D_MODEL = 1024
HEAD_DIM = 64
GRID_W = 64
N_HEADS_A = 8
N_KV_A = 2
GROUP_A = N_HEADS_A // N_KV_A
WINDOW = 128
Q_BLK = 128
N_HEADS_B = 4
NA_ROWS = 8
NA_COLS = 16
CONV_CH = 256
CONV_W = 3
WIDTH_A = N_HEADS_A * HEAD_DIM
KV_WIDTH_A = N_KV_A * HEAD_DIM
WIDTH_B = N_HEADS_B * HEAD_DIM
IN_COLS = WIDTH_A + 2 * KV_WIDTH_A + 3 * WIDTH_B + 3 * CONV_CH
N_EXPERTS = 32
TOP_K = 4
SWIGLU_LIMIT = 7.0
SWIGLU_ALPHA = 1.702
ROPE_BASE = 10000.0
EPS = 1e-6
NEG = -1e30
N_MOD = 6

C_QA = 0
C_KA = C_QA + WIDTH_A
C_VA = C_KA + KV_WIDTH_A
C_QB = C_VA + KV_WIDTH_A
C_KB = C_QB + WIDTH_B
C_VB = C_KB + WIDTH_B
C_U = C_VB + WIDTH_B
C_GB = C_U + CONV_CH
C_GC = C_GB + CONV_CH

MXU_COLS_V7X = 256
COND_ROWS = 8
MOE_BLK = 256
MOE_SHIFT = MOE_BLK.bit_length() - 1
assert 1 << MOE_SHIFT == MOE_BLK
VMEM_LIMIT_V7X = 56 * 1024 * 1024

_NT = (((1,), (1,)), ((), ()))


def _params(sem, vmem=None):
    return pltpu.CompilerParams(dimension_semantics=sem, vmem_limit_bytes=vmem)


def _mod_kernel(c_ref, w_ref, b_ref, o_ref):
    c = c_ref[...]
    s = c * jax.nn.sigmoid(c)
    o_ref[...] = jnp.dot(s.astype(BF16), w_ref[...].astype(BF16), preferred_element_type=F32) + b_ref[...]


def _modulation(cond, w_mod, b_mod):
    depth = w_mod.shape[0]
    out = pl.pallas_call(
        _mod_kernel,
        out_shape=jax.ShapeDtypeStruct((depth, COND_ROWS, N_MOD * D_MODEL), F32),
        grid=(depth, N_MOD),
        in_specs=[
            pl.BlockSpec((COND_ROWS, D_MODEL), lambda l, j: (0, 0)),
            pl.BlockSpec((None, D_MODEL, D_MODEL), lambda l, j: (l, 0, j)),
            pl.BlockSpec((None, 1, D_MODEL), lambda l, j: (l, 0, j)),
        ],
        out_specs=pl.BlockSpec((None, COND_ROWS, D_MODEL), lambda l, j: (l, 0, j)),
        compiler_params=_params(("arbitrary", "arbitrary")),
        name="modulation",
    )(cond, w_mod, b_mod.reshape(depth, 1, N_MOD * D_MODEL))
    return out.reshape(depth, COND_ROWS, N_MOD, D_MODEL)


def _head_norm(x, w_row, ones_ref):
    width = x.shape[1]
    sq = (x * x).astype(BF16)
    parts = []
    for c0 in range(0, width, MXU_COLS_V7X):
        wd = min(MXU_COLS_V7X, width - c0)
        parts.append(jnp.dot(sq[:, c0:c0 + wd], ones_ref[:wd, :wd], preferred_element_type=F32))
    ss = parts[0] if len(parts) == 1 else jnp.concatenate(parts, axis=-1)
    return x * lax.rsqrt(ss * (1.0 / HEAD_DIM) + EPS) * w_row


def _rope(x, cos, sin):
    width = x.shape[1]
    lane = lax.broadcasted_iota(jnp.int32, x.shape, 1)
    quarter = HEAD_DIM // 4
    partner = jnp.where((lane % (2 * quarter)) < quarter,
                        pltpu.roll(x, width - quarter, axis=1), pltpu.roll(x, quarter, axis=1))
    reps = width // cos.shape[1]
    cos_w = cos if reps == 1 else jnp.concatenate([cos] * reps, axis=-1)
    sin_w = sin if reps == 1 else jnp.concatenate([sin] * reps, axis=-1)
    return x * cos_w + partner * sin_w


def _inproj_kernel(*refs, seq_len, rope):
    x_ref, mod_ref, g_ref, w_ref, nw_ref, ones_ref, cw_ref = refs[:7]
    qa_o, qb_o, ka_o, va_o, kb_o, vb_o, oc_o = refs[-7:]
    if rope:
        cos_ref, sin_ref = refs[7:9]
    x = x_ref[...]
    tm = x.shape[0]
    h = x * lax.rsqrt(jnp.mean(x * x, axis=-1, keepdims=True) + EPS) * g_ref[...]
    h = h * (1.0 + mod_ref[1:2, :]) + mod_ref[0:1, :]
    p = jnp.dot(h.astype(BF16), w_ref[...], preferred_element_type=F32)

    qa = _head_norm(p[:, C_QA:C_KA], nw_ref[:, 0:WIDTH_A], ones_ref)
    ka = _head_norm(p[:, C_KA:C_VA], nw_ref[:, WIDTH_A:WIDTH_A + KV_WIDTH_A], ones_ref)
    o_qb = WIDTH_A + KV_WIDTH_A
    qb = _head_norm(p[:, C_QB:C_KB], nw_ref[:, o_qb:o_qb + WIDTH_B], ones_ref)
    kb = _head_norm(p[:, C_KB:C_VB], nw_ref[:, o_qb + WIDTH_B:o_qb + 2 * WIDTH_B], ones_ref)
    if rope:
        cos, sin = cos_ref[...], sin_ref[...]
        qa = _rope(qa, cos, sin)
        ka = _rope(ka, cos, sin)
    qa_o[...] = qa.astype(BF16)
    qb_o[...] = qb.astype(BF16)
    ka_o[...] = ka.reshape(ka_o.shape)
    va_o[...] = p[:, C_VA:C_QB].reshape(va_o.shape)
    kb_o[...] = kb.reshape(kb_o.shape)
    vb_o[...] = p[:, C_VB:C_U].reshape(vb_o.shape)

    z = p[:, C_GC:C_GC + CONV_CH] * p[:, C_U:C_GB]
    row = lax.broadcasted_iota(jnp.int32, z.shape, 0) % seq_len
    z_prev = jnp.where(row == 0, 0.0, pltpu.roll(z, 1, axis=0))
    z_next = jnp.where(row == seq_len - 1, 0.0, pltpu.roll(z, tm - 1, axis=0))
    y = z_prev * cw_ref[0:1, :] + z * cw_ref[1:2, :] + z_next * cw_ref[2:3, :]
    oc_o[...] = (p[:, C_GB:C_GC] * y).astype(BF16)


def _inproj(x, mod, group_of_tile, g_mix, w_in_bf, layer, norm_w, ones_bd, conv_w, tm, seq_len, rope_tabs=None,
            caches=None):
    t = x.shape[0]
    rope = rope_tabs is not None
    in_specs = [
        pl.BlockSpec((tm, D_MODEL), lambda i: (i, 0)),
        pl.BlockSpec((None, None, N_MOD, D_MODEL), lambda i: (layer, group_of_tile(i), 0, 0)),
        pl.BlockSpec((None, 1, D_MODEL), lambda i: (layer, 0, 0)),
        pl.BlockSpec((None, D_MODEL, IN_COLS), lambda i: (layer, 0, 0)),
        pl.BlockSpec((None, 1, norm_w.shape[-1]), lambda i: (layer, 0, 0)),
        pl.BlockSpec(ones_bd.shape, lambda i: (0, 0)),
        pl.BlockSpec((None, CONV_W, CONV_CH), lambda i: (layer, 0, 0)),
    ]
    args = [x, mod, g_mix, w_in_bf, norm_w, ones_bd, conv_w]
    if rope:
        in_specs += [pl.BlockSpec(rope_tabs[0].shape, lambda i: (0, 0))] * 2
        args += list(rope_tabs)
    widths = (WIDTH_A, WIDTH_B, KV_WIDTH_A, KV_WIDTH_A, WIDTH_B, WIDTH_B, CONV_CH)
    dtypes = (BF16, BF16, F32, F32, F32, F32, BF16)
    out_shape = [jax.ShapeDtypeStruct((t, w), dt) for w, dt in zip(widths, dtypes)]
    out_specs = [pl.BlockSpec((tm, w), lambda i: (i, 0)) for w in widths]
    aliases = {}
    if caches is not None:
        seqs = tm // seq_len
        for j, cache in enumerate(caches):
            o = 2 + j
            aliases[len(args)] = o
            in_specs.append(pl.BlockSpec(memory_space=pl.ANY))
            args.append(cache)
            out_shape[o] = jax.ShapeDtypeStruct(cache.shape, F32)
            out_specs[o] = pl.BlockSpec((seqs, None, seq_len, widths[o]), lambda i: (i, layer, 0, 0))
    return pl.pallas_call(
        functools.partial(_inproj_kernel, seq_len=seq_len, rope=rope),
        out_shape=out_shape,
        grid=(t // tm,),
        in_specs=in_specs,
        out_specs=out_specs,
        input_output_aliases=aliases,
        compiler_params=_params(("arbitrary",), VMEM_LIMIT_V7X),
        name="inproj_rope" if rope else "inproj",
    )(*args)


def _pair_attention(q2, k_lo, k_hi, v_lo, v_hi, sinks=None, adjust=None):
    m = k_lo.shape[0]
    k2 = jnp.concatenate([k_lo, k_hi], axis=0)
    v2 = jnp.concatenate([v_lo, v_hi], axis=0)
    s = lax.dot_general(q2, k2, _NT, preferred_element_type=F32)
    ps, dens = [], []
    for i in range(2):
        si = s[:, i * m:(i + 1) * m]
        if adjust is not None:
            si = adjust(i, si)
        mx = jnp.max(si, axis=-1, keepdims=True)
        if sinks is not None:
            mx = jnp.maximum(mx, sinks[i])
        p = jnp.exp(si - mx)
        den = jnp.sum(p, axis=-1, keepdims=True)
        if sinks is not None:
            den = den + jnp.exp(sinks[i] - mx)
        ps.append(p.astype(BF16))
        dens.append(den)
    o = jnp.dot(jnp.concatenate(ps, axis=-1), v2, preferred_element_type=F32)
    lane = lax.broadcasted_iota(jnp.int32, o.shape, 1)
    return o / jnp.where(lane < HEAD_DIM, dens[0], dens[1])


def _lane_halves(x):
    lane = lax.broadcasted_iota(jnp.int32, x.shape, 1)
    lo = lane < HEAD_DIM
    sw = pltpu.roll(x, HEAD_DIM, axis=1)
    z = jnp.zeros_like(x)
    return tuple(jnp.where(c, y, z).astype(BF16) for c, y in ((lo, x), (~lo, x), (lo, sw), (~lo, sw)))


def _ctx_attn_kernel(sink_ref, qa_ref, ka_ref, va_ref, qb_ref, kb_ref, vb_ref, oa_o, ob_o):
    scale = HEAD_DIM ** -0.5
    k0l, k1h, k1l, k0h = _lane_halves(ka_ref[...])
    v0l, v1h, v1l, v0h = _lane_halves(va_ref[...])
    for i in range(N_HEADS_A // 2):
        cols = slice(2 * i * HEAD_DIM, (2 * i + 2) * HEAD_DIM)
        kv = (k0l, k0h, v0l, v0h) if (2 * i) // GROUP_A == 0 else (k1l, k1h, v1l, v1h)
        o = _pair_attention(qa_ref[:, cols] * scale, *kv, sinks=(sink_ref[2 * i], sink_ref[2 * i + 1]))
        oa_o[:, cols] = o.astype(oa_o.dtype)
    for i in range(N_HEADS_B // 2):
        cols = slice(2 * i * HEAD_DIM, (2 * i + 2) * HEAD_DIM)
        kl, kh, _, _ = _lane_halves(kb_ref[:, cols])
        vl, vh, _, _ = _lane_halves(vb_ref[:, cols])
        o = _pair_attention(qb_ref[:, cols] * scale, kl, kh, vl, vh)
        ob_o[:, cols] = o.astype(ob_o.dtype)


def _ctx_attention(sink, qa, ka, va, qb, kb, vb, seq_len, layer):
    t = qa.shape[0]
    q_spec = lambda w: pl.BlockSpec((seq_len, w), lambda i: (i, 0))
    kv_spec = lambda w: pl.BlockSpec((None, None, seq_len, w), lambda i: (i, layer, 0, 0))
    return pl.pallas_call(
        _ctx_attn_kernel,
        out_shape=[jax.ShapeDtypeStruct((t, WIDTH_A), BF16), jax.ShapeDtypeStruct((t, WIDTH_B), BF16)],
        grid=(t // seq_len,),
        in_specs=[pl.BlockSpec(memory_space=pltpu.SMEM), q_spec(WIDTH_A), kv_spec(KV_WIDTH_A), kv_spec(KV_WIDTH_A),
                  q_spec(WIDTH_B), kv_spec(WIDTH_B), kv_spec(WIDTH_B)],
        out_specs=[pl.BlockSpec((seq_len, WIDTH_A), lambda i: (i, 0)),
                   pl.BlockSpec((seq_len, WIDTH_B), lambda i: (i, 0))],
        compiler_params=_params(("arbitrary",)),
        name="ctx_attention",
    )(sink, qa, ka, va, qb, kb, vb)


def _win_attn_kernel(sink_ref, qa_ref, ka_ref, va_ref, ck_ref, cv_ref, oa_o, *, n_lat):
    scale = HEAD_DIM ** -0.5
    span = Q_BLK + 2 * WINDOW
    n = pl.program_id(1)
    start = pl.multiple_of(jnp.clip(n * Q_BLK - WINDOW, 0, n_lat - span), Q_BLK)
    k0l, k1h, k1l, k0h = _lane_halves(jnp.concatenate([ka_ref[pl.ds(start, span), :], ck_ref[...]], axis=0))
    v0l, v1h, v1l, v0h = _lane_halves(jnp.concatenate([va_ref[pl.ds(start, span), :], cv_ref[...]], axis=0))
    n_keys = k0l.shape[0]
    qpos = n * Q_BLK + lax.broadcasted_iota(jnp.int32, (Q_BLK, n_keys), 0)
    col = lax.broadcasted_iota(jnp.int32, (Q_BLK, n_keys), 1)
    ok = (col >= span) | (jnp.abs(qpos - (start + col)) <= WINDOW)
    for i in range(N_HEADS_A // 2):
        cols = slice(2 * i * HEAD_DIM, (2 * i + 2) * HEAD_DIM)
        kv = (k0l, k0h, v0l, v0h) if (2 * i) // GROUP_A == 0 else (k1l, k1h, v1l, v1h)
        o = _pair_attention(qa_ref[:, cols] * scale, *kv, sinks=(sink_ref[2 * i], sink_ref[2 * i + 1]),
                            adjust=lambda _, s: jnp.where(ok, s, NEG))
        oa_o[:, cols] = o.astype(oa_o.dtype)


def _win_attention(sink, qa, ka, va, cache_k, cache_v, layer, n_lat):
    t = qa.shape[0]
    nb = n_lat // Q_BLK
    past = cache_k.shape[2]
    return pl.pallas_call(
        functools.partial(_win_attn_kernel, n_lat=n_lat),
        out_shape=jax.ShapeDtypeStruct((t, WIDTH_A), BF16),
        grid=(t // n_lat, nb),
        in_specs=[
            pl.BlockSpec(memory_space=pltpu.SMEM),
            pl.BlockSpec((Q_BLK, WIDTH_A), lambda b, n: (b * nb + n, 0)),
            pl.BlockSpec((n_lat, KV_WIDTH_A), lambda b, n: (b, 0)),
            pl.BlockSpec((n_lat, KV_WIDTH_A), lambda b, n: (b, 0)),
            pl.BlockSpec((None, None, past, KV_WIDTH_A), lambda b, n: (b, layer, 0, 0)),
            pl.BlockSpec((None, None, past, KV_WIDTH_A), lambda b, n: (b, layer, 0, 0)),
        ],
        out_specs=pl.BlockSpec((Q_BLK, WIDTH_A), lambda b, n: (b * nb + n, 0)),
        compiler_params=_params(("arbitrary", "arbitrary")),
        name="window_attention",
    )(sink, qa, ka, va, cache_k, cache_v)


def _nbr_attn_kernel(qb_ref, kb_ref, vb_ref, ck_ref, cv_ref, bias_ref, ob_o, *, rows, kr):
    scale = HEAD_DIM ** -0.5
    r = pl.program_id(1)
    start = pl.multiple_of(jnp.clip(r - kr // 2, 0, rows - kr) * GRID_W, GRID_W)
    nwin = kr * GRID_W
    kcat = jnp.concatenate([kb_ref[pl.ds(start, nwin), :], ck_ref[...]], axis=0)
    vcat = jnp.concatenate([vb_ref[pl.ds(start, nwin), :], cv_ref[...]], axis=0)
    past = ck_ref.shape[0]
    for i in range(N_HEADS_B // 2):
        cols = slice(2 * i * HEAD_DIM, (2 * i + 2) * HEAD_DIM)
        kl, kh, _, _ = _lane_halves(kcat[:, cols])
        vl, vh, _, _ = _lane_halves(vcat[:, cols])
        o = _pair_attention(
            qb_ref[:, cols] * scale, kl, kh, vl, vh,
            adjust=lambda hh, s, i=i: s + jnp.concatenate([bias_ref[2 * i + hh], jnp.zeros((GRID_W, past), F32)],
                                                          axis=-1))
        ob_o[:, cols] = o.astype(ob_o.dtype)


def _nbr_bias_table(rel_bias, rows):
    kr = min(NA_ROWS, rows)
    cls = np.arange(kr)[:, None]
    m = np.arange(kr)[None, :]
    row_sel = (m - cls + NA_ROWS - 1)[:, :, None] == np.arange(2 * NA_ROWS - 1)[None, None, :]
    c = np.arange(GRID_W)[:, None]
    kc = np.arange(GRID_W)[None, :]
    dc = np.clip(kc - c, -(NA_COLS - 1), NA_COLS - 1) + NA_COLS - 1
    col_sel = dc[:, :, None] == np.arange(2 * NA_COLS - 1)[None, None, :]
    col_start = np.clip(c - NA_COLS // 2, 0, GRID_W - NA_COLS)
    ok = (kc >= col_start) & (kc < col_start + NA_COLS)
    tab = jnp.einsum('hrd,ymr,ckd->hycmk', rel_bias.astype(F32), row_sel.astype(np.float32),
                     col_sel.astype(np.float32), precision=lax.Precision.HIGHEST)
    tab = jnp.where(ok[None, None, :, None, :], tab, NEG)
    return tab.reshape(rel_bias.shape[0], kr, GRID_W, kr * GRID_W)


def _nbr_attention(qb, kb, vb, cache_k, cache_v, bias_tab, layer, n_lat):
    t = qb.shape[0]
    rows = n_lat // GRID_W
    kr = min(NA_ROWS, rows)
    past = cache_k.shape[2]

    def bias_map(b, r):
        return (0, r - jnp.clip(r - kr // 2, 0, rows - kr), 0, 0)

    return pl.pallas_call(
        functools.partial(_nbr_attn_kernel, rows=rows, kr=kr),
        out_shape=jax.ShapeDtypeStruct((t, WIDTH_B), BF16),
        grid=(t // n_lat, rows),
        in_specs=[
            pl.BlockSpec((GRID_W, WIDTH_B), lambda b, r: (b * rows + r, 0)),
            pl.BlockSpec((n_lat, WIDTH_B), lambda b, r: (b, 0)),
            pl.BlockSpec((n_lat, WIDTH_B), lambda b, r: (b, 0)),
            pl.BlockSpec((None, None, past, WIDTH_B), lambda b, r: (b, layer, 0, 0)),
            pl.BlockSpec((None, None, past, WIDTH_B), lambda b, r: (b, layer, 0, 0)),
            pl.BlockSpec((N_HEADS_B, None, GRID_W, kr * GRID_W), bias_map),
        ],
        out_specs=pl.BlockSpec((GRID_W, WIDTH_B), lambda b, r: (b * rows + r, 0)),
        compiler_params=_params(("arbitrary", "arbitrary")),
        name="neighbourhood_attention",
    )(qb, kb, vb, cache_k, cache_v, bias_tab)


ROW_TILE = 8
LANES = 128
assert ROW_TILE * LANES == D_MODEL


PACK_TILE = ROW_TILE // 2


def _store_row_tiles(ref, row0, x):
    n = x.shape[0]
    rt = x.shape[1] // LANES
    for c in range(rt):
        ref[pl.ds(row0 * rt + c, n, stride=rt), :] = x[:, c * LANES:(c + 1) * LANES]


def _pack_bf16_pairs(x):
    bits = lax.bitcast_convert_type(x, jnp.uint32)
    half = x.shape[1] // 2
    return (bits[:, half:] & jnp.uint32(0xFFFF0000)) | (bits[:, :half] >> 16)


def _unpack_bf16_pairs(w):
    lo = lax.bitcast_convert_type(w << 16, F32)
    hi = lax.bitcast_convert_type(w & jnp.uint32(0xFFFF0000), F32)
    return jnp.concatenate([lo, hi], axis=-1).astype(BF16)


def _load_row_tiles(ref, row0, n, rt=ROW_TILE):
    return jnp.concatenate([ref[pl.ds(row0 * rt + c, n, stride=rt), :] for c in range(rt)], axis=-1)


def _outproj_kernel(x_ref, oa_ref, ob_ref, oc_ref, w_ref, mod_ref, g_ref, wr_ref, br_ref, *rest):
    xn_o, h_o, lg_o = rest[-3:]
    mix = jnp.dot(oa_ref[...], w_ref[0:WIDTH_A, :], preferred_element_type=F32)
    mix += jnp.dot(ob_ref[...], w_ref[WIDTH_A:WIDTH_A + WIDTH_B, :], preferred_element_type=F32)
    mix += jnp.dot(oc_ref[...], w_ref[WIDTH_A + WIDTH_B:, :], preferred_element_type=F32)
    xn = x_ref[...] + mod_ref[2:3, :] * mix
    xn_o[...] = xn
    h = xn * lax.rsqrt(jnp.mean(xn * xn, axis=-1, keepdims=True) + EPS) * g_ref[...]
    h = h * (1.0 + mod_ref[4:5, :]) + mod_ref[3:4, :]
    h_hi = h.astype(BF16)
    h_o[...] = h_hi
    h_lo = (h - h_hi.astype(F32)).astype(BF16)
    wr = wr_ref[...]
    w_hi = wr.astype(BF16)
    w_lo = (wr - w_hi.astype(F32)).astype(BF16)
    lg = lax.dot_general(w_hi, h_hi, _NT, preferred_element_type=F32)
    lg += lax.dot_general(w_hi, h_lo, _NT, preferred_element_type=F32)
    lg += lax.dot_general(w_lo, h_hi, _NT, preferred_element_type=F32)
    lg_o[...] = lg + br_ref[...]


def _outproj(x, oa, ob, oc, w_out_bf, mod, group_of_tile, g_ffn, w_router_t, b_router_t, layer, tm, t_all, tok0,
             shared):
    t = x.shape[0]
    b0 = tok0 // tm
    assert tok0 % tm == 0
    in_specs = [
        pl.BlockSpec((tm, D_MODEL), lambda i: (i, 0)),
        pl.BlockSpec((tm, WIDTH_A), lambda i: (i, 0)),
        pl.BlockSpec((tm, WIDTH_B), lambda i: (i, 0)),
        pl.BlockSpec((tm, CONV_CH), lambda i: (i, 0)),
        pl.BlockSpec((None, D_MODEL, D_MODEL), lambda i: (layer, 0, 0)),
        pl.BlockSpec((None, None, N_MOD, D_MODEL), lambda i: (layer, group_of_tile(i), 0, 0)),
        pl.BlockSpec((None, 1, D_MODEL), lambda i: (layer, 0, 0)),
        pl.BlockSpec((None, N_EXPERTS, D_MODEL), lambda i: (layer, 0, 0)),
        pl.BlockSpec((None, N_EXPERTS, 1), lambda i: (layer, 0, 0)),
    ]
    args = [x, oa, ob, oc, w_out_bf, mod, g_ffn, w_router_t, b_router_t]
    aliases = {len(args): 1, len(args) + 1: 2}
    in_specs += [pl.BlockSpec(memory_space=pl.ANY)] * 2
    args += list(shared)
    return pl.pallas_call(
        _outproj_kernel,
        out_shape=[jax.ShapeDtypeStruct((t, D_MODEL), F32), jax.ShapeDtypeStruct((t_all, D_MODEL), BF16),
                   jax.ShapeDtypeStruct((N_EXPERTS, t_all), F32)],
        grid=(t // tm,),
        in_specs=in_specs,
        out_specs=[pl.BlockSpec((tm, D_MODEL), lambda i: (i, 0)),
                   pl.BlockSpec((tm, D_MODEL), lambda i: (b0 + i, 0)),
                   pl.BlockSpec((N_EXPERTS, tm), lambda i: (0, b0 + i))],
        input_output_aliases=aliases,
        compiler_params=_params(("arbitrary",), VMEM_LIMIT_V7X),
        name="outproj_router",
    )(*args)


BLK_GROUP = 1
X_AHEAD = 3
X_SLOTS = 4
Y_SLOTS = 2
assert X_SLOTS >= X_AHEAD + BLK_GROUP and Y_SLOTS >= 2 * BLK_GROUP


def _expert_kernel(seg_ref, n_used_ref, x_hbm, wg_ref, bg_ref, wu_ref, bu_ref, wd_ref, bd_ref, y_hbm,
                   x_buf, y_buf, xsem, ysem, zsem, *, n_blocks):
    e = pl.program_id(0)
    n_used = n_used_ref[0]
    blk_rows = MOE_BLK * ROW_TILE
    first_blk = seg_ref[SEG_DST, e, SEG_BLK]
    n_blk = seg_ref[SEG_CNT, e, SEG_BLK]

    def rows_of(g, rows=blk_rows):
        return pl.ds(pl.multiple_of(g * rows, rows), rows)

    def x_copy(g):
        return pltpu.make_async_copy(x_hbm.at[rows_of(g, MOE_BLK * PACK_TILE)], x_buf.at[g % X_SLOTS],
                                     xsem.at[g % X_SLOTS])

    def y_copy(g):
        return pltpu.make_async_copy(y_buf.at[g % Y_SLOTS], y_hbm.at[rows_of(g)], ysem.at[g % Y_SLOTS])

    def zero_copy(g):
        return pltpu.make_async_copy(y_buf.at[0], y_hbm.at[rows_of(g)], zsem)

    @pl.when(e == 0)
    def _():
        for g in range(X_AHEAD):
            @pl.when(g < n_used)
            def _():
                x_copy(g).start()

    def step(g, nb):
        for d in range(nb):
            @pl.when(g + X_AHEAD + d < n_used)
            def _():
                x_copy(g + X_AHEAD + d).start()
        xs = []
        for d in range(nb):
            x_copy(g + d).wait()

            @pl.when(g + d >= Y_SLOTS)
            def _():
                y_copy(g + d - Y_SLOTS).wait()

            xs.append(_unpack_bf16_pairs(_load_row_tiles(x_buf.at[(g + d) % X_SLOTS], 0, MOE_BLK, PACK_TILE)))
        x = xs[0] if nb == 1 else jnp.concatenate(xs, axis=0)
        gt = jnp.dot(x, wg_ref[...].astype(BF16), preferred_element_type=F32) + bg_ref[...]
        up = jnp.dot(x, wu_ref[...].astype(BF16), preferred_element_type=F32) + bu_ref[...]
        gt = jnp.minimum(gt, SWIGLU_LIMIT)
        up = jnp.clip(up, -SWIGLU_LIMIT, SWIGLU_LIMIT)
        a = gt * jax.nn.sigmoid(SWIGLU_ALPHA * gt) * (up + 1.0)
        y = jnp.dot(a.astype(BF16), wd_ref[...].astype(BF16), preferred_element_type=F32) + bd_ref[...]
        for d in range(nb):
            _store_row_tiles(y_buf.at[(g + d) % Y_SLOTS], 0, y[d * MOE_BLK:(d + 1) * MOE_BLK])
            y_copy(g + d).start()

    def pair(p, carry):
        step(first_blk + BLK_GROUP * p, BLK_GROUP)
        return carry

    lax.fori_loop(0, n_blk // BLK_GROUP, pair, 0)
    for r in range(1, BLK_GROUP):
        @pl.when(n_blk % BLK_GROUP == r)
        def _():
            step(first_blk + n_blk - r, r)

    @pl.when(e == pl.num_programs(0) - 1)
    def _():
        for d in range(Y_SLOTS, 0, -1):
            @pl.when(n_used >= d)
            def _():
                y_copy(n_used - d).wait()

        y_buf[0] = jnp.zeros((blk_rows, LANES), F32)

        def start_zero(g, carry):
            zero_copy(g).start()
            return carry

        def wait_zero(g, carry):
            zero_copy(g).wait()
            return carry

        lax.fori_loop(n_used, n_blocks, start_zero, 0)
        lax.fori_loop(n_used, n_blocks, wait_zero, 0)


def _experts(seg, n_used, x_sorted, w_gate, b_gate, w_up, b_up, w_down, b_down, layer):
    n_blocks = x_sorted.shape[0] // (MOE_BLK * PACK_TILE)
    wspec = pl.BlockSpec((None, None, D_MODEL, D_MODEL), lambda e, sg, nu: (layer, e, 0, 0))
    bspec = pl.BlockSpec((None, None, 1, D_MODEL), lambda e, sg, nu: (layer, e, 0, 0))
    anyspec = pl.BlockSpec(memory_space=pl.ANY)
    depth = w_gate.shape[0]
    b4 = lambda b: b.reshape(depth, N_EXPERTS, 1, D_MODEL)
    return pl.pallas_call(
        functools.partial(_expert_kernel, n_blocks=n_blocks),
        out_shape=jax.ShapeDtypeStruct((n_blocks * MOE_BLK * ROW_TILE, LANES), F32),
        grid_spec=pltpu.PrefetchScalarGridSpec(
            num_scalar_prefetch=2,
            grid=(N_EXPERTS,),
            in_specs=[anyspec, wspec, bspec, wspec, bspec, wspec, bspec],
            out_specs=anyspec,
            scratch_shapes=[
                pltpu.VMEM((X_SLOTS, MOE_BLK * PACK_TILE, LANES), jnp.uint32),
                pltpu.VMEM((Y_SLOTS, MOE_BLK * ROW_TILE, LANES), F32),
                pltpu.SemaphoreType.DMA((X_SLOTS,)),
                pltpu.SemaphoreType.DMA((Y_SLOTS,)),
                pltpu.SemaphoreType.DMA,
            ],
        ),
        compiler_params=_params(("arbitrary",), VMEM_LIMIT_V7X),
        name="experts",
    )(seg, n_used, x_sorted, w_gate, b4(b_gate), w_up, b4(b_up), w_down, b4(b_down))


CUM_CHUNK = 256


def _sublane_cumsum(x):
    row = lax.broadcasted_iota(jnp.int32, x.shape, 0)
    d = 1
    while d < x.shape[0]:
        x = x + jnp.where(row >= d, pltpu.roll(x, d, axis=0), 0)
        d *= 2
    return x


TOK_TILE = CUM_CHUNK
TILE_ROWS = TOP_K * TOK_TILE
SEG_LANES = 128
SEG_PAD_FROM = SEG_LANES - 2
SEG_PAD_LEN = SEG_LANES - 1
SEG_BLK = SEG_LANES - 3
SEG_SRC, SEG_CNT, SEG_DST = 0, 1, 2


def _route_kernel(lg_ref, tri_ref, gate_o, lpos_o, seg_o, used_o):
    lg = lg_ref[...]
    n_e, t = lg.shape
    e_iota = lax.broadcasted_iota(jnp.int32, lg.shape, 0)
    work = lg
    tops, hots = [], []
    for _ in range(TOP_K):
        m = jnp.max(work, axis=0, keepdims=True)
        first = jnp.min(jnp.where(work == m, e_iota, n_e), axis=0, keepdims=True)
        hot = e_iota == first
        work = jnp.where(hot, -jnp.inf, work)
        tops.append(m)
        hots.append(hot)
    ex = [jnp.exp(m - tops[0]) for m in tops]
    den = ex[0] + ex[1] + ex[2] + ex[3]
    for k in range(TOP_K):
        gate_o[k:k + 1, :] = ex[k] / den

    chosen = jnp.where(hots[0] | hots[1] | hots[2] | hots[3], 1.0, 0.0)
    tri = tri_ref[...]
    lane = lax.broadcasted_iota(jnp.int32, (n_e, SEG_LANES), 1)
    seg_cnt = jnp.zeros((n_e, SEG_LANES), jnp.int32)
    seg_before = jnp.zeros((n_e, SEG_LANES), jnp.int32)
    carry = jnp.zeros((n_e, 1), F32)
    rank_in_tile = []
    n_tiles = t // TOK_TILE
    for c in range(n_tiles):
        chunk = chosen[:, c * TOK_TILE:(c + 1) * TOK_TILE]
        inc = jnp.dot(chunk.astype(BF16), tri, preferred_element_type=F32)
        cnt = inc[:, TOK_TILE - 1:TOK_TILE]
        rank_in_tile.append((inc - chunk).astype(jnp.int32))
        seg_cnt = jnp.where(lane == c, cnt.astype(jnp.int32), seg_cnt)
        seg_before = jnp.where(lane == c, carry.astype(jnp.int32), seg_before)
        carry = carry + cnt

    counts = jnp.broadcast_to(carry.astype(jnp.int32), (n_e, SEG_LANES))
    padded = ((counts + (MOE_BLK - 1)) >> MOE_SHIFT) << MOE_SHIFT
    pad_end = _sublane_cumsum(padded)
    pad_start = pad_end - padded
    seg_src = _sublane_cumsum(seg_cnt) - seg_cnt
    seg_o[SEG_SRC] = seg_src
    seg_o[SEG_CNT] = jnp.where(lane == SEG_PAD_LEN, padded - counts,
                               jnp.where(lane == SEG_BLK, padded >> MOE_SHIFT, seg_cnt))
    seg_o[SEG_DST] = jnp.where(lane == SEG_PAD_FROM, pad_start + counts,
                               jnp.where(lane == SEG_BLK, pad_start >> MOE_SHIFT, pad_start + seg_before))

    for c in range(n_tiles):
        pos = seg_src[:, c:c + 1] + rank_in_tile[c]
        for k in range(TOP_K):
            lpos_o[k:k + 1, c * TOK_TILE:(c + 1) * TOK_TILE] = jnp.sum(
                jnp.where(hots[k][:, c * TOK_TILE:(c + 1) * TOK_TILE], pos, 0), axis=0, keepdims=True)

    used_o[...] = jnp.broadcast_to(pad_end[n_e - 1:n_e, :] >> MOE_SHIFT, used_o.shape)


def _route(logits):
    t = logits.shape[1]
    assert t % TOK_TILE == 0 and t // TOK_TILE <= SEG_BLK
    tri = jnp.asarray(np.triu(np.ones((CUM_CHUNK, CUM_CHUNK), np.float32)), dtype=BF16)
    gates, lpos, seg, used = pl.pallas_call(
        _route_kernel,
        out_shape=[jax.ShapeDtypeStruct((TOP_K, t), F32), jax.ShapeDtypeStruct((TOP_K, t), jnp.int32),
                   jax.ShapeDtypeStruct((3, N_EXPERTS, SEG_LANES), jnp.int32),
                   jax.ShapeDtypeStruct((8, SEG_LANES), jnp.int32)],
        compiler_params=_params(None, VMEM_LIMIT_V7X),
        name="route",
    )(logits, tri)
    return gates, lpos, seg, used[0, :1]


def _segment_copies(src, dst, src0, dst0, cnt, sem, wait, rt=ROW_TILE):
    @pl.when(cnt > 0)
    def _():
        size = cnt * rt
        cp = pltpu.make_async_copy(
            src.at[pl.ds(pl.multiple_of(src0 * rt, rt), size)],
            dst.at[pl.ds(pl.multiple_of(dst0 * rt, rt), size)], sem)
        if wait:
            cp.wait()
        else:
            cp.start()


def _dispatch_kernel(seg_ref, nu_ref, h_ref, lpos_ref, x_hbm, z_buf, zero_buf, sem, zsem, *, n_tiles, n_blocks):
    c = pl.program_id(0)
    slot = c % 2

    def wait_tile(s):
        pltpu.make_async_copy(z_buf.at[s], x_hbm.at[pl.ds(0, TILE_ROWS * PACK_TILE)], sem.at[s]).wait()

    def padding(wait):
        def per_expert(e, carry):
            _segment_copies(zero_buf, x_hbm, 0, seg_ref[SEG_DST, e, SEG_PAD_FROM], seg_ref[SEG_CNT, e, SEG_PAD_LEN],
                            zsem, wait, PACK_TILE)
            return carry

        lax.fori_loop(0, N_EXPERTS, per_expert, 0)

        def per_block(b, carry):
            cp = pltpu.make_async_copy(
                zero_buf, x_hbm.at[pl.ds(pl.multiple_of(b * (MOE_BLK * PACK_TILE), MOE_BLK * PACK_TILE),
                                         MOE_BLK * PACK_TILE)], zsem)
            if wait:
                cp.wait()
            else:
                cp.start()
            return carry

        lax.fori_loop(nu_ref[0], n_blocks, per_block, 0)

    @pl.when(c == 0)
    def _():
        zero_buf[...] = jnp.zeros(zero_buf.shape, jnp.uint32)
        padding(False)

    @pl.when(c >= 2)
    def _():
        wait_tile(slot)

    row = lax.broadcasted_iota(jnp.int32, (TILE_ROWS, TOK_TILE), 0)
    hit = row == lpos_ref[0:1, :]
    for k in range(1, TOP_K):
        hit = hit | (row == lpos_ref[k:k + 1, :])
    z = jnp.dot(jnp.where(hit, 1.0, 0.0).astype(BF16), h_ref[...], preferred_element_type=F32)
    _store_row_tiles(z_buf.at[slot], 0, _pack_bf16_pairs(z))

    def per_expert(e, carry):
        _segment_copies(z_buf.at[slot], x_hbm, seg_ref[SEG_SRC, e, c], seg_ref[SEG_DST, e, c], seg_ref[SEG_CNT, e, c],
                        sem.at[slot], False, PACK_TILE)
        return carry

    lax.fori_loop(0, N_EXPERTS, per_expert, 0)

    @pl.when(c == n_tiles - 1)
    def _():
        if n_tiles > 1:
            wait_tile(1 - slot)
        wait_tile(slot)
        padding(True)


def _dispatch(seg, n_used, h_all, lpos, n_blocks):
    t = h_all.shape[0]
    n_tiles = t // TOK_TILE
    return pl.pallas_call(
        functools.partial(_dispatch_kernel, n_tiles=n_tiles, n_blocks=n_blocks),
        out_shape=jax.ShapeDtypeStruct((n_blocks * MOE_BLK * PACK_TILE, LANES), jnp.uint32),
        grid_spec=pltpu.PrefetchScalarGridSpec(
            num_scalar_prefetch=2,
            grid=(n_tiles,),
            in_specs=[pl.BlockSpec((TOK_TILE, D_MODEL), lambda c, sg, nu: (c, 0)),
                      pl.BlockSpec((TOP_K, TOK_TILE), lambda c, sg, nu: (0, c))],
            out_specs=pl.BlockSpec(memory_space=pl.ANY),
            scratch_shapes=[
                pltpu.VMEM((2, TILE_ROWS * PACK_TILE, LANES), jnp.uint32),
                pltpu.VMEM((MOE_BLK * PACK_TILE, LANES), jnp.uint32),
                pltpu.SemaphoreType.DMA((2,)),
                pltpu.SemaphoreType.DMA,
            ],
        ),
        compiler_params=_params(("arbitrary",), VMEM_LIMIT_V7X),
        name="dispatch",
    )(seg, n_used, h_all, lpos)


def _combine_kernel(seg_ref, x_ref, y_hbm, lpos_ref, lpos_t_ref, gate_ref, mod_ref, o_ref, y_buf, sem, *, tile0,
                    n_tiles):
    c = pl.program_id(0)
    slot = c % 2

    def fetch(cc):
        def per_expert(e, carry):
            _segment_copies(y_hbm, y_buf.at[cc % 2], seg_ref[SEG_DST, e, tile0 + cc], seg_ref[SEG_SRC, e, tile0 + cc],
                            seg_ref[SEG_CNT, e, tile0 + cc], sem.at[cc % 2], False)
            return carry

        lax.fori_loop(0, N_EXPERTS, per_expert, 0)

    @pl.when(c == 0)
    def _():
        fetch(0)

    @pl.when(c + 1 < n_tiles)
    def _():
        fetch(c + 1)

    pltpu.make_async_copy(y_hbm.at[pl.ds(0, TILE_ROWS * ROW_TILE)], y_buf.at[slot], sem.at[slot]).wait()

    row = lax.broadcasted_iota(jnp.int32, (TILE_ROWS, TOK_TILE), 0)
    row_gate = jnp.zeros((TILE_ROWS, 1), F32)
    for k in range(TOP_K):
        row_gate += jnp.sum(jnp.where(row == lpos_ref[k:k + 1, :], gate_ref[k:k + 1, :], 0.0), axis=1, keepdims=True)
    ys = _load_row_tiles(y_buf.at[slot], 0, TILE_ROWS) * row_gate
    ys_hi = ys.astype(BF16)
    ys_lo = (ys - ys_hi.astype(F32)).astype(BF16)
    col = lax.broadcasted_iota(jnp.int32, (TOK_TILE, TILE_ROWS), 1)
    hit = col == lpos_t_ref[:, 0:1]
    for k in range(1, TOP_K):
        hit = hit | (col == lpos_t_ref[:, k:k + 1])
    u = jnp.where(hit, 1.0, 0.0).astype(BF16)
    mix = jnp.dot(u, ys_hi, preferred_element_type=F32) + jnp.dot(u, ys_lo, preferred_element_type=F32)
    o_ref[...] = x_ref[...] + mod_ref[5:6, :] * mix


def _combine(seg, x_mid, y_sorted, lpos, lpos_t, gates, mod, group_of_tile, layer, tok0):
    t = x_mid.shape[0]
    tile0 = tok0 // TOK_TILE
    n_tiles = t // TOK_TILE
    return pl.pallas_call(
        functools.partial(_combine_kernel, tile0=tile0, n_tiles=n_tiles),
        out_shape=jax.ShapeDtypeStruct((t, D_MODEL), F32),
        grid_spec=pltpu.PrefetchScalarGridSpec(
            num_scalar_prefetch=1,
            grid=(n_tiles,),
            in_specs=[
                pl.BlockSpec((TOK_TILE, D_MODEL), lambda c, sg: (c, 0)),
                pl.BlockSpec(memory_space=pl.ANY),
                pl.BlockSpec((TOP_K, TOK_TILE), lambda c, sg: (0, tile0 + c)),
                pl.BlockSpec((TOK_TILE, TOP_K), lambda c, sg: (tile0 + c, 0)),
                pl.BlockSpec((TOP_K, TOK_TILE), lambda c, sg: (0, tile0 + c)),
                pl.BlockSpec((None, None, N_MOD, D_MODEL), lambda c, sg: (layer, group_of_tile(c), 0, 0)),
            ],
            out_specs=pl.BlockSpec((TOK_TILE, D_MODEL), lambda c, sg: (c, 0)),
            scratch_shapes=[pltpu.VMEM((2, TILE_ROWS * ROW_TILE, LANES), F32), pltpu.SemaphoreType.DMA((2,))],
        ),
        compiler_params=_params(("arbitrary",), VMEM_LIMIT_V7X),
        name="combine",
    )(seg, x_mid, y_sorted, lpos, lpos_t, gates, mod)


def _rope_tables(n_lat):
    quarter = HEAD_DIM // 4
    t = jnp.arange(n_lat)
    inv = ROPE_BASE ** (-jnp.arange(quarter, dtype=F32) / quarter)
    ang_r = (t // GRID_W).astype(F32)[:, None] * inv
    ang_c = (t % GRID_W).astype(F32)[:, None] * inv
    cos = jnp.concatenate([jnp.cos(ang_r)] * 2 + [jnp.cos(ang_c)] * 2, axis=-1)
    sin = jnp.concatenate([-jnp.sin(ang_r), jnp.sin(ang_r), -jnp.sin(ang_c), jnp.sin(ang_c)], axis=-1)
    return jnp.concatenate([cos, cos], axis=-1), jnp.concatenate([sin, sin], axis=-1)


def _block_diag_ones():
    idx = np.arange(MXU_COLS_V7X) // HEAD_DIM
    return jnp.asarray(idx[:, None] == idx[None, :], dtype=BF16)


def kernel(x_prompt, x_sample, cache_k_win, cache_v_win, cache_k_nbr, cache_v_nbr, c, c_ctx, w_mod, b_mod, g_mix, g_ffn, w_in, w_out, qn_win, kn_win, qn_nbr, kn_nbr, sink_win, rel_bias_nbr, conv_w, w_router, b_router, w_gate, b_gate, w_up, b_up, w_down, b_down):
    bsz, n_ctx, d = x_prompt.shape
    dbs, n_lat, _ = x_sample.shape
    depth = w_in.shape[0]
    past = cache_k_win.shape[2]
    assert d == D_MODEL and dbs + 1 <= COND_ROWS and n_lat % GRID_W == 0 and n_lat >= Q_BLK + 2 * WINDOW
    t_ctx, t_lat = bsz * n_ctx, dbs * n_lat

    cond = jnp.concatenate([c_ctx[None], c, jnp.zeros((COND_ROWS - 1 - dbs, d), F32)], axis=0)
    mod = _modulation(cond, w_mod, b_mod)

    w_in_bf = w_in.astype(BF16)
    w_out_bf = w_out.astype(BF16)
    norm_w = jnp.concatenate([jnp.tile(qn_win, (1, N_HEADS_A)), jnp.tile(kn_win, (1, N_KV_A)),
                              jnp.tile(qn_nbr, (1, N_HEADS_B)), jnp.tile(kn_nbr, (1, N_HEADS_B))], axis=-1)[:, None, :]
    ones_bd = _block_diag_ones()
    rope_tabs = _rope_tables(n_lat)
    g_mix3, g_ffn3 = g_mix[:, None, :], g_ffn[:, None, :]
    w_router_t = jnp.swapaxes(w_router, 1, 2)
    b_router_t = b_router[:, :, None]
    t_all = t_ctx + t_lat
    ck_win = cache_k_win.reshape(dbs, depth, past, KV_WIDTH_A)
    cv_win = cache_v_win.reshape(dbs, depth, past, KV_WIDTH_A)
    ck_nbr = cache_k_nbr.reshape(dbs, depth, past, WIDTH_B)
    cv_nbr = cache_v_nbr.reshape(dbs, depth, past, WIDTH_B)

    tm_ctx = 2 * n_ctx
    tm_lat = 512
    ctx_group = lambda i: 0
    lat_group_in = lambda i: 1 + i
    lat_group_out = lambda i: 1 + (i * tm_lat) // n_lat
    lat_group_comb = lambda i: 1 + (i * TOK_TILE) // n_lat
    assert t_ctx % TOK_TILE == 0 and n_lat % TOK_TILE == 0

    xp = x_prompt.reshape(t_ctx, d)
    xs = x_sample.reshape(t_lat, d)
    caches = tuple(jnp.zeros((bsz, depth, n_ctx, w), F32) for w in (KV_WIDTH_A, KV_WIDTH_A, WIDTH_B, WIDTH_B))
    for l in range(depth):
        qa, qb, *caches, oc = _inproj(xp, mod, ctx_group, g_mix3, w_in_bf, l, norm_w, ones_bd, conv_w,
                                      tm_ctx, n_ctx, caches=caches)
        oa, ob = _ctx_attention(sink_win[l], qa, caches[0], caches[1], qb, caches[2], caches[3], n_ctx, l)
        shared = (jnp.zeros((t_all, d), BF16), jnp.zeros((N_EXPERTS, t_all), F32))
        xp_mid, h_all, lg_all = _outproj(xp, oa, ob, oc, w_out_bf, mod, ctx_group, g_ffn3, w_router_t, b_router_t, l,
                                         tm_ctx, t_all, 0, shared)

        qa, qb, ka, va, kb, vb, oc = _inproj(xs, mod, lat_group_in, g_mix3, w_in_bf, l, norm_w, ones_bd, conv_w,
                                             n_lat, n_lat, rope_tabs)
        oa = _win_attention(sink_win[l], qa, ka, va, ck_win, cv_win, l, n_lat)
        ob = _nbr_attention(qb, kb, vb, ck_nbr, cv_nbr, _nbr_bias_table(rel_bias_nbr[l], n_lat // GRID_W), l, n_lat)
        xs_mid, h_all, lg_all = _outproj(xs, oa, ob, oc, w_out_bf, mod, lat_group_out, g_ffn3, w_router_t, b_router_t,
                                         l, tm_lat, t_all, t_ctx, shared=(h_all, lg_all))

        gates, lpos, seg, n_used = _route(lg_all)
        x_sorted = _dispatch(seg, n_used, h_all, lpos, t_all * TOP_K // MOE_BLK + N_EXPERTS)
        y_sorted = _experts(seg, n_used, x_sorted, w_gate, b_gate, w_up, b_up, w_down, b_down, l)
        lpos_t = lpos.T
        xp = _combine(seg, xp_mid, y_sorted, lpos, lpos_t, gates, mod, ctx_group, l, 0)
        xs = _combine(seg, xs_mid, y_sorted, lpos, lpos_t, gates, mod, lat_group_comb, l, t_ctx)

    new_k_win, new_v_win = (a.reshape(bsz, depth, n_ctx, N_KV_A, HEAD_DIM) for a in caches[:2])
    new_k_nbr, new_v_nbr = (a.reshape(bsz, depth, n_ctx, N_HEADS_B, HEAD_DIM) for a in caches[2:])
    return (xp.reshape(bsz, n_ctx, d), xs.reshape(dbs, n_lat, d), new_k_win, new_v_win, new_k_nbr, new_v_nbr)
```

```python
import functools

import numpy as np
import jax
import jax.numpy as jnp
from jax import lax
from jax.experimental import pallas as pl
from jax.experimental.pallas import tpu as pltpu

F32 = jnp.float32
BF16 = jnp.bfloat16

D_MODEL = 1024
HEAD_DIM = 64
GRID_W = 64
N_HEADS_A = 8
N_KV_A = 2
GROUP_A = N_HEADS_A // N_KV_A
WINDOW = 128
Q_BLK = 128
N_HEADS_B = 4
NA_ROWS = 8
NA_COLS = 16
CONV_CH = 256
CONV_W = 3
WIDTH_A = N_HEADS_A * HEAD_DIM
KV_WIDTH_A = N_KV_A * HEAD_DIM
WIDTH_B = N_HEADS_B * HEAD_DIM
IN_COLS = WIDTH_A + 2 * KV_WIDTH_A + 3 * WIDTH_B + 3 * CONV_CH
N_EXPERTS = 32
TOP_K = 4
SWIGLU_LIMIT = 7.0
SWIGLU_ALPHA = 1.702
ROPE_BASE = 10000.0
EPS = 1e-6
NEG = -1e30
N_MOD = 6

C_QA = 0
C_KA = C_QA + WIDTH_A
C_VA = C_KA + KV_WIDTH_A
C_QB = C_VA + KV_WIDTH_A
C_KB = C_QB + WIDTH_B
C_VB = C_KB + WIDTH_B
C_U = C_VB + WIDTH_B
C_GB = C_U + CONV_CH
C_GC = C_GB + CONV_CH

MXU_COLS_V7X = 256
COND_ROWS = 8
MOE_BLK = 256
MOE_SHIFT = MOE_BLK.bit_length() - 1
assert 1 << MOE_SHIFT == MOE_BLK
VMEM_LIMIT_V7X = 56 * 1024 * 1024

_NT = (((1,), (1,)), ((), ()))


def _params(sem, vmem=None):
    return pltpu.CompilerParams(dimension_semantics=sem, vmem_limit_bytes=vmem)


def _mod_kernel(c_ref, w_ref, b_ref, o_ref):
    c = c_ref[...]
    s = c * jax.nn.sigmoid(c)
    o_ref[...] = jnp.dot(s.astype(BF16), w_ref[...].astype(BF16), preferred_element_type=F32) + b_ref[...]


def _modulation(cond, w_mod, b_mod):
    depth = w_mod.shape[0]
    out = pl.pallas_call(
        _mod_kernel,
        out_shape=jax.ShapeDtypeStruct((depth, COND_ROWS, N_MOD * D_MODEL), F32),
        grid=(depth, N_MOD),
        in_specs=[
            pl.BlockSpec((COND_ROWS, D_MODEL), lambda l, j: (0, 0)),
            pl.BlockSpec((None, D_MODEL, D_MODEL), lambda l, j: (l, 0, j)),
            pl.BlockSpec((None, 1, D_MODEL), lambda l, j: (l, 0, j)),
        ],
        out_specs=pl.BlockSpec((None, COND_ROWS, D_MODEL), lambda l, j: (l, 0, j)),
        compiler_params=_params(("arbitrary", "arbitrary")),
        name="modulation",
    )(cond, w_mod, b_mod.reshape(depth, 1, N_MOD * D_MODEL))
    return out.reshape(depth, COND_ROWS, N_MOD, D_MODEL)


def _head_norm(x, w_row, ones_ref):
    width = x.shape[1]
    sq = (x * x).astype(BF16)
    parts = []
    for c0 in range(0, width, MXU_COLS_V7X):
        wd = min(MXU_COLS_V7X, width - c0)
        parts.append(jnp.dot(sq[:, c0:c0 + wd], ones_ref[:wd, :wd], preferred_element_type=F32))
    ss = parts[0] if len(parts) == 1 else jnp.concatenate(parts, axis=-1)
    return x * lax.rsqrt(ss * (1.0 / HEAD_DIM) + EPS) * w_row


def _rope(x, cos, sin):
    width = x.shape[1]
    lane = lax.broadcasted_iota(jnp.int32, x.shape, 1)
    quarter = HEAD_DIM // 4
    partner = jnp.where((lane % (2 * quarter)) < quarter,
                        pltpu.roll(x, width - quarter, axis=1), pltpu.roll(x, quarter, axis=1))
    reps = width // cos.shape[1]
    cos_w = cos if reps == 1 else jnp.concatenate([cos] * reps, axis=-1)
    sin_w = sin if reps == 1 else jnp.concatenate([sin] * reps, axis=-1)
    return x * cos_w + partner * sin_w


def _inproj_kernel(*refs, seq_len, rope):
    x_ref, mod_ref, g_ref, w_ref, nw_ref, ones_ref, cw_ref = refs[:7]
    qa_o, qb_o, ka_o, va_o, kb_o, vb_o, oc_o = refs[-7:]
    if rope:
        cos_ref, sin_ref = refs[7:9]
    x = x_ref[...]
    tm = x.shape[0]
    h = x * lax.rsqrt(jnp.mean(x * x, axis=-1, keepdims=True) + EPS) * g_ref[...]
    h = h * (1.0 + mod_ref[1:2, :]) + mod_ref[0:1, :]
    p = jnp.dot(h.astype(BF16), w_ref[...], preferred_element_type=F32)

    qa = _head_norm(p[:, C_QA:C_KA], nw_ref[:, 0:WIDTH_A], ones_ref)
    ka = _head_norm(p[:, C_KA:C_VA], nw_ref[:, WIDTH_A:WIDTH_A + KV_WIDTH_A], ones_ref)
    o_qb = WIDTH_A + KV_WIDTH_A
    qb = _head_norm(p[:, C_QB:C_KB], nw_ref[:, o_qb:o_qb + WIDTH_B], ones_ref)
    kb = _head_norm(p[:, C_KB:C_VB], nw_ref[:, o_qb + WIDTH_B:o_qb + 2 * WIDTH_B], ones_ref)
    if rope:
        cos, sin = cos_ref[...], sin_ref[...]
        qa = _rope(qa, cos, sin)
        ka = _rope(ka, cos, sin)
    qa_o[...] = qa.astype(BF16)
    qb_o[...] = qb.astype(BF16)
    ka_o[...] = ka.reshape(ka_o.shape)
    va_o[...] = p[:, C_VA:C_QB].reshape(va_o.shape)
    kb_o[...] = kb.reshape(kb_o.shape)
    vb_o[...] = p[:, C_VB:C_U].reshape(vb_o.shape)

    z = p[:, C_GC:C_GC + CONV_CH] * p[:, C_U:C_GB]
    row = lax.broadcasted_iota(jnp.int32, z.shape, 0) % seq_len
    z_prev = jnp.where(row == 0, 0.0, pltpu.roll(z, 1, axis=0))
    z_next = jnp.where(row == seq_len - 1, 0.0, pltpu.roll(z, tm - 1, axis=0))
    y = z_prev * cw_ref[0:1, :] + z * cw_ref[1:2, :] + z_next * cw_ref[2:3, :]
    oc_o[...] = (p[:, C_GB:C_GC] * y).astype(BF16)


def _inproj(x, mod, group_of_tile, g_mix, w_in_bf, layer, norm_w, ones_bd, conv_w, tm, seq_len, rope_tabs=None,
            caches=None):
    t = x.shape[0]
    rope = rope_tabs is not None
    in_specs = [
        pl.BlockSpec((tm, D_MODEL), lambda i: (i, 0)),
        pl.BlockSpec((None, None, N_MOD, D_MODEL), lambda i: (layer, group_of_tile(i), 0, 0)),
        pl.BlockSpec((None, 1, D_MODEL), lambda i: (layer, 0, 0)),
        pl.BlockSpec((None, D_MODEL, IN_COLS), lambda i: (layer, 0, 0)),
        pl.BlockSpec((None, 1, norm_w.shape[-1]), lambda i: (layer, 0, 0)),
        pl.BlockSpec(ones_bd.shape, lambda i: (0, 0)),
        pl.BlockSpec((None, CONV_W, CONV_CH), lambda i: (layer, 0, 0)),
    ]
    args = [x, mod, g_mix, w_in_bf, norm_w, ones_bd, conv_w]
    if rope:
        in_specs += [pl.BlockSpec(rope_tabs[0].shape, lambda i: (0, 0))] * 2
        args += list(rope_tabs)
    widths = (WIDTH_A, WIDTH_B, KV_WIDTH_A, KV_WIDTH_A, WIDTH_B, WIDTH_B, CONV_CH)
    dtypes = (BF16, BF16, F32, F32, F32, F32, BF16)
    out_shape = [jax.ShapeDtypeStruct((t, w), dt) for w, dt in zip(widths, dtypes)]
    out_specs = [pl.BlockSpec((tm, w), lambda i: (i, 0)) for w in widths]
    aliases = {}
    if caches is not None:
        seqs = tm // seq_len
        for j, cache in enumerate(caches):
            o = 2 + j
            aliases[len(args)] = o
            in_specs.append(pl.BlockSpec(memory_space=pl.ANY))
            args.append(cache)
            out_shape[o] = jax.ShapeDtypeStruct(cache.shape, F32)
            out_specs[o] = pl.BlockSpec((seqs, None, seq_len, widths[o]), lambda i: (i, layer, 0, 0))
    return pl.pallas_call(
        functools.partial(_inproj_kernel, seq_len=seq_len, rope=rope),
        out_shape=out_shape,
        grid=(t // tm,),
        in_specs=in_specs,
        out_specs=out_specs,
        input_output_aliases=aliases,
        compiler_params=_params(("arbitrary",), VMEM_LIMIT_V7X),
        name="inproj_rope" if rope else "inproj",
    )(*args)


def _pair_attention(q2, k_lo, k_hi, v_lo, v_hi, sinks=None, adjust=None):
    m = k_lo.shape[0]
    k2 = jnp.concatenate([k_lo, k_hi], axis=0)
    v2 = jnp.concatenate([v_lo, v_hi], axis=0)
    s = lax.dot_general(q2, k2, _NT, preferred_element_type=F32)
    ps, dens = [], []
    for i in range(2):
        si = s[:, i * m:(i + 1) * m]
        if adjust is not None:
            si = adjust(i, si)
        mx = jnp.max(si, axis=-1, keepdims=True)
        if sinks is not None:
            mx = jnp.maximum(mx, sinks[i])
        p = jnp.exp(si - mx)
        den = jnp.sum(p, axis=-1, keepdims=True)
        if sinks is not None:
            den = den + jnp.exp(sinks[i] - mx)
        ps.append(p.astype(BF16))
        dens.append(den)
    o = jnp.dot(jnp.concatenate(ps, axis=-1), v2, preferred_element_type=F32)
    lane = lax.broadcasted_iota(jnp.int32, o.shape, 1)
    return o / jnp.where(lane < HEAD_DIM, dens[0], dens[1])


def _lane_halves(x):
    lane = lax.broadcasted_iota(jnp.int32, x.shape, 1)
    lo = lane < HEAD_DIM
    sw = pltpu.roll(x, HEAD_DIM, axis=1)
    z = jnp.zeros_like(x)
    return tuple(jnp.where(c, y, z).astype(BF16) for c, y in ((lo, x), (~lo, x), (lo, sw), (~lo, sw)))


def _ctx_attn_kernel(sink_ref, qa_ref, ka_ref, va_ref, qb_ref, kb_ref, vb_ref, oa_o, ob_o):
    scale = HEAD_DIM ** -0.5
    k0l, k1h, k1l, k0h = _lane_halves(ka_ref[...])
    v0l, v1h, v1l, v0h = _lane_halves(va_ref[...])
    for i in range(N_HEADS_A // 2):
        cols = slice(2 * i * HEAD_DIM, (2 * i + 2) * HEAD_DIM)
        kv = (k0l, k0h, v0l, v0h) if (2 * i) // GROUP_A == 0 else (k1l, k1h, v1l, v1h)
        o = _pair_attention(qa_ref[:, cols] * scale, *kv, sinks=(sink_ref[2 * i], sink_ref[2 * i + 1]))
        oa_o[:, cols] = o.astype(oa_o.dtype)
    for i in range(N_HEADS_B // 2):
        cols = slice(2 * i * HEAD_DIM, (2 * i + 2) * HEAD_DIM)
        kl, kh, _, _ = _lane_halves(kb_ref[:, cols])
        vl, vh, _, _ = _lane_halves(vb_ref[:, cols])
        o = _pair_attention(qb_ref[:, cols] * scale, kl, kh, vl, vh)
        ob_o[:, cols] = o.astype(ob_o.dtype)


def _ctx_attention(sink, qa, ka, va, qb, kb, vb, seq_len, layer):
    t = qa.shape[0]
    q_spec = lambda w: pl.BlockSpec((seq_len, w), lambda i: (i, 0))
    kv_spec = lambda w: pl.BlockSpec((None, None, seq_len, w), lambda i: (i, layer, 0, 0))
    return pl.pallas_call(
        _ctx_attn_kernel,
        out_shape=[jax.ShapeDtypeStruct((t, WIDTH_A), BF16), jax.ShapeDtypeStruct((t, WIDTH_B), BF16)],
        grid=(t // seq_len,),
        in_specs=[pl.BlockSpec(memory_space=pltpu.SMEM), q_spec(WIDTH_A), kv_spec(KV_WIDTH_A), kv_spec(KV_WIDTH_A),
                  q_spec(WIDTH_B), kv_spec(WIDTH_B), kv_spec(WIDTH_B)],
        out_specs=[pl.BlockSpec((seq_len, WIDTH_A), lambda i: (i, 0)),
                   pl.BlockSpec((seq_len, WIDTH_B), lambda i: (i, 0))],
        compiler_params=_params(("arbitrary",)),
        name="ctx_attention",
    )(sink, qa, ka, va, qb, kb, vb)


def _win_attn_kernel(sink_ref, qa_ref, ka_ref, va_ref, ck_ref, cv_ref, oa_o, *, n_lat):
    scale = HEAD_DIM ** -0.5
    span = Q_BLK + 2 * WINDOW
    n = pl.program_id(1)
    start = pl.multiple_of(jnp.clip(n * Q_BLK - WINDOW, 0, n_lat - span), Q_BLK)
    k0l, k1h, k1l, k0h = _lane_halves(jnp.concatenate([ka_ref[pl.ds(start, span), :], ck_ref[...]], axis=0))
    v0l, v1h, v1l, v0h = _lane_halves(jnp.concatenate([va_ref[pl.ds(start, span), :], cv_ref[...]], axis=0))
    n_keys = k0l.shape[0]
    qpos = n * Q_BLK + lax.broadcasted_iota(jnp.int32, (Q_BLK, n_keys), 0)
    col = lax.broadcasted_iota(jnp.int32, (Q_BLK, n_keys), 1)
    ok = (col >= span) | (jnp.abs(qpos - (start + col)) <= WINDOW)
    for i in range(N_HEADS_A // 2):
        cols = slice(2 * i * HEAD_DIM, (2 * i + 2) * HEAD_DIM)
        kv = (k0l, k0h, v0l, v0h) if (2 * i) // GROUP_A == 0 else (k1l, k1h, v1l, v1h)
        o = _pair_attention(qa_ref[:, cols] * scale, *kv, sinks=(sink_ref[2 * i], sink_ref[2 * i + 1]),
                            adjust=lambda _, s: jnp.where(ok, s, NEG))
        oa_o[:, cols] = o.astype(oa_o.dtype)


def _win_attention(sink, qa, ka, va, cache_k, cache_v, layer, n_lat):
    t = qa.shape[0]
    nb = n_lat // Q_BLK
    past = cache_k.shape[2]
    return pl.pallas_call(
        functools.partial(_win_attn_kernel, n_lat=n_lat),
        out_shape=jax.ShapeDtypeStruct((t, WIDTH_A), BF16),
        grid=(t // n_lat, nb),
        in_specs=[
            pl.BlockSpec(memory_space=pltpu.SMEM),
            pl.BlockSpec((Q_BLK, WIDTH_A), lambda b, n: (b * nb + n, 0)),
            pl.BlockSpec((n_lat, KV_WIDTH_A), lambda b, n: (b, 0)),
            pl.BlockSpec((n_lat, KV_WIDTH_A), lambda b, n: (b, 0)),
            pl.BlockSpec((None, None, past, KV_WIDTH_A), lambda b, n: (b, layer, 0, 0)),
            pl.BlockSpec((None, None, past, KV_WIDTH_A), lambda b, n: (b, layer, 0, 0)),
        ],
        out_specs=pl.BlockSpec((Q_BLK, WIDTH_A), lambda b, n: (b * nb + n, 0)),
        compiler_params=_params(("arbitrary", "arbitrary")),
        name="window_attention",
    )(sink, qa, ka, va, cache_k, cache_v)


def _nbr_attn_kernel(qb_ref, kb_ref, vb_ref, ck_ref, cv_ref, bias_ref, ob_o, *, rows, kr):
    scale = HEAD_DIM ** -0.5
    r = pl.program_id(1)
    start = pl.multiple_of(jnp.clip(r - kr // 2, 0, rows - kr) * GRID_W, GRID_W)
    nwin = kr * GRID_W
    kcat = jnp.concatenate([kb_ref[pl.ds(start, nwin), :], ck_ref[...]], axis=0)
    vcat = jnp.concatenate([vb_ref[pl.ds(start, nwin), :], cv_ref[...]], axis=0)
    past = ck_ref.shape[0]
    for i in range(N_HEADS_B // 2):
        cols = slice(2 * i * HEAD_DIM, (2 * i + 2) * HEAD_DIM)
        kl, kh, _, _ = _lane_halves(kcat[:, cols])
        vl, vh, _, _ = _lane_halves(vcat[:, cols])
        o = _pair_attention(
            qb_ref[:, cols] * scale, kl, kh, vl, vh,
            adjust=lambda hh, s, i=i: s + jnp.concatenate([bias_ref[2 * i + hh], jnp.zeros((GRID_W, past), F32)],
                                                          axis=-1))
        ob_o[:, cols] = o.astype(ob_o.dtype)


def _nbr_bias_table(rel_bias, rows):
    kr = min(NA_ROWS, rows)
    cls = np.arange(kr)[:, None]
    m = np.arange(kr)[None, :]
    row_sel = (m - cls + NA_ROWS - 1)[:, :, None] == np.arange(2 * NA_ROWS - 1)[None, None, :]
    c = np.arange(GRID_W)[:, None]
    kc = np.arange(GRID_W)[None, :]
    dc = np.clip(kc - c, -(NA_COLS - 1), NA_COLS - 1) + NA_COLS - 1
    col_sel = dc[:, :, None] == np.arange(2 * NA_COLS - 1)[None, None, :]
    col_start = np.clip(c - NA_COLS // 2, 0, GRID_W - NA_COLS)
    ok = (kc >= col_start) & (kc < col_start + NA_COLS)
    tab = jnp.einsum('hrd,ymr,ckd->hycmk', rel_bias.astype(F32), row_sel.astype(np.float32),
                     col_sel.astype(np.float32), precision=lax.Precision.HIGHEST)
    tab = jnp.where(ok[None, None, :, None, :], tab, NEG)
    return tab.reshape(rel_bias.shape[0], kr, GRID_W, kr * GRID_W)


def _nbr_attention(qb, kb, vb, cache_k, cache_v, bias_tab, layer, n_lat):
    t = qb.shape[0]
    rows = n_lat // GRID_W
    kr = min(NA_ROWS, rows)
    past = cache_k.shape[2]

    def bias_map(b, r):
        return (0, r - jnp.clip(r - kr // 2, 0, rows - kr), 0, 0)

    return pl.pallas_call(
        functools.partial(_nbr_attn_kernel, rows=rows, kr=kr),
        out_shape=jax.ShapeDtypeStruct((t, WIDTH_B), BF16),
        grid=(t // n_lat, rows),
        in_specs=[
            pl.BlockSpec((GRID_W, WIDTH_B), lambda b, r: (b * rows + r, 0)),
            pl.BlockSpec((n_lat, WIDTH_B), lambda b, r: (b, 0)),
            pl.BlockSpec((n_lat, WIDTH_B), lambda b, r: (b, 0)),
            pl.BlockSpec((None, None, past, WIDTH_B), lambda b, r: (b, layer, 0, 0)),
            pl.BlockSpec((None, None, past, WIDTH_B), lambda b, r: (b, layer, 0, 0)),
            pl.BlockSpec((N_HEADS_B, None, GRID_W, kr * GRID_W), bias_map),
        ],
        out_specs=pl.BlockSpec((GRID_W, WIDTH_B), lambda b, r: (b * rows + r, 0)),
        compiler_params=_params(("arbitrary", "arbitrary")),
        name="neighbourhood_attention",
    )(qb, kb, vb, cache_k, cache_v, bias_tab)


ROW_TILE = 8
LANES = 128
assert ROW_TILE * LANES == D_MODEL


PACK_TILE = ROW_TILE // 2


def _store_row_tiles(ref, row0, x):
    n = x.shape[0]
    rt = x.shape[1] // LANES
    for c in range(rt):
        ref[pl.ds(row0 * rt + c, n, stride=rt), :] = x[:, c * LANES:(c + 1) * LANES]


def _pack_bf16_pairs(x):
    bits = lax.bitcast_convert_type(x, jnp.uint32)
    half = x.shape[1] // 2
    return (bits[:, half:] & jnp.uint32(0xFFFF0000)) | (bits[:, :half] >> 16)


def _unpack_bf16_pairs(w):
    lo = lax.bitcast_convert_type(w << 16, F32)
    hi = lax.bitcast_convert_type(w & jnp.uint32(0xFFFF0000), F32)
    return jnp.concatenate([lo, hi], axis=-1).astype(BF16)


def _load_row_tiles(ref, row0, n, rt=ROW_TILE):
    return jnp.concatenate([ref[pl.ds(row0 * rt + c, n, stride=rt), :] for c in range(rt)], axis=-1)


def _outproj_kernel(x_ref, oa_ref, ob_ref, oc_ref, w_ref, mod_ref, g_ref, wr_ref, br_ref, *rest):
    xn_o, h_o, lg_o = rest[-3:]
    mix = jnp.dot(oa_ref[...], w_ref[0:WIDTH_A, :], preferred_element_type=F32)
    mix += jnp.dot(ob_ref[...], w_ref[WIDTH_A:WIDTH_A + WIDTH_B, :], preferred_element_type=F32)
    mix += jnp.dot(oc_ref[...], w_ref[WIDTH_A + WIDTH_B:, :], preferred_element_type=F32)
    xn = x_ref[...] + mod_ref[2:3, :] * mix
    xn_o[...] = xn
    h = xn * lax.rsqrt(jnp.mean(xn * xn, axis=-1, keepdims=True) + EPS) * g_ref[...]
    h = h * (1.0 + mod_ref[4:5, :]) + mod_ref[3:4, :]
    h_hi = h.astype(BF16)
    h_o[...] = h_hi
    h_lo = (h - h_hi.astype(F32)).astype(BF16)
    wr = wr_ref[...]
    w_hi = wr.astype(BF16)
    w_lo = (wr - w_hi.astype(F32)).astype(BF16)
    n_e = wr.shape[0]
    both = lax.dot_general(jnp.concatenate([w_hi, w_lo], axis=0), h_hi, _NT, preferred_element_type=F32)
    lg = both[:n_e] + both[n_e:] + lax.dot_general(w_hi, h_lo, _NT, preferred_element_type=F32)
    lg_o[...] = lg + br_ref[...]


def _outproj(x, oa, ob, oc, w_out_bf, mod, group_of_tile, g_ffn, w_router_t, b_router_t, layer, tm, t_all, tok0,
             shared):
    t = x.shape[0]
    b0 = tok0 // tm
    assert tok0 % tm == 0
    in_specs = [
        pl.BlockSpec((tm, D_MODEL), lambda i: (i, 0)),
        pl.BlockSpec((tm, WIDTH_A), lambda i: (i, 0)),
        pl.BlockSpec((tm, WIDTH_B), lambda i: (i, 0)),
        pl.BlockSpec((tm, CONV_CH), lambda i: (i, 0)),
        pl.BlockSpec((None, D_MODEL, D_MODEL), lambda i: (layer, 0, 0)),
        pl.BlockSpec((None, None, N_MOD, D_MODEL), lambda i: (layer, group_of_tile(i), 0, 0)),
        pl.BlockSpec((None, 1, D_MODEL), lambda i: (layer, 0, 0)),
        pl.BlockSpec((None, N_EXPERTS, D_MODEL), lambda i: (layer, 0, 0)),
        pl.BlockSpec((None, N_EXPERTS, 1), lambda i: (layer, 0, 0)),
    ]
    args = [x, oa, ob, oc, w_out_bf, mod, g_ffn, w_router_t, b_router_t]
    aliases = {len(args): 1, len(args) + 1: 2}
    in_specs += [pl.BlockSpec(memory_space=pl.ANY)] * 2
    args += list(shared)
    return pl.pallas_call(
        _outproj_kernel,
        out_shape=[jax.ShapeDtypeStruct((t, D_MODEL), F32), jax.ShapeDtypeStruct((t_all, D_MODEL), BF16),
                   jax.ShapeDtypeStruct((N_EXPERTS, t_all), F32)],
        grid=(t // tm,),
        in_specs=in_specs,
        out_specs=[pl.BlockSpec((tm, D_MODEL), lambda i: (i, 0)),
                   pl.BlockSpec((tm, D_MODEL), lambda i: (b0 + i, 0)),
                   pl.BlockSpec((N_EXPERTS, tm), lambda i: (0, b0 + i))],
        input_output_aliases=aliases,
        compiler_params=_params(("arbitrary",), VMEM_LIMIT_V7X),
        name="outproj_router",
    )(*args)


X_AHEAD = 3
X_SLOTS = X_AHEAD + 1
Y_SLOTS = 2


def _expert_kernel(seg_ref, n_used_ref, x_hbm, wg_ref, bg_ref, wu_ref, bu_ref, wd_ref, bd_ref, y_hbm,
                   x_buf, y_buf, xsem, ysem, zsem, *, n_blocks):
    e = pl.program_id(0)
    n_used = n_used_ref[0]
    blk_rows = MOE_BLK * ROW_TILE
    first_blk = seg_ref[SEG_DST, e, SEG_BLK]
    n_blk = seg_ref[SEG_CNT, e, SEG_BLK]

    def rows_of(g, rows=blk_rows):
        return pl.ds(pl.multiple_of(g * rows, rows), rows)

    def x_copy(g):
        return pltpu.make_async_copy(x_hbm.at[rows_of(g, MOE_BLK * PACK_TILE)], x_buf.at[g % X_SLOTS],
                                     xsem.at[g % X_SLOTS])

    def y_copy(g):
        return pltpu.make_async_copy(y_buf.at[g % Y_SLOTS], y_hbm.at[rows_of(g)], ysem.at[g % Y_SLOTS])

    def zero_copy(g):
        return pltpu.make_async_copy(y_buf.at[0], y_hbm.at[rows_of(g)], zsem)

    @pl.when(e == 0)
    def _():
        for g in range(X_AHEAD):
            @pl.when(g < n_used)
            def _():
                x_copy(g).start()

    def block(b, carry):
        g = first_blk + b

        @pl.when(g + X_AHEAD < n_used)
        def _():
            x_copy(g + X_AHEAD).start()

        x_copy(g).wait()

        @pl.when(g >= Y_SLOTS)
        def _():
            y_copy(g - Y_SLOTS).wait()

        x = _unpack_bf16_pairs(_load_row_tiles(x_buf.at[g % X_SLOTS], 0, MOE_BLK, PACK_TILE))
        gt = jnp.dot(x, wg_ref[...].astype(BF16), preferred_element_type=F32) + bg_ref[...]
        up = jnp.dot(x, wu_ref[...].astype(BF16), preferred_element_type=F32) + bu_ref[...]
        gt = jnp.minimum(gt, SWIGLU_LIMIT)
        up = jnp.clip(up, -SWIGLU_LIMIT, SWIGLU_LIMIT)
        a = gt * jax.nn.sigmoid(SWIGLU_ALPHA * gt) * (up + 1.0)
        y = jnp.dot(a.astype(BF16), wd_ref[...].astype(BF16), preferred_element_type=F32) + bd_ref[...]
        _store_row_tiles(y_buf.at[g % Y_SLOTS], 0, y)
        y_copy(g).start()
        return carry

    lax.fori_loop(0, n_blk, block, 0)

    @pl.when(e == pl.num_programs(0) - 1)
    def _():
        for d in range(Y_SLOTS, 0, -1):
            @pl.when(n_used >= d)
            def _():
                y_copy(n_used - d).wait()

        y_buf[0] = jnp.zeros((blk_rows, LANES), F32)

        def start_zero(g, carry):
            zero_copy(g).start()
            return carry

        def wait_zero(g, carry):
            zero_copy(g).wait()
            return carry

        lax.fori_loop(n_used, n_blocks, start_zero, 0)
        lax.fori_loop(n_used, n_blocks, wait_zero, 0)


def _experts(seg, n_used, x_sorted, w_gate, b_gate, w_up, b_up, w_down, b_down, layer):
    n_blocks = x_sorted.shape[0] // (MOE_BLK * PACK_TILE)
    wspec = pl.BlockSpec((None, None, D_MODEL, D_MODEL), lambda e, sg, nu: (layer, e, 0, 0))
    bspec = pl.BlockSpec((None, None, 1, D_MODEL), lambda e, sg, nu: (layer, e, 0, 0))
    anyspec = pl.BlockSpec(memory_space=pl.ANY)
    depth = w_gate.shape[0]
    b4 = lambda b: b.reshape(depth, N_EXPERTS, 1, D_MODEL)
    return pl.pallas_call(
        functools.partial(_expert_kernel, n_blocks=n_blocks),
        out_shape=jax.ShapeDtypeStruct((n_blocks * MOE_BLK * ROW_TILE, LANES), F32),
        grid_spec=pltpu.PrefetchScalarGridSpec(
            num_scalar_prefetch=2,
            grid=(N_EXPERTS,),
            in_specs=[anyspec, wspec, bspec, wspec, bspec, wspec, bspec],
            out_specs=anyspec,
            scratch_shapes=[
                pltpu.VMEM((X_SLOTS, MOE_BLK * PACK_TILE, LANES), jnp.uint32),
                pltpu.VMEM((Y_SLOTS, MOE_BLK * ROW_TILE, LANES), F32),
                pltpu.SemaphoreType.DMA((X_SLOTS,)),
                pltpu.SemaphoreType.DMA((Y_SLOTS,)),
                pltpu.SemaphoreType.DMA,
            ],
        ),
        compiler_params=_params(("arbitrary",), VMEM_LIMIT_V7X),
        name="experts",
    )(seg, n_used, x_sorted, w_gate, b4(b_gate), w_up, b4(b_up), w_down, b4(b_down))


CUM_CHUNK = 256


def _sublane_cumsum(x):
    row = lax.broadcasted_iota(jnp.int32, x.shape, 0)
    d = 1
    while d < x.shape[0]:
        x = x + jnp.where(row >= d, pltpu.roll(x, d, axis=0), 0)
        d *= 2
    return x


TOK_TILE = CUM_CHUNK
TILE_ROWS = TOP_K * TOK_TILE
SEG_LANES = 128
SEG_PAD_FROM = SEG_LANES - 2
SEG_PAD_LEN = SEG_LANES - 1
SEG_BLK = SEG_LANES - 3
SEG_SRC, SEG_CNT, SEG_DST = 0, 1, 2


def _route_kernel(lg_ref, tri_ref, gate_o, lpos_o, seg_o, used_o):
    lg = lg_ref[...]
    n_e, t = lg.shape
    e_iota = lax.broadcasted_iota(jnp.int32, lg.shape, 0)
    work = lg
    tops, hots = [], []
    for _ in range(TOP_K):
        m = jnp.max(work, axis=0, keepdims=True)
        first = jnp.min(jnp.where(work == m, e_iota, n_e), axis=0, keepdims=True)
        hot = e_iota == first
        work = jnp.where(hot, -jnp.inf, work)
        tops.append(m)
        hots.append(hot)
    ex = [jnp.exp(m - tops[0]) for m in tops]
    den = ex[0] + ex[1] + ex[2] + ex[3]
    for k in range(TOP_K):
        gate_o[k:k + 1, :] = ex[k] / den

    chosen = jnp.where(hots[0] | hots[1] | hots[2] | hots[3], 1.0, 0.0)
    tri = tri_ref[...]
    lane = lax.broadcasted_iota(jnp.int32, (n_e, SEG_LANES), 1)
    seg_cnt = jnp.zeros((n_e, SEG_LANES), jnp.int32)
    seg_before = jnp.zeros((n_e, SEG_LANES), jnp.int32)
    carry = jnp.zeros((n_e, 1), F32)
    rank_in_tile = []
    n_tiles = t // TOK_TILE
    for c in range(n_tiles):
        chunk = chosen[:, c * TOK_TILE:(c + 1) * TOK_TILE]
        inc = jnp.dot(chunk.astype(BF16), tri, preferred_element_type=F32)
        cnt = inc[:, TOK_TILE - 1:TOK_TILE]
        rank_in_tile.append((inc - chunk).astype(jnp.int32))
        seg_cnt = jnp.where(lane == c, cnt.astype(jnp.int32), seg_cnt)
        seg_before = jnp.where(lane == c, carry.astype(jnp.int32), seg_before)
        carry = carry + cnt

    counts = jnp.broadcast_to(carry.astype(jnp.int32), (n_e, SEG_LANES))
    padded = ((counts + (MOE_BLK - 1)) >> MOE_SHIFT) << MOE_SHIFT
    pad_end = _sublane_cumsum(padded)
    pad_start = pad_end - padded
    seg_src = _sublane_cumsum(seg_cnt) - seg_cnt
    seg_o[SEG_SRC] = seg_src
    seg_o[SEG_CNT] = jnp.where(lane == SEG_PAD_LEN, padded - counts,
                               jnp.where(lane == SEG_BLK, padded >> MOE_SHIFT, seg_cnt))
    seg_o[SEG_DST] = jnp.where(lane == SEG_PAD_FROM, pad_start + counts,
                               jnp.where(lane == SEG_BLK, pad_start >> MOE_SHIFT, pad_start + seg_before))

    for c in range(n_tiles):
        pos = seg_src[:, c:c + 1] + rank_in_tile[c]
        for k in range(TOP_K):
            lpos_o[k:k + 1, c * TOK_TILE:(c + 1) * TOK_TILE] = jnp.sum(
                jnp.where(hots[k][:, c * TOK_TILE:(c + 1) * TOK_TILE], pos, 0), axis=0, keepdims=True)

    used_o[...] = jnp.broadcast_to(pad_end[n_e - 1:n_e, :] >> MOE_SHIFT, used_o.shape)


def _route(logits):
    t = logits.shape[1]
    assert t % TOK_TILE == 0 and t // TOK_TILE <= SEG_BLK
    tri = jnp.asarray(np.triu(np.ones((CUM_CHUNK, CUM_CHUNK), np.float32)), dtype=BF16)
    gates, lpos, seg, used = pl.pallas_call(
        _route_kernel,
        out_shape=[jax.ShapeDtypeStruct((TOP_K, t), F32), jax.ShapeDtypeStruct((TOP_K, t), jnp.int32),
                   jax.ShapeDtypeStruct((3, N_EXPERTS, SEG_LANES), jnp.int32),
                   jax.ShapeDtypeStruct((8, SEG_LANES), jnp.int32)],
        compiler_params=_params(None, VMEM_LIMIT_V7X),
        name="route",
    )(logits, tri)
    return gates, lpos, seg, used[0, :1]


def _segment_copies(src, dst, src0, dst0, cnt, sem, wait, rt=ROW_TILE):
    @pl.when(cnt > 0)
    def _():
        size = cnt * rt
        cp = pltpu.make_async_copy(
            src.at[pl.ds(pl.multiple_of(src0 * rt, rt), size)],
            dst.at[pl.ds(pl.multiple_of(dst0 * rt, rt), size)], sem)
        if wait:
            cp.wait()
        else:
            cp.start()


def _dispatch_kernel(seg_ref, nu_ref, h_ref, lpos_ref, x_hbm, z_buf, zero_buf, sem, zsem, *, n_tiles, n_blocks):
    c = pl.program_id(0)
    slot = c % 2

    def wait_tile(s):
        pltpu.make_async_copy(z_buf.at[s], x_hbm.at[pl.ds(0, TILE_ROWS * PACK_TILE)], sem.at[s]).wait()

    def padding(wait):
        def per_expert(e, carry):
            _segment_copies(zero_buf, x_hbm, 0, seg_ref[SEG_DST, e, SEG_PAD_FROM], seg_ref[SEG_CNT, e, SEG_PAD_LEN],
                            zsem, wait, PACK_TILE)
            return carry

        lax.fori_loop(0, N_EXPERTS, per_expert, 0)

        def per_block(b, carry):
            cp = pltpu.make_async_copy(
                zero_buf, x_hbm.at[pl.ds(pl.multiple_of(b * (MOE_BLK * PACK_TILE), MOE_BLK * PACK_TILE),
                                         MOE_BLK * PACK_TILE)], zsem)
            if wait:
                cp.wait()
            else:
                cp.start()
            return carry

        lax.fori_loop(nu_ref[0], n_blocks, per_block, 0)

    @pl.when(c == 0)
    def _():
        zero_buf[...] = jnp.zeros(zero_buf.shape, jnp.uint32)
        padding(False)

    @pl.when(c >= 2)
    def _():
        wait_tile(slot)

    row = lax.broadcasted_iota(jnp.int32, (TILE_ROWS, TOK_TILE), 0)
    hit = row == lpos_ref[0:1, :]
    for k in range(1, TOP_K):
        hit = hit | (row == lpos_ref[k:k + 1, :])
    z = jnp.dot(jnp.where(hit, 1.0, 0.0).astype(BF16), h_ref[...], preferred_element_type=F32)
    _store_row_tiles(z_buf.at[slot], 0, _pack_bf16_pairs(z))

    def per_expert(e, carry):
        _segment_copies(z_buf.at[slot], x_hbm, seg_ref[SEG_SRC, e, c], seg_ref[SEG_DST, e, c], seg_ref[SEG_CNT, e, c],
                        sem.at[slot], False, PACK_TILE)
        return carry

    lax.fori_loop(0, N_EXPERTS, per_expert, 0)

    @pl.when(c == n_tiles - 1)
    def _():
        if n_tiles > 1:
            wait_tile(1 - slot)
        wait_tile(slot)
        padding(True)


def _dispatch(seg, n_used, h_all, lpos, n_blocks):
    t = h_all.shape[0]
    n_tiles = t // TOK_TILE
    return pl.pallas_call(
        functools.partial(_dispatch_kernel, n_tiles=n_tiles, n_blocks=n_blocks),
        out_shape=jax.ShapeDtypeStruct((n_blocks * MOE_BLK * PACK_TILE, LANES), jnp.uint32),
        grid_spec=pltpu.PrefetchScalarGridSpec(
            num_scalar_prefetch=2,
            grid=(n_tiles,),
            in_specs=[pl.BlockSpec((TOK_TILE, D_MODEL), lambda c, sg, nu: (c, 0)),
                      pl.BlockSpec((TOP_K, TOK_TILE), lambda c, sg, nu: (0, c))],
            out_specs=pl.BlockSpec(memory_space=pl.ANY),
            scratch_shapes=[
                pltpu.VMEM((2, TILE_ROWS * PACK_TILE, LANES), jnp.uint32),
                pltpu.VMEM((MOE_BLK * PACK_TILE, LANES), jnp.uint32),
                pltpu.SemaphoreType.DMA((2,)),
                pltpu.SemaphoreType.DMA,
            ],
        ),
        compiler_params=_params(("arbitrary",), VMEM_LIMIT_V7X),
        name="dispatch",
    )(seg, n_used, h_all, lpos)


def _combine_kernel(seg_ref, x_ref, y_hbm, lpos_ref, lpos_t_ref, gate_ref, mod_ref, o_ref, y_buf, sem, *, tile0,
                    n_tiles):
    c = pl.program_id(0)
    slot = c % 2

    def fetch(cc):
        def per_expert(e, carry):
            _segment_copies(y_hbm, y_buf.at[cc % 2], seg_ref[SEG_DST, e, tile0 + cc], seg_ref[SEG_SRC, e, tile0 + cc],
                            seg_ref[SEG_CNT, e, tile0 + cc], sem.at[cc % 2], False)
            return carry

        lax.fori_loop(0, N_EXPERTS, per_expert, 0)

    @pl.when(c == 0)
    def _():
        fetch(0)

    @pl.when(c + 1 < n_tiles)
    def _():
        fetch(c + 1)

    pltpu.make_async_copy(y_hbm.at[pl.ds(0, TILE_ROWS * ROW_TILE)], y_buf.at[slot], sem.at[slot]).wait()

    row = lax.broadcasted_iota(jnp.int32, (TILE_ROWS, TOK_TILE), 0)
    row_gate = jnp.zeros((TILE_ROWS, 1), F32)
    for k in range(TOP_K):
        row_gate += jnp.sum(jnp.where(row == lpos_ref[k:k + 1, :], gate_ref[k:k + 1, :], 0.0), axis=1, keepdims=True)
    ys = _load_row_tiles(y_buf.at[slot], 0, TILE_ROWS) * row_gate
    ys_hi = ys.astype(BF16)
    ys_lo = (ys - ys_hi.astype(F32)).astype(BF16)
    col = lax.broadcasted_iota(jnp.int32, (TOK_TILE, TILE_ROWS), 1)
    hit = col == lpos_t_ref[:, 0:1]
    for k in range(1, TOP_K):
        hit = hit | (col == lpos_t_ref[:, k:k + 1])
    u = jnp.where(hit, 1.0, 0.0).astype(BF16)
    mix = jnp.dot(u, ys_hi, preferred_element_type=F32) + jnp.dot(u, ys_lo, preferred_element_type=F32)
    o_ref[...] = x_ref[...] + mod_ref[5:6, :] * mix


def _combine(seg, x_mid, y_sorted, lpos, lpos_t, gates, mod, group_of_tile, layer, tok0):
    t = x_mid.shape[0]
    tile0 = tok0 // TOK_TILE
    n_tiles = t // TOK_TILE
    return pl.pallas_call(
        functools.partial(_combine_kernel, tile0=tile0, n_tiles=n_tiles),
        out_shape=jax.ShapeDtypeStruct((t, D_MODEL), F32),
        grid_spec=pltpu.PrefetchScalarGridSpec(
            num_scalar_prefetch=1,
            grid=(n_tiles,),
            in_specs=[
                pl.BlockSpec((TOK_TILE, D_MODEL), lambda c, sg: (c, 0)),
                pl.BlockSpec(memory_space=pl.ANY),
                pl.BlockSpec((TOP_K, TOK_TILE), lambda c, sg: (0, tile0 + c)),
                pl.BlockSpec((TOK_TILE, TOP_K), lambda c, sg: (tile0 + c, 0)),
                pl.BlockSpec((TOP_K, TOK_TILE), lambda c, sg: (0, tile0 + c)),
                pl.BlockSpec((None, None, N_MOD, D_MODEL), lambda c, sg: (layer, group_of_tile(c), 0, 0)),
            ],
            out_specs=pl.BlockSpec((TOK_TILE, D_MODEL), lambda c, sg: (c, 0)),
            scratch_shapes=[pltpu.VMEM((2, TILE_ROWS * ROW_TILE, LANES), F32), pltpu.SemaphoreType.DMA((2,))],
        ),
        compiler_params=_params(("arbitrary",), VMEM_LIMIT_V7X),
        name="combine",
    )(seg, x_mid, y_sorted, lpos, lpos_t, gates, mod)


def _rope_tables(n_lat):
    quarter = HEAD_DIM // 4
    t = jnp.arange(n_lat)
    inv = ROPE_BASE ** (-jnp.arange(quarter, dtype=F32) / quarter)
    ang_r = (t // GRID_W).astype(F32)[:, None] * inv
    ang_c = (t % GRID_W).astype(F32)[:, None] * inv
    cos = jnp.concatenate([jnp.cos(ang_r)] * 2 + [jnp.cos(ang_c)] * 2, axis=-1)
    sin = jnp.concatenate([-jnp.sin(ang_r), jnp.sin(ang_r), -jnp.sin(ang_c), jnp.sin(ang_c)], axis=-1)
    return jnp.concatenate([cos, cos], axis=-1), jnp.concatenate([sin, sin], axis=-1)


def _block_diag_ones():
    idx = np.arange(MXU_COLS_V7X) // HEAD_DIM
    return jnp.asarray(idx[:, None] == idx[None, :], dtype=BF16)


def kernel(x_prompt, x_sample, cache_k_win, cache_v_win, cache_k_nbr, cache_v_nbr, c, c_ctx, w_mod, b_mod, g_mix, g_ffn, w_in, w_out, qn_win, kn_win, qn_nbr, kn_nbr, sink_win, rel_bias_nbr, conv_w, w_router, b_router, w_gate, b_gate, w_up, b_up, w_down, b_down):
    bsz, n_ctx, d = x_prompt.shape
    dbs, n_lat, _ = x_sample.shape
    depth = w_in.shape[0]
    past = cache_k_win.shape[2]
    assert d == D_MODEL and dbs + 1 <= COND_ROWS and n_lat % GRID_W == 0 and n_lat >= Q_BLK + 2 * WINDOW
    t_ctx, t_lat = bsz * n_ctx, dbs * n_lat

    cond = jnp.concatenate([c_ctx[None], c, jnp.zeros((COND_ROWS - 1 - dbs, d), F32)], axis=0)
    mod = _modulation(cond, w_mod, b_mod)

    w_in_bf = w_in.astype(BF16)
    w_out_bf = w_out.astype(BF16)
    norm_w = jnp.concatenate([jnp.tile(qn_win, (1, N_HEADS_A)), jnp.tile(kn_win, (1, N_KV_A)),
                              jnp.tile(qn_nbr, (1, N_HEADS_B)), jnp.tile(kn_nbr, (1, N_HEADS_B))], axis=-1)[:, None, :]
    ones_bd = _block_diag_ones()
    rope_tabs = _rope_tables(n_lat)
    g_mix3, g_ffn3 = g_mix[:, None, :], g_ffn[:, None, :]
    w_router_t = jnp.swapaxes(w_router, 1, 2)
    b_router_t = b_router[:, :, None]
    t_all = t_ctx + t_lat
    ck_win = cache_k_win.reshape(dbs, depth, past, KV_WIDTH_A)
    cv_win = cache_v_win.reshape(dbs, depth, past, KV_WIDTH_A)
    ck_nbr = cache_k_nbr.reshape(dbs, depth, past, WIDTH_B)
    cv_nbr = cache_v_nbr.reshape(dbs, depth, past, WIDTH_B)

    tm_ctx = 2 * n_ctx
    tm_lat = 512
    ctx_group = lambda i: 0
    lat_group_in = lambda i: 1 + i
    lat_group_out = lambda i: 1 + (i * tm_lat) // n_lat
    lat_group_comb = lambda i: 1 + (i * TOK_TILE) // n_lat
    assert t_ctx % TOK_TILE == 0 and n_lat % TOK_TILE == 0

    xp = x_prompt.reshape(t_ctx, d)
    xs = x_sample.reshape(t_lat, d)
    caches = tuple(jnp.zeros((bsz, depth, n_ctx, w), F32) for w in (KV_WIDTH_A, KV_WIDTH_A, WIDTH_B, WIDTH_B))
    for l in range(depth):
        qa, qb, *caches, oc = _inproj(xp, mod, ctx_group, g_mix3, w_in_bf, l, norm_w, ones_bd, conv_w,
                                      tm_ctx, n_ctx, caches=caches)
        oa, ob = _ctx_attention(sink_win[l], qa, caches[0], caches[1], qb, caches[2], caches[3], n_ctx, l)
        shared = (jnp.zeros((t_all, d), BF16), jnp.zeros((N_EXPERTS, t_all), F32)) if l == 0 else (h_all, lg_all)
        xp_mid, h_all, lg_all = _outproj(xp, oa, ob, oc, w_out_bf, mod, ctx_group, g_ffn3, w_router_t, b_router_t, l,
                                         tm_ctx, t_all, 0, shared)

        qa, qb, ka, va, kb, vb, oc = _inproj(xs, mod, lat_group_in, g_mix3, w_in_bf, l, norm_w, ones_bd, conv_w,
                                             n_lat, n_lat, rope_tabs)
        oa = _win_attention(sink_win[l], qa, ka, va, ck_win, cv_win, l, n_lat)
        ob = _nbr_attention(qb, kb, vb, ck_nbr, cv_nbr, _nbr_bias_table(rel_bias_nbr[l], n_lat // GRID_W), l, n_lat)
        xs_mid, h_all, lg_all = _outproj(xs, oa, ob, oc, w_out_bf, mod, lat_group_out, g_ffn3, w_router_t, b_router_t,
                                         l, tm_lat, t_all, t_ctx, shared=(h_all, lg_all))

        gates, lpos, seg, n_used = _route(lg_all)
        x_sorted = _dispatch(seg, n_used, h_all, lpos, t_all * TOP_K // MOE_BLK + N_EXPERTS)
        y_sorted = _experts(seg, n_used, x_sorted, w_gate, b_gate, w_up, b_up, w_down, b_down, l)
        lpos_t = lpos.T
        xp = _combine(seg, xp_mid, y_sorted, lpos, lpos_t, gates, mod, ctx_group, l, 0)
        xs = _combine(seg, xs_mid, y_sorted, lpos, lpos_t, gates, mod, lat_group_comb, l, t_ctx)

    new_k_win, new_v_win = (a.reshape(bsz, depth, n_ctx, N_KV_A, HEAD_DIM) for a in caches[:2])
    new_k_nbr, new_v_nbr = (a.reshape(bsz, depth, n_ctx, N_HEADS_B, HEAD_DIM) for a in caches[2:])
    return (xp.reshape(bsz, n_ctx, d), xs.reshape(dbs, n_lat, d), new_k_win, new_v_win, new_k_nbr, new_v_nbr)
```

```python
import functools

import numpy as np
import jax
import jax.numpy as jnp
from jax import lax
from jax.experimental import pallas as pl
from jax.experimental.pallas import tpu as pltpu

F32 = jnp.float32
BF16 = jnp.bfloat16

D_MODEL = 1024
HEAD_DIM = 64
GRID_W = 64
N_HEADS_A = 8
N_KV_A = 2
GROUP_A = N_HEADS_A // N_KV_A
WINDOW = 128
Q_BLK = 128
N_HEADS_B = 4
NA_ROWS = 8
NA_COLS = 16
CONV_CH = 256
CONV_W = 3
WIDTH_A = N_HEADS_A * HEAD_DIM
KV_WIDTH_A = N_KV_A * HEAD_DIM
WIDTH_B = N_HEADS_B * HEAD_DIM
IN_COLS = WIDTH_A + 2 * KV_WIDTH_A + 3 * WIDTH_B + 3 * CONV_CH
N_EXPERTS = 32
TOP_K = 4
SWIGLU_LIMIT = 7.0
SWIGLU_ALPHA = 1.702
ROPE_BASE = 10000.0
EPS = 1e-6
NEG = -1e30
N_MOD = 6

C_QA = 0
C_KA = C_QA + WIDTH_A
C_VA = C_KA + KV_WIDTH_A
C_QB = C_VA + KV_WIDTH_A
C_KB = C_QB + WIDTH_B
C_VB = C_KB + WIDTH_B
C_U = C_VB + WIDTH_B
C_GB = C_U + CONV_CH
C_GC = C_GB + CONV_CH

MXU_COLS_V7X = 256
COND_ROWS = 8
MOE_BLK = 256
MOE_SHIFT = MOE_BLK.bit_length() - 1
assert 1 << MOE_SHIFT == MOE_BLK
VMEM_LIMIT_V7X = 56 * 1024 * 1024

_NT = (((1,), (1,)), ((), ()))


def _params(sem, vmem=None):
    return pltpu.CompilerParams(dimension_semantics=sem, vmem_limit_bytes=vmem)


def _mod_kernel(c_ref, w_ref, b_ref, o_ref):
    c = c_ref[...]
    s = c * jax.nn.sigmoid(c)
    o_ref[...] = jnp.dot(s.astype(BF16), w_ref[...].astype(BF16), preferred_element_type=F32) + b_ref[...]


def _modulation(cond, w_mod, b_mod):
    depth = w_mod.shape[0]
    out = pl.pallas_call(
        _mod_kernel,
        out_shape=jax.ShapeDtypeStruct((depth, COND_ROWS, N_MOD * D_MODEL), F32),
        grid=(depth, N_MOD),
        in_specs=[
            pl.BlockSpec((COND_ROWS, D_MODEL), lambda l, j: (0, 0)),
            pl.BlockSpec((None, D_MODEL, D_MODEL), lambda l, j: (l, 0, j)),
            pl.BlockSpec((None, 1, D_MODEL), lambda l, j: (l, 0, j)),
        ],
        out_specs=pl.BlockSpec((None, COND_ROWS, D_MODEL), lambda l, j: (l, 0, j)),
        compiler_params=_params(("arbitrary", "arbitrary")),
        name="modulation",
    )(cond, w_mod, b_mod.reshape(depth, 1, N_MOD * D_MODEL))
    return out.reshape(depth, COND_ROWS, N_MOD, D_MODEL)


def _head_norm(x, w_row, ones_ref):
    width = x.shape[1]
    sq = (x * x).astype(BF16)
    parts = []
    for c0 in range(0, width, MXU_COLS_V7X):
        wd = min(MXU_COLS_V7X, width - c0)
        parts.append(jnp.dot(sq[:, c0:c0 + wd], ones_ref[:wd, :wd], preferred_element_type=F32))
    ss = parts[0] if len(parts) == 1 else jnp.concatenate(parts, axis=-1)
    return x * lax.rsqrt(ss * (1.0 / HEAD_DIM) + EPS) * w_row


def _rope(x, cos, sin):
    width = x.shape[1]
    lane = lax.broadcasted_iota(jnp.int32, x.shape, 1)
    quarter = HEAD_DIM // 4
    partner = jnp.where((lane % (2 * quarter)) < quarter,
                        pltpu.roll(x, width - quarter, axis=1), pltpu.roll(x, quarter, axis=1))
    reps = width // cos.shape[1]
    cos_w = cos if reps == 1 else jnp.concatenate([cos] * reps, axis=-1)
    sin_w = sin if reps == 1 else jnp.concatenate([sin] * reps, axis=-1)
    return x * cos_w + partner * sin_w


def _inproj_kernel(*refs, seq_len, rope):
    x_ref, mod_ref, g_ref, w_ref, nw_ref, ones_ref, cw_ref = refs[:7]
    qa_o, qb_o, ka_o, va_o, kb_o, vb_o, oc_o = refs[-7:]
    if rope:
        cos_ref, sin_ref = refs[7:9]
    x = x_ref[...]
    tm = x.shape[0]
    h = x * lax.rsqrt(jnp.mean(x * x, axis=-1, keepdims=True) + EPS) * g_ref[...]
    h = h * (1.0 + mod_ref[1:2, :]) + mod_ref[0:1, :]
    p = jnp.dot(h.astype(BF16), w_ref[...], preferred_element_type=F32)

    qa = _head_norm(p[:, C_QA:C_KA], nw_ref[:, 0:WIDTH_A], ones_ref)
    ka = _head_norm(p[:, C_KA:C_VA], nw_ref[:, WIDTH_A:WIDTH_A + KV_WIDTH_A], ones_ref)
    o_qb = WIDTH_A + KV_WIDTH_A
    qb = _head_norm(p[:, C_QB:C_KB], nw_ref[:, o_qb:o_qb + WIDTH_B], ones_ref)
    kb = _head_norm(p[:, C_KB:C_VB], nw_ref[:, o_qb + WIDTH_B:o_qb + 2 * WIDTH_B], ones_ref)
    if rope:
        cos, sin = cos_ref[...], sin_ref[...]
        qa = _rope(qa, cos, sin)
        ka = _rope(ka, cos, sin)
    qa_o[...] = qa.astype(BF16)
    qb_o[...] = qb.astype(BF16)
    ka_o[...] = ka.reshape(ka_o.shape)
    va_o[...] = p[:, C_VA:C_QB].reshape(va_o.shape)
    kb_o[...] = kb.reshape(kb_o.shape)
    vb_o[...] = p[:, C_VB:C_U].reshape(vb_o.shape)

    z = p[:, C_GC:C_GC + CONV_CH] * p[:, C_U:C_GB]
    row = lax.broadcasted_iota(jnp.int32, z.shape, 0) % seq_len
    z_prev = jnp.where(row == 0, 0.0, pltpu.roll(z, 1, axis=0))
    z_next = jnp.where(row == seq_len - 1, 0.0, pltpu.roll(z, tm - 1, axis=0))
    y = z_prev * cw_ref[0:1, :] + z * cw_ref[1:2, :] + z_next * cw_ref[2:3, :]
    oc_o[...] = (p[:, C_GB:C_GC] * y).astype(BF16)


def _inproj(x, mod, group_of_tile, g_mix, w_in_bf, layer, norm_w, ones_bd, conv_w, tm, seq_len, rope_tabs=None,
            caches=None):
    t = x.shape[0]
    rope = rope_tabs is not None
    in_specs = [
        pl.BlockSpec((tm, D_MODEL), lambda i: (i, 0)),
        pl.BlockSpec((None, None, N_MOD, D_MODEL), lambda i: (layer, group_of_tile(i), 0, 0)),
        pl.BlockSpec((None, 1, D_MODEL), lambda i: (layer, 0, 0)),
        pl.BlockSpec((None, D_MODEL, IN_COLS), lambda i: (layer, 0, 0)),
        pl.BlockSpec((None, 1, norm_w.shape[-1]), lambda i: (layer, 0, 0)),
        pl.BlockSpec(ones_bd.shape, lambda i: (0, 0)),
        pl.BlockSpec((None, CONV_W, CONV_CH), lambda i: (layer, 0, 0)),
    ]
    args = [x, mod, g_mix, w_in_bf, norm_w, ones_bd, conv_w]
    if rope:
        in_specs += [pl.BlockSpec(rope_tabs[0].shape, lambda i: (0, 0))] * 2
        args += list(rope_tabs)
    widths = (WIDTH_A, WIDTH_B, KV_WIDTH_A, KV_WIDTH_A, WIDTH_B, WIDTH_B, CONV_CH)
    dtypes = (BF16, BF16, F32, F32, F32, F32, BF16)
    out_shape = [jax.ShapeDtypeStruct((t, w), dt) for w, dt in zip(widths, dtypes)]
    out_specs = [pl.BlockSpec((tm, w), lambda i: (i, 0)) for w in widths]
    aliases = {}
    if caches is not None:
        seqs = tm // seq_len
        for j, cache in enumerate(caches):
            o = 2 + j
            aliases[len(args)] = o
            in_specs.append(pl.BlockSpec(memory_space=pl.ANY))
            args.append(cache)
            out_shape[o] = jax.ShapeDtypeStruct(cache.shape, F32)
            out_specs[o] = pl.BlockSpec((seqs, None, seq_len, widths[o]), lambda i: (i, layer, 0, 0))
    return pl.pallas_call(
        functools.partial(_inproj_kernel, seq_len=seq_len, rope=rope),
        out_shape=out_shape,
        grid=(t // tm,),
        in_specs=in_specs,
        out_specs=out_specs,
        input_output_aliases=aliases,
        compiler_params=_params(("arbitrary",), VMEM_LIMIT_V7X),
        name="inproj_rope" if rope else "inproj",
    )(*args)


def _pair_attention(q2, k_lo, k_hi, v_lo, v_hi, sinks=None, adjust=None):
    m = k_lo.shape[0]
    k2 = jnp.concatenate([k_lo, k_hi], axis=0)
    v2 = jnp.concatenate([v_lo, v_hi], axis=0)
    s = lax.dot_general(q2, k2, _NT, preferred_element_type=F32)
    ps, dens = [], []
    for i in range(2):
        si = s[:, i * m:(i + 1) * m]
        if adjust is not None:
            si = adjust(i, si)
        mx = jnp.max(si, axis=-1, keepdims=True)
        if sinks is not None:
            mx = jnp.maximum(mx, sinks[i])
        p = jnp.exp(si - mx)
        den = jnp.sum(p, axis=-1, keepdims=True)
        if sinks is not None:
            den = den + jnp.exp(sinks[i] - mx)
        ps.append(p.astype(BF16))
        dens.append(den)
    o = jnp.dot(jnp.concatenate(ps, axis=-1), v2, preferred_element_type=F32)
    lane = lax.broadcasted_iota(jnp.int32, o.shape, 1)
    return o / jnp.where(lane < HEAD_DIM, dens[0], dens[1])


def _lane_halves(x):
    lane = lax.broadcasted_iota(jnp.int32, x.shape, 1)
    lo = lane < HEAD_DIM
    sw = pltpu.roll(x, HEAD_DIM, axis=1)
    z = jnp.zeros_like(x)
    return tuple(jnp.where(c, y, z).astype(BF16) for c, y in ((lo, x), (~lo, x), (lo, sw), (~lo, sw)))


def _ctx_attn_kernel(sink_ref, qa_ref, ka_ref, va_ref, qb_ref, kb_ref, vb_ref, oa_o, ob_o):
    scale = HEAD_DIM ** -0.5
    k0l, k1h, k1l, k0h = _lane_halves(ka_ref[...])
    v0l, v1h, v1l, v0h = _lane_halves(va_ref[...])
    for i in range(N_HEADS_A // 2):
        cols = slice(2 * i * HEAD_DIM, (2 * i + 2) * HEAD_DIM)
        kv = (k0l, k0h, v0l, v0h) if (2 * i) // GROUP_A == 0 else (k1l, k1h, v1l, v1h)
        o = _pair_attention(qa_ref[:, cols] * scale, *kv, sinks=(sink_ref[2 * i], sink_ref[2 * i + 1]))
        oa_o[:, cols] = o.astype(oa_o.dtype)
    for i in range(N_HEADS_B // 2):
        cols = slice(2 * i * HEAD_DIM, (2 * i + 2) * HEAD_DIM)
        kl, kh, _, _ = _lane_halves(kb_ref[:, cols])
        vl, vh, _, _ = _lane_halves(vb_ref[:, cols])
        o = _pair_attention(qb_ref[:, cols] * scale, kl, kh, vl, vh)
        ob_o[:, cols] = o.astype(ob_o.dtype)


def _ctx_attention(sink, qa, ka, va, qb, kb, vb, seq_len, layer):
    t = qa.shape[0]
    q_spec = lambda w: pl.BlockSpec((seq_len, w), lambda i: (i, 0))
    kv_spec = lambda w: pl.BlockSpec((None, None, seq_len, w), lambda i: (i, layer, 0, 0))
    return pl.pallas_call(
        _ctx_attn_kernel,
        out_shape=[jax.ShapeDtypeStruct((t, WIDTH_A), BF16), jax.ShapeDtypeStruct((t, WIDTH_B), BF16)],
        grid=(t // seq_len,),
        in_specs=[pl.BlockSpec(memory_space=pltpu.SMEM), q_spec(WIDTH_A), kv_spec(KV_WIDTH_A), kv_spec(KV_WIDTH_A),
                  q_spec(WIDTH_B), kv_spec(WIDTH_B), kv_spec(WIDTH_B)],
        out_specs=[pl.BlockSpec((seq_len, WIDTH_A), lambda i: (i, 0)),
                   pl.BlockSpec((seq_len, WIDTH_B), lambda i: (i, 0))],
        compiler_params=_params(("arbitrary",)),
        name="ctx_attention",
    )(sink, qa, ka, va, qb, kb, vb)


def _win_attn_kernel(sink_ref, qa_ref, ka_ref, va_ref, ck_ref, cv_ref, oa_o, *, n_lat):
    scale = HEAD_DIM ** -0.5
    span = Q_BLK + 2 * WINDOW
    n = pl.program_id(1)
    start = pl.multiple_of(jnp.clip(n * Q_BLK - WINDOW, 0, n_lat - span), Q_BLK)
    k0l, k1h, k1l, k0h = _lane_halves(jnp.concatenate([ka_ref[pl.ds(start, span), :], ck_ref[...]], axis=0))
    v0l, v1h, v1l, v0h = _lane_halves(jnp.concatenate([va_ref[pl.ds(start, span), :], cv_ref[...]], axis=0))
    n_keys = k0l.shape[0]
    qpos = n * Q_BLK + lax.broadcasted_iota(jnp.int32, (Q_BLK, n_keys), 0)
    col = lax.broadcasted_iota(jnp.int32, (Q_BLK, n_keys), 1)
    ok = (col >= span) | (jnp.abs(qpos - (start + col)) <= WINDOW)
    for i in range(N_HEADS_A // 2):
        cols = slice(2 * i * HEAD_DIM, (2 * i + 2) * HEAD_DIM)
        kv = (k0l, k0h, v0l, v0h) if (2 * i) // GROUP_A == 0 else (k1l, k1h, v1l, v1h)
        o = _pair_attention(qa_ref[:, cols] * scale, *kv, sinks=(sink_ref[2 * i], sink_ref[2 * i + 1]),
                            adjust=lambda _, s: jnp.where(ok, s, NEG))
        oa_o[:, cols] = o.astype(oa_o.dtype)


def _win_attention(sink, qa, ka, va, cache_k, cache_v, layer, n_lat):
    t = qa.shape[0]
    nb = n_lat // Q_BLK
    past = cache_k.shape[2]
    return pl.pallas_call(
        functools.partial(_win_attn_kernel, n_lat=n_lat),
        out_shape=jax.ShapeDtypeStruct((t, WIDTH_A), BF16),
        grid=(t // n_lat, nb),
        in_specs=[
            pl.BlockSpec(memory_space=pltpu.SMEM),
            pl.BlockSpec((Q_BLK, WIDTH_A), lambda b, n: (b * nb + n, 0)),
            pl.BlockSpec((n_lat, KV_WIDTH_A), lambda b, n: (b, 0)),
            pl.BlockSpec((n_lat, KV_WIDTH_A), lambda b, n: (b, 0)),
            pl.BlockSpec((None, None, past, KV_WIDTH_A), lambda b, n: (b, layer, 0, 0)),
            pl.BlockSpec((None, None, past, KV_WIDTH_A), lambda b, n: (b, layer, 0, 0)),
        ],
        out_specs=pl.BlockSpec((Q_BLK, WIDTH_A), lambda b, n: (b * nb + n, 0)),
        compiler_params=_params(("arbitrary", "arbitrary")),
        name="window_attention",
    )(sink, qa, ka, va, cache_k, cache_v)


def _nbr_attn_kernel(qb_ref, kb_ref, vb_ref, ck_ref, cv_ref, bias_ref, ob_o, *, rows, kr):
    scale = HEAD_DIM ** -0.5
    r = pl.program_id(1)
    start = pl.multiple_of(jnp.clip(r - kr // 2, 0, rows - kr) * GRID_W, GRID_W)
    nwin = kr * GRID_W
    kcat = jnp.concatenate([kb_ref[pl.ds(start, nwin), :], ck_ref[...]], axis=0)
    vcat = jnp.concatenate([vb_ref[pl.ds(start, nwin), :], cv_ref[...]], axis=0)
    past = ck_ref.shape[0]
    for i in range(N_HEADS_B // 2):
        cols = slice(2 * i * HEAD_DIM, (2 * i + 2) * HEAD_DIM)
        kl, kh, _, _ = _lane_halves(kcat[:, cols])
        vl, vh, _, _ = _lane_halves(vcat[:, cols])
        o = _pair_attention(
            qb_ref[:, cols] * scale, kl, kh, vl, vh,
            adjust=lambda hh, s, i=i: s + jnp.concatenate([bias_ref[2 * i + hh], jnp.zeros((GRID_W, past), F32)],
                                                          axis=-1))
        ob_o[:, cols] = o.astype(ob_o.dtype)


def _nbr_bias_table(rel_bias, rows):
    kr = min(NA_ROWS, rows)
    cls = np.arange(kr)[:, None]
    m = np.arange(kr)[None, :]
    row_sel = (m - cls + NA_ROWS - 1)[:, :, None] == np.arange(2 * NA_ROWS - 1)[None, None, :]
    c = np.arange(GRID_W)[:, None]
    kc = np.arange(GRID_W)[None, :]
    dc = np.clip(kc - c, -(NA_COLS - 1), NA_COLS - 1) + NA_COLS - 1
    col_sel = dc[:, :, None] == np.arange(2 * NA_COLS - 1)[None, None, :]
    col_start = np.clip(c - NA_COLS // 2, 0, GRID_W - NA_COLS)
    ok = (kc >= col_start) & (kc < col_start + NA_COLS)
    tab = jnp.einsum('hrd,ymr,ckd->hycmk', rel_bias.astype(F32), row_sel.astype(np.float32),
                     col_sel.astype(np.float32), precision=lax.Precision.HIGHEST)
    tab = jnp.where(ok[None, None, :, None, :], tab, NEG)
    return tab.reshape(rel_bias.shape[0], kr, GRID_W, kr * GRID_W)


def _nbr_attention(qb, kb, vb, cache_k, cache_v, bias_tab, layer, n_lat):
    t = qb.shape[0]
    rows = n_lat // GRID_W
    kr = min(NA_ROWS, rows)
    past = cache_k.shape[2]

    def bias_map(b, r):
        return (0, r - jnp.clip(r - kr // 2, 0, rows - kr), 0, 0)

    return pl.pallas_call(
        functools.partial(_nbr_attn_kernel, rows=rows, kr=kr),
        out_shape=jax.ShapeDtypeStruct((t, WIDTH_B), BF16),
        grid=(t // n_lat, rows),
        in_specs=[
            pl.BlockSpec((GRID_W, WIDTH_B), lambda b, r: (b * rows + r, 0)),
            pl.BlockSpec((n_lat, WIDTH_B), lambda b, r: (b, 0)),
            pl.BlockSpec((n_lat, WIDTH_B), lambda b, r: (b, 0)),
            pl.BlockSpec((None, None, past, WIDTH_B), lambda b, r: (b, layer, 0, 0)),
            pl.BlockSpec((None, None, past, WIDTH_B), lambda b, r: (b, layer, 0, 0)),
            pl.BlockSpec((N_HEADS_B, None, GRID_W, kr * GRID_W), bias_map),
        ],
        out_specs=pl.BlockSpec((GRID_W, WIDTH_B), lambda b, r: (b * rows + r, 0)),
        compiler_params=_params(("arbitrary", "arbitrary")),
        name="neighbourhood_attention",
    )(qb, kb, vb, cache_k, cache_v, bias_tab)


ROW_TILE = 8
LANES = 128
assert ROW_TILE * LANES == D_MODEL


PACK_TILE = ROW_TILE // 2


def _store_row_tiles(ref, row0, x):
    n = x.shape[0]
    rt = x.shape[1] // LANES
    for c in range(rt):
        ref[pl.ds(row0 * rt + c, n, stride=rt), :] = x[:, c * LANES:(c + 1) * LANES]


def _pack_bf16_pairs(x):
    bits = lax.bitcast_convert_type(x, jnp.uint32)
    half = x.shape[1] // 2
    return (bits[:, half:] & jnp.uint32(0xFFFF0000)) | (bits[:, :half] >> 16)


def _unpack_bf16_pairs(w):
    lo = lax.bitcast_convert_type(w << 16, F32)
    hi = lax.bitcast_convert_type(w & jnp.uint32(0xFFFF0000), F32)
    return jnp.concatenate([lo, hi], axis=-1).astype(BF16)


def _load_row_tiles(ref, row0, n, rt=ROW_TILE):
    return jnp.concatenate([ref[pl.ds(row0 * rt + c, n, stride=rt), :] for c in range(rt)], axis=-1)


def _outproj_kernel(x_ref, oa_ref, ob_ref, oc_ref, w_ref, mod_ref, g_ref, wr_ref, br_ref, *rest):
    xn_o, h_o, lg_o = rest[-3:]
    mix = jnp.dot(oa_ref[...], w_ref[0:WIDTH_A, :], preferred_element_type=F32)
    mix += jnp.dot(ob_ref[...], w_ref[WIDTH_A:WIDTH_A + WIDTH_B, :], preferred_element_type=F32)
    mix += jnp.dot(oc_ref[...], w_ref[WIDTH_A + WIDTH_B:, :], preferred_element_type=F32)
    xn = x_ref[...] + mod_ref[2:3, :] * mix
    xn_o[...] = xn
    h = xn * lax.rsqrt(jnp.mean(xn * xn, axis=-1, keepdims=True) + EPS) * g_ref[...]
    h = h * (1.0 + mod_ref[4:5, :]) + mod_ref[3:4, :]
    h_hi = h.astype(BF16)
    h_o[...] = h_hi
    h_lo = (h - h_hi.astype(F32)).astype(BF16)
    wr = wr_ref[...]
    w_hi = wr.astype(BF16)
    w_lo = (wr - w_hi.astype(F32)).astype(BF16)
    n_e = wr.shape[0]
    both = lax.dot_general(jnp.concatenate([w_hi, w_lo], axis=0), h_hi, _NT, preferred_element_type=F32)
    lg = both[:n_e] + both[n_e:] + lax.dot_general(w_hi, h_lo, _NT, preferred_element_type=F32)
    lg_o[...] = lg + br_ref[...]


def _outproj(x, oa, ob, oc, w_out_bf, mod, group_of_tile, g_ffn, w_router_t, b_router_t, layer, tm, t_all, tok0,
             shared):
    t = x.shape[0]
    b0 = tok0 // tm
    assert tok0 % tm == 0
    in_specs = [
        pl.BlockSpec((tm, D_MODEL), lambda i: (i, 0)),
        pl.BlockSpec((tm, WIDTH_A), lambda i: (i, 0)),
        pl.BlockSpec((tm, WIDTH_B), lambda i: (i, 0)),
        pl.BlockSpec((tm, CONV_CH), lambda i: (i, 0)),
        pl.BlockSpec((None, D_MODEL, D_MODEL), lambda i: (layer, 0, 0)),
        pl.BlockSpec((None, None, N_MOD, D_MODEL), lambda i: (layer, group_of_tile(i), 0, 0)),
        pl.BlockSpec((None, 1, D_MODEL), lambda i: (layer, 0, 0)),
        pl.BlockSpec((None, N_EXPERTS, D_MODEL), lambda i: (layer, 0, 0)),
        pl.BlockSpec((None, N_EXPERTS, 1), lambda i: (layer, 0, 0)),
    ]
    args = [x, oa, ob, oc, w_out_bf, mod, g_ffn, w_router_t, b_router_t]
    aliases = {len(args): 1, len(args) + 1: 2}
    in_specs += [pl.BlockSpec(memory_space=pl.ANY)] * 2
    args += list(shared)
    return pl.pallas_call(
        _outproj_kernel,
        out_shape=[jax.ShapeDtypeStruct((t, D_MODEL), F32), jax.ShapeDtypeStruct((t_all, D_MODEL), BF16),
                   jax.ShapeDtypeStruct((N_EXPERTS, t_all), F32)],
        grid=(t // tm,),
        in_specs=in_specs,
        out_specs=[pl.BlockSpec((tm, D_MODEL), lambda i: (i, 0)),
                   pl.BlockSpec((tm, D_MODEL), lambda i: (b0 + i, 0)),
                   pl.BlockSpec((N_EXPERTS, tm), lambda i: (0, b0 + i))],
        input_output_aliases=aliases,
        compiler_params=_params(("arbitrary",), VMEM_LIMIT_V7X),
        name="outproj_router",
    )(*args)


X_AHEAD = 3
X_SLOTS = X_AHEAD + 1
Y_SLOTS = 2


def _expert_kernel(seg_ref, n_used_ref, x_hbm, wg_ref, bg_ref, wu_ref, bu_ref, wd_ref, bd_ref, y_hbm,
                   x_buf, y_buf, xsem, ysem, zsem, *, n_blocks):
    e = pl.program_id(0)
    n_used = n_used_ref[0]
    blk_rows = MOE_BLK * ROW_TILE
    first_blk = seg_ref[SEG_DST, e, SEG_BLK]
    n_blk = seg_ref[SEG_CNT, e, SEG_BLK]

    def rows_of(g, rows=blk_rows):
        return pl.ds(pl.multiple_of(g * rows, rows), rows)

    def x_copy(g):
        return pltpu.make_async_copy(x_hbm.at[rows_of(g, MOE_BLK * PACK_TILE)], x_buf.at[g % X_SLOTS],
                                     xsem.at[g % X_SLOTS])

    def y_copy(g):
        return pltpu.make_async_copy(y_buf.at[g % Y_SLOTS], y_hbm.at[rows_of(g)], ysem.at[g % Y_SLOTS])

    def zero_copy(g):
        return pltpu.make_async_copy(y_buf.at[0], y_hbm.at[rows_of(g)], zsem)

    @pl.when(e == 0)
    def _():
        for g in range(X_AHEAD):
            @pl.when(g < n_used)
            def _():
                x_copy(g).start()

    def block(b, carry):
        g = first_blk + b

        @pl.when(g + X_AHEAD < n_used)
        def _():
            x_copy(g + X_AHEAD).start()

        x_copy(g).wait()

        @pl.when(g >= Y_SLOTS)
        def _():
            y_copy(g - Y_SLOTS).wait()

        x = _unpack_bf16_pairs(_load_row_tiles(x_buf.at[g % X_SLOTS], 0, MOE_BLK, PACK_TILE))
        gt = jnp.dot(x, wg_ref[...].astype(BF16), preferred_element_type=F32) + bg_ref[...]
        up = jnp.dot(x, wu_ref[...].astype(BF16), preferred_element_type=F32) + bu_ref[...]
        gt = jnp.minimum(gt, SWIGLU_LIMIT)
        up = jnp.clip(up, -SWIGLU_LIMIT, SWIGLU_LIMIT)
        a = gt * jax.nn.sigmoid(SWIGLU_ALPHA * gt) * (up + 1.0)
        y = jnp.dot(a.astype(BF16), wd_ref[...].astype(BF16), preferred_element_type=F32) + bd_ref[...]
        _store_row_tiles(y_buf.at[g % Y_SLOTS], 0, y)
        y_copy(g).start()
        return carry

    lax.fori_loop(0, n_blk, block, 0)

    @pl.when(e == pl.num_programs(0) - 1)
    def _():
        for d in range(Y_SLOTS, 0, -1):
            @pl.when(n_used >= d)
            def _():
                y_copy(n_used - d).wait()

        y_buf[0] = jnp.zeros((blk_rows, LANES), F32)

        def start_zero(g, carry):
            zero_copy(g).start()
            return carry

        def wait_zero(g, carry):
            zero_copy(g).wait()
            return carry

        lax.fori_loop(n_used, n_blocks, start_zero, 0)
        lax.fori_loop(n_used, n_blocks, wait_zero, 0)


def _experts(seg, n_used, x_sorted, w_gate, b_gate, w_up, b_up, w_down, b_down, layer):
    n_blocks = x_sorted.shape[0] // (MOE_BLK * PACK_TILE)
    wspec = pl.BlockSpec((None, None, D_MODEL, D_MODEL), lambda e, sg, nu: (layer, e, 0, 0))
    bspec = pl.BlockSpec((None, None, 1, D_MODEL), lambda e, sg, nu: (layer, e, 0, 0))
    anyspec = pl.BlockSpec(memory_space=pl.ANY)
    depth = w_gate.shape[0]
    b4 = lambda b: b.reshape(depth, N_EXPERTS, 1, D_MODEL)
    return pl.pallas_call(
        functools.partial(_expert_kernel, n_blocks=n_blocks),
        out_shape=jax.ShapeDtypeStruct((n_blocks * MOE_BLK * ROW_TILE, LANES), F32),
        grid_spec=pltpu.PrefetchScalarGridSpec(
            num_scalar_prefetch=2,
            grid=(N_EXPERTS,),
            in_specs=[anyspec, wspec, bspec, wspec, bspec, wspec, bspec],
            out_specs=anyspec,
            scratch_shapes=[
                pltpu.VMEM((X_SLOTS, MOE_BLK * PACK_TILE, LANES), jnp.uint32),
                pltpu.VMEM((Y_SLOTS, MOE_BLK * ROW_TILE, LANES), F32),
                pltpu.SemaphoreType.DMA((X_SLOTS,)),
                pltpu.SemaphoreType.DMA((Y_SLOTS,)),
                pltpu.SemaphoreType.DMA,
            ],
        ),
        compiler_params=_params(("arbitrary",), VMEM_LIMIT_V7X),
        name="experts",
    )(seg, n_used, x_sorted, w_gate, b4(b_gate), w_up, b4(b_up), w_down, b4(b_down))


CUM_CHUNK = 256


def _sublane_cumsum(x):
    row = lax.broadcasted_iota(jnp.int32, x.shape, 0)
    d = 1
    while d < x.shape[0]:
        x = x + jnp.where(row >= d, pltpu.roll(x, d, axis=0), 0)
        d *= 2
    return x


TOK_TILE = CUM_CHUNK
TILE_ROWS = TOP_K * TOK_TILE
SEG_LANES = 128
SEG_PAD_FROM = SEG_LANES - 2
SEG_PAD_LEN = SEG_LANES - 1
SEG_BLK = SEG_LANES - 3
SEG_SRC, SEG_CNT, SEG_DST = 0, 1, 2


def _route_kernel(lg_ref, tri_ref, gate_o, lpos_o, seg_o, used_o):
    lg = lg_ref[...]
    n_e, t = lg.shape
    e_iota = lax.broadcasted_iota(jnp.int32, lg.shape, 0)
    work = lg
    tops, hots = [], []
    for _ in range(TOP_K):
        m = jnp.max(work, axis=0, keepdims=True)
        first = jnp.min(jnp.where(work == m, e_iota, n_e), axis=0, keepdims=True)
        hot = e_iota == first
        work = jnp.where(hot, -jnp.inf, work)
        tops.append(m)
        hots.append(hot)
    ex = [jnp.exp(m - tops[0]) for m in tops]
    den = ex[0] + ex[1] + ex[2] + ex[3]
    for k in range(TOP_K):
        gate_o[k:k + 1, :] = ex[k] / den

    chosen = jnp.where(hots[0] | hots[1] | hots[2] | hots[3], 1.0, 0.0)
    tri = tri_ref[...]
    lane = lax.broadcasted_iota(jnp.int32, (n_e, SEG_LANES), 1)
    seg_cnt = jnp.zeros((n_e, SEG_LANES), jnp.int32)
    seg_before = jnp.zeros((n_e, SEG_LANES), jnp.int32)
    carry = jnp.zeros((n_e, 1), F32)
    rank_in_tile = []
    n_tiles = t // TOK_TILE
    for c in range(n_tiles):
        chunk = chosen[:, c * TOK_TILE:(c + 1) * TOK_TILE]
        inc = jnp.dot(chunk.astype(BF16), tri, preferred_element_type=F32)
        cnt = inc[:, TOK_TILE - 1:TOK_TILE]
        rank_in_tile.append((inc - chunk).astype(jnp.int32))
        seg_cnt = jnp.where(lane == c, cnt.astype(jnp.int32), seg_cnt)
        seg_before = jnp.where(lane == c, carry.astype(jnp.int32), seg_before)
        carry = carry + cnt

    counts = jnp.broadcast_to(carry.astype(jnp.int32), (n_e, SEG_LANES))
    padded = ((counts + (MOE_BLK - 1)) >> MOE_SHIFT) << MOE_SHIFT
    pad_end = _sublane_cumsum(padded)
    pad_start = pad_end - padded
    seg_src = _sublane_cumsum(seg_cnt) - seg_cnt
    seg_o[SEG_SRC] = seg_src
    seg_o[SEG_CNT] = jnp.where(lane == SEG_PAD_LEN, padded - counts,
                               jnp.where(lane == SEG_BLK, padded >> MOE_SHIFT, seg_cnt))
    seg_o[SEG_DST] = jnp.where(lane == SEG_PAD_FROM, pad_start + counts,
                               jnp.where(lane == SEG_BLK, pad_start >> MOE_SHIFT, pad_start + seg_before))

    for c in range(n_tiles):
        pos = seg_src[:, c:c + 1] + rank_in_tile[c]
        for k in range(TOP_K):
            lpos_o[k:k + 1, c * TOK_TILE:(c + 1) * TOK_TILE] = jnp.sum(
                jnp.where(hots[k][:, c * TOK_TILE:(c + 1) * TOK_TILE], pos, 0), axis=0, keepdims=True)

    used_o[...] = jnp.broadcast_to(pad_end[n_e - 1:n_e, :] >> MOE_SHIFT, used_o.shape)


def _route(logits):
    t = logits.shape[1]
    assert t % TOK_TILE == 0 and t // TOK_TILE <= SEG_BLK
    tri = jnp.asarray(np.triu(np.ones((CUM_CHUNK, CUM_CHUNK), np.float32)), dtype=BF16)
    gates, lpos, seg, used = pl.pallas_call(
        _route_kernel,
        out_shape=[jax.ShapeDtypeStruct((TOP_K, t), F32), jax.ShapeDtypeStruct((TOP_K, t), jnp.int32),
                   jax.ShapeDtypeStruct((3, N_EXPERTS, SEG_LANES), jnp.int32),
                   jax.ShapeDtypeStruct((8, SEG_LANES), jnp.int32)],
        compiler_params=_params(None, VMEM_LIMIT_V7X),
        name="route",
    )(logits, tri)
    return gates, lpos, seg, used[0, :1]


def _segment_copies(src, dst, src0, dst0, cnt, sem, wait, rt=ROW_TILE):
    @pl.when(cnt > 0)
    def _():
        size = cnt * rt
        cp = pltpu.make_async_copy(
            src.at[pl.ds(pl.multiple_of(src0 * rt, rt), size)],
            dst.at[pl.ds(pl.multiple_of(dst0 * rt, rt), size)], sem)
        if wait:
            cp.wait()
        else:
            cp.start()


Z_SLOTS = 3


def _dispatch_kernel(seg_ref, nu_ref, h_ref, lpos_ref, x_hbm, z_buf, zero_buf, sem, zsem, *, n_tiles, n_blocks):
    c = pl.program_id(0)
    slot = c % Z_SLOTS

    def wait_tile(s):
        pltpu.make_async_copy(z_buf.at[s], x_hbm.at[pl.ds(0, TILE_ROWS * PACK_TILE)], sem.at[s]).wait()

    def padding(wait):
        def per_expert(e, carry):
            _segment_copies(zero_buf, x_hbm, 0, seg_ref[SEG_DST, e, SEG_PAD_FROM], seg_ref[SEG_CNT, e, SEG_PAD_LEN],
                            zsem, wait, PACK_TILE)
            return carry

        lax.fori_loop(0, N_EXPERTS, per_expert, 0)

        def per_block(b, carry):
            cp = pltpu.make_async_copy(
                zero_buf, x_hbm.at[pl.ds(pl.multiple_of(b * (MOE_BLK * PACK_TILE), MOE_BLK * PACK_TILE),
                                         MOE_BLK * PACK_TILE)], zsem)
            if wait:
                cp.wait()
            else:
                cp.start()
            return carry

        lax.fori_loop(nu_ref[0], n_blocks, per_block, 0)

    @pl.when(c >= Z_SLOTS)
    def _():
        wait_tile(slot)

    row = lax.broadcasted_iota(jnp.int32, (TILE_ROWS, TOK_TILE), 0)
    hit = row == lpos_ref[0:1, :]
    for k in range(1, TOP_K):
        hit = hit | (row == lpos_ref[k:k + 1, :])
    z = jnp.dot(jnp.where(hit, 1.0, 0.0).astype(BF16), h_ref[...], preferred_element_type=F32)
    _store_row_tiles(z_buf.at[slot], 0, _pack_bf16_pairs(z))

    def per_expert(e, carry):
        _segment_copies(z_buf.at[slot], x_hbm, seg_ref[SEG_SRC, e, c], seg_ref[SEG_DST, e, c], seg_ref[SEG_CNT, e, c],
                        sem.at[slot], False, PACK_TILE)
        return carry

    lax.fori_loop(0, N_EXPERTS, per_expert, 0)

    @pl.when(c == n_tiles - 1)
    def _():
        zero_buf[...] = jnp.zeros(zero_buf.shape, jnp.uint32)
        padding(False)
        for back in range(min(Z_SLOTS, n_tiles) - 1, -1, -1):
            wait_tile((n_tiles - 1 - back) % Z_SLOTS)
        padding(True)


def _dispatch(seg, n_used, h_all, lpos, n_blocks):
    t = h_all.shape[0]
    n_tiles = t // TOK_TILE
    return pl.pallas_call(
        functools.partial(_dispatch_kernel, n_tiles=n_tiles, n_blocks=n_blocks),
        out_shape=jax.ShapeDtypeStruct((n_blocks * MOE_BLK * PACK_TILE, LANES), jnp.uint32),
        grid_spec=pltpu.PrefetchScalarGridSpec(
            num_scalar_prefetch=2,
            grid=(n_tiles,),
            in_specs=[pl.BlockSpec((TOK_TILE, D_MODEL), lambda c, sg, nu: (c, 0)),
                      pl.BlockSpec((TOP_K, TOK_TILE), lambda c, sg, nu: (0, c))],
            out_specs=pl.BlockSpec(memory_space=pl.ANY),
            scratch_shapes=[
                pltpu.VMEM((Z_SLOTS, TILE_ROWS * PACK_TILE, LANES), jnp.uint32),
                pltpu.VMEM((MOE_BLK * PACK_TILE, LANES), jnp.uint32),
                pltpu.SemaphoreType.DMA((Z_SLOTS,)),
                pltpu.SemaphoreType.DMA,
            ],
        ),
        compiler_params=_params(("arbitrary",), VMEM_LIMIT_V7X),
        name="dispatch",
    )(seg, n_used, h_all, lpos)


def _combine_kernel(seg_ref, x_ref, y_hbm, lpos_ref, lpos_t_ref, gate_ref, mod_ref, o_ref, y_buf, sem, *, tile0,
                    n_tiles):
    c = pl.program_id(0)
    slot = c % 2

    def fetch(cc):
        def per_expert(e, carry):
            _segment_copies(y_hbm, y_buf.at[cc % 2], seg_ref[SEG_DST, e, tile0 + cc], seg_ref[SEG_SRC, e, tile0 + cc],
                            seg_ref[SEG_CNT, e, tile0 + cc], sem.at[cc % 2], False)
            return carry

        lax.fori_loop(0, N_EXPERTS, per_expert, 0)

    @pl.when(c == 0)
    def _():
        fetch(0)

    @pl.when(c + 1 < n_tiles)
    def _():
        fetch(c + 1)

    pltpu.make_async_copy(y_hbm.at[pl.ds(0, TILE_ROWS * ROW_TILE)], y_buf.at[slot], sem.at[slot]).wait()

    row = lax.broadcasted_iota(jnp.int32, (TILE_ROWS, TOK_TILE), 0)
    row_gate = jnp.zeros((TILE_ROWS, 1), F32)
    for k in range(TOP_K):
        row_gate += jnp.sum(jnp.where(row == lpos_ref[k:k + 1, :], gate_ref[k:k + 1, :], 0.0), axis=1, keepdims=True)
    ys = _load_row_tiles(y_buf.at[slot], 0, TILE_ROWS) * row_gate
    ys_hi = ys.astype(BF16)
    ys_lo = (ys - ys_hi.astype(F32)).astype(BF16)
    col = lax.broadcasted_iota(jnp.int32, (TOK_TILE, TILE_ROWS), 1)
    hit = col == lpos_t_ref[:, 0:1]
    for k in range(1, TOP_K):
        hit = hit | (col == lpos_t_ref[:, k:k + 1])
    u = jnp.where(hit, 1.0, 0.0).astype(BF16)
    mix = jnp.dot(u, ys_hi, preferred_element_type=F32) + jnp.dot(u, ys_lo, preferred_element_type=F32)
    o_ref[...] = x_ref[...] + mod_ref[5:6, :] * mix


def _combine(seg, x_mid, y_sorted, lpos, lpos_t, gates, mod, group_of_tile, layer, tok0):
    t = x_mid.shape[0]
    tile0 = tok0 // TOK_TILE
    n_tiles = t // TOK_TILE
    return pl.pallas_call(
        functools.partial(_combine_kernel, tile0=tile0, n_tiles=n_tiles),
        out_shape=jax.ShapeDtypeStruct((t, D_MODEL), F32),
        grid_spec=pltpu.PrefetchScalarGridSpec(
            num_scalar_prefetch=1,
            grid=(n_tiles,),
            in_specs=[
                pl.BlockSpec((TOK_TILE, D_MODEL), lambda c, sg: (c, 0)),
                pl.BlockSpec(memory_space=pl.ANY),
                pl.BlockSpec((TOP_K, TOK_TILE), lambda c, sg: (0, tile0 + c)),
                pl.BlockSpec((TOK_TILE, TOP_K), lambda c, sg: (tile0 + c, 0)),
                pl.BlockSpec((TOP_K, TOK_TILE), lambda c, sg: (0, tile0 + c)),
                pl.BlockSpec((None, None, N_MOD, D_MODEL), lambda c, sg: (layer, group_of_tile(c), 0, 0)),
            ],
            out_specs=pl.BlockSpec((TOK_TILE, D_MODEL), lambda c, sg: (c, 0)),
            scratch_shapes=[pltpu.VMEM((2, TILE_ROWS * ROW_TILE, LANES), F32), pltpu.SemaphoreType.DMA((2,))],
        ),
        compiler_params=_params(("arbitrary",), VMEM_LIMIT_V7X),
        name="combine",
    )(seg, x_mid, y_sorted, lpos, lpos_t, gates, mod)


def _rope_tables(n_lat):
    quarter = HEAD_DIM // 4
    t = jnp.arange(n_lat)
    inv = ROPE_BASE ** (-jnp.arange(quarter, dtype=F32) / quarter)
    ang_r = (t // GRID_W).astype(F32)[:, None] * inv
    ang_c = (t % GRID_W).astype(F32)[:, None] * inv
    cos = jnp.concatenate([jnp.cos(ang_r)] * 2 + [jnp.cos(ang_c)] * 2, axis=-1)
    sin = jnp.concatenate([-jnp.sin(ang_r), jnp.sin(ang_r), -jnp.sin(ang_c), jnp.sin(ang_c)], axis=-1)
    return jnp.concatenate([cos, cos], axis=-1), jnp.concatenate([sin, sin], axis=-1)


def _block_diag_ones():
    idx = np.arange(MXU_COLS_V7X) // HEAD_DIM
    return jnp.asarray(idx[:, None] == idx[None, :], dtype=BF16)


def kernel(x_prompt, x_sample, cache_k_win, cache_v_win, cache_k_nbr, cache_v_nbr, c, c_ctx, w_mod, b_mod, g_mix, g_ffn, w_in, w_out, qn_win, kn_win, qn_nbr, kn_nbr, sink_win, rel_bias_nbr, conv_w, w_router, b_router, w_gate, b_gate, w_up, b_up, w_down, b_down):
    bsz, n_ctx, d = x_prompt.shape
    dbs, n_lat, _ = x_sample.shape
    depth = w_in.shape[0]
    past = cache_k_win.shape[2]
    assert d == D_MODEL and dbs + 1 <= COND_ROWS and n_lat % GRID_W == 0 and n_lat >= Q_BLK + 2 * WINDOW
    t_ctx, t_lat = bsz * n_ctx, dbs * n_lat

    cond = jnp.concatenate([c_ctx[None], c, jnp.zeros((COND_ROWS - 1 - dbs, d), F32)], axis=0)
    mod = _modulation(cond, w_mod, b_mod)

    w_in_bf = w_in.astype(BF16)
    w_out_bf = w_out.astype(BF16)
    norm_w = jnp.concatenate([jnp.tile(qn_win, (1, N_HEADS_A)), jnp.tile(kn_win, (1, N_KV_A)),
                              jnp.tile(qn_nbr, (1, N_HEADS_B)), jnp.tile(kn_nbr, (1, N_HEADS_B))], axis=-1)[:, None, :]
    ones_bd = _block_diag_ones()
    rope_tabs = _rope_tables(n_lat)
    g_mix3, g_ffn3 = g_mix[:, None, :], g_ffn[:, None, :]
    w_router_t = jnp.swapaxes(w_router, 1, 2)
    b_router_t = b_router[:, :, None]
    t_all = t_ctx + t_lat
    ck_win = cache_k_win.reshape(dbs, depth, past, KV_WIDTH_A)
    cv_win = cache_v_win.reshape(dbs, depth, past, KV_WIDTH_A)
    ck_nbr = cache_k_nbr.reshape(dbs, depth, past, WIDTH_B)
    cv_nbr = cache_v_nbr.reshape(dbs, depth, past, WIDTH_B)

    tm_ctx = 2 * n_ctx
    tm_lat = 512
    ctx_group = lambda i: 0
    lat_group_in = lambda i: 1 + i
    lat_group_out = lambda i: 1 + (i * tm_lat) // n_lat
    lat_group_comb = lambda i: 1 + (i * TOK_TILE) // n_lat
    assert t_ctx % TOK_TILE == 0 and n_lat % TOK_TILE == 0

    xp = x_prompt.reshape(t_ctx, d)
    xs = x_sample.reshape(t_lat, d)
    caches = tuple(jnp.zeros((bsz, depth, n_ctx, w), F32) for w in (KV_WIDTH_A, KV_WIDTH_A, WIDTH_B, WIDTH_B))
    for l in range(depth):
        qa, qb, *caches, oc = _inproj(xp, mod, ctx_group, g_mix3, w_in_bf, l, norm_w, ones_bd, conv_w,
                                      tm_ctx, n_ctx, caches=caches)
        oa, ob = _ctx_attention(sink_win[l], qa, caches[0], caches[1], qb, caches[2], caches[3], n_ctx, l)
        shared = (jnp.zeros((t_all, d), BF16), jnp.zeros((N_EXPERTS, t_all), F32)) if l == 0 else (h_all, lg_all)
        xp_mid, h_all, lg_all = _outproj(xp, oa, ob, oc, w_out_bf, mod, ctx_group, g_ffn3, w_router_t, b_router_t, l,
                                         tm_ctx, t_all, 0, shared)

        qa, qb, ka, va, kb, vb, oc = _inproj(xs, mod, lat_group_in, g_mix3, w_in_bf, l, norm_w, ones_bd, conv_w,
                                             n_lat, n_lat, rope_tabs)
        oa = _win_attention(sink_win[l], qa, ka, va, ck_win, cv_win, l, n_lat)
        ob = _nbr_attention(qb, kb, vb, ck_nbr, cv_nbr, _nbr_bias_table(rel_bias_nbr[l], n_lat // GRID_W), l, n_lat)
        xs_mid, h_all, lg_all = _outproj(xs, oa, ob, oc, w_out_bf, mod, lat_group_out, g_ffn3, w_router_t, b_router_t,
                                         l, tm_lat, t_all, t_ctx, shared=(h_all, lg_all))

        gates, lpos, seg, n_used = _route(lg_all)
        x_sorted = _dispatch(seg, n_used, h_all, lpos, t_all * TOP_K // MOE_BLK + N_EXPERTS)
        y_sorted = _experts(seg, n_used, x_sorted, w_gate, b_gate, w_up, b_up, w_down, b_down, l)
        lpos_t = lpos.T
        xp = _combine(seg, xp_mid, y_sorted, lpos, lpos_t, gates, mod, ctx_group, l, 0)
        xs = _combine(seg, xs_mid, y_sorted, lpos, lpos_t, gates, mod, lat_group_comb, l, t_ctx)

    new_k_win, new_v_win = (a.reshape(bsz, depth, n_ctx, N_KV_A, HEAD_DIM) for a in caches[:2])
    new_k_nbr, new_v_nbr = (a.reshape(bsz, depth, n_ctx, N_HEADS_B, HEAD_DIM) for a in caches[2:])
    return (xp.reshape(bsz, n_ctx, d), xs.reshape(dbs, n_lat, d), new_k_win, new_v_win, new_k_nbr, new_v_nbr)
```

```python
import functools

import numpy as np
import jax
import jax.numpy as jnp
from jax import lax
from jax.experimental import pallas as pl
from jax.experimental.pallas import tpu as pltpu

F32 = jnp.float32
BF16 = jnp.bfloat16

D_MODEL = 1024
HEAD_DIM = 64
GRID_W = 64
N_HEADS_A = 8
N_KV_A = 2
GROUP_A = N_HEADS_A // N_KV_A
WINDOW = 128
Q_BLK = 128
N_HEADS_B = 4
NA_ROWS = 8
NA_COLS = 16
CONV_CH = 256
CONV_W = 3
WIDTH_A = N_HEADS_A * HEAD_DIM
KV_WIDTH_A = N_KV_A * HEAD_DIM
WIDTH_B = N_HEADS_B * HEAD_DIM
IN_COLS = WIDTH_A + 2 * KV_WIDTH_A + 3 * WIDTH_B + 3 * CONV_CH
N_EXPERTS = 32
TOP_K = 4
SWIGLU_LIMIT = 7.0
SWIGLU_ALPHA = 1.702
ROPE_BASE = 10000.0
EPS = 1e-6
NEG = -1e30
N_MOD = 6

C_QA = 0
C_KA = C_QA + WIDTH_A
C_VA = C_KA + KV_WIDTH_A
C_QB = C_VA + KV_WIDTH_A
C_KB = C_QB + WIDTH_B
C_VB = C_KB + WIDTH_B
C_U = C_VB + WIDTH_B
C_GB = C_U + CONV_CH
C_GC = C_GB + CONV_CH

MXU_COLS_V7X = 256
COND_ROWS = 8
MOE_BLK = 256
MOE_SHIFT = MOE_BLK.bit_length() - 1
assert 1 << MOE_SHIFT == MOE_BLK
VMEM_LIMIT_V7X = 56 * 1024 * 1024

_NT = (((1,), (1,)), ((), ()))


def _params(sem, vmem=None):
    return pltpu.CompilerParams(dimension_semantics=sem, vmem_limit_bytes=vmem)


def _mod_kernel(c_ref, w_ref, b_ref, o_ref):
    c = c_ref[...]
    s = c * jax.nn.sigmoid(c)
    o_ref[...] = jnp.dot(s.astype(BF16), w_ref[...].astype(BF16), preferred_element_type=F32) + b_ref[...]


def _modulation(cond, w_mod, b_mod):
    depth = w_mod.shape[0]
    out = pl.pallas_call(
        _mod_kernel,
        out_shape=jax.ShapeDtypeStruct((depth, COND_ROWS, N_MOD * D_MODEL), F32),
        grid=(depth, N_MOD),
        in_specs=[
            pl.BlockSpec((COND_ROWS, D_MODEL), lambda l, j: (0, 0)),
            pl.BlockSpec((None, D_MODEL, D_MODEL), lambda l, j: (l, 0, j)),
            pl.BlockSpec((None, 1, D_MODEL), lambda l, j: (l, 0, j)),
        ],
        out_specs=pl.BlockSpec((None, COND_ROWS, D_MODEL), lambda l, j: (l, 0, j)),
        compiler_params=_params(("arbitrary", "arbitrary")),
        name="modulation",
    )(cond, w_mod, b_mod.reshape(depth, 1, N_MOD * D_MODEL))
    return out.reshape(depth, COND_ROWS, N_MOD, D_MODEL)


def _head_norm(x, w_row, ones_ref):
    width = x.shape[1]
    sq = (x * x).astype(BF16)
    parts = []
    for c0 in range(0, width, MXU_COLS_V7X):
        wd = min(MXU_COLS_V7X, width - c0)
        parts.append(jnp.dot(sq[:, c0:c0 + wd], ones_ref[:wd, :wd], preferred_element_type=F32))
    ss = parts[0] if len(parts) == 1 else jnp.concatenate(parts, axis=-1)
    return x * lax.rsqrt(ss * (1.0 / HEAD_DIM) + EPS) * w_row


def _rope(x, cos, sin):
    width = x.shape[1]
    lane = lax.broadcasted_iota(jnp.int32, x.shape, 1)
    quarter = HEAD_DIM // 4
    partner = jnp.where((lane % (2 * quarter)) < quarter,
                        pltpu.roll(x, width - quarter, axis=1), pltpu.roll(x, quarter, axis=1))
    reps = width // cos.shape[1]
    cos_w = cos if reps == 1 else jnp.concatenate([cos] * reps, axis=-1)
    sin_w = sin if reps == 1 else jnp.concatenate([sin] * reps, axis=-1)
    return x * cos_w + partner * sin_w


def _inproj_kernel(*refs, seq_len, rope):
    x_ref, mod_ref, g_ref, w_ref, nw_ref, ones_ref, cw_ref = refs[:7]
    qa_o, qb_o, ka_o, va_o, kb_o, vb_o, oc_o = refs[-7:]
    if rope:
        cos_ref, sin_ref = refs[7:9]
    x = x_ref[...]
    tm = x.shape[0]
    h = x * lax.rsqrt(jnp.mean(x * x, axis=-1, keepdims=True) + EPS) * g_ref[...]
    h = h * (1.0 + mod_ref[1:2, :]) + mod_ref[0:1, :]
    p = jnp.dot(h.astype(BF16), w_ref[...], preferred_element_type=F32)

    qa = _head_norm(p[:, C_QA:C_KA], nw_ref[:, 0:WIDTH_A], ones_ref)
    ka = _head_norm(p[:, C_KA:C_VA], nw_ref[:, WIDTH_A:WIDTH_A + KV_WIDTH_A], ones_ref)
    o_qb = WIDTH_A + KV_WIDTH_A
    qb = _head_norm(p[:, C_QB:C_KB], nw_ref[:, o_qb:o_qb + WIDTH_B], ones_ref)
    kb = _head_norm(p[:, C_KB:C_VB], nw_ref[:, o_qb + WIDTH_B:o_qb + 2 * WIDTH_B], ones_ref)
    if rope:
        cos, sin = cos_ref[...], sin_ref[...]
        qa = _rope(qa, cos, sin)
        ka = _rope(ka, cos, sin)
    qa_o[...] = qa.astype(BF16)
    qb_o[...] = qb.astype(BF16)
    ka_o[...] = ka.reshape(ka_o.shape)
    va_o[...] = p[:, C_VA:C_QB].reshape(va_o.shape)
    kb_o[...] = kb.reshape(kb_o.shape)
    vb_o[...] = p[:, C_VB:C_U].reshape(vb_o.shape)

    z = p[:, C_GC:C_GC + CONV_CH] * p[:, C_U:C_GB]
    row = lax.broadcasted_iota(jnp.int32, z.shape, 0) % seq_len
    z_prev = jnp.where(row == 0, 0.0, pltpu.roll(z, 1, axis=0))
    z_next = jnp.where(row == seq_len - 1, 0.0, pltpu.roll(z, tm - 1, axis=0))
    y = z_prev * cw_ref[0:1, :] + z * cw_ref[1:2, :] + z_next * cw_ref[2:3, :]
    oc_o[...] = (p[:, C_GB:C_GC] * y).astype(BF16)


def _inproj(x, mod, group_of_tile, g_mix, w_in_bf, layer, norm_w, ones_bd, conv_w, tm, seq_len, rope_tabs=None,
            caches=None):
    t = x.shape[0]
    rope = rope_tabs is not None
    in_specs = [
        pl.BlockSpec((tm, D_MODEL), lambda i: (i, 0)),
        pl.BlockSpec((None, None, N_MOD, D_MODEL), lambda i: (layer, group_of_tile(i), 0, 0)),
        pl.BlockSpec((None, 1, D_MODEL), lambda i: (layer, 0, 0)),
        pl.BlockSpec((None, D_MODEL, IN_COLS), lambda i: (layer, 0, 0)),
        pl.BlockSpec((None, 1, norm_w.shape[-1]), lambda i: (layer, 0, 0)),
        pl.BlockSpec(ones_bd.shape, lambda i: (0, 0)),
        pl.BlockSpec((None, CONV_W, CONV_CH), lambda i: (layer, 0, 0)),
    ]
    args = [x, mod, g_mix, w_in_bf, norm_w, ones_bd, conv_w]
    if rope:
        in_specs += [pl.BlockSpec(rope_tabs[0].shape, lambda i: (0, 0))] * 2
        args += list(rope_tabs)
    widths = (WIDTH_A, WIDTH_B, KV_WIDTH_A, KV_WIDTH_A, WIDTH_B, WIDTH_B, CONV_CH)
    dtypes = (BF16, BF16, F32, F32, F32, F32, BF16)
    out_shape = [jax.ShapeDtypeStruct((t, w), dt) for w, dt in zip(widths, dtypes)]
    out_specs = [pl.BlockSpec((tm, w), lambda i: (i, 0)) for w in widths]
    aliases = {}
    if caches is not None:
        seqs = tm // seq_len
        for j, cache in enumerate(caches):
            o = 2 + j
            aliases[len(args)] = o
            in_specs.append(pl.BlockSpec(memory_space=pl.ANY))
            args.append(cache)
            out_shape[o] = jax.ShapeDtypeStruct(cache.shape, F32)
            out_specs[o] = pl.BlockSpec((seqs, None, seq_len, widths[o]), lambda i: (i, layer, 0, 0))
    return pl.pallas_call(
        functools.partial(_inproj_kernel, seq_len=seq_len, rope=rope),
        out_shape=out_shape,
        grid=(t // tm,),
        in_specs=in_specs,
        out_specs=out_specs,
        input_output_aliases=aliases,
        compiler_params=_params(("arbitrary",), VMEM_LIMIT_V7X),
        name="inproj_rope" if rope else "inproj",
    )(*args)


def _pair_attention(q2, k_lo, k_hi, v_lo, v_hi, sinks=None, adjust=None):
    m = k_lo.shape[0]
    k2 = jnp.concatenate([k_lo, k_hi], axis=0)
    v2 = jnp.concatenate([v_lo, v_hi], axis=0)
    s = lax.dot_general(q2, k2, _NT, preferred_element_type=F32)
    ps, dens = [], []
    for i in range(2):
        si = s[:, i * m:(i + 1) * m]
        if adjust is not None:
            si = adjust(i, si)
        mx = jnp.max(si, axis=-1, keepdims=True)
        if sinks is not None:
            mx = jnp.maximum(mx, sinks[i])
        p = jnp.exp(si - mx)
        den = jnp.sum(p, axis=-1, keepdims=True)
        if sinks is not None:
            den = den + jnp.exp(sinks[i] - mx)
        ps.append(p.astype(BF16))
        dens.append(den)
    o = jnp.dot(jnp.concatenate(ps, axis=-1), v2, preferred_element_type=F32)
    lane = lax.broadcasted_iota(jnp.int32, o.shape, 1)
    return o / jnp.where(lane < HEAD_DIM, dens[0], dens[1])


def _lane_halves(x):
    lane = lax.broadcasted_iota(jnp.int32, x.shape, 1)
    lo = lane < HEAD_DIM
    sw = pltpu.roll(x, HEAD_DIM, axis=1)
    z = jnp.zeros_like(x)
    return tuple(jnp.where(c, y, z).astype(BF16) for c, y in ((lo, x), (~lo, x), (lo, sw), (~lo, sw)))


def _ctx_attn_kernel(sink_ref, qa_ref, ka_ref, va_ref, qb_ref, kb_ref, vb_ref, oa_o, ob_o):
    scale = HEAD_DIM ** -0.5
    k0l, k1h, k1l, k0h = _lane_halves(ka_ref[...])
    v0l, v1h, v1l, v0h = _lane_halves(va_ref[...])
    for i in range(N_HEADS_A // 2):
        cols = slice(2 * i * HEAD_DIM, (2 * i + 2) * HEAD_DIM)
        kv = (k0l, k0h, v0l, v0h) if (2 * i) // GROUP_A == 0 else (k1l, k1h, v1l, v1h)
        o = _pair_attention(qa_ref[:, cols] * scale, *kv, sinks=(sink_ref[2 * i], sink_ref[2 * i + 1]))
        oa_o[:, cols] = o.astype(oa_o.dtype)
    for i in range(N_HEADS_B // 2):
        cols = slice(2 * i * HEAD_DIM, (2 * i + 2) * HEAD_DIM)
        kl, kh, _, _ = _lane_halves(kb_ref[:, cols])
        vl, vh, _, _ = _lane_halves(vb_ref[:, cols])
        o = _pair_attention(qb_ref[:, cols] * scale, kl, kh, vl, vh)
        ob_o[:, cols] = o.astype(ob_o.dtype)


def _ctx_attention(sink, qa, ka, va, qb, kb, vb, seq_len, layer):
    t = qa.shape[0]
    q_spec = lambda w: pl.BlockSpec((seq_len, w), lambda i: (i, 0))
    kv_spec = lambda w: pl.BlockSpec((None, None, seq_len, w), lambda i: (i, layer, 0, 0))
    return pl.pallas_call(
        _ctx_attn_kernel,
        out_shape=[jax.ShapeDtypeStruct((t, WIDTH_A), BF16), jax.ShapeDtypeStruct((t, WIDTH_B), BF16)],
        grid=(t // seq_len,),
        in_specs=[pl.BlockSpec(memory_space=pltpu.SMEM), q_spec(WIDTH_A), kv_spec(KV_WIDTH_A), kv_spec(KV_WIDTH_A),
                  q_spec(WIDTH_B), kv_spec(WIDTH_B), kv_spec(WIDTH_B)],
        out_specs=[pl.BlockSpec((seq_len, WIDTH_A), lambda i: (i, 0)),
                   pl.BlockSpec((seq_len, WIDTH_B), lambda i: (i, 0))],
        compiler_params=_params(("arbitrary",)),
        name="ctx_attention",
    )(sink, qa, ka, va, qb, kb, vb)


def _win_attn_kernel(sink_ref, qa_ref, ka_ref, va_ref, ck_ref, cv_ref, oa_o, *, n_lat):
    scale = HEAD_DIM ** -0.5
    span = Q_BLK + 2 * WINDOW
    n = pl.program_id(1)
    start = pl.multiple_of(jnp.clip(n * Q_BLK - WINDOW, 0, n_lat - span), Q_BLK)
    k0l, k1h, k1l, k0h = _lane_halves(jnp.concatenate([ka_ref[pl.ds(start, span), :], ck_ref[...]], axis=0))
    v0l, v1h, v1l, v0h = _lane_halves(jnp.concatenate([va_ref[pl.ds(start, span), :], cv_ref[...]], axis=0))
    n_keys = k0l.shape[0]
    qpos = n * Q_BLK + lax.broadcasted_iota(jnp.int32, (Q_BLK, n_keys), 0)
    col = lax.broadcasted_iota(jnp.int32, (Q_BLK, n_keys), 1)
    ok = (col >= span) | (jnp.abs(qpos - (start + col)) <= WINDOW)
    for i in range(N_HEADS_A // 2):
        cols = slice(2 * i * HEAD_DIM, (2 * i + 2) * HEAD_DIM)
        kv = (k0l, k0h, v0l, v0h) if (2 * i) // GROUP_A == 0 else (k1l, k1h, v1l, v1h)
        o = _pair_attention(qa_ref[:, cols] * scale, *kv, sinks=(sink_ref[2 * i], sink_ref[2 * i + 1]),
                            adjust=lambda _, s: jnp.where(ok, s, NEG))
        oa_o[:, cols] = o.astype(oa_o.dtype)


def _win_attention(sink, qa, ka, va, cache_k, cache_v, layer, n_lat):
    t = qa.shape[0]
    nb = n_lat // Q_BLK
    past = cache_k.shape[2]
    return pl.pallas_call(
        functools.partial(_win_attn_kernel, n_lat=n_lat),
        out_shape=jax.ShapeDtypeStruct((t, WIDTH_A), BF16),
        grid=(t // n_lat, nb),
        in_specs=[
            pl.BlockSpec(memory_space=pltpu.SMEM),
            pl.BlockSpec((Q_BLK, WIDTH_A), lambda b, n: (b * nb + n, 0)),
            pl.BlockSpec((n_lat, KV_WIDTH_A), lambda b, n: (b, 0)),
            pl.BlockSpec((n_lat, KV_WIDTH_A), lambda b, n: (b, 0)),
            pl.BlockSpec((None, None, past, KV_WIDTH_A), lambda b, n: (b, layer, 0, 0)),
            pl.BlockSpec((None, None, past, KV_WIDTH_A), lambda b, n: (b, layer, 0, 0)),
        ],
        out_specs=pl.BlockSpec((Q_BLK, WIDTH_A), lambda b, n: (b * nb + n, 0)),
        compiler_params=_params(("arbitrary", "arbitrary")),
        name="window_attention",
    )(sink, qa, ka, va, cache_k, cache_v)


def _nbr_attn_kernel(qb_ref, kb_ref, vb_ref, ck_ref, cv_ref, bias_ref, ob_o, *, rows, kr):
    scale = HEAD_DIM ** -0.5
    r = pl.program_id(1)
    start = pl.multiple_of(jnp.clip(r - kr // 2, 0, rows - kr) * GRID_W, GRID_W)
    nwin = kr * GRID_W
    kcat = jnp.concatenate([kb_ref[pl.ds(start, nwin), :], ck_ref[...]], axis=0)
    vcat = jnp.concatenate([vb_ref[pl.ds(start, nwin), :], cv_ref[...]], axis=0)
    past = ck_ref.shape[0]
    for i in range(N_HEADS_B // 2):
        cols = slice(2 * i * HEAD_DIM, (2 * i + 2) * HEAD_DIM)
        kl, kh, _, _ = _lane_halves(kcat[:, cols])
        vl, vh, _, _ = _lane_halves(vcat[:, cols])
        o = _pair_attention(
            qb_ref[:, cols] * scale, kl, kh, vl, vh,
            adjust=lambda hh, s, i=i: s + jnp.concatenate([bias_ref[2 * i + hh], jnp.zeros((GRID_W, past), F32)],
                                                          axis=-1))
        ob_o[:, cols] = o.astype(ob_o.dtype)


def _nbr_bias_table(rel_bias, rows):
    kr = min(NA_ROWS, rows)
    cls = np.arange(kr)[:, None]
    m = np.arange(kr)[None, :]
    row_sel = (m - cls + NA_ROWS - 1)[:, :, None] == np.arange(2 * NA_ROWS - 1)[None, None, :]
    c = np.arange(GRID_W)[:, None]
    kc = np.arange(GRID_W)[None, :]
    dc = np.clip(kc - c, -(NA_COLS - 1), NA_COLS - 1) + NA_COLS - 1
    col_sel = dc[:, :, None] == np.arange(2 * NA_COLS - 1)[None, None, :]
    col_start = np.clip(c - NA_COLS // 2, 0, GRID_W - NA_COLS)
    ok = (kc >= col_start) & (kc < col_start + NA_COLS)
    tab = jnp.einsum('hrd,ymr,ckd->hycmk', rel_bias.astype(F32), row_sel.astype(np.float32),
                     col_sel.astype(np.float32), precision=lax.Precision.HIGHEST)
    tab = jnp.where(ok[None, None, :, None, :], tab, NEG)
    return tab.reshape(rel_bias.shape[0], kr, GRID_W, kr * GRID_W)


def _nbr_attention(qb, kb, vb, cache_k, cache_v, bias_tab, layer, n_lat):
    t = qb.shape[0]
    rows = n_lat // GRID_W
    kr = min(NA_ROWS, rows)
    past = cache_k.shape[2]

    def bias_map(b, r):
        return (0, r - jnp.clip(r - kr // 2, 0, rows - kr), 0, 0)

    return pl.pallas_call(
        functools.partial(_nbr_attn_kernel, rows=rows, kr=kr),
        out_shape=jax.ShapeDtypeStruct((t, WIDTH_B), BF16),
        grid=(t // n_lat, rows),
        in_specs=[
            pl.BlockSpec((GRID_W, WIDTH_B), lambda b, r: (b * rows + r, 0)),
            pl.BlockSpec((n_lat, WIDTH_B), lambda b, r: (b, 0)),
            pl.BlockSpec((n_lat, WIDTH_B), lambda b, r: (b, 0)),
            pl.BlockSpec((None, None, past, WIDTH_B), lambda b, r: (b, layer, 0, 0)),
            pl.BlockSpec((None, None, past, WIDTH_B), lambda b, r: (b, layer, 0, 0)),
            pl.BlockSpec((N_HEADS_B, None, GRID_W, kr * GRID_W), bias_map),
        ],
        out_specs=pl.BlockSpec((GRID_W, WIDTH_B), lambda b, r: (b * rows + r, 0)),
        compiler_params=_params(("arbitrary", "arbitrary")),
        name="neighbourhood_attention",
    )(qb, kb, vb, cache_k, cache_v, bias_tab)


ROW_TILE = 8
LANES = 128
assert ROW_TILE * LANES == D_MODEL


PACK_TILE = ROW_TILE // 2


def _store_row_tiles(ref, row0, x):
    n = x.shape[0]
    rt = x.shape[1] // LANES
    for c in range(rt):
        ref[pl.ds(row0 * rt + c, n, stride=rt), :] = x[:, c * LANES:(c + 1) * LANES]


def _pack_bf16_pairs(x):
    bits = lax.bitcast_convert_type(x, jnp.uint32)
    half = x.shape[1] // 2
    return (bits[:, half:] & jnp.uint32(0xFFFF0000)) | (bits[:, :half] >> 16)


def _unpack_bf16_pairs(w):
    lo = lax.bitcast_convert_type(w << 16, F32)
    hi = lax.bitcast_convert_type(w & jnp.uint32(0xFFFF0000), F32)
    return jnp.concatenate([lo, hi], axis=-1).astype(BF16)


def _load_row_tiles(ref, row0, n, rt=ROW_TILE):
    return jnp.concatenate([ref[pl.ds(row0 * rt + c, n, stride=rt), :] for c in range(rt)], axis=-1)


def _outproj_kernel(x_ref, oa_ref, ob_ref, oc_ref, w_ref, mod_ref, g_ref, wr_ref, br_ref, *rest):
    xn_o, h_o, lg_o = rest[-3:]
    mix = jnp.dot(oa_ref[...], w_ref[0:WIDTH_A, :], preferred_element_type=F32)
    mix += jnp.dot(ob_ref[...], w_ref[WIDTH_A:WIDTH_A + WIDTH_B, :], preferred_element_type=F32)
    mix += jnp.dot(oc_ref[...], w_ref[WIDTH_A + WIDTH_B:, :], preferred_element_type=F32)
    xn = x_ref[...] + mod_ref[2:3, :] * mix
    xn_o[...] = xn
    h = xn * lax.rsqrt(jnp.mean(xn * xn, axis=-1, keepdims=True) + EPS) * g_ref[...]
    h = h * (1.0 + mod_ref[4:5, :]) + mod_ref[3:4, :]
    h_hi = h.astype(BF16)
    h_o[...] = h_hi
    h_lo = (h - h_hi.astype(F32)).astype(BF16)
    wr = wr_ref[...]
    w_hi = wr.astype(BF16)
    w_lo = (wr - w_hi.astype(F32)).astype(BF16)
    n_e = wr.shape[0]
    both = lax.dot_general(jnp.concatenate([w_hi, w_lo], axis=0), h_hi, _NT, preferred_element_type=F32)
    lg = both[:n_e] + both[n_e:] + lax.dot_general(w_hi, h_lo, _NT, preferred_element_type=F32)
    lg_o[...] = lg + br_ref[...]


def _outproj(x, oa, ob, oc, w_out_bf, mod, group_of_tile, g_ffn, w_router_t, b_router_t, layer, tm, t_all, tok0,
             shared):
    t = x.shape[0]
    b0 = tok0 // tm
    assert tok0 % tm == 0
    in_specs = [
        pl.BlockSpec((tm, D_MODEL), lambda i: (i, 0)),
        pl.BlockSpec((tm, WIDTH_A), lambda i: (i, 0)),
        pl.BlockSpec((tm, WIDTH_B), lambda i: (i, 0)),
        pl.BlockSpec((tm, CONV_CH), lambda i: (i, 0)),
        pl.BlockSpec((None, D_MODEL, D_MODEL), lambda i: (layer, 0, 0)),
        pl.BlockSpec((None, None, N_MOD, D_MODEL), lambda i: (layer, group_of_tile(i), 0, 0)),
        pl.BlockSpec((None, 1, D_MODEL), lambda i: (layer, 0, 0)),
        pl.BlockSpec((None, N_EXPERTS, D_MODEL), lambda i: (layer, 0, 0)),
        pl.BlockSpec((None, N_EXPERTS, 1), lambda i: (layer, 0, 0)),
    ]
    args = [x, oa, ob, oc, w_out_bf, mod, g_ffn, w_router_t, b_router_t]
    aliases = {len(args): 1, len(args) + 1: 2}
    in_specs += [pl.BlockSpec(memory_space=pl.ANY)] * 2
    args += list(shared)
    return pl.pallas_call(
        _outproj_kernel,
        out_shape=[jax.ShapeDtypeStruct((t, D_MODEL), F32), jax.ShapeDtypeStruct((t_all, D_MODEL), BF16),
                   jax.ShapeDtypeStruct((N_EXPERTS, t_all), F32)],
        grid=(t // tm,),
        in_specs=in_specs,
        out_specs=[pl.BlockSpec((tm, D_MODEL), lambda i: (i, 0)),
                   pl.BlockSpec((tm, D_MODEL), lambda i: (b0 + i, 0)),
                   pl.BlockSpec((N_EXPERTS, tm), lambda i: (0, b0 + i))],
        input_output_aliases=aliases,
        compiler_params=_params(("arbitrary",), VMEM_LIMIT_V7X),
        name="outproj_router",
    )(*args)


X_AHEAD = 3
X_SLOTS = X_AHEAD + 1
Y_SLOTS = 2


def _expert_kernel(seg_ref, n_used_ref, x_hbm, wg_ref, bg_ref, wu_ref, bu_ref, wd_ref, bd_ref, y_hbm,
                   x_buf, y_buf, xsem, ysem, zsem, *, n_blocks):
    e = pl.program_id(0)
    n_used = n_used_ref[0]
    blk_rows = MOE_BLK * ROW_TILE
    first_blk = seg_ref[SEG_DST, e, SEG_BLK]
    n_blk = seg_ref[SEG_CNT, e, SEG_BLK]

    def rows_of(g, rows=blk_rows):
        return pl.ds(pl.multiple_of(g * rows, rows), rows)

    def x_copy(g):
        return pltpu.make_async_copy(x_hbm.at[rows_of(g, MOE_BLK * PACK_TILE)], x_buf.at[g % X_SLOTS],
                                     xsem.at[g % X_SLOTS])

    def y_copy(g):
        return pltpu.make_async_copy(y_buf.at[g % Y_SLOTS], y_hbm.at[rows_of(g)], ysem.at[g % Y_SLOTS])

    def zero_copy(g):
        return pltpu.make_async_copy(y_buf.at[0], y_hbm.at[rows_of(g)], zsem)

    @pl.when(e == 0)
    def _():
        for g in range(X_AHEAD):
            @pl.when(g < n_used)
            def _():
                x_copy(g).start()

    def block(b, carry):
        g = first_blk + b

        @pl.when(g + X_AHEAD < n_used)
        def _():
            x_copy(g + X_AHEAD).start()

        x_copy(g).wait()

        @pl.when(g >= Y_SLOTS)
        def _():
            y_copy(g - Y_SLOTS).wait()

        x = _unpack_bf16_pairs(_load_row_tiles(x_buf.at[g % X_SLOTS], 0, MOE_BLK, PACK_TILE))
        gt = jnp.dot(x, wg_ref[...].astype(BF16), preferred_element_type=F32) + bg_ref[...]
        up = jnp.dot(x, wu_ref[...].astype(BF16), preferred_element_type=F32) + bu_ref[...]
        gt = jnp.minimum(gt, SWIGLU_LIMIT)
        up = jnp.clip(up, -SWIGLU_LIMIT, SWIGLU_LIMIT)
        a = gt * jax.nn.sigmoid(SWIGLU_ALPHA * gt) * (up + 1.0)
        y = jnp.dot(a.astype(BF16), wd_ref[...].astype(BF16), preferred_element_type=F32) + bd_ref[...]
        _store_row_tiles(y_buf.at[g % Y_SLOTS], 0, y)
        y_copy(g).start()
        return carry

    lax.fori_loop(0, n_blk, block, 0)

    @pl.when(e == pl.num_programs(0) - 1)
    def _():
        for d in range(Y_SLOTS, 0, -1):
            @pl.when(n_used >= d)
            def _():
                y_copy(n_used - d).wait()

        y_buf[0] = jnp.zeros((blk_rows, LANES), F32)

        def start_zero(g, carry):
            zero_copy(g).start()
            return carry

        def wait_zero(g, carry):
            zero_copy(g).wait()
            return carry

        lax.fori_loop(n_used, n_blocks, start_zero, 0)
        lax.fori_loop(n_used, n_blocks, wait_zero, 0)


def _experts(seg, n_used, x_sorted, w_gate, b_gate, w_up, b_up, w_down, b_down, layer):
    n_blocks = x_sorted.shape[0] // (MOE_BLK * PACK_TILE)
    wspec = pl.BlockSpec((None, None, D_MODEL, D_MODEL), lambda e, sg, nu: (layer, e, 0, 0))
    bspec = pl.BlockSpec((None, None, 1, D_MODEL), lambda e, sg, nu: (layer, e, 0, 0))
    anyspec = pl.BlockSpec(memory_space=pl.ANY)
    depth = w_gate.shape[0]
    b4 = lambda b: b.reshape(depth, N_EXPERTS, 1, D_MODEL)
    return pl.pallas_call(
        functools.partial(_expert_kernel, n_blocks=n_blocks),
        out_shape=jax.ShapeDtypeStruct((n_blocks * MOE_BLK * ROW_TILE, LANES), F32),
        grid_spec=pltpu.PrefetchScalarGridSpec(
            num_scalar_prefetch=2,
            grid=(N_EXPERTS,),
            in_specs=[anyspec, wspec, bspec, wspec, bspec, wspec, bspec],
            out_specs=anyspec,
            scratch_shapes=[
                pltpu.VMEM((X_SLOTS, MOE_BLK * PACK_TILE, LANES), jnp.uint32),
                pltpu.VMEM((Y_SLOTS, MOE_BLK * ROW_TILE, LANES), F32),
                pltpu.SemaphoreType.DMA((X_SLOTS,)),
                pltpu.SemaphoreType.DMA((Y_SLOTS,)),
                pltpu.SemaphoreType.DMA,
            ],
        ),
        compiler_params=_params(("arbitrary",), VMEM_LIMIT_V7X),
        name="experts",
    )(seg, n_used, x_sorted, w_gate, b4(b_gate), w_up, b4(b_up), w_down, b4(b_down))


CUM_CHUNK = 256


def _sublane_cumsum(x):
    row = lax.broadcasted_iota(jnp.int32, x.shape, 0)
    d = 1
    while d < x.shape[0]:
        x = x + jnp.where(row >= d, pltpu.roll(x, d, axis=0), 0)
        d *= 2
    return x


TOK_TILE = CUM_CHUNK
TILE_ROWS = TOP_K * TOK_TILE
SEG_LANES = 128
SEG_PAD_FROM = SEG_LANES - 2
SEG_PAD_LEN = SEG_LANES - 1
SEG_BLK = SEG_LANES - 3
SEG_SRC, SEG_CNT, SEG_DST = 0, 1, 2


def _route_kernel(lg_ref, tri_ref, gate_o, lpos_o, seg_o, used_o):
    lg = lg_ref[...]
    n_e, t = lg.shape
    e_iota = lax.broadcasted_iota(jnp.int32, lg.shape, 0)
    work = lg
    tops, hots = [], []
    for _ in range(TOP_K):
        m = jnp.max(work, axis=0, keepdims=True)
        first = jnp.min(jnp.where(work == m, e_iota, n_e), axis=0, keepdims=True)
        hot = e_iota == first
        work = jnp.where(hot, -jnp.inf, work)
        tops.append(m)
        hots.append(hot)
    ex = [jnp.exp(m - tops[0]) for m in tops]
    den = ex[0] + ex[1] + ex[2] + ex[3]
    for k in range(TOP_K):
        gate_o[k:k + 1, :] = ex[k] / den

    chosen = jnp.where(hots[0] | hots[1] | hots[2] | hots[3], 1.0, 0.0)
    tri = tri_ref[...]
    lane = lax.broadcasted_iota(jnp.int32, (n_e, SEG_LANES), 1)
    seg_cnt = jnp.zeros((n_e, SEG_LANES), jnp.int32)
    seg_before = jnp.zeros((n_e, SEG_LANES), jnp.int32)
    carry = jnp.zeros((n_e, 1), F32)
    rank_in_tile = []
    n_tiles = t // TOK_TILE
    for c in range(n_tiles):
        chunk = chosen[:, c * TOK_TILE:(c + 1) * TOK_TILE]
        inc = jnp.dot(chunk.astype(BF16), tri, preferred_element_type=F32)
        cnt = inc[:, TOK_TILE - 1:TOK_TILE]
        rank_in_tile.append((inc - chunk).astype(jnp.int32))
        seg_cnt = jnp.where(lane == c, cnt.astype(jnp.int32), seg_cnt)
        seg_before = jnp.where(lane == c, carry.astype(jnp.int32), seg_before)
        carry = carry + cnt

    counts = jnp.broadcast_to(carry.astype(jnp.int32), (n_e, SEG_LANES))
    padded = ((counts + (MOE_BLK - 1)) >> MOE_SHIFT) << MOE_SHIFT
    pad_end = _sublane_cumsum(padded)
    pad_start = pad_end - padded
    seg_src = _sublane_cumsum(seg_cnt) - seg_cnt
    seg_o[SEG_SRC] = seg_src
    seg_o[SEG_CNT] = jnp.where(lane == SEG_PAD_LEN, padded - counts,
                               jnp.where(lane == SEG_BLK, padded >> MOE_SHIFT, seg_cnt))
    seg_o[SEG_DST] = jnp.where(lane == SEG_PAD_FROM, pad_start + counts,
                               jnp.where(lane == SEG_BLK, pad_start >> MOE_SHIFT, pad_start + seg_before))

    for c in range(n_tiles):
        pos = seg_src[:, c:c + 1] + rank_in_tile[c]
        for k in range(TOP_K):
            lpos_o[k:k + 1, c * TOK_TILE:(c + 1) * TOK_TILE] = jnp.sum(
                jnp.where(hots[k][:, c * TOK_TILE:(c + 1) * TOK_TILE], pos, 0), axis=0, keepdims=True)

    used_o[...] = jnp.broadcast_to(pad_end[n_e - 1:n_e, :] >> MOE_SHIFT, used_o.shape)


def _route(logits):
    t = logits.shape[1]
    assert t % TOK_TILE == 0 and t // TOK_TILE <= SEG_BLK
    tri = jnp.asarray(np.triu(np.ones((CUM_CHUNK, CUM_CHUNK), np.float32)), dtype=BF16)
    gates, lpos, seg, used = pl.pallas_call(
        _route_kernel,
        out_shape=[jax.ShapeDtypeStruct((TOP_K, t), F32), jax.ShapeDtypeStruct((TOP_K, t), jnp.int32),
                   jax.ShapeDtypeStruct((3, N_EXPERTS, SEG_LANES), jnp.int32),
                   jax.ShapeDtypeStruct((8, SEG_LANES), jnp.int32)],
        compiler_params=_params(None, VMEM_LIMIT_V7X),
        name="route",
    )(logits, tri)
    return gates, lpos, seg, used[0, :1]


def _segment_copies(src, dst, src0, dst0, cnt, sem, wait, rt=ROW_TILE):
    @pl.when(cnt > 0)
    def _():
        size = cnt * rt
        cp = pltpu.make_async_copy(
            src.at[pl.ds(pl.multiple_of(src0 * rt, rt), size)],
            dst.at[pl.ds(pl.multiple_of(dst0 * rt, rt), size)], sem)
        if wait:
            cp.wait()
        else:
            cp.start()


Z_SLOTS = 3


def _dispatch_kernel(seg_ref, nu_ref, h_ref, lpos_ref, x_hbm, z_buf, zero_buf, sem, zsem, *, n_tiles, n_blocks):
    c = pl.program_id(0)
    slot = c % Z_SLOTS

    def wait_tile(s):
        pltpu.make_async_copy(z_buf.at[s], x_hbm.at[pl.ds(0, TILE_ROWS * PACK_TILE)], sem.at[s]).wait()

    def padding(wait):
        def per_expert(e, carry):
            _segment_copies(zero_buf, x_hbm, 0, seg_ref[SEG_DST, e, SEG_PAD_FROM], seg_ref[SEG_CNT, e, SEG_PAD_LEN],
                            zsem, wait, PACK_TILE)
            return carry

        lax.fori_loop(0, N_EXPERTS, per_expert, 0)

        def per_block(b, carry):
            cp = pltpu.make_async_copy(
                zero_buf, x_hbm.at[pl.ds(pl.multiple_of(b * (MOE_BLK * PACK_TILE), MOE_BLK * PACK_TILE),
                                         MOE_BLK * PACK_TILE)], zsem)
            if wait:
                cp.wait()
            else:
                cp.start()
            return carry

        lax.fori_loop(nu_ref[0], n_blocks, per_block, 0)

    @pl.when(c >= Z_SLOTS)
    def _():
        wait_tile(slot)

    row = lax.broadcasted_iota(jnp.int32, (TILE_ROWS, TOK_TILE), 0)
    hit = row == lpos_ref[0:1, :]
    for k in range(1, TOP_K):
        hit = hit | (row == lpos_ref[k:k + 1, :])
    z = jnp.dot(jnp.where(hit, 1.0, 0.0).astype(BF16), h_ref[...], preferred_element_type=F32)
    _store_row_tiles(z_buf.at[slot], 0, _pack_bf16_pairs(z))

    def per_expert(e, carry):
        _segment_copies(z_buf.at[slot], x_hbm, seg_ref[SEG_SRC, e, c], seg_ref[SEG_DST, e, c], seg_ref[SEG_CNT, e, c],
                        sem.at[slot], False, PACK_TILE)
        return carry

    lax.fori_loop(0, N_EXPERTS, per_expert, 0)

    @pl.when(c == n_tiles - 1)
    def _():
        zero_buf[...] = jnp.zeros(zero_buf.shape, jnp.uint32)
        padding(False)
        for back in range(min(Z_SLOTS, n_tiles) - 1, -1, -1):
            wait_tile((n_tiles - 1 - back) % Z_SLOTS)
        padding(True)


def _dispatch(seg, n_used, h_all, lpos, n_blocks):
    t = h_all.shape[0]
    n_tiles = t // TOK_TILE
    return pl.pallas_call(
        functools.partial(_dispatch_kernel, n_tiles=n_tiles, n_blocks=n_blocks),
        out_shape=jax.ShapeDtypeStruct((n_blocks * MOE_BLK * PACK_TILE, LANES), jnp.uint32),
        grid_spec=pltpu.PrefetchScalarGridSpec(
            num_scalar_prefetch=2,
            grid=(n_tiles,),
            in_specs=[pl.BlockSpec((TOK_TILE, D_MODEL), lambda c, sg, nu: (c, 0)),
                      pl.BlockSpec((TOP_K, TOK_TILE), lambda c, sg, nu: (0, c))],
            out_specs=pl.BlockSpec(memory_space=pl.ANY),
            scratch_shapes=[
                pltpu.VMEM((Z_SLOTS, TILE_ROWS * PACK_TILE, LANES), jnp.uint32),
                pltpu.VMEM((MOE_BLK * PACK_TILE, LANES), jnp.uint32),
                pltpu.SemaphoreType.DMA((Z_SLOTS,)),
                pltpu.SemaphoreType.DMA,
            ],
        ),
        compiler_params=_params(("arbitrary",), VMEM_LIMIT_V7X),
        name="dispatch",
    )(seg, n_used, h_all, lpos)


def _combine_kernel(seg_ref, xa_ref, xb_ref, y_hbm, lpos_ref, lpos_t_ref, gate_ref, mod_ref, oa_ref, ob_ref, y_buf, sem,
                    *, n_a, n_tiles):
    c = pl.program_id(0)
    slot = c % 2

    def fetch(cc):
        def per_expert(e, carry):
            _segment_copies(y_hbm, y_buf.at[cc % 2], seg_ref[SEG_DST, e, cc], seg_ref[SEG_SRC, e, cc],
                            seg_ref[SEG_CNT, e, cc], sem.at[cc % 2], False)
            return carry

        lax.fori_loop(0, N_EXPERTS, per_expert, 0)

    @pl.when(c == 0)
    def _():
        fetch(0)

    @pl.when(c + 1 < n_tiles)
    def _():
        fetch(c + 1)

    pltpu.make_async_copy(y_hbm.at[pl.ds(0, TILE_ROWS * ROW_TILE)], y_buf.at[slot], sem.at[slot]).wait()

    row = lax.broadcasted_iota(jnp.int32, (TILE_ROWS, TOK_TILE), 0)
    row_gate = jnp.zeros((TILE_ROWS, 1), F32)
    for k in range(TOP_K):
        row_gate += jnp.sum(jnp.where(row == lpos_ref[k:k + 1, :], gate_ref[k:k + 1, :], 0.0), axis=1, keepdims=True)
    ys = _load_row_tiles(y_buf.at[slot], 0, TILE_ROWS) * row_gate
    ys_hi = ys.astype(BF16)
    ys_lo = (ys - ys_hi.astype(F32)).astype(BF16)
    col = lax.broadcasted_iota(jnp.int32, (TOK_TILE, TILE_ROWS), 1)
    hit = col == lpos_t_ref[:, 0:1]
    for k in range(1, TOP_K):
        hit = hit | (col == lpos_t_ref[:, k:k + 1])
    u = jnp.where(hit, 1.0, 0.0).astype(BF16)
    mix = mod_ref[5:6, :] * (jnp.dot(u, ys_hi, preferred_element_type=F32)
                             + jnp.dot(u, ys_lo, preferred_element_type=F32))

    @pl.when(c < n_a)
    def _():
        oa_ref[...] = xa_ref[...] + mix

    @pl.when(c >= n_a)
    def _():
        ob_ref[...] = xb_ref[...] + mix


def _combine(seg, xa_mid, xb_mid, y_sorted, lpos, lpos_t, gates, mod, group_of_tile, layer):
    n_a = xa_mid.shape[0] // TOK_TILE
    n_tiles = n_a + xb_mid.shape[0] // TOK_TILE
    a_spec = pl.BlockSpec((TOK_TILE, D_MODEL), lambda c, sg: (jnp.minimum(c, n_a - 1), 0))
    b_spec = pl.BlockSpec((TOK_TILE, D_MODEL), lambda c, sg: (jnp.maximum(c - n_a, 0), 0))
    return pl.pallas_call(
        functools.partial(_combine_kernel, n_a=n_a, n_tiles=n_tiles),
        out_shape=[jax.ShapeDtypeStruct(xa_mid.shape, F32), jax.ShapeDtypeStruct(xb_mid.shape, F32)],
        grid_spec=pltpu.PrefetchScalarGridSpec(
            num_scalar_prefetch=1,
            grid=(n_tiles,),
            in_specs=[
                a_spec, b_spec,
                pl.BlockSpec(memory_space=pl.ANY),
                pl.BlockSpec((TOP_K, TOK_TILE), lambda c, sg: (0, c)),
                pl.BlockSpec((TOK_TILE, TOP_K), lambda c, sg: (c, 0)),
                pl.BlockSpec((TOP_K, TOK_TILE), lambda c, sg: (0, c)),
                pl.BlockSpec((None, None, N_MOD, D_MODEL), lambda c, sg: (layer, group_of_tile(c), 0, 0)),
            ],
            out_specs=[a_spec, b_spec],
            scratch_shapes=[pltpu.VMEM((2, TILE_ROWS * ROW_TILE, LANES), F32), pltpu.SemaphoreType.DMA((2,))],
        ),
        compiler_params=_params(("arbitrary",), VMEM_LIMIT_V7X),
        name="combine",
    )(seg, xa_mid, xb_mid, y_sorted, lpos, lpos_t, gates, mod)


def _rope_tables(n_lat):
    quarter = HEAD_DIM // 4
    t = np.arange(n_lat)
    inv = np.float32(ROPE_BASE) ** (-np.arange(quarter, dtype=np.float32) / np.float32(quarter))
    ang_r = (t // GRID_W).astype(np.float32)[:, None] * inv
    ang_c = (t % GRID_W).astype(np.float32)[:, None] * inv
    cos = np.concatenate([np.cos(ang_r)] * 2 + [np.cos(ang_c)] * 2, axis=-1)
    sin = np.concatenate([-np.sin(ang_r), np.sin(ang_r), -np.sin(ang_c), np.sin(ang_c)], axis=-1)
    return (jnp.asarray(np.concatenate([cos, cos], axis=-1), dtype=F32),
            jnp.asarray(np.concatenate([sin, sin], axis=-1), dtype=F32))


def _block_diag_ones():
    idx = np.arange(MXU_COLS_V7X) // HEAD_DIM
    return jnp.asarray(idx[:, None] == idx[None, :], dtype=BF16)


def kernel(x_prompt, x_sample, cache_k_win, cache_v_win, cache_k_nbr, cache_v_nbr, c, c_ctx, w_mod, b_mod, g_mix, g_ffn, w_in, w_out, qn_win, kn_win, qn_nbr, kn_nbr, sink_win, rel_bias_nbr, conv_w, w_router, b_router, w_gate, b_gate, w_up, b_up, w_down, b_down):
    bsz, n_ctx, d = x_prompt.shape
    dbs, n_lat, _ = x_sample.shape
    depth = w_in.shape[0]
    past = cache_k_win.shape[2]
    assert d == D_MODEL and dbs + 1 <= COND_ROWS and n_lat % GRID_W == 0 and n_lat >= Q_BLK + 2 * WINDOW
    t_ctx, t_lat = bsz * n_ctx, dbs * n_lat

    cond = jnp.concatenate([c_ctx[None], c, jnp.zeros((COND_ROWS - 1 - dbs, d), F32)], axis=0)
    mod = _modulation(cond, w_mod, b_mod)

    w_in_bf = w_in.astype(BF16)
    w_out_bf = w_out.astype(BF16)
    norm_w = jnp.concatenate([jnp.tile(qn_win, (1, N_HEADS_A)), jnp.tile(kn_win, (1, N_KV_A)),
                              jnp.tile(qn_nbr, (1, N_HEADS_B)), jnp.tile(kn_nbr, (1, N_HEADS_B))], axis=-1)[:, None, :]
    ones_bd = _block_diag_ones()
    rope_tabs = _rope_tables(n_lat)
    g_mix3, g_ffn3 = g_mix[:, None, :], g_ffn[:, None, :]
    w_router_t = jnp.swapaxes(w_router, 1, 2)
    b_router_t = b_router[:, :, None]
    t_all = t_ctx + t_lat
    ck_win = cache_k_win.reshape(dbs, depth, past, KV_WIDTH_A)
    cv_win = cache_v_win.reshape(dbs, depth, past, KV_WIDTH_A)
    ck_nbr = cache_k_nbr.reshape(dbs, depth, past, WIDTH_B)
    cv_nbr = cache_v_nbr.reshape(dbs, depth, past, WIDTH_B)

    tm_ctx = 2 * n_ctx
    tm_lat = 512
    ctx_group = lambda i: 0
    lat_group_in = lambda i: 1 + i
    lat_group_out = lambda i: 1 + (i * tm_lat) // n_lat
    ctx_tiles = t_ctx // TOK_TILE
    all_group_comb = lambda i: jnp.where(i < ctx_tiles, 0, 1 + ((i - ctx_tiles) * TOK_TILE) // n_lat)
    assert t_ctx % TOK_TILE == 0 and n_lat % TOK_TILE == 0

    xp = x_prompt.reshape(t_ctx, d)
    xs = x_sample.reshape(t_lat, d)
    caches = tuple(jnp.zeros((bsz, depth, n_ctx, w), F32) for w in (KV_WIDTH_A, KV_WIDTH_A, WIDTH_B, WIDTH_B))
    for l in range(depth):
        qa, qb, *caches, oc = _inproj(xp, mod, ctx_group, g_mix3, w_in_bf, l, norm_w, ones_bd, conv_w,
                                      tm_ctx, n_ctx, caches=caches)
        oa, ob = _ctx_attention(sink_win[l], qa, caches[0], caches[1], qb, caches[2], caches[3], n_ctx, l)
        shared = (jnp.zeros((t_all, d), BF16), jnp.zeros((N_EXPERTS, t_all), F32)) if l == 0 else (h_all, lg_all)
        xp_mid, h_all, lg_all = _outproj(xp, oa, ob, oc, w_out_bf, mod, ctx_group, g_ffn3, w_router_t, b_router_t, l,
                                         tm_ctx, t_all, 0, shared)

        qa, qb, ka, va, kb, vb, oc = _inproj(xs, mod, lat_group_in, g_mix3, w_in_bf, l, norm_w, ones_bd, conv_w,
                                             n_lat, n_lat, rope_tabs)
        oa = _win_attention(sink_win[l], qa, ka, va, ck_win, cv_win, l, n_lat)
        ob = _nbr_attention(qb, kb, vb, ck_nbr, cv_nbr, _nbr_bias_table(rel_bias_nbr[l], n_lat // GRID_W), l, n_lat)
        xs_mid, h_all, lg_all = _outproj(xs, oa, ob, oc, w_out_bf, mod, lat_group_out, g_ffn3, w_router_t, b_router_t,
                                         l, tm_lat, t_all, t_ctx, shared=(h_all, lg_all))

        gates, lpos, seg, n_used = _route(lg_all)
        x_sorted = _dispatch(seg, n_used, h_all, lpos, t_all * TOP_K // MOE_BLK + N_EXPERTS)
        y_sorted = _experts(seg, n_used, x_sorted, w_gate, b_gate, w_up, b_up, w_down, b_down, l)
        lpos_t = lpos.T
        xp, xs = _combine(seg, xp_mid, xs_mid, y_sorted, lpos, lpos_t, gates, mod, all_group_comb, l)

    new_k_win, new_v_win = (a.reshape(bsz, depth, n_ctx, N_KV_A, HEAD_DIM) for a in caches[:2])
    new_k_nbr, new_v_nbr = (a.reshape(bsz, depth, n_ctx, N_HEADS_B, HEAD_DIM) for a in caches[2:])
    return (xp.reshape(bsz, n_ctx, d), xs.reshape(dbs, n_lat, d), new_k_win, new_v_win, new_k_nbr, new_v_nbr)
```

```python
import functools

import numpy as np
import jax
import jax.numpy as jnp
from jax import lax
from jax.experimental import pallas as pl
from jax.experimental.pallas import tpu as pltpu

F32 = jnp.float32
BF16 = jnp.bfloat16

D_MODEL = 1024
HEAD_DIM = 64
GRID_W = 64
N_HEADS_A = 8
N_KV_A = 2
GROUP_A = N_HEADS_A // N_KV_A
WINDOW = 128
Q_BLK = 128
N_HEADS_B = 4
NA_ROWS = 8
NA_COLS = 16
CONV_CH = 256
CONV_W = 3
WIDTH_A = N_HEADS_A * HEAD_DIM
KV_WIDTH_A = N_KV_A * HEAD_DIM
WIDTH_B = N_HEADS_B * HEAD_DIM
IN_COLS = WIDTH_A + 2 * KV_WIDTH_A + 3 * WIDTH_B + 3 * CONV_CH
N_EXPERTS = 32
TOP_K = 4
SWIGLU_LIMIT = 7.0
SWIGLU_ALPHA = 1.702
ROPE_BASE = 10000.0
EPS = 1e-6
NEG = -1e30
N_MOD = 6

C_QA = 0
C_KA = C_QA + WIDTH_A
C_VA = C_KA + KV_WIDTH_A
C_QB = C_VA + KV_WIDTH_A
C_KB = C_QB + WIDTH_B
C_VB = C_KB + WIDTH_B
C_U = C_VB + WIDTH_B
C_GB = C_U + CONV_CH
C_GC = C_GB + CONV_CH

MXU_COLS_V7X = 256
COND_ROWS = 8
MOE_BLK = 256
MOE_SHIFT = MOE_BLK.bit_length() - 1
assert 1 << MOE_SHIFT == MOE_BLK
VMEM_LIMIT_V7X = 56 * 1024 * 1024

_NT = (((1,), (1,)), ((), ()))


def _params(sem, vmem=None):
    return pltpu.CompilerParams(dimension_semantics=sem, vmem_limit_bytes=vmem)


def _mod_kernel(c_ref, w_ref, b_ref, o_ref):
    c = c_ref[...]
    s = c * jax.nn.sigmoid(c)
    o_ref[...] = jnp.dot(s.astype(BF16), w_ref[...].astype(BF16), preferred_element_type=F32) + b_ref[...]


def _modulation(cond, w_mod, b_mod):
    depth = w_mod.shape[0]
    out = pl.pallas_call(
        _mod_kernel,
        out_shape=jax.ShapeDtypeStruct((depth, COND_ROWS, N_MOD * D_MODEL), F32),
        grid=(depth, N_MOD),
        in_specs=[
            pl.BlockSpec((COND_ROWS, D_MODEL), lambda l, j: (0, 0)),
            pl.BlockSpec((None, D_MODEL, D_MODEL), lambda l, j: (l, 0, j)),
            pl.BlockSpec((None, 1, D_MODEL), lambda l, j: (l, 0, j)),
        ],
        out_specs=pl.BlockSpec((None, COND_ROWS, D_MODEL), lambda l, j: (l, 0, j)),
        compiler_params=_params(("arbitrary", "arbitrary")),
        name="modulation",
    )(cond, w_mod, b_mod.reshape(depth, 1, N_MOD * D_MODEL))
    return out.reshape(depth, COND_ROWS, N_MOD, D_MODEL)


def _head_norm(x, w_row, ones_ref):
    width = x.shape[1]
    sq = (x * x).astype(BF16)
    parts = []
    for c0 in range(0, width, MXU_COLS_V7X):
        wd = min(MXU_COLS_V7X, width - c0)
        parts.append(jnp.dot(sq[:, c0:c0 + wd], ones_ref[:wd, :wd], preferred_element_type=F32))
    ss = parts[0] if len(parts) == 1 else jnp.concatenate(parts, axis=-1)
    return x * lax.rsqrt(ss * (1.0 / HEAD_DIM) + EPS) * w_row


def _rope(x, cos, sin):
    width = x.shape[1]
    lane = lax.broadcasted_iota(jnp.int32, x.shape, 1)
    quarter = HEAD_DIM // 4
    partner = jnp.where((lane % (2 * quarter)) < quarter,
                        pltpu.roll(x, width - quarter, axis=1), pltpu.roll(x, quarter, axis=1))
    reps = width // cos.shape[1]
    cos_w = cos if reps == 1 else jnp.concatenate([cos] * reps, axis=-1)
    sin_w = sin if reps == 1 else jnp.concatenate([sin] * reps, axis=-1)
    return x * cos_w + partner * sin_w


def _inproj_kernel(*refs, seq_len, rope):
    x_ref, mod_ref, g_ref, w_ref, nw_ref, ones_ref, cw_ref = refs[:7]
    qa_o, qb_o, ka_o, va_o, kb_o, vb_o, oc_o = refs[-7:]
    if rope:
        cos_ref, sin_ref = refs[7:9]
    x = x_ref[...]
    tm = x.shape[0]
    h = x * lax.rsqrt(jnp.mean(x * x, axis=-1, keepdims=True) + EPS) * g_ref[...]
    h = h * (1.0 + mod_ref[1:2, :]) + mod_ref[0:1, :]
    p = jnp.dot(h.astype(BF16), w_ref[...], preferred_element_type=F32)

    qa = _head_norm(p[:, C_QA:C_KA], nw_ref[:, 0:WIDTH_A], ones_ref)
    ka = _head_norm(p[:, C_KA:C_VA], nw_ref[:, WIDTH_A:WIDTH_A + KV_WIDTH_A], ones_ref)
    o_qb = WIDTH_A + KV_WIDTH_A
    qb = _head_norm(p[:, C_QB:C_KB], nw_ref[:, o_qb:o_qb + WIDTH_B], ones_ref)
    kb = _head_norm(p[:, C_KB:C_VB], nw_ref[:, o_qb + WIDTH_B:o_qb + 2 * WIDTH_B], ones_ref)
    if rope:
        cos, sin = cos_ref[...], sin_ref[...]
        qa = _rope(qa, cos, sin)
        ka = _rope(ka, cos, sin)
    qa_o[...] = qa.astype(BF16)
    qb_o[...] = qb.astype(BF16)
    ka_o[...] = ka.reshape(ka_o.shape)
    va_o[...] = p[:, C_VA:C_QB].reshape(va_o.shape)
    kb_o[...] = kb.reshape(kb_o.shape)
    vb_o[...] = p[:, C_VB:C_U].reshape(vb_o.shape)

    z = p[:, C_GC:C_GC + CONV_CH] * p[:, C_U:C_GB]
    row = lax.broadcasted_iota(jnp.int32, z.shape, 0) % seq_len
    z_prev = jnp.where(row == 0, 0.0, pltpu.roll(z, 1, axis=0))
    z_next = jnp.where(row == seq_len - 1, 0.0, pltpu.roll(z, tm - 1, axis=0))
    y = z_prev * cw_ref[0:1, :] + z * cw_ref[1:2, :] + z_next * cw_ref[2:3, :]
    oc_o[...] = (p[:, C_GB:C_GC] * y).astype(BF16)


def _inproj(x, mod, group_of_tile, g_mix, w_in_bf, layer, norm_w, ones_bd, conv_w, tm, seq_len, rope_tabs=None,
            caches=None):
    t = x.shape[0]
    rope = rope_tabs is not None
    in_specs = [
        pl.BlockSpec((tm, D_MODEL), lambda i: (i, 0)),
        pl.BlockSpec((None, None, N_MOD, D_MODEL), lambda i: (layer, group_of_tile(i), 0, 0)),
        pl.BlockSpec((None, 1, D_MODEL), lambda i: (layer, 0, 0)),
        pl.BlockSpec((None, D_MODEL, IN_COLS), lambda i: (layer, 0, 0)),
        pl.BlockSpec((None, 1, norm_w.shape[-1]), lambda i: (layer, 0, 0)),
        pl.BlockSpec(ones_bd.shape, lambda i: (0, 0)),
        pl.BlockSpec((None, CONV_W, CONV_CH), lambda i: (layer, 0, 0)),
    ]
    args = [x, mod, g_mix, w_in_bf, norm_w, ones_bd, conv_w]
    if rope:
        in_specs += [pl.BlockSpec(rope_tabs[0].shape, lambda i: (0, 0))] * 2
        args += list(rope_tabs)
    widths = (WIDTH_A, WIDTH_B, KV_WIDTH_A, KV_WIDTH_A, WIDTH_B, WIDTH_B, CONV_CH)
    dtypes = (BF16, BF16, F32, F32, F32, F32, BF16)
    out_shape = [jax.ShapeDtypeStruct((t, w), dt) for w, dt in zip(widths, dtypes)]
    out_specs = [pl.BlockSpec((tm, w), lambda i: (i, 0)) for w in widths]
    aliases = {}
    if caches is not None:
        seqs = tm // seq_len
        for j, cache in enumerate(caches):
            o = 2 + j
            aliases[len(args)] = o
            in_specs.append(pl.BlockSpec(memory_space=pl.ANY))
            args.append(cache)
            out_shape[o] = jax.ShapeDtypeStruct(cache.shape, F32)
            out_specs[o] = pl.BlockSpec((seqs, None, seq_len, widths[o]), lambda i: (i, layer, 0, 0))
    return pl.pallas_call(
        functools.partial(_inproj_kernel, seq_len=seq_len, rope=rope),
        out_shape=out_shape,
        grid=(t // tm,),
        in_specs=in_specs,
        out_specs=out_specs,
        input_output_aliases=aliases,
        compiler_params=_params(("arbitrary",), VMEM_LIMIT_V7X),
        name="inproj_rope" if rope else "inproj",
    )(*args)


def _pair_attention(q2, k_lo, k_hi, v_lo, v_hi, sinks=None, adjust=None):
    m = k_lo.shape[0]
    k2 = jnp.concatenate([k_lo, k_hi], axis=0)
    v2 = jnp.concatenate([v_lo, v_hi], axis=0)
    s = lax.dot_general(q2, k2, _NT, preferred_element_type=F32)
    ps, dens = [], []
    for i in range(2):
        si = s[:, i * m:(i + 1) * m]
        if adjust is not None:
            si = adjust(i, si)
        mx = jnp.max(si, axis=-1, keepdims=True)
        if sinks is not None:
            mx = jnp.maximum(mx, sinks[i])
        p = jnp.exp(si - mx)
        den = jnp.sum(p, axis=-1, keepdims=True)
        if sinks is not None:
            den = den + jnp.exp(sinks[i] - mx)
        ps.append(p.astype(BF16))
        dens.append(den)
    o = jnp.dot(jnp.concatenate(ps, axis=-1), v2, preferred_element_type=F32)
    lane = lax.broadcasted_iota(jnp.int32, o.shape, 1)
    return o / jnp.where(lane < HEAD_DIM, dens[0], dens[1])


def _lane_halves(x):
    lane = lax.broadcasted_iota(jnp.int32, x.shape, 1)
    lo = lane < HEAD_DIM
    sw = pltpu.roll(x, HEAD_DIM, axis=1)
    z = jnp.zeros_like(x)
    return tuple(jnp.where(c, y, z).astype(BF16) for c, y in ((lo, x), (~lo, x), (lo, sw), (~lo, sw)))


def _ctx_attn_kernel(sink_ref, qa_ref, ka_ref, va_ref, qb_ref, kb_ref, vb_ref, oa_o, ob_o):
    scale = HEAD_DIM ** -0.5
    k0l, k1h, k1l, k0h = _lane_halves(ka_ref[...])
    v0l, v1h, v1l, v0h = _lane_halves(va_ref[...])
    for i in range(N_HEADS_A // 2):
        cols = slice(2 * i * HEAD_DIM, (2 * i + 2) * HEAD_DIM)
        kv = (k0l, k0h, v0l, v0h) if (2 * i) // GROUP_A == 0 else (k1l, k1h, v1l, v1h)
        o = _pair_attention(qa_ref[:, cols] * scale, *kv, sinks=(sink_ref[2 * i], sink_ref[2 * i + 1]))
        oa_o[:, cols] = o.astype(oa_o.dtype)
    for i in range(N_HEADS_B // 2):
        cols = slice(2 * i * HEAD_DIM, (2 * i + 2) * HEAD_DIM)
        kl, kh, _, _ = _lane_halves(kb_ref[:, cols])
        vl, vh, _, _ = _lane_halves(vb_ref[:, cols])
        o = _pair_attention(qb_ref[:, cols] * scale, kl, kh, vl, vh)
        ob_o[:, cols] = o.astype(ob_o.dtype)


def _ctx_attention(sink, qa, ka, va, qb, kb, vb, seq_len, layer):
    t = qa.shape[0]
    q_spec = lambda w: pl.BlockSpec((seq_len, w), lambda i: (i, 0))
    kv_spec = lambda w: pl.BlockSpec((None, None, seq_len, w), lambda i: (i, layer, 0, 0))
    return pl.pallas_call(
        _ctx_attn_kernel,
        out_shape=[jax.ShapeDtypeStruct((t, WIDTH_A), BF16), jax.ShapeDtypeStruct((t, WIDTH_B), BF16)],
        grid=(t // seq_len,),
        in_specs=[pl.BlockSpec(memory_space=pltpu.SMEM), q_spec(WIDTH_A), kv_spec(KV_WIDTH_A), kv_spec(KV_WIDTH_A),
                  q_spec(WIDTH_B), kv_spec(WIDTH_B), kv_spec(WIDTH_B)],
        out_specs=[pl.BlockSpec((seq_len, WIDTH_A), lambda i: (i, 0)),
                   pl.BlockSpec((seq_len, WIDTH_B), lambda i: (i, 0))],
        compiler_params=_params(("arbitrary",)),
        name="ctx_attention",
    )(sink, qa, ka, va, qb, kb, vb)


def _win_attn_kernel(sink_ref, qa_ref, ka_ref, va_ref, ck_ref, cv_ref, oa_o, *, n_lat):
    scale = HEAD_DIM ** -0.5
    span = Q_BLK + 2 * WINDOW
    n = pl.program_id(1)
    start = pl.multiple_of(jnp.clip(n * Q_BLK - WINDOW, 0, n_lat - span), Q_BLK)
    k0l, k1h, k1l, k0h = _lane_halves(jnp.concatenate([ka_ref[pl.ds(start, span), :], ck_ref[...]], axis=0))
    v0l, v1h, v1l, v0h = _lane_halves(jnp.concatenate([va_ref[pl.ds(start, span), :], cv_ref[...]], axis=0))
    n_keys = k0l.shape[0]
    qpos = n * Q_BLK + lax.broadcasted_iota(jnp.int32, (Q_BLK, n_keys), 0)
    col = lax.broadcasted_iota(jnp.int32, (Q_BLK, n_keys), 1)
    ok = (col >= span) | (jnp.abs(qpos - (start + col)) <= WINDOW)
    for i in range(N_HEADS_A // 2):
        cols = slice(2 * i * HEAD_DIM, (2 * i + 2) * HEAD_DIM)
        kv = (k0l, k0h, v0l, v0h) if (2 * i) // GROUP_A == 0 else (k1l, k1h, v1l, v1h)
        o = _pair_attention(qa_ref[:, cols] * scale, *kv, sinks=(sink_ref[2 * i], sink_ref[2 * i + 1]),
                            adjust=lambda _, s: jnp.where(ok, s, NEG))
        oa_o[:, cols] = o.astype(oa_o.dtype)


def _win_attention(sink, qa, ka, va, cache_k, cache_v, layer, n_lat):
    t = qa.shape[0]
    nb = n_lat // Q_BLK
    past = cache_k.shape[2]
    return pl.pallas_call(
        functools.partial(_win_attn_kernel, n_lat=n_lat),
        out_shape=jax.ShapeDtypeStruct((t, WIDTH_A), BF16),
        grid=(t // n_lat, nb),
        in_specs=[
            pl.BlockSpec(memory_space=pltpu.SMEM),
            pl.BlockSpec((Q_BLK, WIDTH_A), lambda b, n: (b * nb + n, 0)),
            pl.BlockSpec((n_lat, KV_WIDTH_A), lambda b, n: (b, 0)),
            pl.BlockSpec((n_lat, KV_WIDTH_A), lambda b, n: (b, 0)),
            pl.BlockSpec((None, None, past, KV_WIDTH_A), lambda b, n: (b, layer, 0, 0)),
            pl.BlockSpec((None, None, past, KV_WIDTH_A), lambda b, n: (b, layer, 0, 0)),
        ],
        out_specs=pl.BlockSpec((Q_BLK, WIDTH_A), lambda b, n: (b * nb + n, 0)),
        compiler_params=_params(("arbitrary", "arbitrary")),
        name="window_attention",
    )(sink, qa, ka, va, cache_k, cache_v)


def _nbr_attn_kernel(qb_ref, kb_ref, vb_ref, ck_ref, cv_ref, bias_ref, ob_o, *, rows, kr):
    scale = HEAD_DIM ** -0.5
    r = pl.program_id(1)
    start = pl.multiple_of(jnp.clip(r - kr // 2, 0, rows - kr) * GRID_W, GRID_W)
    nwin = kr * GRID_W
    kcat = jnp.concatenate([kb_ref[pl.ds(start, nwin), :], ck_ref[...]], axis=0)
    vcat = jnp.concatenate([vb_ref[pl.ds(start, nwin), :], cv_ref[...]], axis=0)
    past = ck_ref.shape[0]
    for i in range(N_HEADS_B // 2):
        cols = slice(2 * i * HEAD_DIM, (2 * i + 2) * HEAD_DIM)
        kl, kh, _, _ = _lane_halves(kcat[:, cols])
        vl, vh, _, _ = _lane_halves(vcat[:, cols])
        o = _pair_attention(
            qb_ref[:, cols] * scale, kl, kh, vl, vh,
            adjust=lambda hh, s, i=i: s + jnp.concatenate([bias_ref[2 * i + hh], jnp.zeros((GRID_W, past), F32)],
                                                          axis=-1))
        ob_o[:, cols] = o.astype(ob_o.dtype)


def _nbr_bias_table(rel_bias, rows):
    kr = min(NA_ROWS, rows)
    cls = np.arange(kr)[:, None]
    m = np.arange(kr)[None, :]
    row_sel = (m - cls + NA_ROWS - 1)[:, :, None] == np.arange(2 * NA_ROWS - 1)[None, None, :]
    c = np.arange(GRID_W)[:, None]
    kc = np.arange(GRID_W)[None, :]
    dc = np.clip(kc - c, -(NA_COLS - 1), NA_COLS - 1) + NA_COLS - 1
    col_sel = dc[:, :, None] == np.arange(2 * NA_COLS - 1)[None, None, :]
    col_start = np.clip(c - NA_COLS // 2, 0, GRID_W - NA_COLS)
    ok = (kc >= col_start) & (kc < col_start + NA_COLS)
    tab = jnp.einsum('hrd,ymr,ckd->hycmk', rel_bias.astype(F32), row_sel.astype(np.float32),
                     col_sel.astype(np.float32), precision=lax.Precision.HIGHEST)
    tab = jnp.where(ok[None, None, :, None, :], tab, NEG)
    return tab.reshape(rel_bias.shape[0], kr, GRID_W, kr * GRID_W)


def _nbr_attention(qb, kb, vb, cache_k, cache_v, bias_tab, layer, n_lat):
    t = qb.shape[0]
    rows = n_lat // GRID_W
    kr = min(NA_ROWS, rows)
    past = cache_k.shape[2]

    def bias_map(b, r):
        return (layer, r - jnp.clip(r - kr // 2, 0, rows - kr), 0, 0)

    return pl.pallas_call(
        functools.partial(_nbr_attn_kernel, rows=rows, kr=kr),
        out_shape=jax.ShapeDtypeStruct((t, WIDTH_B), BF16),
        grid=(t // n_lat, rows),
        in_specs=[
            pl.BlockSpec((GRID_W, WIDTH_B), lambda b, r: (b * rows + r, 0)),
            pl.BlockSpec((n_lat, WIDTH_B), lambda b, r: (b, 0)),
            pl.BlockSpec((n_lat, WIDTH_B), lambda b, r: (b, 0)),
            pl.BlockSpec((None, None, past, WIDTH_B), lambda b, r: (b, layer, 0, 0)),
            pl.BlockSpec((None, None, past, WIDTH_B), lambda b, r: (b, layer, 0, 0)),
            pl.BlockSpec((N_HEADS_B, None, GRID_W, kr * GRID_W), bias_map),
        ],
        out_specs=pl.BlockSpec((GRID_W, WIDTH_B), lambda b, r: (b * rows + r, 0)),
        compiler_params=_params(("arbitrary", "arbitrary")),
        name="neighbourhood_attention",
    )(qb, kb, vb, cache_k, cache_v, bias_tab)


ROW_TILE = 8
LANES = 128
assert ROW_TILE * LANES == D_MODEL


PACK_TILE = ROW_TILE // 2


def _store_row_tiles(ref, row0, x):
    n = x.shape[0]
    rt = x.shape[1] // LANES
    for c in range(rt):
        ref[pl.ds(row0 * rt + c, n, stride=rt), :] = x[:, c * LANES:(c + 1) * LANES]


def _pack_bf16_pairs(x):
    bits = lax.bitcast_convert_type(x, jnp.uint32)
    half = x.shape[1] // 2
    return (bits[:, half:] & jnp.uint32(0xFFFF0000)) | (bits[:, :half] >> 16)


def _unpack_bf16_pairs(w):
    lo = lax.bitcast_convert_type(w << 16, F32)
    hi = lax.bitcast_convert_type(w & jnp.uint32(0xFFFF0000), F32)
    return jnp.concatenate([lo, hi], axis=-1).astype(BF16)


def _load_row_tiles(ref, row0, n, rt=ROW_TILE):
    return jnp.concatenate([ref[pl.ds(row0 * rt + c, n, stride=rt), :] for c in range(rt)], axis=-1)


def _outproj_kernel(x_ref, oa_ref, ob_ref, oc_ref, w_ref, mod_ref, g_ref, wr_ref, br_ref, *rest):
    xn_o, h_o, lg_o = rest[-3:]
    mix = jnp.dot(oa_ref[...], w_ref[0:WIDTH_A, :], preferred_element_type=F32)
    mix += jnp.dot(ob_ref[...], w_ref[WIDTH_A:WIDTH_A + WIDTH_B, :], preferred_element_type=F32)
    mix += jnp.dot(oc_ref[...], w_ref[WIDTH_A + WIDTH_B:, :], preferred_element_type=F32)
    xn = x_ref[...] + mod_ref[2:3, :] * mix
    xn_o[...] = xn
    h = xn * lax.rsqrt(jnp.mean(xn * xn, axis=-1, keepdims=True) + EPS) * g_ref[...]
    h = h * (1.0 + mod_ref[4:5, :]) + mod_ref[3:4, :]
    h_hi = h.astype(BF16)
    h_o[...] = h_hi
    h_lo = (h - h_hi.astype(F32)).astype(BF16)
    wr = wr_ref[...]
    w_hi = wr.astype(BF16)
    w_lo = (wr - w_hi.astype(F32)).astype(BF16)
    n_e = wr.shape[0]
    both = lax.dot_general(jnp.concatenate([w_hi, w_lo], axis=0), h_hi, _NT, preferred_element_type=F32)
    lg = both[:n_e] + both[n_e:] + lax.dot_general(w_hi, h_lo, _NT, preferred_element_type=F32)
    lg_o[...] = lg + br_ref[...]


def _outproj(x, oa, ob, oc, w_out_bf, mod, group_of_tile, g_ffn, w_router_t, b_router_t, layer, tm, t_all, tok0,
             shared):
    t = x.shape[0]
    b0 = tok0 // tm
    assert tok0 % tm == 0
    in_specs = [
        pl.BlockSpec((tm, D_MODEL), lambda i: (i, 0)),
        pl.BlockSpec((tm, WIDTH_A), lambda i: (i, 0)),
        pl.BlockSpec((tm, WIDTH_B), lambda i: (i, 0)),
        pl.BlockSpec((tm, CONV_CH), lambda i: (i, 0)),
        pl.BlockSpec((None, D_MODEL, D_MODEL), lambda i: (layer, 0, 0)),
        pl.BlockSpec((None, None, N_MOD, D_MODEL), lambda i: (layer, group_of_tile(i), 0, 0)),
        pl.BlockSpec((None, 1, D_MODEL), lambda i: (layer, 0, 0)),
        pl.BlockSpec((None, N_EXPERTS, D_MODEL), lambda i: (layer, 0, 0)),
        pl.BlockSpec((None, N_EXPERTS, 1), lambda i: (layer, 0, 0)),
    ]
    args = [x, oa, ob, oc, w_out_bf, mod, g_ffn, w_router_t, b_router_t]
    aliases = {len(args): 1, len(args) + 1: 2}
    in_specs += [pl.BlockSpec(memory_space=pl.ANY)] * 2
    args += list(shared)
    return pl.pallas_call(
        _outproj_kernel,
        out_shape=[jax.ShapeDtypeStruct((t, D_MODEL), F32), jax.ShapeDtypeStruct((t_all, D_MODEL), BF16),
                   jax.ShapeDtypeStruct((N_EXPERTS, t_all), F32)],
        grid=(t // tm,),
        in_specs=in_specs,
        out_specs=[pl.BlockSpec((tm, D_MODEL), lambda i: (i, 0)),
                   pl.BlockSpec((tm, D_MODEL), lambda i: (b0 + i, 0)),
                   pl.BlockSpec((N_EXPERTS, tm), lambda i: (0, b0 + i))],
        input_output_aliases=aliases,
        compiler_params=_params(("arbitrary",), VMEM_LIMIT_V7X),
        name="outproj_router",
    )(*args)


X_AHEAD = 3
X_SLOTS = X_AHEAD + 1
Y_SLOTS = 2


def _expert_kernel(seg_ref, n_used_ref, x_hbm, wg_ref, bg_ref, wu_ref, bu_ref, wd_ref, bd_ref, y_hbm,
                   x_buf, y_buf, xsem, ysem, zsem, *, n_blocks):
    e = pl.program_id(0)
    n_used = n_used_ref[0]
    blk_rows = MOE_BLK * ROW_TILE
    first_blk = seg_ref[SEG_DST, e, SEG_BLK]
    n_blk = seg_ref[SEG_CNT, e, SEG_BLK]

    def rows_of(g, rows=blk_rows):
        return pl.ds(pl.multiple_of(g * rows, rows), rows)

    def x_copy(g):
        return pltpu.make_async_copy(x_hbm.at[rows_of(g, MOE_BLK * PACK_TILE)], x_buf.at[g % X_SLOTS],
                                     xsem.at[g % X_SLOTS])

    def y_copy(g):
        return pltpu.make_async_copy(y_buf.at[g % Y_SLOTS], y_hbm.at[rows_of(g)], ysem.at[g % Y_SLOTS])

    def zero_copy(g):
        return pltpu.make_async_copy(y_buf.at[0], y_hbm.at[rows_of(g)], zsem)

    @pl.when(e == 0)
    def _():
        for g in range(X_AHEAD):
            @pl.when(g < n_used)
            def _():
                x_copy(g).start()

    def block(b, carry):
        g = first_blk + b

        @pl.when(g + X_AHEAD < n_used)
        def _():
            x_copy(g + X_AHEAD).start()

        x_copy(g).wait()

        @pl.when(g >= Y_SLOTS)
        def _():
            y_copy(g - Y_SLOTS).wait()

        x = _unpack_bf16_pairs(_load_row_tiles(x_buf.at[g % X_SLOTS], 0, MOE_BLK, PACK_TILE))
        gt = jnp.dot(x, wg_ref[...].astype(BF16), preferred_element_type=F32) + bg_ref[...]
        up = jnp.dot(x, wu_ref[...].astype(BF16), preferred_element_type=F32) + bu_ref[...]
        gt = jnp.minimum(gt, SWIGLU_LIMIT)
        up = jnp.clip(up, -SWIGLU_LIMIT, SWIGLU_LIMIT)
        a = gt * jax.nn.sigmoid(SWIGLU_ALPHA * gt) * (up + 1.0)
        y = jnp.dot(a.astype(BF16), wd_ref[...].astype(BF16), preferred_element_type=F32) + bd_ref[...]
        _store_row_tiles(y_buf.at[g % Y_SLOTS], 0, y)
        y_copy(g).start()
        return carry

    lax.fori_loop(0, n_blk, block, 0)

    @pl.when(e == pl.num_programs(0) - 1)
    def _():
        for d in range(Y_SLOTS, 0, -1):
            @pl.when(n_used >= d)
            def _():
                y_copy(n_used - d).wait()

        y_buf[0] = jnp.zeros((blk_rows, LANES), F32)

        def start_zero(g, carry):
            zero_copy(g).start()
            return carry

        def wait_zero(g, carry):
            zero_copy(g).wait()
            return carry

        lax.fori_loop(n_used, n_blocks, start_zero, 0)
        lax.fori_loop(n_used, n_blocks, wait_zero, 0)


def _experts(seg, n_used, x_sorted, w_gate, b_gate, w_up, b_up, w_down, b_down, layer):
    n_blocks = x_sorted.shape[0] // (MOE_BLK * PACK_TILE)
    wspec = pl.BlockSpec((None, None, D_MODEL, D_MODEL), lambda e, sg, nu: (layer, e, 0, 0))
    bspec = pl.BlockSpec((None, None, 1, D_MODEL), lambda e, sg, nu: (layer, e, 0, 0))
    anyspec = pl.BlockSpec(memory_space=pl.ANY)
    depth = w_gate.shape[0]
    b4 = lambda b: b.reshape(depth, N_EXPERTS, 1, D_MODEL)
    return pl.pallas_call(
        functools.partial(_expert_kernel, n_blocks=n_blocks),
        out_shape=jax.ShapeDtypeStruct((n_blocks * MOE_BLK * ROW_TILE, LANES), F32),
        grid_spec=pltpu.PrefetchScalarGridSpec(
            num_scalar_prefetch=2,
            grid=(N_EXPERTS,),
            in_specs=[anyspec, wspec, bspec, wspec, bspec, wspec, bspec],
            out_specs=anyspec,
            scratch_shapes=[
                pltpu.VMEM((X_SLOTS, MOE_BLK * PACK_TILE, LANES), jnp.uint32),
                pltpu.VMEM((Y_SLOTS, MOE_BLK * ROW_TILE, LANES), F32),
                pltpu.SemaphoreType.DMA((X_SLOTS,)),
                pltpu.SemaphoreType.DMA((Y_SLOTS,)),
                pltpu.SemaphoreType.DMA,
            ],
        ),
        compiler_params=_params(("arbitrary",), VMEM_LIMIT_V7X),
        name="experts",
    )(seg, n_used, x_sorted, w_gate, b4(b_gate), w_up, b4(b_up), w_down, b4(b_down))


CUM_CHUNK = 256


def _sublane_cumsum(x):
    row = lax.broadcasted_iota(jnp.int32, x.shape, 0)
    d = 1
    while d < x.shape[0]:
        x = x + jnp.where(row >= d, pltpu.roll(x, d, axis=0), 0)
        d *= 2
    return x


TOK_TILE = CUM_CHUNK
TILE_ROWS = TOP_K * TOK_TILE
SEG_LANES = 128
SEG_PAD_FROM = SEG_LANES - 2
SEG_PAD_LEN = SEG_LANES - 1
SEG_BLK = SEG_LANES - 3
SEG_SRC, SEG_CNT, SEG_DST = 0, 1, 2


def _route_kernel(lg_ref, tri_ref, gate_o, lpos_o, seg_o, used_o):
    lg = lg_ref[...]
    n_e, t = lg.shape
    e_iota = lax.broadcasted_iota(jnp.int32, lg.shape, 0)
    work = lg
    tops, hots = [], []
    for _ in range(TOP_K):
        m = jnp.max(work, axis=0, keepdims=True)
        first = jnp.min(jnp.where(work == m, e_iota, n_e), axis=0, keepdims=True)
        hot = e_iota == first
        work = jnp.where(hot, -jnp.inf, work)
        tops.append(m)
        hots.append(hot)
    ex = [jnp.exp(m - tops[0]) for m in tops]
    den = ex[0] + ex[1] + ex[2] + ex[3]
    for k in range(TOP_K):
        gate_o[k:k + 1, :] = ex[k] / den

    chosen = jnp.where(hots[0] | hots[1] | hots[2] | hots[3], 1.0, 0.0)
    tri = tri_ref[...]
    lane = lax.broadcasted_iota(jnp.int32, (n_e, SEG_LANES), 1)
    seg_cnt = jnp.zeros((n_e, SEG_LANES), jnp.int32)
    seg_before = jnp.zeros((n_e, SEG_LANES), jnp.int32)
    carry = jnp.zeros((n_e, 1), F32)
    rank_in_tile = []
    n_tiles = t // TOK_TILE
    for c in range(n_tiles):
        chunk = chosen[:, c * TOK_TILE:(c + 1) * TOK_TILE]
        inc = jnp.dot(chunk.astype(BF16), tri, preferred_element_type=F32)
        cnt = inc[:, TOK_TILE - 1:TOK_TILE]
        rank_in_tile.append((inc - chunk).astype(jnp.int32))
        seg_cnt = jnp.where(lane == c, cnt.astype(jnp.int32), seg_cnt)
        seg_before = jnp.where(lane == c, carry.astype(jnp.int32), seg_before)
        carry = carry + cnt

    counts = jnp.broadcast_to(carry.astype(jnp.int32), (n_e, SEG_LANES))
    padded = ((counts + (MOE_BLK - 1)) >> MOE_SHIFT) << MOE_SHIFT
    pad_end = _sublane_cumsum(padded)
    pad_start = pad_end - padded
    seg_src = _sublane_cumsum(seg_cnt) - seg_cnt
    seg_o[SEG_SRC] = seg_src
    seg_o[SEG_CNT] = jnp.where(lane == SEG_PAD_LEN, padded - counts,
                               jnp.where(lane == SEG_BLK, padded >> MOE_SHIFT, seg_cnt))
    seg_o[SEG_DST] = jnp.where(lane == SEG_PAD_FROM, pad_start + counts,
                               jnp.where(lane == SEG_BLK, pad_start >> MOE_SHIFT, pad_start + seg_before))

    for c in range(n_tiles):
        pos = seg_src[:, c:c + 1] + rank_in_tile[c]
        for k in range(TOP_K):
            lpos_o[k:k + 1, c * TOK_TILE:(c + 1) * TOK_TILE] = jnp.sum(
                jnp.where(hots[k][:, c * TOK_TILE:(c + 1) * TOK_TILE], pos, 0), axis=0, keepdims=True)

    used_o[...] = jnp.broadcast_to(pad_end[n_e - 1:n_e, :] >> MOE_SHIFT, used_o.shape)


def _route(logits):
    t = logits.shape[1]
    assert t % TOK_TILE == 0 and t // TOK_TILE <= SEG_BLK
    tri = jnp.asarray(np.triu(np.ones((CUM_CHUNK, CUM_CHUNK), np.float32)), dtype=BF16)
    gates, lpos, seg, used = pl.pallas_call(
        _route_kernel,
        out_shape=[jax.ShapeDtypeStruct((TOP_K, t), F32), jax.ShapeDtypeStruct((TOP_K, t), jnp.int32),
                   jax.ShapeDtypeStruct((3, N_EXPERTS, SEG_LANES), jnp.int32),
                   jax.ShapeDtypeStruct((8, SEG_LANES), jnp.int32)],
        compiler_params=_params(None, VMEM_LIMIT_V7X),
        name="route",
    )(logits, tri)
    return gates, lpos, seg, used[0, :1]


def _segment_copies(src, dst, src0, dst0, cnt, sem, wait, rt=ROW_TILE):
    @pl.when(cnt > 0)
    def _():
        size = cnt * rt
        cp = pltpu.make_async_copy(
            src.at[pl.ds(pl.multiple_of(src0 * rt, rt), size)],
            dst.at[pl.ds(pl.multiple_of(dst0 * rt, rt), size)], sem)
        if wait:
            cp.wait()
        else:
            cp.start()


Z_SLOTS = 3


def _dispatch_kernel(seg_ref, nu_ref, h_ref, lpos_ref, x_hbm, z_buf, zero_buf, sem, zsem, *, n_tiles, n_blocks):
    c = pl.program_id(0)
    slot = c % Z_SLOTS

    def wait_tile(s):
        pltpu.make_async_copy(z_buf.at[s], x_hbm.at[pl.ds(0, TILE_ROWS * PACK_TILE)], sem.at[s]).wait()

    def padding(wait):
        def per_expert(e, carry):
            _segment_copies(zero_buf, x_hbm, 0, seg_ref[SEG_DST, e, SEG_PAD_FROM], seg_ref[SEG_CNT, e, SEG_PAD_LEN],
                            zsem, wait, PACK_TILE)
            return carry

        lax.fori_loop(0, N_EXPERTS, per_expert, 0)

        def per_block(b, carry):
            cp = pltpu.make_async_copy(
                zero_buf, x_hbm.at[pl.ds(pl.multiple_of(b * (MOE_BLK * PACK_TILE), MOE_BLK * PACK_TILE),
                                         MOE_BLK * PACK_TILE)], zsem)
            if wait:
                cp.wait()
            else:
                cp.start()
            return carry

        lax.fori_loop(nu_ref[0], n_blocks, per_block, 0)

    @pl.when(c >= Z_SLOTS)
    def _():
        wait_tile(slot)

    row = lax.broadcasted_iota(jnp.int32, (TILE_ROWS, TOK_TILE), 0)
    hit = row == lpos_ref[0:1, :]
    for k in range(1, TOP_K):
        hit = hit | (row == lpos_ref[k:k + 1, :])
    z = jnp.dot(jnp.where(hit, 1.0, 0.0).astype(BF16), h_ref[...], preferred_element_type=F32)
    _store_row_tiles(z_buf.at[slot], 0, _pack_bf16_pairs(z))

    def per_expert(e, carry):
        _segment_copies(z_buf.at[slot], x_hbm, seg_ref[SEG_SRC, e, c], seg_ref[SEG_DST, e, c], seg_ref[SEG_CNT, e, c],
                        sem.at[slot], False, PACK_TILE)
        return carry

    lax.fori_loop(0, N_EXPERTS, per_expert, 0)

    @pl.when(c == n_tiles - 1)
    def _():
        zero_buf[...] = jnp.zeros(zero_buf.shape, jnp.uint32)
        padding(False)
        for back in range(min(Z_SLOTS, n_tiles) - 1, -1, -1):
            wait_tile((n_tiles - 1 - back) % Z_SLOTS)
        padding(True)


def _dispatch(seg, n_used, h_all, lpos, n_blocks):
    t = h_all.shape[0]
    n_tiles = t // TOK_TILE
    return pl.pallas_call(
        functools.partial(_dispatch_kernel, n_tiles=n_tiles, n_blocks=n_blocks),
        out_shape=jax.ShapeDtypeStruct((n_blocks * MOE_BLK * PACK_TILE, LANES), jnp.uint32),
        grid_spec=pltpu.PrefetchScalarGridSpec(
            num_scalar_prefetch=2,
            grid=(n_tiles,),
            in_specs=[pl.BlockSpec((TOK_TILE, D_MODEL), lambda c, sg, nu: (c, 0)),
                      pl.BlockSpec((TOP_K, TOK_TILE), lambda c, sg, nu: (0, c))],
            out_specs=pl.BlockSpec(memory_space=pl.ANY),
            scratch_shapes=[
                pltpu.VMEM((Z_SLOTS, TILE_ROWS * PACK_TILE, LANES), jnp.uint32),
                pltpu.VMEM((MOE_BLK * PACK_TILE, LANES), jnp.uint32),
                pltpu.SemaphoreType.DMA((Z_SLOTS,)),
                pltpu.SemaphoreType.DMA,
            ],
        ),
        compiler_params=_params(("arbitrary",), VMEM_LIMIT_V7X),
        name="dispatch",
    )(seg, n_used, h_all, lpos)


def _combine_kernel(seg_ref, xa_ref, xb_ref, y_hbm, lpos_ref, lpos_t_ref, gate_ref, mod_ref, oa_ref, ob_ref, y_buf, sem,
                    *, n_a, n_tiles):
    c = pl.program_id(0)
    slot = c % 2

    def fetch(cc):
        def per_expert(e, carry):
            _segment_copies(y_hbm, y_buf.at[cc % 2], seg_ref[SEG_DST, e, cc], seg_ref[SEG_SRC, e, cc],
                            seg_ref[SEG_CNT, e, cc], sem.at[cc % 2], False)
            return carry

        lax.fori_loop(0, N_EXPERTS, per_expert, 0)

    @pl.when(c == 0)
    def _():
        fetch(0)

    @pl.when(c + 1 < n_tiles)
    def _():
        fetch(c + 1)

    pltpu.make_async_copy(y_hbm.at[pl.ds(0, TILE_ROWS * ROW_TILE)], y_buf.at[slot], sem.at[slot]).wait()

    row = lax.broadcasted_iota(jnp.int32, (TILE_ROWS, TOK_TILE), 0)
    row_gate = jnp.zeros((TILE_ROWS, 1), F32)
    for k in range(TOP_K):
        row_gate += jnp.sum(jnp.where(row == lpos_ref[k:k + 1, :], gate_ref[k:k + 1, :], 0.0), axis=1, keepdims=True)
    ys = _load_row_tiles(y_buf.at[slot], 0, TILE_ROWS) * row_gate
    ys_hi = ys.astype(BF16)
    ys_lo = (ys - ys_hi.astype(F32)).astype(BF16)
    col = lax.broadcasted_iota(jnp.int32, (TOK_TILE, TILE_ROWS), 1)
    hit = col == lpos_t_ref[:, 0:1]
    for k in range(1, TOP_K):
        hit = hit | (col == lpos_t_ref[:, k:k + 1])
    u = jnp.where(hit, 1.0, 0.0).astype(BF16)
    mix = mod_ref[5:6, :] * (jnp.dot(u, ys_hi, preferred_element_type=F32)
                             + jnp.dot(u, ys_lo, preferred_element_type=F32))

    @pl.when(c < n_a)
    def _():
        oa_ref[...] = xa_ref[...] + mix

    @pl.when(c >= n_a)
    def _():
        ob_ref[...] = xb_ref[...] + mix


def _combine(seg, xa_mid, xb_mid, y_sorted, lpos, lpos_t, gates, mod, group_of_tile, layer):
    n_a = xa_mid.shape[0] // TOK_TILE
    n_tiles = n_a + xb_mid.shape[0] // TOK_TILE
    a_spec = pl.BlockSpec((TOK_TILE, D_MODEL), lambda c, sg: (jnp.minimum(c, n_a - 1), 0))
    b_spec = pl.BlockSpec((TOK_TILE, D_MODEL), lambda c, sg: (jnp.maximum(c - n_a, 0), 0))
    return pl.pallas_call(
        functools.partial(_combine_kernel, n_a=n_a, n_tiles=n_tiles),
        out_shape=[jax.ShapeDtypeStruct(xa_mid.shape, F32), jax.ShapeDtypeStruct(xb_mid.shape, F32)],
        grid_spec=pltpu.PrefetchScalarGridSpec(
            num_scalar_prefetch=1,
            grid=(n_tiles,),
            in_specs=[
                a_spec, b_spec,
                pl.BlockSpec(memory_space=pl.ANY),
                pl.BlockSpec((TOP_K, TOK_TILE), lambda c, sg: (0, c)),
                pl.BlockSpec((TOK_TILE, TOP_K), lambda c, sg: (c, 0)),
                pl.BlockSpec((TOP_K, TOK_TILE), lambda c, sg: (0, c)),
                pl.BlockSpec((None, None, N_MOD, D_MODEL), lambda c, sg: (layer, group_of_tile(c), 0, 0)),
            ],
            out_specs=[a_spec, b_spec],
            scratch_shapes=[pltpu.VMEM((2, TILE_ROWS * ROW_TILE, LANES), F32), pltpu.SemaphoreType.DMA((2,))],
        ),
        compiler_params=_params(("arbitrary",), VMEM_LIMIT_V7X),
        name="combine",
    )(seg, xa_mid, xb_mid, y_sorted, lpos, lpos_t, gates, mod)


def _rope_tables(n_lat):
    quarter = HEAD_DIM // 4
    t = np.arange(n_lat)
    inv = np.float32(ROPE_BASE) ** (-np.arange(quarter, dtype=np.float32) / np.float32(quarter))
    ang_r = (t // GRID_W).astype(np.float32)[:, None] * inv
    ang_c = (t % GRID_W).astype(np.float32)[:, None] * inv
    cos = np.concatenate([np.cos(ang_r)] * 2 + [np.cos(ang_c)] * 2, axis=-1)
    sin = np.concatenate([-np.sin(ang_r), np.sin(ang_r), -np.sin(ang_c), np.sin(ang_c)], axis=-1)
    return (jnp.asarray(np.concatenate([cos, cos], axis=-1), dtype=F32),
            jnp.asarray(np.concatenate([sin, sin], axis=-1), dtype=F32))


def _block_diag_ones():
    idx = np.arange(MXU_COLS_V7X) // HEAD_DIM
    return jnp.asarray(idx[:, None] == idx[None, :], dtype=BF16)


def kernel(x_prompt, x_sample, cache_k_win, cache_v_win, cache_k_nbr, cache_v_nbr, c, c_ctx, w_mod, b_mod, g_mix, g_ffn, w_in, w_out, qn_win, kn_win, qn_nbr, kn_nbr, sink_win, rel_bias_nbr, conv_w, w_router, b_router, w_gate, b_gate, w_up, b_up, w_down, b_down):
    bsz, n_ctx, d = x_prompt.shape
    dbs, n_lat, _ = x_sample.shape
    depth = w_in.shape[0]
    past = cache_k_win.shape[2]
    assert d == D_MODEL and dbs + 1 <= COND_ROWS and n_lat % GRID_W == 0 and n_lat >= Q_BLK + 2 * WINDOW
    t_ctx, t_lat = bsz * n_ctx, dbs * n_lat

    cond = jnp.concatenate([c_ctx[None], c, jnp.zeros((COND_ROWS - 1 - dbs, d), F32)], axis=0)
    mod = _modulation(cond, w_mod, b_mod)

    w_in_bf = w_in.astype(BF16)
    w_out_bf = w_out.astype(BF16)
    norm_w = jnp.concatenate([jnp.tile(qn_win, (1, N_HEADS_A)), jnp.tile(kn_win, (1, N_KV_A)),
                              jnp.tile(qn_nbr, (1, N_HEADS_B)), jnp.tile(kn_nbr, (1, N_HEADS_B))], axis=-1)[:, None, :]
    ones_bd = _block_diag_ones()
    rope_tabs = _rope_tables(n_lat)
    bias_tab = _nbr_bias_table(rel_bias_nbr.reshape((depth * N_HEADS_B,) + rel_bias_nbr.shape[2:]), n_lat // GRID_W)
    g_mix3, g_ffn3 = g_mix[:, None, :], g_ffn[:, None, :]
    w_router_t = jnp.swapaxes(w_router, 1, 2)
    b_router_t = b_router[:, :, None]
    t_all = t_ctx + t_lat
    ck_win = cache_k_win.reshape(dbs, depth, past, KV_WIDTH_A)
    cv_win = cache_v_win.reshape(dbs, depth, past, KV_WIDTH_A)
    ck_nbr = cache_k_nbr.reshape(dbs, depth, past, WIDTH_B)
    cv_nbr = cache_v_nbr.reshape(dbs, depth, past, WIDTH_B)

    tm_ctx = 2 * n_ctx
    tm_lat = 512
    ctx_group = lambda i: 0
    lat_group_in = lambda i: 1 + i
    lat_group_out = lambda i: 1 + (i * tm_lat) // n_lat
    ctx_tiles = t_ctx // TOK_TILE
    all_group_comb = lambda i: jnp.where(i < ctx_tiles, 0, 1 + ((i - ctx_tiles) * TOK_TILE) // n_lat)
    assert t_ctx % TOK_TILE == 0 and n_lat % TOK_TILE == 0

    xp = x_prompt.reshape(t_ctx, d)
    xs = x_sample.reshape(t_lat, d)
    caches = tuple(jnp.zeros((bsz, depth, n_ctx, w), F32) for w in (KV_WIDTH_A, KV_WIDTH_A, WIDTH_B, WIDTH_B))
    for l in range(depth):
        qa, qb, *caches, oc = _inproj(xp, mod, ctx_group, g_mix3, w_in_bf, l, norm_w, ones_bd, conv_w,
                                      tm_ctx, n_ctx, caches=caches)
        oa, ob = _ctx_attention(sink_win[l], qa, caches[0], caches[1], qb, caches[2], caches[3], n_ctx, l)
        shared = (jnp.zeros((t_all, d), BF16), jnp.zeros((N_EXPERTS, t_all), F32)) if l == 0 else (h_all, lg_all)
        xp_mid, h_all, lg_all = _outproj(xp, oa, ob, oc, w_out_bf, mod, ctx_group, g_ffn3, w_router_t, b_router_t, l,
                                         tm_ctx, t_all, 0, shared)

        qa, qb, ka, va, kb, vb, oc = _inproj(xs, mod, lat_group_in, g_mix3, w_in_bf, l, norm_w, ones_bd, conv_w,
                                             n_lat, n_lat, rope_tabs)
        oa = _win_attention(sink_win[l], qa, ka, va, ck_win, cv_win, l, n_lat)
        ob = _nbr_attention(qb, kb, vb, ck_nbr, cv_nbr, bias_tab, l, n_lat)
        xs_mid, h_all, lg_all = _outproj(xs, oa, ob, oc, w_out_bf, mod, lat_group_out, g_ffn3, w_router_t, b_router_t,
                                         l, tm_lat, t_all, t_ctx, shared=(h_all, lg_all))

        gates, lpos, seg, n_used = _route(lg_all)
        x_sorted = _dispatch(seg, n_used, h_all, lpos, t_all * TOP_K // MOE_BLK + N_EXPERTS)
        y_sorted = _experts(seg, n_used, x_sorted, w_gate, b_gate, w_up, b_up, w_down, b_down, l)
        lpos_t = lpos.T
        xp, xs = _combine(seg, xp_mid, xs_mid, y_sorted, lpos, lpos_t, gates, mod, all_group_comb, l)

    new_k_win, new_v_win = (a.reshape(bsz, depth, n_ctx, N_KV_A, HEAD_DIM) for a in caches[:2])
    new_k_nbr, new_v_nbr = (a.reshape(bsz, depth, n_ctx, N_HEADS_B, HEAD_DIM) for a in caches[2:])
    return (xp.reshape(bsz, n_ctx, d), xs.reshape(dbs, n_lat, d), new_k_win, new_v_win, new_k_nbr, new_v_nbr)
```

```python
import functools

import numpy as np
import jax
import jax.numpy as jnp
from jax import lax
from jax.experimental import pallas as pl
from jax.experimental.pallas import tpu as pltpu

F32 = jnp.float32
BF16 = jnp.bfloat16

D_MODEL = 1024
HEAD_DIM = 64
GRID_W = 64
N_HEADS_A = 8
N_KV_A = 2
GROUP_A = N_HEADS_A // N_KV_A
WINDOW = 128
Q_BLK = 128
N_HEADS_B = 4
NA_ROWS = 8
NA_COLS = 16
CONV_CH = 256
CONV_W = 3
WIDTH_A = N_HEADS_A * HEAD_DIM
KV_WIDTH_A = N_KV_A * HEAD_DIM
WIDTH_B = N_HEADS_B * HEAD_DIM
IN_COLS = WIDTH_A + 2 * KV_WIDTH_A + 3 * WIDTH_B + 3 * CONV_CH
N_EXPERTS = 32
TOP_K = 4
SWIGLU_LIMIT = 7.0
SWIGLU_ALPHA = 1.702
ROPE_BASE = 10000.0
EPS = 1e-6
NEG = -1e30
N_MOD = 6

C_QA = 0
C_KA = C_QA + WIDTH_A
C_VA = C_KA + KV_WIDTH_A
C_QB = C_VA + KV_WIDTH_A
C_KB = C_QB + WIDTH_B
C_VB = C_KB + WIDTH_B
C_U = C_VB + WIDTH_B
C_GB = C_U + CONV_CH
C_GC = C_GB + CONV_CH

MXU_COLS_V7X = 256
COND_ROWS = 8
MOE_BLK = 256
MOE_SHIFT = MOE_BLK.bit_length() - 1
assert 1 << MOE_SHIFT == MOE_BLK
VMEM_LIMIT_V7X = 56 * 1024 * 1024

_NT = (((1,), (1,)), ((), ()))


def _params(sem, vmem=None):
    return pltpu.CompilerParams(dimension_semantics=sem, vmem_limit_bytes=vmem)


def _mod_kernel(c_ref, w_ref, b_ref, o_ref):
    c = c_ref[...]
    s = c * jax.nn.sigmoid(c)
    o_ref[...] = jnp.dot(s.astype(BF16), w_ref[...].astype(BF16), preferred_element_type=F32) + b_ref[...]


def _modulation(cond, w_mod, b_mod):
    depth = w_mod.shape[0]
    out = pl.pallas_call(
        _mod_kernel,
        out_shape=jax.ShapeDtypeStruct((depth, COND_ROWS, N_MOD * D_MODEL), F32),
        grid=(depth, N_MOD),
        in_specs=[
            pl.BlockSpec((COND_ROWS, D_MODEL), lambda l, j: (0, 0)),
            pl.BlockSpec((None, D_MODEL, D_MODEL), lambda l, j: (l, 0, j)),
            pl.BlockSpec((None, 1, D_MODEL), lambda l, j: (l, 0, j)),
        ],
        out_specs=pl.BlockSpec((None, COND_ROWS, D_MODEL), lambda l, j: (l, 0, j)),
        compiler_params=_params(("arbitrary", "arbitrary")),
        name="modulation",
    )(cond, w_mod, b_mod.reshape(depth, 1, N_MOD * D_MODEL))
    return out.reshape(depth, COND_ROWS, N_MOD, D_MODEL)


def _head_norm(x, w_row, ones_ref):
    width = x.shape[1]
    sq = (x * x).astype(BF16)
    parts = []
    for c0 in range(0, width, MXU_COLS_V7X):
        wd = min(MXU_COLS_V7X, width - c0)
        parts.append(jnp.dot(sq[:, c0:c0 + wd], ones_ref[:wd, :wd], preferred_element_type=F32))
    ss = parts[0] if len(parts) == 1 else jnp.concatenate(parts, axis=-1)
    return x * lax.rsqrt(ss * (1.0 / HEAD_DIM) + EPS) * w_row


def _rope(x, cos, sin):
    width = x.shape[1]
    lane = lax.broadcasted_iota(jnp.int32, x.shape, 1)
    quarter = HEAD_DIM // 4
    partner = jnp.where((lane % (2 * quarter)) < quarter,
                        pltpu.roll(x, width - quarter, axis=1), pltpu.roll(x, quarter, axis=1))
    reps = width // cos.shape[1]
    cos_w = cos if reps == 1 else jnp.concatenate([cos] * reps, axis=-1)
    sin_w = sin if reps == 1 else jnp.concatenate([sin] * reps, axis=-1)
    return x * cos_w + partner * sin_w


def _inproj_kernel(*refs, seq_len, rope):
    x_ref, mod_ref, g_ref, w_ref, nw_ref, ones_ref, cw_ref = refs[:7]
    qa_o, qb_o, ka_o, va_o, kb_o, vb_o, oc_o = refs[-7:]
    if rope:
        cos_ref, sin_ref = refs[7:9]
    x = x_ref[...]
    tm = x.shape[0]
    h = x * lax.rsqrt(jnp.mean(x * x, axis=-1, keepdims=True) + EPS) * g_ref[...]
    h = h * (1.0 + mod_ref[1:2, :]) + mod_ref[0:1, :]
    p = jnp.dot(h.astype(BF16), w_ref[...], preferred_element_type=F32)

    qa = _head_norm(p[:, C_QA:C_KA], nw_ref[:, 0:WIDTH_A], ones_ref)
    ka = _head_norm(p[:, C_KA:C_VA], nw_ref[:, WIDTH_A:WIDTH_A + KV_WIDTH_A], ones_ref)
    o_qb = WIDTH_A + KV_WIDTH_A
    qb = _head_norm(p[:, C_QB:C_KB], nw_ref[:, o_qb:o_qb + WIDTH_B], ones_ref)
    kb = _head_norm(p[:, C_KB:C_VB], nw_ref[:, o_qb + WIDTH_B:o_qb + 2 * WIDTH_B], ones_ref)
    if rope:
        cos, sin = cos_ref[...], sin_ref[...]
        qa = _rope(qa, cos, sin)
        ka = _rope(ka, cos, sin)
    qa_o[...] = qa.astype(BF16)
    qb_o[...] = qb.astype(BF16)
    ka_o[...] = ka.reshape(ka_o.shape)
    va_o[...] = p[:, C_VA:C_QB].reshape(va_o.shape)
    kb_o[...] = kb.reshape(kb_o.shape)
    vb_o[...] = p[:, C_VB:C_U].reshape(vb_o.shape)

    z = p[:, C_GC:C_GC + CONV_CH] * p[:, C_U:C_GB]
    row = lax.broadcasted_iota(jnp.int32, z.shape, 0) % seq_len
    z_prev = jnp.where(row == 0, 0.0, pltpu.roll(z, 1, axis=0))
    z_next = jnp.where(row == seq_len - 1, 0.0, pltpu.roll(z, tm - 1, axis=0))
    y = z_prev * cw_ref[0:1, :] + z * cw_ref[1:2, :] + z_next * cw_ref[2:3, :]
    oc_o[...] = (p[:, C_GB:C_GC] * y).astype(BF16)


def _inproj(x, mod, group_of_tile, g_mix, w_in_bf, layer, norm_w, ones_bd, conv_w, tm, seq_len, rope_tabs=None,
            caches=None):
    t = x.shape[0]
    rope = rope_tabs is not None
    in_specs = [
        pl.BlockSpec((tm, D_MODEL), lambda i: (i, 0)),
        pl.BlockSpec((None, None, N_MOD, D_MODEL), lambda i: (layer, group_of_tile(i), 0, 0)),
        pl.BlockSpec((None, 1, D_MODEL), lambda i: (layer, 0, 0)),
        pl.BlockSpec((None, D_MODEL, IN_COLS), lambda i: (layer, 0, 0)),
        pl.BlockSpec((None, 1, norm_w.shape[-1]), lambda i: (layer, 0, 0)),
        pl.BlockSpec(ones_bd.shape, lambda i: (0, 0)),
        pl.BlockSpec((None, CONV_W, CONV_CH), lambda i: (layer, 0, 0)),
    ]
    args = [x, mod, g_mix, w_in_bf, norm_w, ones_bd, conv_w]
    if rope:
        in_specs += [pl.BlockSpec(rope_tabs[0].shape, lambda i: (0, 0))] * 2
        args += list(rope_tabs)
    widths = (WIDTH_A, WIDTH_B, KV_WIDTH_A, KV_WIDTH_A, WIDTH_B, WIDTH_B, CONV_CH)
    dtypes = (BF16, BF16, F32, F32, F32, F32, BF16)
    out_shape = [jax.ShapeDtypeStruct((t, w), dt) for w, dt in zip(widths, dtypes)]
    out_specs = [pl.BlockSpec((tm, w), lambda i: (i, 0)) for w in widths]
    aliases = {}
    if caches is not None:
        seqs = tm // seq_len
        for j, cache in enumerate(caches):
            o = 2 + j
            aliases[len(args)] = o
            in_specs.append(pl.BlockSpec(memory_space=pl.ANY))
            args.append(cache)
            out_shape[o] = jax.ShapeDtypeStruct(cache.shape, F32)
            out_specs[o] = pl.BlockSpec((seqs, None, seq_len, widths[o]), lambda i: (i, layer, 0, 0))
    return pl.pallas_call(
        functools.partial(_inproj_kernel, seq_len=seq_len, rope=rope),
        out_shape=out_shape,
        grid=(t // tm,),
        in_specs=in_specs,
        out_specs=out_specs,
        input_output_aliases=aliases,
        compiler_params=_params(("arbitrary",), VMEM_LIMIT_V7X),
        name="inproj_rope" if rope else "inproj",
    )(*args)


def _pair_attention(q2, k_lo, k_hi, v_lo, v_hi, sinks=None, adjust=None):
    m = k_lo.shape[0]
    k2 = jnp.concatenate([k_lo, k_hi], axis=0)
    v2 = jnp.concatenate([v_lo, v_hi], axis=0)
    s = lax.dot_general(q2, k2, _NT, preferred_element_type=F32)
    ps, dens = [], []
    for i in range(2):
        si = s[:, i * m:(i + 1) * m]
        if adjust is not None:
            si = adjust(i, si)
        mx = jnp.max(si, axis=-1, keepdims=True)
        if sinks is not None:
            mx = jnp.maximum(mx, sinks[i])
        p = jnp.exp(si - mx)
        den = jnp.sum(p, axis=-1, keepdims=True)
        if sinks is not None:
            den = den + jnp.exp(sinks[i] - mx)
        ps.append(p.astype(BF16))
        dens.append(den)
    o = jnp.dot(jnp.concatenate(ps, axis=-1), v2, preferred_element_type=F32)
    lane = lax.broadcasted_iota(jnp.int32, o.shape, 1)
    return o / jnp.where(lane < HEAD_DIM, dens[0], dens[1])


def _lane_halves(x):
    lane = lax.broadcasted_iota(jnp.int32, x.shape, 1)
    lo = lane < HEAD_DIM
    sw = pltpu.roll(x, HEAD_DIM, axis=1)
    z = jnp.zeros_like(x)
    return tuple(jnp.where(c, y, z).astype(BF16) for c, y in ((lo, x), (~lo, x), (lo, sw), (~lo, sw)))


def _ctx_attn_kernel(sink_ref, qa_ref, ka_ref, va_ref, qb_ref, kb_ref, vb_ref, oa_o, ob_o):
    scale = HEAD_DIM ** -0.5
    k0l, k1h, k1l, k0h = _lane_halves(ka_ref[...])
    v0l, v1h, v1l, v0h = _lane_halves(va_ref[...])
    for i in range(N_HEADS_A // 2):
        cols = slice(2 * i * HEAD_DIM, (2 * i + 2) * HEAD_DIM)
        kv = (k0l, k0h, v0l, v0h) if (2 * i) // GROUP_A == 0 else (k1l, k1h, v1l, v1h)
        o = _pair_attention(qa_ref[:, cols] * scale, *kv, sinks=(sink_ref[2 * i], sink_ref[2 * i + 1]))
        oa_o[:, cols] = o.astype(oa_o.dtype)
    for i in range(N_HEADS_B // 2):
        cols = slice(2 * i * HEAD_DIM, (2 * i + 2) * HEAD_DIM)
        kl, kh, _, _ = _lane_halves(kb_ref[:, cols])
        vl, vh, _, _ = _lane_halves(vb_ref[:, cols])
        o = _pair_attention(qb_ref[:, cols] * scale, kl, kh, vl, vh)
        ob_o[:, cols] = o.astype(ob_o.dtype)


def _ctx_attention(sink, qa, ka, va, qb, kb, vb, seq_len, layer):
    t = qa.shape[0]
    q_spec = lambda w: pl.BlockSpec((seq_len, w), lambda i: (i, 0))
    kv_spec = lambda w: pl.BlockSpec((None, None, seq_len, w), lambda i: (i, layer, 0, 0))
    return pl.pallas_call(
        _ctx_attn_kernel,
        out_shape=[jax.ShapeDtypeStruct((t, WIDTH_A), BF16), jax.ShapeDtypeStruct((t, WIDTH_B), BF16)],
        grid=(t // seq_len,),
        in_specs=[pl.BlockSpec(memory_space=pltpu.SMEM), q_spec(WIDTH_A), kv_spec(KV_WIDTH_A), kv_spec(KV_WIDTH_A),
                  q_spec(WIDTH_B), kv_spec(WIDTH_B), kv_spec(WIDTH_B)],
        out_specs=[pl.BlockSpec((seq_len, WIDTH_A), lambda i: (i, 0)),
                   pl.BlockSpec((seq_len, WIDTH_B), lambda i: (i, 0))],
        compiler_params=_params(("arbitrary",)),
        name="ctx_attention",
    )(sink, qa, ka, va, qb, kb, vb)


def _win_attn_kernel(sink_ref, qa_ref, ka_ref, va_ref, ck_ref, cv_ref, oa_o, *, n_lat):
    scale = HEAD_DIM ** -0.5
    span = Q_BLK + 2 * WINDOW
    n = pl.program_id(1)
    start = pl.multiple_of(jnp.clip(n * Q_BLK - WINDOW, 0, n_lat - span), Q_BLK)
    k0l, k1h, k1l, k0h = _lane_halves(jnp.concatenate([ka_ref[pl.ds(start, span), :], ck_ref[...]], axis=0))
    v0l, v1h, v1l, v0h = _lane_halves(jnp.concatenate([va_ref[pl.ds(start, span), :], cv_ref[...]], axis=0))
    n_keys = k0l.shape[0]
    qpos = n * Q_BLK + lax.broadcasted_iota(jnp.int32, (Q_BLK, n_keys), 0)
    col = lax.broadcasted_iota(jnp.int32, (Q_BLK, n_keys), 1)
    ok = (col >= span) | (jnp.abs(qpos - (start + col)) <= WINDOW)
    for i in range(N_HEADS_A // 2):
        cols = slice(2 * i * HEAD_DIM, (2 * i + 2) * HEAD_DIM)
        kv = (k0l, k0h, v0l, v0h) if (2 * i) // GROUP_A == 0 else (k1l, k1h, v1l, v1h)
        o = _pair_attention(qa_ref[:, cols] * scale, *kv, sinks=(sink_ref[2 * i], sink_ref[2 * i + 1]),
                            adjust=lambda _, s: jnp.where(ok, s, NEG))
        oa_o[:, cols] = o.astype(oa_o.dtype)


def _win_attention(sink, qa, ka, va, cache_k, cache_v, layer, n_lat):
    t = qa.shape[0]
    nb = n_lat // Q_BLK
    past = cache_k.shape[2]
    return pl.pallas_call(
        functools.partial(_win_attn_kernel, n_lat=n_lat),
        out_shape=jax.ShapeDtypeStruct((t, WIDTH_A), BF16),
        grid=(t // n_lat, nb),
        in_specs=[
            pl.BlockSpec(memory_space=pltpu.SMEM),
            pl.BlockSpec((Q_BLK, WIDTH_A), lambda b, n: (b * nb + n, 0)),
            pl.BlockSpec((n_lat, KV_WIDTH_A), lambda b, n: (b, 0)),
            pl.BlockSpec((n_lat, KV_WIDTH_A), lambda b, n: (b, 0)),
            pl.BlockSpec((None, None, past, KV_WIDTH_A), lambda b, n: (b, layer, 0, 0)),
            pl.BlockSpec((None, None, past, KV_WIDTH_A), lambda b, n: (b, layer, 0, 0)),
        ],
        out_specs=pl.BlockSpec((Q_BLK, WIDTH_A), lambda b, n: (b * nb + n, 0)),
        compiler_params=_params(("arbitrary", "arbitrary")),
        name="window_attention",
    )(sink, qa, ka, va, cache_k, cache_v)


def _nbr_attn_kernel(qb_ref, kb_ref, vb_ref, ck_ref, cv_ref, bias_ref, ob_o, *, rows, kr):
    scale = HEAD_DIM ** -0.5
    r = pl.program_id(1)
    start = pl.multiple_of(jnp.clip(r - kr // 2, 0, rows - kr) * GRID_W, GRID_W)
    nwin = kr * GRID_W
    kcat = jnp.concatenate([kb_ref[pl.ds(start, nwin), :], ck_ref[...]], axis=0)
    vcat = jnp.concatenate([vb_ref[pl.ds(start, nwin), :], cv_ref[...]], axis=0)
    past = ck_ref.shape[0]
    for i in range(N_HEADS_B // 2):
        cols = slice(2 * i * HEAD_DIM, (2 * i + 2) * HEAD_DIM)
        kl, kh, _, _ = _lane_halves(kcat[:, cols])
        vl, vh, _, _ = _lane_halves(vcat[:, cols])
        o = _pair_attention(
            qb_ref[:, cols] * scale, kl, kh, vl, vh,
            adjust=lambda hh, s, i=i: s + jnp.concatenate([bias_ref[2 * i + hh], jnp.zeros((GRID_W, past), F32)],
                                                          axis=-1))
        ob_o[:, cols] = o.astype(ob_o.dtype)


def _nbr_bias_table(rel_bias, rows):
    kr = min(NA_ROWS, rows)
    cls = np.arange(kr)[:, None]
    m = np.arange(kr)[None, :]
    row_sel = (m - cls + NA_ROWS - 1)[:, :, None] == np.arange(2 * NA_ROWS - 1)[None, None, :]
    c = np.arange(GRID_W)[:, None]
    kc = np.arange(GRID_W)[None, :]
    dc = np.clip(kc - c, -(NA_COLS - 1), NA_COLS - 1) + NA_COLS - 1
    col_sel = dc[:, :, None] == np.arange(2 * NA_COLS - 1)[None, None, :]
    col_start = np.clip(c - NA_COLS // 2, 0, GRID_W - NA_COLS)
    ok = (kc >= col_start) & (kc < col_start + NA_COLS)
    tab = jnp.einsum('hrd,ymr,ckd->hycmk', rel_bias.astype(F32), row_sel.astype(np.float32),
                     col_sel.astype(np.float32), precision=lax.Precision.HIGHEST)
    tab = jnp.where(ok[None, None, :, None, :], tab, NEG)
    return tab.reshape(rel_bias.shape[0], kr, GRID_W, kr * GRID_W)


def _nbr_attention(qb, kb, vb, cache_k, cache_v, bias_tab, layer, n_lat):
    t = qb.shape[0]
    rows = n_lat // GRID_W
    kr = min(NA_ROWS, rows)
    past = cache_k.shape[2]

    def bias_map(b, r):
        return (0, r - jnp.clip(r - kr // 2, 0, rows - kr), 0, 0)

    return pl.pallas_call(
        functools.partial(_nbr_attn_kernel, rows=rows, kr=kr),
        out_shape=jax.ShapeDtypeStruct((t, WIDTH_B), BF16),
        grid=(t // n_lat, rows),
        in_specs=[
            pl.BlockSpec((GRID_W, WIDTH_B), lambda b, r: (b * rows + r, 0)),
            pl.BlockSpec((n_lat, WIDTH_B), lambda b, r: (b, 0)),
            pl.BlockSpec((n_lat, WIDTH_B), lambda b, r: (b, 0)),
            pl.BlockSpec((None, None, past, WIDTH_B), lambda b, r: (b, layer, 0, 0)),
            pl.BlockSpec((None, None, past, WIDTH_B), lambda b, r: (b, layer, 0, 0)),
            pl.BlockSpec((N_HEADS_B, None, GRID_W, kr * GRID_W), bias_map),
        ],
        out_specs=pl.BlockSpec((GRID_W, WIDTH_B), lambda b, r: (b * rows + r, 0)),
        compiler_params=_params(("arbitrary", "arbitrary")),
        name="neighbourhood_attention",
    )(qb, kb, vb, cache_k, cache_v, bias_tab)


ROW_TILE = 8
LANES = 128
assert ROW_TILE * LANES == D_MODEL


PACK_TILE = ROW_TILE // 2


def _store_row_tiles(ref, row0, x):
    n = x.shape[0]
    rt = x.shape[1] // LANES
    for c in range(rt):
        ref[pl.ds(row0 * rt + c, n, stride=rt), :] = x[:, c * LANES:(c + 1) * LANES]


def _pack_bf16_pairs(x):
    bits = lax.bitcast_convert_type(x, jnp.uint32)
    half = x.shape[1] // 2
    return (bits[:, half:] & jnp.uint32(0xFFFF0000)) | (bits[:, :half] >> 16)


def _unpack_bf16_pairs(w):
    lo = lax.bitcast_convert_type(w << 16, F32)
    hi = lax.bitcast_convert_type(w & jnp.uint32(0xFFFF0000), F32)
    return jnp.concatenate([lo, hi], axis=-1).astype(BF16)


def _load_row_tiles(ref, row0, n, rt=ROW_TILE):
    return jnp.concatenate([ref[pl.ds(row0 * rt + c, n, stride=rt), :] for c in range(rt)], axis=-1)


def _outproj_kernel(x_ref, oa_ref, ob_ref, oc_ref, w_ref, mod_ref, g_ref, wr_ref, br_ref, *rest):
    xn_o, h_o, lg_o = rest[-3:]
    mix = jnp.dot(oa_ref[...], w_ref[0:WIDTH_A, :], preferred_element_type=F32)
    mix += jnp.dot(ob_ref[...], w_ref[WIDTH_A:WIDTH_A + WIDTH_B, :], preferred_element_type=F32)
    mix += jnp.dot(oc_ref[...], w_ref[WIDTH_A + WIDTH_B:, :], preferred_element_type=F32)
    xn = x_ref[...] + mod_ref[2:3, :] * mix
    xn_o[...] = xn
    h = xn * lax.rsqrt(jnp.mean(xn * xn, axis=-1, keepdims=True) + EPS) * g_ref[...]
    h = h * (1.0 + mod_ref[4:5, :]) + mod_ref[3:4, :]
    h_hi = h.astype(BF16)
    h_o[...] = h_hi
    h_lo = (h - h_hi.astype(F32)).astype(BF16)
    wr = wr_ref[...]
    w_hi = wr.astype(BF16)
    w_lo = (wr - w_hi.astype(F32)).astype(BF16)
    n_e = wr.shape[0]
    both = lax.dot_general(jnp.concatenate([w_hi, w_lo], axis=0), h_hi, _NT, preferred_element_type=F32)
    lg = both[:n_e] + both[n_e:] + lax.dot_general(w_hi, h_lo, _NT, preferred_element_type=F32)
    lg_o[...] = lg + br_ref[...]


def _outproj(x, oa, ob, oc, w_out_bf, mod, group_of_tile, g_ffn, w_router_t, b_router_t, layer, tm, t_all, tok0,
             shared):
    t = x.shape[0]
    b0 = tok0 // tm
    assert tok0 % tm == 0
    in_specs = [
        pl.BlockSpec((tm, D_MODEL), lambda i: (i, 0)),
        pl.BlockSpec((tm, WIDTH_A), lambda i: (i, 0)),
        pl.BlockSpec((tm, WIDTH_B), lambda i: (i, 0)),
        pl.BlockSpec((tm, CONV_CH), lambda i: (i, 0)),
        pl.BlockSpec((None, D_MODEL, D_MODEL), lambda i: (layer, 0, 0)),
        pl.BlockSpec((None, None, N_MOD, D_MODEL), lambda i: (layer, group_of_tile(i), 0, 0)),
        pl.BlockSpec((None, 1, D_MODEL), lambda i: (layer, 0, 0)),
        pl.BlockSpec((None, N_EXPERTS, D_MODEL), lambda i: (layer, 0, 0)),
        pl.BlockSpec((None, N_EXPERTS, 1), lambda i: (layer, 0, 0)),
    ]
    args = [x, oa, ob, oc, w_out_bf, mod, g_ffn, w_router_t, b_router_t]
    aliases = {len(args): 1, len(args) + 1: 2}
    in_specs += [pl.BlockSpec(memory_space=pl.ANY)] * 2
    args += list(shared)
    return pl.pallas_call(
        _outproj_kernel,
        out_shape=[jax.ShapeDtypeStruct((t, D_MODEL), F32), jax.ShapeDtypeStruct((t_all, D_MODEL), BF16),
                   jax.ShapeDtypeStruct((N_EXPERTS, t_all), F32)],
        grid=(t // tm,),
        in_specs=in_specs,
        out_specs=[pl.BlockSpec((tm, D_MODEL), lambda i: (i, 0)),
                   pl.BlockSpec((tm, D_MODEL), lambda i: (b0 + i, 0)),
                   pl.BlockSpec((N_EXPERTS, tm), lambda i: (0, b0 + i))],
        input_output_aliases=aliases,
        compiler_params=_params(("arbitrary",), VMEM_LIMIT_V7X),
        name="outproj_router",
    )(*args)


X_AHEAD = 3
X_SLOTS = X_AHEAD + 1
Y_SLOTS = 2


def _expert_kernel(seg_ref, n_used_ref, x_hbm, wg_ref, bg_ref, wu_ref, bu_ref, wd_ref, bd_ref, y_hbm,
                   x_buf, y_buf, xsem, ysem, zsem, *, n_blocks):
    e = pl.program_id(0)
    n_used = n_used_ref[0]
    blk_rows = MOE_BLK * ROW_TILE
    first_blk = seg_ref[SEG_DST, e, SEG_BLK]
    n_blk = seg_ref[SEG_CNT, e, SEG_BLK]

    def rows_of(g, rows=blk_rows):
        return pl.ds(pl.multiple_of(g * rows, rows), rows)

    def x_copy(g):
        return pltpu.make_async_copy(x_hbm.at[rows_of(g, MOE_BLK * PACK_TILE)], x_buf.at[g % X_SLOTS],
                                     xsem.at[g % X_SLOTS])

    def y_copy(g):
        return pltpu.make_async_copy(y_buf.at[g % Y_SLOTS], y_hbm.at[rows_of(g)], ysem.at[g % Y_SLOTS])

    def zero_copy(g):
        return pltpu.make_async_copy(y_buf.at[0], y_hbm.at[rows_of(g)], zsem)

    @pl.when(e == 0)
    def _():
        for g in range(X_AHEAD):
            @pl.when(g < n_used)
            def _():
                x_copy(g).start()

    def block(b, carry):
        g = first_blk + b

        @pl.when(g + X_AHEAD < n_used)
        def _():
            x_copy(g + X_AHEAD).start()

        x_copy(g).wait()

        @pl.when(g >= Y_SLOTS)
        def _():
            y_copy(g - Y_SLOTS).wait()

        x = _unpack_bf16_pairs(_load_row_tiles(x_buf.at[g % X_SLOTS], 0, MOE_BLK, PACK_TILE))
        gt = jnp.dot(x, wg_ref[...].astype(BF16), preferred_element_type=F32) + bg_ref[...]
        up = jnp.dot(x, wu_ref[...].astype(BF16), preferred_element_type=F32) + bu_ref[...]
        gt = jnp.minimum(gt, SWIGLU_LIMIT)
        up = jnp.clip(up, -SWIGLU_LIMIT, SWIGLU_LIMIT)
        a = gt * jax.nn.sigmoid(SWIGLU_ALPHA * gt) * (up + 1.0)
        y = jnp.dot(a.astype(BF16), wd_ref[...].astype(BF16), preferred_element_type=F32) + bd_ref[...]
        _store_row_tiles(y_buf.at[g % Y_SLOTS], 0, y)
        y_copy(g).start()
        return carry

    lax.fori_loop(0, n_blk, block, 0)

    @pl.when(e == pl.num_programs(0) - 1)
    def _():
        for d in range(Y_SLOTS, 0, -1):
            @pl.when(n_used >= d)
            def _():
                y_copy(n_used - d).wait()

        y_buf[0] = jnp.zeros((blk_rows, LANES), F32)

        def start_zero(g, carry):
            zero_copy(g).start()
            return carry

        def wait_zero(g, carry):
            zero_copy(g).wait()
            return carry

        lax.fori_loop(n_used, n_blocks, start_zero, 0)
        lax.fori_loop(n_used, n_blocks, wait_zero, 0)


def _experts(seg, n_used, x_sorted, w_gate, b_gate, w_up, b_up, w_down, b_down, layer):
    n_blocks = x_sorted.shape[0] // (MOE_BLK * PACK_TILE)
    wspec = pl.BlockSpec((None, None, D_MODEL, D_MODEL), lambda e, sg, nu: (layer, e, 0, 0))
    bspec = pl.BlockSpec((None, None, 1, D_MODEL), lambda e, sg, nu: (layer, e, 0, 0))
    anyspec = pl.BlockSpec(memory_space=pl.ANY)
    depth = w_gate.shape[0]
    b4 = lambda b: b.reshape(depth, N_EXPERTS, 1, D_MODEL)
    return pl.pallas_call(
        functools.partial(_expert_kernel, n_blocks=n_blocks),
        out_shape=jax.ShapeDtypeStruct((n_blocks * MOE_BLK * ROW_TILE, LANES), F32),
        grid_spec=pltpu.PrefetchScalarGridSpec(
            num_scalar_prefetch=2,
            grid=(N_EXPERTS,),
            in_specs=[anyspec, wspec, bspec, wspec, bspec, wspec, bspec],
            out_specs=anyspec,
            scratch_shapes=[
                pltpu.VMEM((X_SLOTS, MOE_BLK * PACK_TILE, LANES), jnp.uint32),
                pltpu.VMEM((Y_SLOTS, MOE_BLK * ROW_TILE, LANES), F32),
                pltpu.SemaphoreType.DMA((X_SLOTS,)),
                pltpu.SemaphoreType.DMA((Y_SLOTS,)),
                pltpu.SemaphoreType.DMA,
            ],
        ),
        compiler_params=_params(("arbitrary",), VMEM_LIMIT_V7X),
        name="experts",
    )(seg, n_used, x_sorted, w_gate, b4(b_gate), w_up, b4(b_up), w_down, b4(b_down))


CUM_CHUNK = 256


def _sublane_cumsum(x):
    row = lax.broadcasted_iota(jnp.int32, x.shape, 0)
    d = 1
    while d < x.shape[0]:
        x = x + jnp.where(row >= d, pltpu.roll(x, d, axis=0), 0)
        d *= 2
    return x


TOK_TILE = CUM_CHUNK
TILE_ROWS = TOP_K * TOK_TILE
SEG_LANES = 128
SEG_PAD_FROM = SEG_LANES - 2
SEG_PAD_LEN = SEG_LANES - 1
SEG_BLK = SEG_LANES - 3
SEG_SRC, SEG_CNT, SEG_DST = 0, 1, 2


def _route_kernel(lg_ref, tri_ref, gate_o, lpos_o, seg_o, used_o):
    lg = lg_ref[...]
    n_e, t = lg.shape
    e_iota = lax.broadcasted_iota(jnp.int32, lg.shape, 0)
    work = lg
    tops, hots = [], []
    for _ in range(TOP_K):
        m = jnp.max(work, axis=0, keepdims=True)
        first = jnp.min(jnp.where(work == m, e_iota, n_e), axis=0, keepdims=True)
        hot = e_iota == first
        work = jnp.where(hot, -jnp.inf, work)
        tops.append(m)
        hots.append(hot)
    ex = [jnp.exp(m - tops[0]) for m in tops]
    den = ex[0] + ex[1] + ex[2] + ex[3]
    for k in range(TOP_K):
        gate_o[k:k + 1, :] = ex[k] / den

    chosen = jnp.where(hots[0] | hots[1] | hots[2] | hots[3], 1.0, 0.0)
    tri = tri_ref[...]
    lane = lax.broadcasted_iota(jnp.int32, (n_e, SEG_LANES), 1)
    seg_cnt = jnp.zeros((n_e, SEG_LANES), jnp.int32)
    seg_before = jnp.zeros((n_e, SEG_LANES), jnp.int32)
    carry = jnp.zeros((n_e, 1), F32)
    rank_in_tile = []
    n_tiles = t // TOK_TILE
    for c in range(n_tiles):
        chunk = chosen[:, c * TOK_TILE:(c + 1) * TOK_TILE]
        inc = jnp.dot(chunk.astype(BF16), tri, preferred_element_type=F32)
        cnt = inc[:, TOK_TILE - 1:TOK_TILE]
        rank_in_tile.append((inc - chunk).astype(jnp.int32))
        seg_cnt = jnp.where(lane == c, cnt.astype(jnp.int32), seg_cnt)
        seg_before = jnp.where(lane == c, carry.astype(jnp.int32), seg_before)
        carry = carry + cnt

    counts = jnp.broadcast_to(carry.astype(jnp.int32), (n_e, SEG_LANES))
    padded = ((counts + (MOE_BLK - 1)) >> MOE_SHIFT) << MOE_SHIFT
    pad_end = _sublane_cumsum(padded)
    pad_start = pad_end - padded
    seg_src = _sublane_cumsum(seg_cnt) - seg_cnt
    seg_o[SEG_SRC] = seg_src
    seg_o[SEG_CNT] = jnp.where(lane == SEG_PAD_LEN, padded - counts,
                               jnp.where(lane == SEG_BLK, padded >> MOE_SHIFT, seg_cnt))
    seg_o[SEG_DST] = jnp.where(lane == SEG_PAD_FROM, pad_start + counts,
                               jnp.where(lane == SEG_BLK, pad_start >> MOE_SHIFT, pad_start + seg_before))

    for c in range(n_tiles):
        pos = seg_src[:, c:c + 1] + rank_in_tile[c]
        for k in range(TOP_K):
            lpos_o[k:k + 1, c * TOK_TILE:(c + 1) * TOK_TILE] = jnp.sum(
                jnp.where(hots[k][:, c * TOK_TILE:(c + 1) * TOK_TILE], pos, 0), axis=0, keepdims=True)

    used_o[...] = jnp.broadcast_to(pad_end[n_e - 1:n_e, :] >> MOE_SHIFT, used_o.shape)


def _route(logits):
    t = logits.shape[1]
    assert t % TOK_TILE == 0 and t // TOK_TILE <= SEG_BLK
    tri = jnp.asarray(np.triu(np.ones((CUM_CHUNK, CUM_CHUNK), np.float32)), dtype=BF16)
    gates, lpos, seg, used = pl.pallas_call(
        _route_kernel,
        out_shape=[jax.ShapeDtypeStruct((TOP_K, t), F32), jax.ShapeDtypeStruct((TOP_K, t), jnp.int32),
                   jax.ShapeDtypeStruct((3, N_EXPERTS, SEG_LANES), jnp.int32),
                   jax.ShapeDtypeStruct((8, SEG_LANES), jnp.int32)],
        compiler_params=_params(None, VMEM_LIMIT_V7X),
        name="route",
    )(logits, tri)
    return gates, lpos, seg, used[0, :1]


def _segment_copies(src, dst, src0, dst0, cnt, sem, wait, rt=ROW_TILE, priority=0):
    @pl.when(cnt > 0)
    def _():
        size = cnt * rt
        cp = pltpu.make_async_copy(
            src.at[pl.ds(pl.multiple_of(src0 * rt, rt), size)],
            dst.at[pl.ds(pl.multiple_of(dst0 * rt, rt), size)], sem)
        if wait:
            cp.wait()
        else:
            cp.start(priority=priority)


Z_SLOTS = 3


def _dispatch_kernel(seg_ref, nu_ref, h_ref, lpos_ref, x_hbm, z_buf, zero_buf, sem, zsem, *, n_tiles, n_blocks):
    c = pl.program_id(0)
    slot = c % Z_SLOTS

    def wait_tile(s):
        pltpu.make_async_copy(z_buf.at[s], x_hbm.at[pl.ds(0, TILE_ROWS * PACK_TILE)], sem.at[s]).wait()

    def padding(wait):
        def per_expert(e, carry):
            _segment_copies(zero_buf, x_hbm, 0, seg_ref[SEG_DST, e, SEG_PAD_FROM], seg_ref[SEG_CNT, e, SEG_PAD_LEN],
                            zsem, wait, PACK_TILE)
            return carry

        lax.fori_loop(0, N_EXPERTS, per_expert, 0)

        def per_block(b, carry):
            cp = pltpu.make_async_copy(
                zero_buf, x_hbm.at[pl.ds(pl.multiple_of(b * (MOE_BLK * PACK_TILE), MOE_BLK * PACK_TILE),
                                         MOE_BLK * PACK_TILE)], zsem)
            if wait:
                cp.wait()
            else:
                cp.start()
            return carry

        lax.fori_loop(nu_ref[0], n_blocks, per_block, 0)

    @pl.when(c >= Z_SLOTS)
    def _():
        wait_tile(slot)

    row = lax.broadcasted_iota(jnp.int32, (TILE_ROWS, TOK_TILE), 0)
    hit = row == lpos_ref[0:1, :]
    for k in range(1, TOP_K):
        hit = hit | (row == lpos_ref[k:k + 1, :])
    z = jnp.dot(jnp.where(hit, 1.0, 0.0).astype(BF16), h_ref[...], preferred_element_type=F32)
    _store_row_tiles(z_buf.at[slot], 0, _pack_bf16_pairs(z))

    for e in range(N_EXPERTS):
        _segment_copies(z_buf.at[slot], x_hbm, seg_ref[SEG_SRC, e, c], seg_ref[SEG_DST, e, c], seg_ref[SEG_CNT, e, c],
                        sem.at[slot], False, PACK_TILE, priority=e % 2)

    @pl.when(c == n_tiles - 1)
    def _():
        zero_buf[...] = jnp.zeros(zero_buf.shape, jnp.uint32)
        padding(False)
        for back in range(min(Z_SLOTS, n_tiles) - 1, -1, -1):
            wait_tile((n_tiles - 1 - back) % Z_SLOTS)
        padding(True)


def _dispatch(seg, n_used, h_all, lpos, n_blocks):
    t = h_all.shape[0]
    n_tiles = t // TOK_TILE
    return pl.pallas_call(
        functools.partial(_dispatch_kernel, n_tiles=n_tiles, n_blocks=n_blocks),
        out_shape=jax.ShapeDtypeStruct((n_blocks * MOE_BLK * PACK_TILE, LANES), jnp.uint32),
        grid_spec=pltpu.PrefetchScalarGridSpec(
            num_scalar_prefetch=2,
            grid=(n_tiles,),
            in_specs=[pl.BlockSpec((TOK_TILE, D_MODEL), lambda c, sg, nu: (c, 0)),
                      pl.BlockSpec((TOP_K, TOK_TILE), lambda c, sg, nu: (0, c))],
            out_specs=pl.BlockSpec(memory_space=pl.ANY),
            scratch_shapes=[
                pltpu.VMEM((Z_SLOTS, TILE_ROWS * PACK_TILE, LANES), jnp.uint32),
                pltpu.VMEM((MOE_BLK * PACK_TILE, LANES), jnp.uint32),
                pltpu.SemaphoreType.DMA((Z_SLOTS,)),
                pltpu.SemaphoreType.DMA,
            ],
        ),
        compiler_params=_params(("arbitrary",), VMEM_LIMIT_V7X),
        name="dispatch",
    )(seg, n_used, h_all, lpos)


def _combine_kernel(seg_ref, xa_ref, xb_ref, y_hbm, lpos_ref, lpos_t_ref, gate_ref, mod_ref, oa_ref, ob_ref, y_buf, sem,
                    *, n_a, n_tiles):
    c = pl.program_id(0)
    slot = c % 2

    def fetch(cc):
        def per_expert(e, carry):
            _segment_copies(y_hbm, y_buf.at[cc % 2], seg_ref[SEG_DST, e, cc], seg_ref[SEG_SRC, e, cc],
                            seg_ref[SEG_CNT, e, cc], sem.at[cc % 2], False)
            return carry

        lax.fori_loop(0, N_EXPERTS, per_expert, 0)

    @pl.when(c == 0)
    def _():
        fetch(0)

    @pl.when(c + 1 < n_tiles)
    def _():
        fetch(c + 1)

    pltpu.make_async_copy(y_hbm.at[pl.ds(0, TILE_ROWS * ROW_TILE)], y_buf.at[slot], sem.at[slot]).wait()

    row = lax.broadcasted_iota(jnp.int32, (TILE_ROWS, TOK_TILE), 0)
    row_gate = jnp.zeros((TILE_ROWS, 1), F32)
    for k in range(TOP_K):
        row_gate += jnp.sum(jnp.where(row == lpos_ref[k:k + 1, :], gate_ref[k:k + 1, :], 0.0), axis=1, keepdims=True)
    ys = _load_row_tiles(y_buf.at[slot], 0, TILE_ROWS) * row_gate
    ys_hi = ys.astype(BF16)
    ys_lo = (ys - ys_hi.astype(F32)).astype(BF16)
    col = lax.broadcasted_iota(jnp.int32, (TOK_TILE, TILE_ROWS), 1)
    hit = col == lpos_t_ref[:, 0:1]
    for k in range(1, TOP_K):
        hit = hit | (col == lpos_t_ref[:, k:k + 1])
    u = jnp.where(hit, 1.0, 0.0).astype(BF16)
    mix = mod_ref[5:6, :] * (jnp.dot(u, ys_hi, preferred_element_type=F32)
                             + jnp.dot(u, ys_lo, preferred_element_type=F32))

    @pl.when(c < n_a)
    def _():
        oa_ref[...] = xa_ref[...] + mix

    @pl.when(c >= n_a)
    def _():
        ob_ref[...] = xb_ref[...] + mix


def _combine(seg, xa_mid, xb_mid, y_sorted, lpos, lpos_t, gates, mod, group_of_tile, layer):
    n_a = xa_mid.shape[0] // TOK_TILE
    n_tiles = n_a + xb_mid.shape[0] // TOK_TILE
    a_spec = pl.BlockSpec((TOK_TILE, D_MODEL), lambda c, sg: (jnp.minimum(c, n_a - 1), 0))
    b_spec = pl.BlockSpec((TOK_TILE, D_MODEL), lambda c, sg: (jnp.maximum(c - n_a, 0), 0))
    return pl.pallas_call(
        functools.partial(_combine_kernel, n_a=n_a, n_tiles=n_tiles),
        out_shape=[jax.ShapeDtypeStruct(xa_mid.shape, F32), jax.ShapeDtypeStruct(xb_mid.shape, F32)],
        grid_spec=pltpu.PrefetchScalarGridSpec(
            num_scalar_prefetch=1,
            grid=(n_tiles,),
            in_specs=[
                a_spec, b_spec,
                pl.BlockSpec(memory_space=pl.ANY),
                pl.BlockSpec((TOP_K, TOK_TILE), lambda c, sg: (0, c)),
                pl.BlockSpec((TOK_TILE, TOP_K), lambda c, sg: (c, 0)),
                pl.BlockSpec((TOP_K, TOK_TILE), lambda c, sg: (0, c)),
                pl.BlockSpec((None, None, N_MOD, D_MODEL), lambda c, sg: (layer, group_of_tile(c), 0, 0)),
            ],
            out_specs=[a_spec, b_spec],
            scratch_shapes=[pltpu.VMEM((2, TILE_ROWS * ROW_TILE, LANES), F32), pltpu.SemaphoreType.DMA((2,))],
        ),
        compiler_params=_params(("arbitrary",), VMEM_LIMIT_V7X),
        name="combine",
    )(seg, xa_mid, xb_mid, y_sorted, lpos, lpos_t, gates, mod)


def _rope_tables(n_lat):
    quarter = HEAD_DIM // 4
    t = np.arange(n_lat)
    inv = np.float32(ROPE_BASE) ** (-np.arange(quarter, dtype=np.float32) / np.float32(quarter))
    ang_r = (t // GRID_W).astype(np.float32)[:, None] * inv
    ang_c = (t % GRID_W).astype(np.float32)[:, None] * inv
    cos = np.concatenate([np.cos(ang_r)] * 2 + [np.cos(ang_c)] * 2, axis=-1)
    sin = np.concatenate([-np.sin(ang_r), np.sin(ang_r), -np.sin(ang_c), np.sin(ang_c)], axis=-1)
    return (jnp.asarray(np.concatenate([cos, cos], axis=-1), dtype=F32),
            jnp.asarray(np.concatenate([sin, sin], axis=-1), dtype=F32))


def _block_diag_ones():
    idx = np.arange(MXU_COLS_V7X) // HEAD_DIM
    return jnp.asarray(idx[:, None] == idx[None, :], dtype=BF16)


def kernel(x_prompt, x_sample, cache_k_win, cache_v_win, cache_k_nbr, cache_v_nbr, c, c_ctx, w_mod, b_mod, g_mix, g_ffn, w_in, w_out, qn_win, kn_win, qn_nbr, kn_nbr, sink_win, rel_bias_nbr, conv_w, w_router, b_router, w_gate, b_gate, w_up, b_up, w_down, b_down):
    bsz, n_ctx, d = x_prompt.shape
    dbs, n_lat, _ = x_sample.shape
    depth = w_in.shape[0]
    past = cache_k_win.shape[2]
    assert d == D_MODEL and dbs + 1 <= COND_ROWS and n_lat % GRID_W == 0 and n_lat >= Q_BLK + 2 * WINDOW
    t_ctx, t_lat = bsz * n_ctx, dbs * n_lat

    cond = jnp.concatenate([c_ctx[None], c, jnp.zeros((COND_ROWS - 1 - dbs, d), F32)], axis=0)
    mod = _modulation(cond, w_mod, b_mod)

    w_in_bf = w_in.astype(BF16)
    w_out_bf = w_out.astype(BF16)
    norm_w = jnp.concatenate([jnp.tile(qn_win, (1, N_HEADS_A)), jnp.tile(kn_win, (1, N_KV_A)),
                              jnp.tile(qn_nbr, (1, N_HEADS_B)), jnp.tile(kn_nbr, (1, N_HEADS_B))], axis=-1)[:, None, :]
    ones_bd = _block_diag_ones()
    rope_tabs = _rope_tables(n_lat)
    g_mix3, g_ffn3 = g_mix[:, None, :], g_ffn[:, None, :]
    w_router_t = jnp.swapaxes(w_router, 1, 2)
    b_router_t = b_router[:, :, None]
    t_all = t_ctx + t_lat
    ck_win = cache_k_win.reshape(dbs, depth, past, KV_WIDTH_A)
    cv_win = cache_v_win.reshape(dbs, depth, past, KV_WIDTH_A)
    ck_nbr = cache_k_nbr.reshape(dbs, depth, past, WIDTH_B)
    cv_nbr = cache_v_nbr.reshape(dbs, depth, past, WIDTH_B)

    tm_ctx = 2 * n_ctx
    tm_lat = 512
    ctx_group = lambda i: 0
    lat_group_in = lambda i: 1 + i
    lat_group_out = lambda i: 1 + (i * tm_lat) // n_lat
    ctx_tiles = t_ctx // TOK_TILE
    all_group_comb = lambda i: jnp.where(i < ctx_tiles, 0, 1 + ((i - ctx_tiles) * TOK_TILE) // n_lat)
    assert t_ctx % TOK_TILE == 0 and n_lat % TOK_TILE == 0

    xp = x_prompt.reshape(t_ctx, d)
    xs = x_sample.reshape(t_lat, d)
    caches = tuple(jnp.zeros((bsz, depth, n_ctx, w), F32) for w in (KV_WIDTH_A, KV_WIDTH_A, WIDTH_B, WIDTH_B))
    for l in range(depth):
        qa, qb, *caches, oc = _inproj(xp, mod, ctx_group, g_mix3, w_in_bf, l, norm_w, ones_bd, conv_w,
                                      tm_ctx, n_ctx, caches=caches)
        oa, ob = _ctx_attention(sink_win[l], qa, caches[0], caches[1], qb, caches[2], caches[3], n_ctx, l)
        shared = (jnp.zeros((t_all, d), BF16), jnp.zeros((N_EXPERTS, t_all), F32)) if l == 0 else (h_all, lg_all)
        xp_mid, h_all, lg_all = _outproj(xp, oa, ob, oc, w_out_bf, mod, ctx_group, g_ffn3, w_router_t, b_router_t, l,
                                         tm_ctx, t_all, 0, shared)

        qa, qb, ka, va, kb, vb, oc = _inproj(xs, mod, lat_group_in, g_mix3, w_in_bf, l, norm_w, ones_bd, conv_w,
                                             n_lat, n_lat, rope_tabs)
        oa = _win_attention(sink_win[l], qa, ka, va, ck_win, cv_win, l, n_lat)
        ob = _nbr_attention(qb, kb, vb, ck_nbr, cv_nbr, _nbr_bias_table(rel_bias_nbr[l], n_lat // GRID_W), l, n_lat)
        xs_mid, h_all, lg_all = _outproj(xs, oa, ob, oc, w_out_bf, mod, lat_group_out, g_ffn3, w_router_t, b_router_t,
                                         l, tm_lat, t_all, t_ctx, shared=(h_all, lg_all))

        gates, lpos, seg, n_used = _route(lg_all)
        x_sorted = _dispatch(seg, n_used, h_all, lpos, t_all * TOP_K // MOE_BLK + N_EXPERTS)
        y_sorted = _experts(seg, n_used, x_sorted, w_gate, b_gate, w_up, b_up, w_down, b_down, l)
        lpos_t = lpos.T
        xp, xs = _combine(seg, xp_mid, xs_mid, y_sorted, lpos, lpos_t, gates, mod, all_group_comb, l)

    new_k_win, new_v_win = (a.reshape(bsz, depth, n_ctx, N_KV_A, HEAD_DIM) for a in caches[:2])
    new_k_nbr, new_v_nbr = (a.reshape(bsz, depth, n_ctx, N_HEADS_B, HEAD_DIM) for a in caches[2:])
    return (xp.reshape(bsz, n_ctx, d), xs.reshape(dbs, n_lat, d), new_k_win, new_v_win, new_k_nbr, new_v_nbr)
```
